```python
import math
import jax, jax.numpy as jnp
from jax import lax
import numpy as np

D_MODEL = 2048
BATCH = 4
SEQ = 2048
DEPTH = 2
DEC_BATCH = 128
DEC_SEQ = 4
PAST_LEN = 8192
PAGE_SIZE = 128

MIX_W = D_MODEL
SSM_W = MIX_W // 2
ATTN_W = MIX_W - SSM_W
SSM_GROUP_CH = 16
SSM_GROUPS = SSM_W // SSM_GROUP_CH
SSM_STATE = 64
HEAD_DIM = 64
N_HEADS = ATTN_W // HEAD_DIM
N_KV_HEADS = 4
Q_PER_KV = N_HEADS // N_KV_HEADS
KV_W = N_KV_HEADS * HEAD_DIM
IN_W = SSM_W + ATTN_W + 2 * KV_W
WINDOW = 128
BLOCK = WINDOW
CACHE_W = min(WINDOW, PAST_LEN)
PROMPT_CACHE_W = min(WINDOW, SEQ)
D_FF = ((8 * D_MODEL // 3 + 255) // 256) * 256
EPS = 1e-6

kernel_name = "hymba_s5_swa_macaron_step"


def rms_norm(x, g):
    x32 = x.astype(jnp.float32)
    y = x32 * lax.rsqrt(jnp.mean(x32 * x32, axis=-1, keepdims=True) + EPS)
    return (y * g.astype(jnp.float32)).astype(x.dtype)


def swiglu(x, w_gate, w_up, w_down):
    return (jax.nn.silu(x @ w_gate) * (x @ w_up)) @ w_down


def alibi_slopes():
    h = jnp.arange(1, N_HEADS + 1, dtype=jnp.float32)
    return jnp.exp2(-8.0 * h / N_HEADS)


def _ssm_combine(e1, e2):
    a1, b1 = e1
    a2, b2 = e2
    return a1 * a2, a2 * b1 + b2


def s5_mixer(u, h0, A_re, A_im, B_re, B_im, C_re, C_im, D, log_dt, w_glu):
    f32 = jnp.float32
    lam = lax.complex(A_re.astype(f32), A_im.astype(f32))
    dt = jnp.exp(log_dt.astype(f32))[:, None]
    lam_bar = jnp.exp(lam * dt)
    Bc = lax.complex(B_re.astype(f32), B_im.astype(f32))
    B_bar = ((lam_bar - 1.0) / lam)[..., None] * Bc
    Cc = lax.complex(C_re.astype(f32), C_im.astype(f32))
    u32 = u.astype(f32)
    bu = jnp.einsum('blgc,gpc->blgp', u32.astype(jnp.complex64), B_bar)
    bu = bu.at[:, 0].add(lam_bar * h0)
    a = jnp.broadcast_to(lam_bar, bu.shape)
    _, h = lax.associative_scan(_ssm_combine, (a, bu), axis=1)
    y = jnp.einsum('blgp,gcp->blgc', h, Cc).real + D.astype(f32) * u32
    y = jax.nn.gelu(y)
    y = y * jax.nn.sigmoid(jnp.einsum('blgc,gcd->blgd', y, w_glu.astype(f32)))
    return y, h[:, -1]


def window_attention(q, k, v, q_pos, k_pos, sinks):
    b, n, lq = q.shape[:3]
    qg = q.reshape(b, n, lq, N_KV_HEADS, Q_PER_KV, HEAD_DIM)
    s = jnp.einsum('bnqkgd,bnskd->bnkgqs', qg, k).astype(jnp.float32) * (HEAD_DIM ** -0.5)
    dist = q_pos[:, :, None] - k_pos[:, None, :]
    valid = (dist >= 0) & (dist < WINDOW) & (k_pos[:, None, :] >= 0)
    slopes = alibi_slopes().reshape(N_KV_HEADS, Q_PER_KV)
    bias = -slopes[None, :, :, None, None] * dist[:, None, None].astype(jnp.float32)
    logits = jnp.where(valid[:, None, None], s + bias, -jnp.inf)
    sink = sinks.astype(jnp.float32).reshape(N_KV_HEADS, Q_PER_KV)[:, :, None, None]
    m = jnp.maximum(logits.max(axis=-1, keepdims=True), sink)
    p = jnp.exp(logits - m)
    p = p / (p.sum(axis=-1, keepdims=True) + jnp.exp(sink - m))
    o = jnp.einsum('bnkgqs,bnskd->bnqkgd', p.astype(v.dtype), v)
    return o.reshape(b, n * lq, N_HEADS * HEAD_DIM)


def decoder_layer(x, lp, h0, k_past, v_past):
    bsz, L, _ = x.shape
    x = x + 0.5 * swiglu(rms_norm(x, lp['ffn1_norm']), lp['ffn1_w_gate'], lp['ffn1_w_up'], lp['ffn1_w_down'])
    hn = rms_norm(x, lp['mix_norm'])
    z = hn @ lp['w_in']
    u, q, k, v = jnp.split(z, [SSM_W, SSM_W + ATTN_W, SSM_W + ATTN_W + KV_W], axis=-1)
    y_ssm, h_last = s5_mixer(u.reshape(bsz, L, SSM_GROUPS, SSM_GROUP_CH), h0,
                             lp['ssm_A_re'], lp['ssm_A_im'], lp['ssm_B_re'], lp['ssm_B_im'],
                             lp['ssm_C_re'], lp['ssm_C_im'], lp['ssm_D'], lp['ssm_log_dt'], lp['ssm_w_glu'])
    y_ssm = y_ssm.reshape(bsz, L, SSM_W).astype(x.dtype)
    q = rms_norm(q.reshape(bsz, L, N_HEADS, HEAD_DIM), lp['q_norm'])
    k = rms_norm(k.reshape(bsz, L, N_KV_HEADS, HEAD_DIM), lp['k_norm'])
    v = v.reshape(bsz, L, N_KV_HEADS, HEAD_DIM)
    if k_past is None:
        nb = L // BLOCK
        qb = q.reshape(bsz, nb, BLOCK, N_HEADS, HEAD_DIM)
        kb = k.reshape(bsz, nb, BLOCK, N_KV_HEADS, HEAD_DIM)
        vb = v.reshape(bsz, nb, BLOCK, N_KV_HEADS, HEAD_DIM)
        kk = jnp.concatenate([jnp.concatenate([jnp.zeros_like(kb[:, :1]), kb[:, :-1]], axis=1), kb], axis=2)
        vv = jnp.concatenate([jnp.concatenate([jnp.zeros_like(vb[:, :1]), vb[:, :-1]], axis=1), vb], axis=2)
        q_pos = jnp.arange(L, dtype=jnp.int32).reshape(nb, BLOCK)
        k_pos = jnp.concatenate([q_pos - BLOCK, q_pos], axis=1)
        y_attn = window_attention(qb, kk, vv, q_pos, k_pos, lp['sinks'])
        k_state = k[:, L - PROMPT_CACHE_W:]
        v_state = v[:, L - PROMPT_CACHE_W:]
    else:
        kk = jnp.concatenate([k_past.astype(k.dtype), k], axis=1)
        vv = jnp.concatenate([v_past.astype(v.dtype), v], axis=1)
        q_pos = (PAST_LEN + jnp.arange(L, dtype=jnp.int32))[None]
        k_pos = jnp.concatenate([PAST_LEN - CACHE_W + jnp.arange(CACHE_W, dtype=jnp.int32),
                                 q_pos[0]])[None]
        y_attn = window_attention(q[:, None], kk[:, None], vv[:, None], q_pos, k_pos, lp['sinks'])
        k_state = kk[:, L:]
        v_state = vv[:, L:]
    y_attn = y_attn.astype(x.dtype)
    merged = jnp.concatenate([rms_norm(y_ssm, lp['ssm_out_norm']), rms_norm(y_attn, lp['attn_out_norm'])], axis=-1)
    x = x + merged @ lp['w_out']
    x = x + 0.5 * swiglu(rms_norm(x, lp['ffn2_norm']), lp['ffn2_w_gate'], lp['ffn2_w_up'], lp['ffn2_w_down'])
    return x, k_state, v_state, h_last


def setup_inputs(seed: int = 0) -> dict:
    key = jax.random.key(seed)
    ks = jax.random.split(key, 40)
    f32 = jnp.float32
    nrm = lambda i, shape, s: jax.random.normal(ks[i], shape, f32) * s
    gain = lambda i, shape: 1.0 + 0.05 * jax.random.normal(ks[i], shape, f32)
    a_im = math.pi * jnp.arange(SSM_STATE, dtype=f32)[None, None, :] + nrm(14, (DEPTH, SSM_GROUPS, SSM_STATE), 0.01)
    log_dt = jax.random.uniform(ks[21], (DEPTH, SSM_GROUPS), f32, math.log(1e-3), math.log(1e-1))
    return {
        "x_prompt": nrm(0, (BATCH, SEQ, D_MODEL), 1.0),
        "x_sample": nrm(1, (DEC_BATCH, DEC_SEQ, D_MODEL), 1.0),
        "cache_k": nrm(2, (DEPTH, DEC_BATCH, CACHE_W, N_KV_HEADS, HEAD_DIM), 1.0),
        "cache_v": nrm(3, (DEPTH, DEC_BATCH, CACHE_W, N_KV_HEADS, HEAD_DIM), 1.0),
        "state_ssm_re": nrm(4, (DEPTH, DEC_BATCH, SSM_GROUPS, SSM_STATE), 0.5),
        "state_ssm_im": nrm(5, (DEPTH, DEC_BATCH, SSM_GROUPS, SSM_STATE), 0.5),
        "ffn1_norm": gain(6, (DEPTH, D_MODEL)),
        "ffn1_w_gate": nrm(7, (DEPTH, D_MODEL, D_FF), D_MODEL ** -0.5),
        "ffn1_w_up": nrm(8, (DEPTH, D_MODEL, D_FF), D_MODEL ** -0.5),
        "ffn1_w_down": nrm(9, (DEPTH, D_FF, D_MODEL), D_FF ** -0.5),
        "mix_norm": gain(10, (DEPTH, D_MODEL)),
        "w_in": nrm(11, (DEPTH, D_MODEL, IN_W), D_MODEL ** -0.5),
        "ssm_A_re": -0.5 + nrm(13, (DEPTH, SSM_GROUPS, SSM_STATE), 0.01),
        "ssm_A_im": a_im,
        "ssm_B_re": nrm(15, (DEPTH, SSM_GROUPS, SSM_STATE, SSM_GROUP_CH), (2 * SSM_GROUP_CH) ** -0.5),
        "ssm_B_im": nrm(16, (DEPTH, SSM_GROUPS, SSM_STATE, SSM_GROUP_CH), (2 * SSM_GROUP_CH) ** -0.5),
        "ssm_C_re": nrm(17, (DEPTH, SSM_GROUPS, SSM_GROUP_CH, SSM_STATE), (2 * SSM_STATE) ** -0.5),
        "ssm_C_im": nrm(18, (DEPTH, SSM_GROUPS, SSM_GROUP_CH, SSM_STATE), (2 * SSM_STATE) ** -0.5),
        "ssm_D": nrm(19, (DEPTH, SSM_GROUPS, SSM_GROUP_CH), 1.0),
        "ssm_log_dt": log_dt,
        "ssm_w_glu": nrm(22, (DEPTH, SSM_GROUPS, SSM_GROUP_CH, SSM_GROUP_CH), SSM_GROUP_CH ** -0.5),
        "q_norm": gain(23, (DEPTH, HEAD_DIM)),
        "k_norm": gain(24, (DEPTH, HEAD_DIM)),
        "sinks": nrm(25, (DEPTH, N_HEADS), 0.5),
        "ssm_out_norm": gain(26, (DEPTH, SSM_W)),
        "attn_out_norm": gain(27, (DEPTH, ATTN_W)),
        "w_out": nrm(28, (DEPTH, MIX_W, D_MODEL), MIX_W ** -0.5),
        "ffn2_norm": gain(29, (DEPTH, D_MODEL)),
        "ffn2_w_gate": nrm(30, (DEPTH, D_MODEL, D_FF), D_MODEL ** -0.5),
        "ffn2_w_up": nrm(31, (DEPTH, D_MODEL, D_FF), D_MODEL ** -0.5),
        "ffn2_w_down": nrm(32, (DEPTH, D_FF, D_MODEL), D_FF ** -0.5),
    }


def reference(x_prompt, x_sample, cache_k, cache_v, state_ssm_re, state_ssm_im,
              ffn1_norm, ffn1_w_gate, ffn1_w_up, ffn1_w_down, mix_norm, w_in,
              ssm_A_re, ssm_A_im, ssm_B_re, ssm_B_im, ssm_C_re, ssm_C_im, ssm_D, ssm_log_dt, ssm_w_glu,
              q_norm, k_norm, sinks, ssm_out_norm, attn_out_norm, w_out,
              ffn2_norm, ffn2_w_gate, ffn2_w_up, ffn2_w_down):
    xp, xs = x_prompt, x_sample
    pk, pv, pre, pim, sk, sv, sre, sim = [], [], [], [], [], [], [], []
    for l in range(DEPTH):
        lp = dict(ffn1_norm=ffn1_norm[l], ffn1_w_gate=ffn1_w_gate[l], ffn1_w_up=ffn1_w_up[l],
                  ffn1_w_down=ffn1_w_down[l], mix_norm=mix_norm[l], w_in=w_in[l],
                  ssm_A_re=ssm_A_re[l], ssm_A_im=ssm_A_im[l], ssm_B_re=ssm_B_re[l], ssm_B_im=ssm_B_im[l],
                  ssm_C_re=ssm_C_re[l], ssm_C_im=ssm_C_im[l], ssm_D=ssm_D[l], ssm_log_dt=ssm_log_dt[l],
                  ssm_w_glu=ssm_w_glu[l], q_norm=q_norm[l], k_norm=k_norm[l], sinks=sinks[l],
                  ssm_out_norm=ssm_out_norm[l], attn_out_norm=attn_out_norm[l], w_out=w_out[l],
                  ffn2_norm=ffn2_norm[l], ffn2_w_gate=ffn2_w_gate[l], ffn2_w_up=ffn2_w_up[l],
                  ffn2_w_down=ffn2_w_down[l])
        h0_p = jnp.zeros((xp.shape[0], SSM_GROUPS, SSM_STATE), jnp.complex64)
        xp, k_p, v_p, h_p = decoder_layer(xp, lp, h0_p, None, None)
        h0_s = lax.complex(state_ssm_re[l].astype(jnp.float32), state_ssm_im[l].astype(jnp.float32))
        xs, k_s, v_s, h_s = decoder_layer(xs, lp, h0_s, cache_k[l], cache_v[l])
        pk.append(k_p); pv.append(v_p); pre.append(h_p.real); pim.append(h_p.imag)
        sk.append(k_s); sv.append(v_s); sre.append(h_s.real); sim.append(h_s.imag)
    prompt_k, prompt_v = jnp.stack(pk), jnp.stack(pv)
    prompt_ssm_re, prompt_ssm_im = jnp.stack(pre), jnp.stack(pim)
    sample_k, sample_v = jnp.stack(sk), jnp.stack(sv)
    sample_ssm_re, sample_ssm_im = jnp.stack(sre), jnp.stack(sim)
    return (xp, xs, prompt_k, prompt_v, prompt_ssm_re, prompt_ssm_im,
            sample_k, sample_v, sample_ssm_re, sample_ssm_im)
```

```python
import functools

import jax
import jax.numpy as jnp
import numpy as np
from jax import lax
from jax.experimental import pallas as pl
from jax.experimental.pallas import tpu as pltpu

F32 = jnp.float32
BF16 = jnp.bfloat16

EPS = 1e-6
SSM_GROUP_CH = 16
SSM_STATE = 64
HEAD_DIM = 64
N_KV_HEADS = 4
WINDOW = 128
PAST_LEN = 8192
QK_SCALE = HEAD_DIM ** -0.5
PROMPT_CHUNK = 16

TOKEN_TILE = 512
FF_TILE = 512
SAMPLE_BATCH_TILE = 8
VMEM_LIMIT_BYTES = 48 * 1024 * 1024

HIGHEST = lax.Precision.HIGHEST


def _rms(x, gain):
    return x * lax.rsqrt(jnp.mean(x * x, axis=-1, keepdims=True) + EPS) * gain


def _dot(a, b, precision=None):
    return jnp.dot(a, b, preferred_element_type=F32, precision=precision)


def _dot_nt(a, b, precision=None):
    return lax.dot_general(a, b, (((1,), (1,)), ((), ())), preferred_element_type=F32, precision=precision)


def _compiler_params(*semantics):
    return pltpu.CompilerParams(dimension_semantics=semantics, vmem_limit_bytes=VMEM_LIMIT_BYTES)


def _group_spec(*shape):
    return pl.BlockSpec((1,) + shape, lambda i: (i,) + (0,) * len(shape))


def _ffn_kernel(x_ref, g_ref, wg_ref, wu_ref, wd_ref, o_ref, xn_ref):
    j = pl.program_id(1)

    @pl.when(j == 0)
    def _():
        xn_ref[...] = _rms(x_ref[...], g_ref[...]).astype(BF16)

    xn = xn_ref[...]
    gate = _dot(xn, wg_ref[...])
    up = _dot(xn, wu_ref[...])
    h = (gate * jax.nn.sigmoid(gate) * up).astype(BF16)
    part = _dot(h, wd_ref[...])

    @pl.when(j == 0)
    def _():
        o_ref[...] = part

    @pl.when(j > 0)
    def _():
        o_ref[...] += part

    @pl.when(j == pl.num_programs(1) - 1)
    def _():
        o_ref[...] = x_ref[...] + 0.5 * o_ref[...]


def _ffn(x, gain, w_gate, w_up, w_down):
    n_tok, d = x.shape
    d_ff = w_gate.shape[1]
    tm, tf = TOKEN_TILE, FF_TILE
    return pl.pallas_call(
        _ffn_kernel,
        grid=(n_tok // tm, d_ff // tf),
        in_specs=[
            pl.BlockSpec((tm, d), lambda i, j: (i, 0)),
            pl.BlockSpec((1, d), lambda i, j: (0, 0)),
            pl.BlockSpec((d, tf), lambda i, j: (0, j)),
            pl.BlockSpec((d, tf), lambda i, j: (0, j)),
            pl.BlockSpec((tf, d), lambda i, j: (j, 0)),
        ],
        out_specs=pl.BlockSpec((tm, d), lambda i, j: (i, 0)),
        out_shape=jax.ShapeDtypeStruct((n_tok, d), F32),
        scratch_shapes=[pltpu.VMEM((tm, d), BF16)],
        compiler_params=_compiler_params("parallel", "arbitrary"),
        name="ffn",
    )(x, gain.reshape(1, d), w_gate, w_up, w_down)


def _norm_matmul_kernel(x_ref, g_ref, w_ref, o_ref, xn_ref):
    @pl.when(pl.program_id(1) == 0)
    def _():
        xn_ref[...] = _rms(x_ref[...], g_ref[...]).astype(BF16)

    o_ref[...] = _dot(xn_ref[...], w_ref[...])


def _norm_matmul(x, gain, w, tn):
    n_tok, d = x.shape
    n_out = w.shape[1]
    tm = TOKEN_TILE
    return pl.pallas_call(
        _norm_matmul_kernel,
        grid=(n_tok // tm, n_out // tn),
        in_specs=[
            pl.BlockSpec((tm, d), lambda i, j: (i, 0)),
            pl.BlockSpec((1, d), lambda i, j: (0, 0)),
            pl.BlockSpec((d, tn), lambda i, j: (0, j)),
        ],
        out_specs=pl.BlockSpec((tm, tn), lambda i, j: (i, j)),
        out_shape=jax.ShapeDtypeStruct((n_tok, n_out), F32),
        scratch_shapes=[pltpu.VMEM((tm, d), BF16)],
        compiler_params=_compiler_params("parallel", "arbitrary"),
        name="in_proj",
    )(x, gain.reshape(1, d), w)


def _out_proj_kernel(x_ref, ys_ref, ya_ref, gs_ref, ga_ref, ws_ref, wa_ref, o_ref):
    ys = _rms(ys_ref[...], gs_ref[...]).astype(BF16)
    ya = _rms(ya_ref[...], ga_ref[...]).astype(BF16)
    o_ref[...] = x_ref[...] + _dot(ys, ws_ref[...]) + _dot(ya, wa_ref[...])


def _out_proj(x, y_ssm, y_attn, g_ssm, g_attn, w_out):
    n_tok, d = x.shape
    ssm_w, attn_w = y_ssm.shape[1], y_attn.shape[1]
    tm = TOKEN_TILE
    return pl.pallas_call(
        _out_proj_kernel,
        grid=(n_tok // tm,),
        in_specs=[
            pl.BlockSpec((tm, d), lambda i: (i, 0)),
            pl.BlockSpec((tm, ssm_w), lambda i: (i, 0)),
            pl.BlockSpec((tm, attn_w), lambda i: (i, 0)),
            pl.BlockSpec((1, ssm_w), lambda i: (0, 0)),
            pl.BlockSpec((1, attn_w), lambda i: (0, 0)),
            pl.BlockSpec((ssm_w, d), lambda i: (0, 0)),
            pl.BlockSpec((attn_w, d), lambda i: (0, 0)),
        ],
        out_specs=pl.BlockSpec((tm, d), lambda i: (i, 0)),
        out_shape=jax.ShapeDtypeStruct((n_tok, d), F32),
        compiler_params=_compiler_params("parallel"),
        name="out_proj",
    )(x, y_ssm, y_attn, g_ssm.reshape(1, ssm_w), g_attn.reshape(1, attn_w), w_out[:ssm_w], w_out[ssm_w:])


def _ssm_prep_kernel(are_ref, aim_ref, ldt_ref, btre_ref, btim_ref, cre_ref, cim_ref,
                     wbre_ref, wbim_ref, ztre_ref, ztim_ref, k_ref, powre_ref, powim_ref):
    a_re, a_im = are_ref[0], aim_ref[0]
    dt = jnp.exp(ldt_ref[0])
    mag = jnp.exp(a_re * dt)
    lb_re, lb_im = mag * jnp.cos(a_im * dt), mag * jnp.sin(a_im * dt)
    den = a_re * a_re + a_im * a_im
    q_re = ((lb_re - 1.0) * a_re + lb_im * a_im) / den
    q_im = (lb_im * a_re - (lb_re - 1.0) * a_im) / den
    bt_re, bt_im = btre_ref[0], btim_ref[0]
    bb_re = q_re * bt_re - q_im * bt_im
    bb_im = q_re * bt_im + q_im * bt_re
    c_re, c_im = cre_ref[0], cim_ref[0]

    pw_re, pw_im = [jnp.ones_like(lb_re)], [jnp.zeros_like(lb_im)]
    for _ in range(PROMPT_CHUNK):
        pr, pi = pw_re[-1], pw_im[-1]
        pw_re.append(pr * lb_re - pi * lb_im)
        pw_im.append(pr * lb_im + pi * lb_re)

    wb_re = jnp.concatenate([bb_re * pw_re[d] - bb_im * pw_im[d] for d in range(PROMPT_CHUNK)], axis=0)
    wb_im = jnp.concatenate([bb_re * pw_im[d] + bb_im * pw_re[d] for d in range(PROMPT_CHUNK)], axis=0)
    wbre_ref[0] = wb_re
    wbim_ref[0] = wb_im
    ztre_ref[0] = jnp.concatenate([c_re * pw_re[t + 1] - c_im * pw_im[t + 1] for t in range(PROMPT_CHUNK)], axis=0)
    ztim_ref[0] = jnp.concatenate([-(c_re * pw_im[t + 1] + c_im * pw_re[t + 1]) for t in range(PROMPT_CHUNK)], axis=0)
    k_ref[0] = _dot_nt(wb_re, c_re, HIGHEST) - _dot_nt(wb_im, c_im, HIGHEST)
    pad = jnp.zeros((powre_ref.shape[1] - (PROMPT_CHUNK + 1), a_re.shape[1]), F32)
    powre_ref[0] = jnp.concatenate(pw_re + [pad], axis=0)
    powim_ref[0] = jnp.concatenate(pw_im + [pad], axis=0)


def _ssm_prep(a_re, a_im, log_dt, b_re, b_im, c_re, c_im):
    g, p = a_re.shape
    c = c_re.shape[1]
    rows = PROMPT_CHUNK * c
    pow_rows = 24
    sds = lambda *shape: jax.ShapeDtypeStruct((g,) + shape, F32)
    gs = _group_spec
    return pl.pallas_call(
        _ssm_prep_kernel,
        grid=(g,),
        in_specs=[gs(1, p), gs(1, p), gs(1, 1), gs(c, p), gs(c, p), gs(c, p), gs(c, p)],
        out_specs=[gs(rows, p), gs(rows, p), gs(rows, p), gs(rows, p), gs(rows, c), gs(pow_rows, p), gs(pow_rows, p)],
        out_shape=[sds(rows, p), sds(rows, p), sds(rows, p), sds(rows, p), sds(rows, c), sds(pow_rows, p), sds(pow_rows, p)],
        compiler_params=_compiler_params("parallel"),
        name="ssm_prep",
    )(a_re.reshape(g, 1, p), a_im.reshape(g, 1, p), log_dt.reshape(g, 1, 1),
      jnp.swapaxes(b_re, 1, 2), jnp.swapaxes(b_im, 1, 2), c_re, c_im)


def _ssm_tables(prep, d_skip, w_glu, t):
    wb_re, wb_im, zt_re, zt_im, k, pow_re, pow_im = prep
    g, _, p = wb_re.shape
    c = k.shape[2]
    lag = np.arange(t)[None, :] - np.arange(t)[:, None]
    kl = k.reshape(g, PROMPT_CHUNK, c, c)[:, np.clip(lag, 0, None)]
    kl = jnp.where((lag >= 0)[None, :, :, None, None], kl, 0.0)
    m = kl.transpose(0, 1, 3, 2, 4).reshape(g, t * c, t * c)
    rev = lambda a: a.reshape(g, PROMPT_CHUNK, c, p)[:, t - 1::-1].reshape(g, t * c, p)
    eye = np.eye(t, dtype=np.float32)
    glu = (eye[None, :, None, :, None] * w_glu[:, None, :, None, :]).reshape(g, t * c, t * c)
    d_t = jnp.tile(d_skip, (1, t)).reshape(g, 1, t * c)
    return dict(m=m, w_re=rev(wb_re), w_im=rev(wb_im), zt_re=zt_re[:, :t * c], zt_im=zt_im[:, :t * c],
                a_re=pow_re[:, t:t + 1], a_im=pow_im[:, t:t + 1], glu=glu, d=d_t)


def _ssm_mix_kernel(u_ref, m_ref, wre_ref, wim_ref, ztre_ref, ztim_ref, are_ref, aim_ref, d_ref, glu_ref,
                    h0re_ref, h0im_ref, y_ref, hre_ref, him_ref, vre_ref, vim_ref, pre_ref, pim_ref, *, bsz):
    u = u_ref[0]
    vre_ref[...] = _dot(u, wre_ref[0], HIGHEST)
    vim_ref[...] = _dot(u, wim_ref[0], HIGHEST)
    a_re, a_im = are_ref[0], aim_ref[0]
    h_re, h_im = h0re_ref[0], h0im_ref[0]
    for k in range(u.shape[0] // bsz):
        rows = pl.ds(k * bsz, bsz)
        pre_ref[rows, :] = h_re
        pim_ref[rows, :] = h_im
        h_re, h_im = (a_re * h_re - a_im * h_im + vre_ref[rows, :],
                      a_re * h_im + a_im * h_re + vim_ref[rows, :])
    hre_ref[0] = h_re
    him_ref[0] = h_im
    y = (_dot(u, m_ref[0], HIGHEST) + _dot_nt(pre_ref[...], ztre_ref[0], HIGHEST)
         + _dot_nt(pim_ref[...], ztim_ref[0], HIGHEST) + d_ref[0] * u)
    y = jax.nn.gelu(y)
    y_ref[0] = y * jax.nn.sigmoid(_dot(y, glu_ref[0], HIGHEST))


def _ssm_mix(u, tab, h0_re, h0_im, bsz):
    g, rows, tc = u.shape
    p = h0_re.shape[2]
    gs = _group_spec
    return pl.pallas_call(
        functools.partial(_ssm_mix_kernel, bsz=bsz),
        grid=(g,),
        in_specs=[gs(rows, tc), gs(tc, tc), gs(tc, p), gs(tc, p), gs(tc, p), gs(tc, p),
                  gs(1, p), gs(1, p), gs(1, tc), gs(tc, tc), gs(bsz, p), gs(bsz, p)],
        out_specs=[gs(rows, tc), gs(bsz, p), gs(bsz, p)],
        out_shape=[jax.ShapeDtypeStruct((g, rows, tc), F32), jax.ShapeDtypeStruct((g, bsz, p), F32),
                   jax.ShapeDtypeStruct((g, bsz, p), F32)],
        scratch_shapes=[pltpu.VMEM((rows, p), F32)] * 4,
        compiler_params=_compiler_params("parallel"),
        name="ssm_mix",
    )(u, tab["m"], tab["w_re"], tab["w_im"], tab["zt_re"], tab["zt_im"], tab["a_re"], tab["a_im"],
      tab["d"], tab["glu"], h0_re, h0_im)


def _ssm(u, tab, h0_re, h0_im, bsz, t):
    n_tok, width = u.shape
    c = SSM_GROUP_CH
    g = width // c
    chunks = n_tok // (bsz * t)
    ug = u.reshape(bsz, chunks, t, g, c).transpose(3, 1, 0, 2, 4).reshape(g, chunks * bsz, t * c)
    h0_re, h0_im = h0_re.transpose(1, 0, 2), h0_im.transpose(1, 0, 2)
    y, h_re, h_im = _ssm_mix(ug, tab, h0_re, h0_im, bsz)
    y = y.reshape(g, chunks, bsz, t, c).transpose(2, 1, 3, 0, 4).reshape(n_tok, width)
    return y, h_re.transpose(1, 0, 2), h_im.transpose(1, 0, 2)


def _bias_table(q_pos, k_pos, n_heads):
    slopes = np.exp2(-8.0 * np.arange(1, n_heads + 1, dtype=np.float64) / n_heads)
    dist = q_pos[:, None] - k_pos[None, :]
    valid = (dist >= 0) & (dist < WINDOW) & (k_pos[None, :] >= 0)
    bias = -slopes[:, None, None] * dist[None].astype(np.float64)
    return np.where(valid[None], bias, -np.inf).astype(np.float32)


def _head_rms(x, gain):
    heads = [_rms(x[:, i:i + HEAD_DIM], gain) for i in range(0, x.shape[1], HEAD_DIM)]
    return jnp.concatenate(heads, axis=1)


def _softmax_sink_pv(logits, sink, v):
    m = jnp.maximum(jnp.max(logits, axis=-1, keepdims=True), sink)
    p = jnp.exp(logits - m)
    denom = jnp.sum(p, axis=-1, keepdims=True) + jnp.exp(sink - m)
    return _dot(p.astype(BF16), v) / denom


def _prompt_attn_kernel(sinks_ref, q_ref, k_ref, v_ref, qg_ref, kg_ref, bias_ref, y_ref, ks_ref, vs_ref,
                        kp_ref, vp_ref, *, n_heads):
    n = pl.program_id(1)

    @pl.when(n == 0)
    def _():
        kp_ref[...] = jnp.zeros_like(kp_ref)
        vp_ref[...] = jnp.zeros_like(vp_ref)

    q_all = q_ref[...]
    k_cur = _head_rms(k_ref[...], kg_ref[...])
    v_cur = v_ref[...]
    k2 = jnp.concatenate([kp_ref[...], k_cur], axis=0).astype(BF16)
    v2 = jnp.concatenate([vp_ref[...], v_cur], axis=0).astype(BF16)
    table = jnp.minimum(n, 1)
    q_per_kv = n_heads // N_KV_HEADS
    outs = []
    for h in range(n_heads):
        kv = h // q_per_kv
        q = (_rms(q_all[:, h * HEAD_DIM:(h + 1) * HEAD_DIM], qg_ref[...]) * QK_SCALE).astype(BF16)
        logits = _dot_nt(q, k2[:, kv * HEAD_DIM:(kv + 1) * HEAD_DIM]) + bias_ref[table, h]
        outs.append(_softmax_sink_pv(logits, sinks_ref[h], v2[:, kv * HEAD_DIM:(kv + 1) * HEAD_DIM]))
    y_ref[...] = jnp.concatenate(outs, axis=1)
    kp_ref[...] = k_cur
    vp_ref[...] = v_cur

    @pl.when(n == pl.num_programs(1) - 1)
    def _():
        ks_ref[0] = k_cur
        vs_ref[0] = v_cur


def _prompt_attn(z, sinks, q_gain, k_gain, bsz, seq, n_tok, col_q, attn_w, kv_w):
    blk = WINDOW
    nb = seq // blk
    n_heads = attn_w // HEAD_DIM
    row = lambda b, n: b * nb + n
    q_blk, k_blk, v_blk = col_q // attn_w, (col_q + attn_w) // kv_w, (col_q + attn_w + kv_w) // kv_w
    pos = np.arange(blk)
    bias = np.stack([_bias_table(pos, np.arange(2 * blk) - blk, n_heads),
                     _bias_table(pos + blk, np.arange(2 * blk), n_heads)])
    return pl.pallas_call(
        functools.partial(_prompt_attn_kernel, n_heads=n_heads),
        grid=(bsz, nb),
        in_specs=[
            pl.BlockSpec(memory_space=pltpu.SMEM),
            pl.BlockSpec((blk, attn_w), lambda b, n: (row(b, n), q_blk)),
            pl.BlockSpec((blk, kv_w), lambda b, n: (row(b, n), k_blk)),
            pl.BlockSpec((blk, kv_w), lambda b, n: (row(b, n), v_blk)),
            pl.BlockSpec((1, HEAD_DIM), lambda b, n: (0, 0)),
            pl.BlockSpec((1, HEAD_DIM), lambda b, n: (0, 0)),
            pl.BlockSpec(bias.shape, lambda b, n: (0, 0, 0, 0)),
        ],
        out_specs=[
            pl.BlockSpec((blk, attn_w), lambda b, n: (row(b, n), 0)),
            pl.BlockSpec((1, blk, kv_w), lambda b, n: (b, 0, 0)),
            pl.BlockSpec((1, blk, kv_w), lambda b, n: (b, 0, 0)),
        ],
        out_shape=[jax.ShapeDtypeStruct((n_tok, attn_w), F32),
                   jax.ShapeDtypeStruct((bsz, blk, kv_w), F32),
                   jax.ShapeDtypeStruct((bsz, blk, kv_w), F32)],
        scratch_shapes=[pltpu.VMEM((blk, kv_w), F32)] * 2,
        compiler_params=_compiler_params("parallel", "arbitrary"),
        name="prompt_attn",
    )(sinks, z, z, z, q_gain.reshape(1, HEAD_DIM), k_gain.reshape(1, HEAD_DIM), jnp.asarray(bias))


def _sample_attn_kernel(q_ref, k_ref, v_ref, ck_ref, cv_ref, qg_ref, kg_ref, bias_ref, sink_ref,
                        y_ref, ks_ref, vs_ref, kbuf_ref, vbuf_ref, *, dec_seq):
    cache_w = ck_ref.shape[1]
    k_new = _head_rms(k_ref[...], kg_ref[...])
    v_new = v_ref[...]
    kbuf_ref[...] = jnp.zeros_like(kbuf_ref)
    vbuf_ref[...] = jnp.zeros_like(vbuf_ref)
    for b in range(q_ref.shape[0]):
        rows = slice(b * dec_seq, (b + 1) * dec_seq)
        kbuf_ref[0:cache_w, :] = ck_ref[b]
        vbuf_ref[0:cache_w, :] = cv_ref[b]
        kbuf_ref[cache_w:cache_w + dec_seq, :] = k_new[rows]
        vbuf_ref[cache_w:cache_w + dec_seq, :] = v_new[rows]
        k2 = kbuf_ref[...].astype(BF16)
        v2 = vbuf_ref[...].astype(BF16)
        for kv in range(q_ref.shape[1]):
            q = (_rms(q_ref[b, kv], qg_ref[...]) * QK_SCALE).astype(BF16)
            logits = _dot_nt(q, k2[:, kv * HEAD_DIM:(kv + 1) * HEAD_DIM]) + bias_ref[kv]
            y_ref[b, kv] = _softmax_sink_pv(logits, sink_ref[kv], v2[:, kv * HEAD_DIM:(kv + 1) * HEAD_DIM])
        ks_ref[b, 0:cache_w - dec_seq, :] = ck_ref[b, dec_seq:cache_w, :]
        vs_ref[b, 0:cache_w - dec_seq, :] = cv_ref[b, dec_seq:cache_w, :]
        ks_ref[b, cache_w - dec_seq:cache_w, :] = k_new[rows]
        vs_ref[b, cache_w - dec_seq:cache_w, :] = v_new[rows]


def _sample_attn(z, cache_k, cache_v, sinks, q_gain, k_gain, row0, dec_seq, col_q, attn_w, kv_w):
    dec_batch, cache_w, _ = cache_k.shape
    n_heads = attn_w // HEAD_DIM
    q_per_kv = n_heads // N_KV_HEADS
    q_rows = q_per_kv * dec_seq
    tile = SAMPLE_BATCH_TILE
    r0 = row0 // (tile * dec_seq)
    k_blk, v_blk = (col_q + attn_w) // kv_w, (col_q + attn_w + kv_w) // kv_w
    keys = 2 * cache_w
    k_pos = np.full(keys, -1)
    k_pos[:cache_w] = PAST_LEN - cache_w + np.arange(cache_w)
    k_pos[cache_w:cache_w + dec_seq] = PAST_LEN + np.arange(dec_seq)
    bias = _bias_table(PAST_LEN + np.arange(dec_seq), k_pos, n_heads).reshape(N_KV_HEADS, q_rows, keys)
    sink_col = jnp.repeat(sinks, dec_seq).reshape(N_KV_HEADS, q_rows, 1)
    q = z[row0:, col_q:col_q + attn_w].reshape(dec_batch, dec_seq, N_KV_HEADS, q_per_kv, HEAD_DIM)
    q = q.transpose(0, 2, 3, 1, 4).reshape(dec_batch, N_KV_HEADS, q_rows, HEAD_DIM)
    y, ks, vs = pl.pallas_call(
        functools.partial(_sample_attn_kernel, dec_seq=dec_seq),
        grid=(dec_batch // tile,),
        in_specs=[
            pl.BlockSpec((tile, N_KV_HEADS, q_rows, HEAD_DIM), lambda i: (i, 0, 0, 0)),
            pl.BlockSpec((tile * dec_seq, kv_w), lambda i: (r0 + i, k_blk)),
            pl.BlockSpec((tile * dec_seq, kv_w), lambda i: (r0 + i, v_blk)),
            pl.BlockSpec((tile, cache_w, kv_w), lambda i: (i, 0, 0)),
            pl.BlockSpec((tile, cache_w, kv_w), lambda i: (i, 0, 0)),
            pl.BlockSpec((1, HEAD_DIM), lambda i: (0, 0)),
            pl.BlockSpec((1, HEAD_DIM), lambda i: (0, 0)),
            pl.BlockSpec(bias.shape, lambda i: (0, 0, 0)),
            pl.BlockSpec(sink_col.shape, lambda i: (0, 0, 0)),
        ],
        out_specs=[
            pl.BlockSpec((tile, N_KV_HEADS, q_rows, HEAD_DIM), lambda i: (i, 0, 0, 0)),
            pl.BlockSpec((tile, cache_w, kv_w), lambda i: (i, 0, 0)),
            pl.BlockSpec((tile, cache_w, kv_w), lambda i: (i, 0, 0)),
        ],
        out_shape=[jax.ShapeDtypeStruct(q.shape, F32),
                   jax.ShapeDtypeStruct(cache_k.shape, F32),
                   jax.ShapeDtypeStruct(cache_v.shape, F32)],
        scratch_shapes=[pltpu.VMEM((keys, kv_w), F32)] * 2,
        compiler_params=_compiler_params("parallel"),
        name="sample_attn",
    )(q, z, z, cache_k, cache_v, q_gain.reshape(1, HEAD_DIM), k_gain.reshape(1, HEAD_DIM), jnp.asarray(bias), sink_col)
    y = y.reshape(dec_batch, N_KV_HEADS, q_per_kv, dec_seq, HEAD_DIM).transpose(0, 3, 1, 2, 4)
    return y.reshape(dec_batch * dec_seq, attn_w), ks, vs


def kernel(x_prompt, x_sample, cache_k, cache_v, state_ssm_re, state_ssm_im, ffn1_norm, ffn1_w_gate, ffn1_w_up, ffn1_w_down, mix_norm, w_in, ssm_A_re, ssm_A_im, ssm_B_re, ssm_B_im, ssm_C_re, ssm_C_im, ssm_D, ssm_log_dt, ssm_w_glu, q_norm, k_norm, sinks, ssm_out_norm, attn_out_norm, w_out, ffn2_norm, ffn2_w_gate, ffn2_w_up, ffn2_w_down):
    bsz, seq, d = x_prompt.shape
    dec_batch, dec_seq, _ = x_sample.shape
    depth = w_in.shape[0]
    ssm_w = ssm_A_re.shape[1] * SSM_GROUP_CH
    attn_w = w_out.shape[1] - ssm_w
    kv_w = N_KV_HEADS * HEAD_DIM
    cache_w = cache_k.shape[2]
    n_prompt = bsz * seq
    n_tok = n_prompt + dec_batch * dec_seq

    x = jnp.concatenate([x_prompt.reshape(n_prompt, d), x_sample.reshape(dec_batch * dec_seq, d)], axis=0)
    zeros_state = jnp.zeros((bsz, ssm_w // SSM_GROUP_CH, SSM_STATE), F32)
    outs = [[] for _ in range(8)]
    for l in range(depth):
        bf = lambda w: w[l].astype(BF16)
        x = _ffn(x, ffn1_norm[l], bf(ffn1_w_gate), bf(ffn1_w_up), bf(ffn1_w_down))
        z = _norm_matmul(x, mix_norm[l], bf(w_in), tn=1280)

        prep = _ssm_prep(ssm_A_re[l], ssm_A_im[l], ssm_log_dt[l], ssm_B_re[l], ssm_B_im[l], ssm_C_re[l], ssm_C_im[l])
        tab_p = _ssm_tables(prep, ssm_D[l], ssm_w_glu[l], PROMPT_CHUNK)
        tab_s = _ssm_tables(prep, ssm_D[l], ssm_w_glu[l], dec_seq)
        yp, hp_re, hp_im = _ssm(z[:n_prompt, :ssm_w], tab_p, zeros_state, zeros_state, bsz, PROMPT_CHUNK)
        ys, hs_re, hs_im = _ssm(z[n_prompt:, :ssm_w], tab_s, state_ssm_re[l], state_ssm_im[l], dec_batch, dec_seq)
        y_ssm = jnp.concatenate([yp, ys], axis=0)

        y_attn, pk, pv = _prompt_attn(z, sinks[l], q_norm[l], k_norm[l], bsz, seq, n_tok, ssm_w, attn_w, kv_w)
        ya_s, sk, sv = _sample_attn(z, cache_k[l].reshape(dec_batch, cache_w, kv_w),
                                    cache_v[l].reshape(dec_batch, cache_w, kv_w), sinks[l], q_norm[l], k_norm[l],
                                    n_prompt, dec_seq, ssm_w, attn_w, kv_w)
        y_attn = lax.dynamic_update_slice(y_attn, ya_s, (n_prompt, 0))

        x = _out_proj(x, y_ssm, y_attn, ssm_out_norm[l], attn_out_norm[l], bf(w_out))
        x = _ffn(x, ffn2_norm[l], bf(ffn2_w_gate), bf(ffn2_w_up), bf(ffn2_w_down))

        kv_shape = (-1, N_KV_HEADS, HEAD_DIM)
        for dst, val in zip(outs, (pk.reshape(bsz, *kv_shape), pv.reshape(bsz, *kv_shape), hp_re, hp_im,
                                   sk.reshape(dec_batch, *kv_shape), sv.reshape(dec_batch, *kv_shape), hs_re, hs_im)):
            dst.append(val)
    y_prompt = x[:n_prompt].reshape(bsz, seq, d)
    y_sample = x[n_prompt:].reshape(dec_batch, dec_seq, d)
    return (y_prompt, y_sample) + tuple(jnp.stack(o) for o in outs)
```

```python
import functools

import jax
import jax.numpy as jnp
import numpy as np
from jax import lax
from jax.experimental import pallas as pl
from jax.experimental.pallas import tpu as pltpu

F32 = jnp.float32
BF16 = jnp.bfloat16

EPS = 1e-6
SSM_GROUP_CH = 16
SSM_STATE = 64
HEAD_DIM = 64
N_KV_HEADS = 4
WINDOW = 128
PAST_LEN = 8192
QK_SCALE = HEAD_DIM ** -0.5
PROMPT_CHUNK = 16

TOKEN_TILE = 512
FF_TILE = 512
SAMPLE_BATCH_TILE = 8
VMEM_LIMIT_BYTES = 48 * 1024 * 1024

HIGHEST = lax.Precision.HIGHEST
SSM_STATE_PRECISION = HIGHEST
SSM_GROUPS_PER_STEP = 8


def _rms(x, gain):
    return x * lax.rsqrt(jnp.mean(x * x, axis=-1, keepdims=True) + EPS) * gain


def _dot(a, b, precision=None):
    return jnp.dot(a, b, preferred_element_type=F32, precision=precision)


def _dot_nt(a, b, precision=None):
    return lax.dot_general(a, b, (((1,), (1,)), ((), ())), preferred_element_type=F32, precision=precision)


def _compiler_params(*semantics):
    return pltpu.CompilerParams(dimension_semantics=semantics, vmem_limit_bytes=VMEM_LIMIT_BYTES)


def _group_spec(*shape):
    return pl.BlockSpec((1,) + shape, lambda i: (i,) + (0,) * len(shape))


def _ffn_kernel(x_ref, g_ref, wg_ref, wu_ref, wd_ref, o_ref, xn_ref):
    j = pl.program_id(1)

    @pl.when(j == 0)
    def _():
        xn_ref[...] = _rms(x_ref[...], g_ref[...]).astype(BF16)
        o_ref[...] = jnp.zeros_like(o_ref)

    xn = xn_ref[...]
    gate = _dot(xn, wg_ref[...])
    up = _dot(xn, wu_ref[...])
    h = (gate * jax.nn.sigmoid(gate) * up).astype(BF16)
    o_ref[...] += _dot(h, wd_ref[...])

    @pl.when(j == pl.num_programs(1) - 1)
    def _():
        o_ref[...] = x_ref[...] + 0.5 * o_ref[...]


def _ffn(x, gain, w_gate, w_up, w_down):
    n_tok, d = x.shape
    d_ff = w_gate.shape[1]
    tm, tf = TOKEN_TILE, FF_TILE
    return pl.pallas_call(
        _ffn_kernel,
        grid=(n_tok // tm, d_ff // tf),
        in_specs=[
            pl.BlockSpec((tm, d), lambda i, j: (i, 0)),
            pl.BlockSpec((1, d), lambda i, j: (0, 0)),
            pl.BlockSpec((d, tf), lambda i, j: (0, j)),
            pl.BlockSpec((d, tf), lambda i, j: (0, j)),
            pl.BlockSpec((tf, d), lambda i, j: (j, 0)),
        ],
        out_specs=pl.BlockSpec((tm, d), lambda i, j: (i, 0)),
        out_shape=jax.ShapeDtypeStruct((n_tok, d), F32),
        scratch_shapes=[pltpu.VMEM((tm, d), BF16)],
        compiler_params=_compiler_params("parallel", "arbitrary"),
        name="ffn",
    )(x, gain.reshape(1, d), w_gate, w_up, w_down)


def _norm_matmul_kernel(x_ref, g_ref, w_ref, o_ref, xn_ref):
    @pl.when(pl.program_id(1) == 0)
    def _():
        xn_ref[...] = _rms(x_ref[...], g_ref[...]).astype(BF16)

    o_ref[...] = _dot(xn_ref[...], w_ref[...])


def _norm_matmul(x, gain, w, tn):
    n_tok, d = x.shape
    n_out = w.shape[1]
    tm = TOKEN_TILE
    return pl.pallas_call(
        _norm_matmul_kernel,
        grid=(n_tok // tm, n_out // tn),
        in_specs=[
            pl.BlockSpec((tm, d), lambda i, j: (i, 0)),
            pl.BlockSpec((1, d), lambda i, j: (0, 0)),
            pl.BlockSpec((d, tn), lambda i, j: (0, j)),
        ],
        out_specs=pl.BlockSpec((tm, tn), lambda i, j: (i, j)),
        out_shape=jax.ShapeDtypeStruct((n_tok, n_out), F32),
        scratch_shapes=[pltpu.VMEM((tm, d), BF16)],
        compiler_params=_compiler_params("parallel", "arbitrary"),
        name="in_proj",
    )(x, gain.reshape(1, d), w)


def _out_proj_kernel(x_ref, ys_ref, ya_ref, gs_ref, ga_ref, ws_ref, wa_ref, o_ref):
    ys = _rms(ys_ref[...], gs_ref[...]).astype(BF16)
    ya = _rms(ya_ref[...], ga_ref[...]).astype(BF16)
    o_ref[...] = x_ref[...] + _dot(ys, ws_ref[...]) + _dot(ya, wa_ref[...])


def _out_proj(x, y_ssm, y_attn, g_ssm, g_attn, w_out):
    n_tok, d = x.shape
    ssm_w, attn_w = y_ssm.shape[1], y_attn.shape[1]
    tm = TOKEN_TILE
    return pl.pallas_call(
        _out_proj_kernel,
        grid=(n_tok // tm,),
        in_specs=[
            pl.BlockSpec((tm, d), lambda i: (i, 0)),
            pl.BlockSpec((tm, ssm_w), lambda i: (i, 0)),
            pl.BlockSpec((tm, attn_w), lambda i: (i, 0)),
            pl.BlockSpec((1, ssm_w), lambda i: (0, 0)),
            pl.BlockSpec((1, attn_w), lambda i: (0, 0)),
            pl.BlockSpec((ssm_w, d), lambda i: (0, 0)),
            pl.BlockSpec((attn_w, d), lambda i: (0, 0)),
        ],
        out_specs=pl.BlockSpec((tm, d), lambda i: (i, 0)),
        out_shape=jax.ShapeDtypeStruct((n_tok, d), F32),
        compiler_params=_compiler_params("parallel"),
        name="out_proj",
    )(x, y_ssm, y_attn, g_ssm.reshape(1, ssm_w), g_attn.reshape(1, attn_w), w_out[:ssm_w], w_out[ssm_w:])


def _ssm_prep_kernel(are_ref, aim_ref, ldt_ref, btre_ref, btim_ref, cre_ref, cim_ref,
                     wbre_ref, wbim_ref, ztre_ref, ztim_ref, k_ref, powre_ref, powim_ref):
    a_re, a_im = are_ref[0], aim_ref[0]
    dt = jnp.exp(ldt_ref[0])
    mag = jnp.exp(a_re * dt)
    lb_re, lb_im = mag * jnp.cos(a_im * dt), mag * jnp.sin(a_im * dt)
    den = a_re * a_re + a_im * a_im
    q_re = ((lb_re - 1.0) * a_re + lb_im * a_im) / den
    q_im = (lb_im * a_re - (lb_re - 1.0) * a_im) / den
    bt_re, bt_im = btre_ref[0], btim_ref[0]
    bb_re = q_re * bt_re - q_im * bt_im
    bb_im = q_re * bt_im + q_im * bt_re
    c_re, c_im = cre_ref[0], cim_ref[0]

    pw_re, pw_im = [jnp.ones_like(lb_re)], [jnp.zeros_like(lb_im)]
    for _ in range(PROMPT_CHUNK):
        pr, pi = pw_re[-1], pw_im[-1]
        pw_re.append(pr * lb_re - pi * lb_im)
        pw_im.append(pr * lb_im + pi * lb_re)

    wb_re = jnp.concatenate([bb_re * pw_re[d] - bb_im * pw_im[d] for d in range(PROMPT_CHUNK)], axis=0)
    wb_im = jnp.concatenate([bb_re * pw_im[d] + bb_im * pw_re[d] for d in range(PROMPT_CHUNK)], axis=0)
    wbre_ref[0] = wb_re
    wbim_ref[0] = wb_im
    ztre_ref[0] = jnp.concatenate([c_re * pw_re[t + 1] - c_im * pw_im[t + 1] for t in range(PROMPT_CHUNK)], axis=0)
    ztim_ref[0] = jnp.concatenate([-(c_re * pw_im[t + 1] + c_im * pw_re[t + 1]) for t in range(PROMPT_CHUNK)], axis=0)
    k_ref[0] = _dot_nt(wb_re, c_re, HIGHEST) - _dot_nt(wb_im, c_im, HIGHEST)
    pad = jnp.zeros((powre_ref.shape[1] - (PROMPT_CHUNK + 1), a_re.shape[1]), F32)
    powre_ref[0] = jnp.concatenate(pw_re + [pad], axis=0)
    powim_ref[0] = jnp.concatenate(pw_im + [pad], axis=0)


def _ssm_prep(a_re, a_im, log_dt, b_re, b_im, c_re, c_im):
    g, p = a_re.shape
    c = c_re.shape[1]
    rows = PROMPT_CHUNK * c
    pow_rows = 24
    sds = lambda *shape: jax.ShapeDtypeStruct((g,) + shape, F32)
    gs = _group_spec
    return pl.pallas_call(
        _ssm_prep_kernel,
        grid=(g,),
        in_specs=[gs(1, p), gs(1, p), gs(1, 1), gs(c, p), gs(c, p), gs(c, p), gs(c, p)],
        out_specs=[gs(rows, p), gs(rows, p), gs(rows, p), gs(rows, p), gs(rows, c), gs(pow_rows, p), gs(pow_rows, p)],
        out_shape=[sds(rows, p), sds(rows, p), sds(rows, p), sds(rows, p), sds(rows, c), sds(pow_rows, p), sds(pow_rows, p)],
        compiler_params=_compiler_params("parallel"),
        name="ssm_prep",
    )(a_re.reshape(g, 1, p), a_im.reshape(g, 1, p), log_dt.reshape(g, 1, 1),
      jnp.swapaxes(b_re, 1, 2), jnp.swapaxes(b_im, 1, 2), c_re, c_im)


def _ssm_tables(prep, d_skip, w_glu, t):
    wb_re, wb_im, zt_re, zt_im, k, pow_re, pow_im = prep
    g, _, p = wb_re.shape
    c = k.shape[2]
    lag = np.arange(t)[None, :] - np.arange(t)[:, None]
    kl = k.reshape(g, PROMPT_CHUNK, c, c)[:, np.clip(lag, 0, None)]
    kl = jnp.where((lag >= 0)[None, :, :, None, None], kl, 0.0)
    m = kl.transpose(0, 1, 3, 2, 4).reshape(g, t * c, t * c)
    rev = lambda a: a.reshape(g, PROMPT_CHUNK, c, p)[:, t - 1::-1].reshape(g, t * c, p)
    eye = np.eye(t, dtype=np.float32)
    glu = (eye[None, :, None, :, None] * w_glu[:, None, :, None, :]).reshape(g, t * c, t * c)
    d_t = jnp.tile(d_skip, (1, t)).reshape(g, 1, t * c)
    return dict(m=m.astype(BF16), w_re=rev(wb_re), w_im=rev(wb_im),
                zt_re=zt_re[:, :t * c].astype(BF16), zt_im=zt_im[:, :t * c].astype(BF16),
                a_re=pow_re[:, t:t + 1], a_im=pow_im[:, t:t + 1], glu=glu.astype(BF16), d=d_t)


def _ssm_mix_kernel(*refs, bsz, t, aliased):
    (u_ref, m_ref, wre_ref, wim_ref, ztre_ref, ztim_ref, are_ref, aim_ref, d_ref, glu_ref, h0re_ref, h0im_ref) = refs[:12]
    y_ref, hre_ref, him_ref, xs_ref, ug_ref, vre_ref, vim_ref, pre_ref, pim_ref = refs[12 + aliased:]
    rows = u_ref.shape[0] // t
    chunks = rows // bsz
    c = SSM_GROUP_CH
    n_grp = m_ref.shape[0]
    for s in range(t):
        xs_ref[s] = u_ref[pl.ds(s, rows, stride=t), :]
    for gi in range(n_grp):
        lanes = slice(gi * c, (gi + 1) * c)
        u = jnp.concatenate([xs_ref[s, :, lanes] for s in range(t)], axis=1)
        ug_ref[gi] = u
        vre_ref[gi] = _dot(u, wre_ref[gi], SSM_STATE_PRECISION)
        vim_ref[gi] = _dot(u, wim_ref[gi], SSM_STATE_PRECISION)

    def chunk_step(k, h):
        sel = pl.ds(k, bsz, stride=chunks)
        nxt = []
        for gi in range(n_grp):
            h_re, h_im = h[2 * gi], h[2 * gi + 1]
            pre_ref[gi, sel, :] = h_re
            pim_ref[gi, sel, :] = h_im
            a_re, a_im = are_ref[gi], aim_ref[gi]
            nxt.append(a_re * h_re - a_im * h_im + vre_ref[gi, sel, :])
            nxt.append(a_re * h_im + a_im * h_re + vim_ref[gi, sel, :])
        return tuple(nxt)

    h = tuple(r[gi] for gi in range(n_grp) for r in (h0re_ref, h0im_ref))
    h = lax.fori_loop(0, chunks, chunk_step, h)
    for gi in range(n_grp):
        hre_ref[gi] = h[2 * gi]
        him_ref[gi] = h[2 * gi + 1]

    for gi in range(n_grp):
        u = ug_ref[gi]
        y = (_dot(u.astype(BF16), m_ref[gi]) + _dot_nt(pre_ref[gi].astype(BF16), ztre_ref[gi])
             + _dot_nt(pim_ref[gi].astype(BF16), ztim_ref[gi]) + d_ref[gi] * u)
        y = jax.nn.gelu(y)
        y = y * jax.nn.sigmoid(_dot(y.astype(BF16), glu_ref[gi]))
        for s in range(t):
            xs_ref[s, :, gi * c:(gi + 1) * c] = y[:, s * c:(s + 1) * c]
    for s in range(t):
        y_ref[pl.ds(s, rows, stride=t), :] = xs_ref[s]


def _ssm_mix(z, y_prev, tab, h0_re, h0_im, row0, n_rows, bsz, t):
    g, tc, _ = tab["m"].shape
    p = h0_re.shape[2]
    n_tok = z.shape[0]
    width = g * SSM_GROUP_CH
    gps = SSM_GROUPS_PER_STEP
    lanes = gps * SSM_GROUP_CH
    rows = n_rows // t
    blk = lambda *shape: pl.BlockSpec((gps,) + shape, lambda i: (i,) + (0,) * len(shape))
    tok = pl.BlockSpec((n_rows, lanes), lambda i: (row0 // n_rows, i))
    aliased = y_prev is not None
    in_specs = [tok, blk(tc, tc), blk(tc, p), blk(tc, p), blk(tc, p), blk(tc, p), blk(1, p), blk(1, p), blk(1, tc),
                blk(tc, tc), blk(bsz, p), blk(bsz, p)]
    args = [z, tab["m"], tab["w_re"], tab["w_im"], tab["zt_re"], tab["zt_im"], tab["a_re"], tab["a_im"], tab["d"],
            tab["glu"], h0_re, h0_im]
    if aliased:
        in_specs.append(pl.BlockSpec(memory_space=pl.ANY))
        args.append(y_prev)
    return pl.pallas_call(
        functools.partial(_ssm_mix_kernel, bsz=bsz, t=t, aliased=aliased),
        grid=(g // gps,),
        in_specs=in_specs,
        out_specs=[tok, blk(bsz, p), blk(bsz, p)],
        out_shape=[jax.ShapeDtypeStruct((n_tok, width), F32), jax.ShapeDtypeStruct((g, bsz, p), F32),
                   jax.ShapeDtypeStruct((g, bsz, p), F32)],
        scratch_shapes=[pltpu.VMEM((t, rows, lanes), F32), pltpu.VMEM((gps, rows, tc), F32)]
        + [pltpu.VMEM((gps, rows, p), F32)] * 4,
        input_output_aliases={len(args) - 1: 0} if aliased else {},
        compiler_params=_compiler_params("parallel"),
        name="ssm_mix",
    )(*args)


def _bias_table(q_pos, k_pos, n_heads):
    slopes = np.exp2(-8.0 * np.arange(1, n_heads + 1, dtype=np.float64) / n_heads)
    dist = q_pos[:, None] - k_pos[None, :]
    valid = (dist >= 0) & (dist < WINDOW) & (k_pos[None, :] >= 0)
    bias = -slopes[:, None, None] * dist[None].astype(np.float64)
    return np.where(valid[None], bias, -np.inf).astype(np.float32)


def _head_rms(x, gain):
    heads = [_rms(x[:, i:i + HEAD_DIM], gain) for i in range(0, x.shape[1], HEAD_DIM)]
    return jnp.concatenate(heads, axis=1)


def _softmax_sink_pv(logits, sink, v):
    m = jnp.maximum(jnp.max(logits, axis=-1, keepdims=True), sink)
    p = jnp.exp(logits - m)
    denom = jnp.sum(p, axis=-1, keepdims=True) + jnp.exp(sink - m)
    return _dot(p.astype(BF16), v) / denom


def _prompt_attn_kernel(sinks_ref, q_ref, k_ref, v_ref, qg_ref, kg_ref, bias_ref, y_ref, ks_ref, vs_ref,
                        kp_ref, vp_ref, *, n_heads):
    n = pl.program_id(1)

    @pl.when(n == 0)
    def _():
        kp_ref[...] = jnp.zeros_like(kp_ref)
        vp_ref[...] = jnp.zeros_like(vp_ref)

    q_all = q_ref[...]
    k_cur = _head_rms(k_ref[...], kg_ref[...])
    v_cur = v_ref[...]
    k2 = jnp.concatenate([kp_ref[...], k_cur], axis=0).astype(BF16)
    v2 = jnp.concatenate([vp_ref[...], v_cur], axis=0).astype(BF16)
    table = jnp.minimum(n, 1)
    q_per_kv = n_heads // N_KV_HEADS
    outs = []
    for h in range(n_heads):
        kv = h // q_per_kv
        q = (_rms(q_all[:, h * HEAD_DIM:(h + 1) * HEAD_DIM], qg_ref[...]) * QK_SCALE).astype(BF16)
        logits = _dot_nt(q, k2[:, kv * HEAD_DIM:(kv + 1) * HEAD_DIM]) + bias_ref[table, h]
        outs.append(_softmax_sink_pv(logits, sinks_ref[h], v2[:, kv * HEAD_DIM:(kv + 1) * HEAD_DIM]))
    y_ref[...] = jnp.concatenate(outs, axis=1)
    kp_ref[...] = k_cur
    vp_ref[...] = v_cur

    @pl.when(n == pl.num_programs(1) - 1)
    def _():
        ks_ref[0] = k_cur
        vs_ref[0] = v_cur


def _prompt_attn(z, sinks, q_gain, k_gain, bsz, seq, n_tok, col_q, attn_w, kv_w):
    blk = WINDOW
    nb = seq // blk
    n_heads = attn_w // HEAD_DIM
    row = lambda b, n: b * nb + n
    q_blk, k_blk, v_blk = col_q // attn_w, (col_q + attn_w) // kv_w, (col_q + attn_w + kv_w) // kv_w
    pos = np.arange(blk)
    bias = np.stack([_bias_table(pos, np.arange(2 * blk) - blk, n_heads),
                     _bias_table(pos + blk, np.arange(2 * blk), n_heads)])
    return pl.pallas_call(
        functools.partial(_prompt_attn_kernel, n_heads=n_heads),
        grid=(bsz, nb),
        in_specs=[
            pl.BlockSpec(memory_space=pltpu.SMEM),
            pl.BlockSpec((blk, attn_w), lambda b, n: (row(b, n), q_blk)),
            pl.BlockSpec((blk, kv_w), lambda b, n: (row(b, n), k_blk)),
            pl.BlockSpec((blk, kv_w), lambda b, n: (row(b, n), v_blk)),
            pl.BlockSpec((1, HEAD_DIM), lambda b, n: (0, 0)),
            pl.BlockSpec((1, HEAD_DIM), lambda b, n: (0, 0)),
            pl.BlockSpec(bias.shape, lambda b, n: (0, 0, 0, 0)),
        ],
        out_specs=[
            pl.BlockSpec((blk, attn_w), lambda b, n: (row(b, n), 0)),
            pl.BlockSpec((1, blk, kv_w), lambda b, n: (b, 0, 0)),
            pl.BlockSpec((1, blk, kv_w), lambda b, n: (b, 0, 0)),
        ],
        out_shape=[jax.ShapeDtypeStruct((n_tok, attn_w), F32),
                   jax.ShapeDtypeStruct((bsz, blk, kv_w), F32),
                   jax.ShapeDtypeStruct((bsz, blk, kv_w), F32)],
        scratch_shapes=[pltpu.VMEM((blk, kv_w), F32)] * 2,
        compiler_params=_compiler_params("parallel", "arbitrary"),
        name="prompt_attn",
    )(sinks, z, z, z, q_gain.reshape(1, HEAD_DIM), k_gain.reshape(1, HEAD_DIM), jnp.asarray(bias))


def _sample_attn_kernel(q_ref, k_ref, v_ref, ck_ref, cv_ref, qg_ref, kg_ref, bias_ref, sink_ref,
                        y_ref, ks_ref, vs_ref, kbuf_ref, vbuf_ref, *, dec_seq):
    cache_w = ck_ref.shape[1]
    k_new = _head_rms(k_ref[...], kg_ref[...])
    v_new = v_ref[...]
    kbuf_ref[...] = jnp.zeros_like(kbuf_ref)
    vbuf_ref[...] = jnp.zeros_like(vbuf_ref)
    for b in range(q_ref.shape[0]):
        rows = slice(b * dec_seq, (b + 1) * dec_seq)
        kbuf_ref[0:cache_w, :] = ck_ref[b]
        vbuf_ref[0:cache_w, :] = cv_ref[b]
        kbuf_ref[cache_w:cache_w + dec_seq, :] = k_new[rows]
        vbuf_ref[cache_w:cache_w + dec_seq, :] = v_new[rows]
        k2 = kbuf_ref[...].astype(BF16)
        v2 = vbuf_ref[...].astype(BF16)
        for kv in range(q_ref.shape[1]):
            q = (_rms(q_ref[b, kv], qg_ref[...]) * QK_SCALE).astype(BF16)
            logits = _dot_nt(q, k2[:, kv * HEAD_DIM:(kv + 1) * HEAD_DIM]) + bias_ref[kv]
            y_ref[b, kv] = _softmax_sink_pv(logits, sink_ref[kv], v2[:, kv * HEAD_DIM:(kv + 1) * HEAD_DIM])
        ks_ref[b, 0:cache_w - dec_seq, :] = ck_ref[b, dec_seq:cache_w, :]
        vs_ref[b, 0:cache_w - dec_seq, :] = cv_ref[b, dec_seq:cache_w, :]
        ks_ref[b, cache_w - dec_seq:cache_w, :] = k_new[rows]
        vs_ref[b, cache_w - dec_seq:cache_w, :] = v_new[rows]


def _sample_attn(z, cache_k, cache_v, sinks, q_gain, k_gain, row0, dec_seq, col_q, attn_w, kv_w):
    dec_batch, cache_w, _ = cache_k.shape
    n_heads = attn_w // HEAD_DIM
    q_per_kv = n_heads // N_KV_HEADS
    q_rows = q_per_kv * dec_seq
    tile = SAMPLE_BATCH_TILE
    r0 = row0 // (tile * dec_seq)
    k_blk, v_blk = (col_q + attn_w) // kv_w, (col_q + attn_w + kv_w) // kv_w
    keys = 2 * cache_w
    k_pos = np.full(keys, -1)
    k_pos[:cache_w] = PAST_LEN - cache_w + np.arange(cache_w)
    k_pos[cache_w:cache_w + dec_seq] = PAST_LEN + np.arange(dec_seq)
    bias = _bias_table(PAST_LEN + np.arange(dec_seq), k_pos, n_heads).reshape(N_KV_HEADS, q_rows, keys)
    sink_col = jnp.repeat(sinks, dec_seq).reshape(N_KV_HEADS, q_rows, 1)
    q = z[row0:, col_q:col_q + attn_w].reshape(dec_batch, dec_seq, N_KV_HEADS, q_per_kv, HEAD_DIM)
    q = q.transpose(0, 2, 3, 1, 4).reshape(dec_batch, N_KV_HEADS, q_rows, HEAD_DIM)
    y, ks, vs = pl.pallas_call(
        functools.partial(_sample_attn_kernel, dec_seq=dec_seq),
        grid=(dec_batch // tile,),
        in_specs=[
            pl.BlockSpec((tile, N_KV_HEADS, q_rows, HEAD_DIM), lambda i: (i, 0, 0, 0)),
            pl.BlockSpec((tile * dec_seq, kv_w), lambda i: (r0 + i, k_blk)),
            pl.BlockSpec((tile * dec_seq, kv_w), lambda i: (r0 + i, v_blk)),
            pl.BlockSpec((tile, cache_w, kv_w), lambda i: (i, 0, 0)),
            pl.BlockSpec((tile, cache_w, kv_w), lambda i: (i, 0, 0)),
            pl.BlockSpec((1, HEAD_DIM), lambda i: (0, 0)),
            pl.BlockSpec((1, HEAD_DIM), lambda i: (0, 0)),
            pl.BlockSpec(bias.shape, lambda i: (0, 0, 0)),
            pl.BlockSpec(sink_col.shape, lambda i: (0, 0, 0)),
        ],
        out_specs=[
            pl.BlockSpec((tile, N_KV_HEADS, q_rows, HEAD_DIM), lambda i: (i, 0, 0, 0)),
            pl.BlockSpec((tile, cache_w, kv_w), lambda i: (i, 0, 0)),
            pl.BlockSpec((tile, cache_w, kv_w), lambda i: (i, 0, 0)),
        ],
        out_shape=[jax.ShapeDtypeStruct(q.shape, F32),
                   jax.ShapeDtypeStruct(cache_k.shape, F32),
                   jax.ShapeDtypeStruct(cache_v.shape, F32)],
        scratch_shapes=[pltpu.VMEM((keys, kv_w), F32)] * 2,
        compiler_params=_compiler_params("parallel"),
        name="sample_attn",
    )(q, z, z, cache_k, cache_v, q_gain.reshape(1, HEAD_DIM), k_gain.reshape(1, HEAD_DIM), jnp.asarray(bias), sink_col)
    y = y.reshape(dec_batch, N_KV_HEADS, q_per_kv, dec_seq, HEAD_DIM).transpose(0, 3, 1, 2, 4)
    return y.reshape(dec_batch * dec_seq, attn_w), ks, vs


def kernel(x_prompt, x_sample, cache_k, cache_v, state_ssm_re, state_ssm_im, ffn1_norm, ffn1_w_gate, ffn1_w_up, ffn1_w_down, mix_norm, w_in, ssm_A_re, ssm_A_im, ssm_B_re, ssm_B_im, ssm_C_re, ssm_C_im, ssm_D, ssm_log_dt, ssm_w_glu, q_norm, k_norm, sinks, ssm_out_norm, attn_out_norm, w_out, ffn2_norm, ffn2_w_gate, ffn2_w_up, ffn2_w_down):
    bsz, seq, d = x_prompt.shape
    dec_batch, dec_seq, _ = x_sample.shape
    depth = w_in.shape[0]
    ssm_w = ssm_A_re.shape[1] * SSM_GROUP_CH
    attn_w = w_out.shape[1] - ssm_w
    kv_w = N_KV_HEADS * HEAD_DIM
    cache_w = cache_k.shape[2]
    n_prompt = bsz * seq
    n_tok = n_prompt + dec_batch * dec_seq

    x = jnp.concatenate([x_prompt.reshape(n_prompt, d), x_sample.reshape(dec_batch * dec_seq, d)], axis=0)
    zeros_state = jnp.zeros((ssm_w // SSM_GROUP_CH, bsz, SSM_STATE), F32)
    outs = [[] for _ in range(8)]
    for l in range(depth):
        bf = lambda w: w[l].astype(BF16)
        x = _ffn(x, ffn1_norm[l], bf(ffn1_w_gate), bf(ffn1_w_up), bf(ffn1_w_down))
        z = _norm_matmul(x, mix_norm[l], bf(w_in), tn=1280)

        prep = _ssm_prep(ssm_A_re[l], ssm_A_im[l], ssm_log_dt[l], ssm_B_re[l], ssm_B_im[l], ssm_C_re[l], ssm_C_im[l])
        tab_p = _ssm_tables(prep, ssm_D[l], ssm_w_glu[l], PROMPT_CHUNK)
        tab_s = _ssm_tables(prep, ssm_D[l], ssm_w_glu[l], dec_seq)
        grp_major = lambda a: a.transpose(1, 0, 2)
        y_ssm, hp_re, hp_im = _ssm_mix(z, None, tab_p, zeros_state, zeros_state, 0, n_prompt, bsz, PROMPT_CHUNK)
        y_ssm, hs_re, hs_im = _ssm_mix(z, y_ssm, tab_s, grp_major(state_ssm_re[l]), grp_major(state_ssm_im[l]),
                                       n_prompt, dec_batch * dec_seq, dec_batch, dec_seq)
        hp_re, hp_im, hs_re, hs_im = map(grp_major, (hp_re, hp_im, hs_re, hs_im))

        y_attn, pk, pv = _prompt_attn(z, sinks[l], q_norm[l], k_norm[l], bsz, seq, n_tok, ssm_w, attn_w, kv_w)
        ya_s, sk, sv = _sample_attn(z, cache_k[l].reshape(dec_batch, cache_w, kv_w),
                                    cache_v[l].reshape(dec_batch, cache_w, kv_w), sinks[l], q_norm[l], k_norm[l],
                                    n_prompt, dec_seq, ssm_w, attn_w, kv_w)
        y_attn = lax.dynamic_update_slice(y_attn, ya_s, (n_prompt, 0))

        x = _out_proj(x, y_ssm, y_attn, ssm_out_norm[l], attn_out_norm[l], bf(w_out))
        x = _ffn(x, ffn2_norm[l], bf(ffn2_w_gate), bf(ffn2_w_up), bf(ffn2_w_down))

        kv_shape = (-1, N_KV_HEADS, HEAD_DIM)
        for dst, val in zip(outs, (pk.reshape(bsz, *kv_shape), pv.reshape(bsz, *kv_shape), hp_re, hp_im,
                                   sk.reshape(dec_batch, *kv_shape), sv.reshape(dec_batch, *kv_shape), hs_re, hs_im)):
            dst.append(val)
    y_prompt = x[:n_prompt].reshape(bsz, seq, d)
    y_sample = x[n_prompt:].reshape(dec_batch, dec_seq, d)
    return (y_prompt, y_sample) + tuple(jnp.stack(o) for o in outs)
```

```python
import functools

import jax
import jax.numpy as jnp
import numpy as np
from jax import lax
from jax.experimental import pallas as pl
from jax.experimental.pallas import tpu as pltpu

F32 = jnp.float32
BF16 = jnp.bfloat16

EPS = 1e-6
SSM_GROUP_CH = 16
SSM_STATE = 64
HEAD_DIM = 64
N_KV_HEADS = 4
WINDOW = 128
PAST_LEN = 8192
QK_SCALE = HEAD_DIM ** -0.5
PROMPT_CHUNK = 16

TOKEN_TILE = 512
FFN_TOKEN_TILE = 1088
FFN_VMEM_LIMIT_BYTES = 56 * 1024 * 1024
FF_TILE = 512
SAMPLE_BATCH_TILE = 8
VMEM_LIMIT_BYTES = 48 * 1024 * 1024

HIGHEST = lax.Precision.HIGHEST
SSM_STATE_PRECISION = HIGHEST
SSM_GROUPS_PER_STEP = 8


def _rms(x, gain):
    return x * lax.rsqrt(jnp.mean(x * x, axis=-1, keepdims=True) + EPS) * gain


def _dot(a, b, precision=None):
    return jnp.dot(a, b, preferred_element_type=F32, precision=precision)


def _dot_nt(a, b, precision=None):
    return lax.dot_general(a, b, (((1,), (1,)), ((), ())), preferred_element_type=F32, precision=precision)


def _compiler_params(*semantics):
    return pltpu.CompilerParams(dimension_semantics=semantics, vmem_limit_bytes=VMEM_LIMIT_BYTES)


def _group_spec(*shape):
    return pl.BlockSpec((1,) + shape, lambda i: (i,) + (0,) * len(shape))


def _ffn_kernel(x_ref, g_ref, wg_ref, wu_ref, wd_ref, o_ref, xn_ref):
    j = pl.program_id(1)

    @pl.when(j == 0)
    def _():
        xn_ref[...] = _rms(x_ref[...], g_ref[...]).astype(BF16)
        o_ref[...] = jnp.zeros_like(o_ref)

    xn = xn_ref[...]
    gate = _dot(xn, wg_ref[...])
    up = _dot(xn, wu_ref[...])
    h = (gate * jax.nn.sigmoid(gate) * up).astype(BF16)
    o_ref[...] += _dot(h, wd_ref[...])

    @pl.when(j == pl.num_programs(1) - 1)
    def _():
        o_ref[...] = x_ref[...] + 0.5 * o_ref[...]


def _ffn(x, gain, w_gate, w_up, w_down, layer):
    n_tok, d = x.shape
    d_ff = w_gate.shape[2]
    tm = FFN_TOKEN_TILE if n_tok % FFN_TOKEN_TILE == 0 else TOKEN_TILE
    tf = FF_TILE
    return pl.pallas_call(
        _ffn_kernel,
        grid=(n_tok // tm, d_ff // tf),
        in_specs=[
            pl.BlockSpec((tm, d), lambda i, j: (i, 0), pipeline_mode=pl.Buffered(1)),
            pl.BlockSpec((1, d), lambda i, j: (0, 0)),
            pl.BlockSpec((None, d, tf), lambda i, j: (layer, 0, j)),
            pl.BlockSpec((None, d, tf), lambda i, j: (layer, 0, j)),
            pl.BlockSpec((None, tf, d), lambda i, j: (layer, j, 0)),
        ],
        out_specs=pl.BlockSpec((tm, d), lambda i, j: (i, 0)),
        out_shape=jax.ShapeDtypeStruct((n_tok, d), F32),
        scratch_shapes=[pltpu.VMEM((tm, d), BF16)],
        compiler_params=pltpu.CompilerParams(dimension_semantics=("parallel", "arbitrary"),
                                             vmem_limit_bytes=FFN_VMEM_LIMIT_BYTES),
        name="ffn",
    )(x, gain.reshape(1, d), w_gate, w_up, w_down)


def _norm_matmul_kernel(x_ref, g_ref, w_ref, o_ref, xn_ref):
    @pl.when(pl.program_id(1) == 0)
    def _():
        xn_ref[...] = _rms(x_ref[...], g_ref[...]).astype(BF16)

    o_ref[...] = _dot(xn_ref[...], w_ref[...])


def _norm_matmul(x, gain, w, layer, tn):
    n_tok, d = x.shape
    n_out = w.shape[2]
    tm = TOKEN_TILE
    return pl.pallas_call(
        _norm_matmul_kernel,
        grid=(n_tok // tm, n_out // tn),
        in_specs=[
            pl.BlockSpec((tm, d), lambda i, j: (i, 0)),
            pl.BlockSpec((1, d), lambda i, j: (0, 0)),
            pl.BlockSpec((None, d, tn), lambda i, j: (layer, 0, j)),
        ],
        out_specs=pl.BlockSpec((tm, tn), lambda i, j: (i, j)),
        out_shape=jax.ShapeDtypeStruct((n_tok, n_out), F32),
        scratch_shapes=[pltpu.VMEM((tm, d), BF16)],
        compiler_params=_compiler_params("parallel", "arbitrary"),
        name="in_proj",
    )(x, gain.reshape(1, d), w)


def _out_proj_kernel(x_ref, ys_ref, ya_ref, gs_ref, ga_ref, ws_ref, wa_ref, o_ref):
    ys = _rms(ys_ref[...], gs_ref[...]).astype(BF16)
    ya = _rms(ya_ref[...], ga_ref[...]).astype(BF16)
    o_ref[...] = x_ref[...] + _dot(ys, ws_ref[...]) + _dot(ya, wa_ref[...])


def _out_proj(x, y_ssm, y_attn, g_ssm, g_attn, w_out, layer):
    n_tok, d = x.shape
    ssm_w, attn_w = y_ssm.shape[1], y_attn.shape[1]
    tm = TOKEN_TILE
    return pl.pallas_call(
        _out_proj_kernel,
        grid=(n_tok // tm,),
        in_specs=[
            pl.BlockSpec((tm, d), lambda i: (i, 0)),
            pl.BlockSpec((tm, ssm_w), lambda i: (i, 0)),
            pl.BlockSpec((tm, attn_w), lambda i: (i, 0)),
            pl.BlockSpec((1, ssm_w), lambda i: (0, 0)),
            pl.BlockSpec((1, attn_w), lambda i: (0, 0)),
            pl.BlockSpec((None, ssm_w, d), lambda i: (layer, 0, 0)),
            pl.BlockSpec((None, attn_w, d), lambda i: (layer, ssm_w // attn_w, 0)),
        ],
        out_specs=pl.BlockSpec((tm, d), lambda i: (i, 0)),
        out_shape=jax.ShapeDtypeStruct((n_tok, d), F32),
        compiler_params=_compiler_params("parallel"),
        name="out_proj",
    )(x, y_ssm, y_attn, g_ssm.reshape(1, ssm_w), g_attn.reshape(1, attn_w), w_out, w_out)


def _ssm_prep_kernel(are_ref, aim_ref, ldt_ref, btre_ref, btim_ref, cre_ref, cim_ref, d_ref, wglu_ref,
                     m_ref, wre_ref, wim_ref, ztre_ref, ztim_ref, glu_ref, dt_ref, powre_ref, powim_ref):
    a_re, a_im = are_ref[0], aim_ref[0]
    dt = jnp.exp(ldt_ref[0])
    mag = jnp.exp(a_re * dt)
    lb_re, lb_im = mag * jnp.cos(a_im * dt), mag * jnp.sin(a_im * dt)
    den = a_re * a_re + a_im * a_im
    q_re = ((lb_re - 1.0) * a_re + lb_im * a_im) / den
    q_im = (lb_im * a_re - (lb_re - 1.0) * a_im) / den
    bt_re, bt_im = btre_ref[0], btim_ref[0]
    bb_re = q_re * bt_re - q_im * bt_im
    bb_im = q_re * bt_im + q_im * bt_re
    c_re, c_im = cre_ref[0], cim_ref[0]

    pw_re, pw_im = [jnp.ones_like(lb_re)], [jnp.zeros_like(lb_im)]
    for _ in range(PROMPT_CHUNK):
        pr, pi = pw_re[-1], pw_im[-1]
        pw_re.append(pr * lb_re - pi * lb_im)
        pw_im.append(pr * lb_im + pi * lb_re)

    steps = PROMPT_CHUNK
    c = c_re.shape[0]
    w_re = jnp.concatenate([bb_re * pw_re[steps - 1 - s] - bb_im * pw_im[steps - 1 - s] for s in range(steps)], axis=0)
    w_im = jnp.concatenate([bb_re * pw_im[steps - 1 - s] + bb_im * pw_re[steps - 1 - s] for s in range(steps)], axis=0)
    wre_ref[0] = w_re
    wim_ref[0] = w_im
    ztre_ref[0] = jnp.concatenate([c_re * pw_re[t + 1] - c_im * pw_im[t + 1] for t in range(steps)], axis=0).astype(BF16)
    ztim_ref[0] = jnp.concatenate([-(c_re * pw_im[t + 1] + c_im * pw_re[t + 1]) for t in range(steps)], axis=0).astype(BF16)

    lag = _dot_nt(w_re, c_re, HIGHEST) - _dot_nt(w_im, c_im, HIGHEST)
    cols = []
    for t in range(steps):
        live = lag[(steps - 1 - t) * c:, :]
        cols.append(live if t == steps - 1 else
                    jnp.concatenate([live, jnp.zeros(((steps - 1 - t) * c, c), F32)], axis=0))
    m_ref[0] = jnp.concatenate(cols, axis=1).astype(BF16)

    tiled = jnp.concatenate([jnp.concatenate([wglu_ref[0]] * steps, axis=0)] * steps, axis=1)
    shift = c.bit_length() - 1
    row_step = lax.shift_right_logical(lax.broadcasted_iota(jnp.int32, tiled.shape, 0), shift)
    col_step = lax.shift_right_logical(lax.broadcasted_iota(jnp.int32, tiled.shape, 1), shift)
    glu_ref[0] = jnp.where(row_step == col_step, tiled, 0.0).astype(BF16)
    dt_ref[0] = jnp.concatenate([d_ref[0]] * steps, axis=1)

    pad = jnp.zeros((powre_ref.shape[1] - (steps + 1), a_re.shape[1]), F32)
    powre_ref[0] = jnp.concatenate(pw_re + [pad], axis=0)
    powim_ref[0] = jnp.concatenate(pw_im + [pad], axis=0)


def _ssm_prep(a_re, a_im, log_dt, b_re, b_im, c_re, c_im, d_skip, w_glu):
    g, p = a_re.shape
    c = c_re.shape[1]
    assert c & (c - 1) == 0
    tc = PROMPT_CHUNK * c
    pow_rows = 24
    sds = lambda dtype, *shape: jax.ShapeDtypeStruct((g,) + shape, dtype)
    gs = _group_spec
    names = ("m", "w_re", "w_im", "zt_re", "zt_im", "glu", "d", "pow_re", "pow_im")
    outs = pl.pallas_call(
        _ssm_prep_kernel,
        grid=(g,),
        in_specs=[gs(1, p), gs(1, p), gs(1, 1), gs(c, p), gs(c, p), gs(c, p), gs(c, p), gs(1, c), gs(c, c)],
        out_specs=[gs(tc, tc), gs(tc, p), gs(tc, p), gs(tc, p), gs(tc, p), gs(tc, tc), gs(1, tc),
                   gs(pow_rows, p), gs(pow_rows, p)],
        out_shape=[sds(BF16, tc, tc), sds(F32, tc, p), sds(F32, tc, p), sds(BF16, tc, p), sds(BF16, tc, p),
                   sds(BF16, tc, tc), sds(F32, 1, tc), sds(F32, pow_rows, p), sds(F32, pow_rows, p)],
        compiler_params=_compiler_params("parallel"),
        name="ssm_prep",
    )(a_re.reshape(g, 1, p), a_im.reshape(g, 1, p), log_dt.reshape(g, 1, 1),
      jnp.swapaxes(b_re, 1, 2), jnp.swapaxes(b_im, 1, 2), c_re, c_im, d_skip.reshape(g, 1, c), w_glu)
    return dict(zip(names, outs))


def _ssm_tables(prep, t):
    n = t * SSM_GROUP_CH
    full = prep["m"].shape[1]
    return dict(m=prep["m"][:, :n, :n], w_re=prep["w_re"][:, full - n:], w_im=prep["w_im"][:, full - n:],
                zt_re=prep["zt_re"][:, :n], zt_im=prep["zt_im"][:, :n], glu=prep["glu"][:, :n, :n],
                d=prep["d"][:, :, :n], a_re=prep["pow_re"][:, t:t + 1], a_im=prep["pow_im"][:, t:t + 1])


def _ssm_mix_kernel(*refs, bsz, t, aliased):
    (u_ref, m_ref, wre_ref, wim_ref, ztre_ref, ztim_ref, are_ref, aim_ref, d_ref, glu_ref, h0re_ref, h0im_ref) = refs[:12]
    y_ref, hre_ref, him_ref, xs_ref, ug_ref, vre_ref, vim_ref, pre_ref, pim_ref = refs[12 + aliased:]
    rows = u_ref.shape[0] // t
    chunks = rows // bsz
    c = SSM_GROUP_CH
    n_grp = m_ref.shape[0]
    for s in range(t):
        xs_ref[s] = u_ref[pl.ds(s, rows, stride=t), :]
    for gi in range(n_grp):
        lanes = slice(gi * c, (gi + 1) * c)
        u = jnp.concatenate([xs_ref[s, :, lanes] for s in range(t)], axis=1)
        ug_ref[gi] = u
        vre_ref[gi] = _dot(u, wre_ref[gi], SSM_STATE_PRECISION)
        vim_ref[gi] = _dot(u, wim_ref[gi], SSM_STATE_PRECISION)

    def chunk_step(k, h):
        sel = pl.ds(k, bsz, stride=chunks)
        nxt = []
        for gi in range(n_grp):
            h_re, h_im = h[2 * gi], h[2 * gi + 1]
            pre_ref[gi, sel, :] = h_re
            pim_ref[gi, sel, :] = h_im
            a_re, a_im = are_ref[gi], aim_ref[gi]
            nxt.append(a_re * h_re - a_im * h_im + vre_ref[gi, sel, :])
            nxt.append(a_re * h_im + a_im * h_re + vim_ref[gi, sel, :])
        return tuple(nxt)

    h = tuple(r[gi] for gi in range(n_grp) for r in (h0re_ref, h0im_ref))
    h = lax.fori_loop(0, chunks, chunk_step, h)
    for gi in range(n_grp):
        hre_ref[gi] = h[2 * gi]
        him_ref[gi] = h[2 * gi + 1]

    for gi in range(n_grp):
        u = ug_ref[gi]
        y = (_dot(u.astype(BF16), m_ref[gi]) + _dot_nt(pre_ref[gi].astype(BF16), ztre_ref[gi])
             + _dot_nt(pim_ref[gi].astype(BF16), ztim_ref[gi]) + d_ref[gi] * u)
        y = jax.nn.gelu(y)
        y = y * jax.nn.sigmoid(_dot(y.astype(BF16), glu_ref[gi]))
        for s in range(t):
            xs_ref[s, :, gi * c:(gi + 1) * c] = y[:, s * c:(s + 1) * c]
    for s in range(t):
        y_ref[pl.ds(s, rows, stride=t), :] = xs_ref[s]


def _ssm_mix(z, y_prev, tab, h0_re, h0_im, row0, n_rows, bsz, t):
    g, tc, _ = tab["m"].shape
    p = h0_re.shape[2]
    n_tok = z.shape[0]
    width = g * SSM_GROUP_CH
    gps = SSM_GROUPS_PER_STEP
    lanes = gps * SSM_GROUP_CH
    rows = n_rows // t
    blk = lambda *shape: pl.BlockSpec((gps,) + shape, lambda i: (i,) + (0,) * len(shape))
    tok = pl.BlockSpec((n_rows, lanes), lambda i: (row0 // n_rows, i))
    aliased = y_prev is not None
    in_specs = [tok, blk(tc, tc), blk(tc, p), blk(tc, p), blk(tc, p), blk(tc, p), blk(1, p), blk(1, p), blk(1, tc),
                blk(tc, tc), blk(bsz, p), blk(bsz, p)]
    args = [z, tab["m"], tab["w_re"], tab["w_im"], tab["zt_re"], tab["zt_im"], tab["a_re"], tab["a_im"], tab["d"],
            tab["glu"], h0_re, h0_im]
    if aliased:
        in_specs.append(pl.BlockSpec(memory_space=pl.ANY))
        args.append(y_prev)
    return pl.pallas_call(
        functools.partial(_ssm_mix_kernel, bsz=bsz, t=t, aliased=aliased),
        grid=(g // gps,),
        in_specs=in_specs,
        out_specs=[tok, blk(bsz, p), blk(bsz, p)],
        out_shape=[jax.ShapeDtypeStruct((n_tok, width), F32), jax.ShapeDtypeStruct((g, bsz, p), F32),
                   jax.ShapeDtypeStruct((g, bsz, p), F32)],
        scratch_shapes=[pltpu.VMEM((t, rows, lanes), F32), pltpu.VMEM((gps, rows, tc), F32)]
        + [pltpu.VMEM((gps, rows, p), F32)] * 4,
        input_output_aliases={len(args) - 1: 0} if aliased else {},
        compiler_params=_compiler_params("parallel"),
        name="ssm_mix",
    )(*args)


def _bias_table(q_pos, k_pos, n_heads):
    slopes = np.exp2(-8.0 * np.arange(1, n_heads + 1, dtype=np.float64) / n_heads)
    dist = q_pos[:, None] - k_pos[None, :]
    valid = (dist >= 0) & (dist < WINDOW) & (k_pos[None, :] >= 0)
    bias = -slopes[:, None, None] * dist[None].astype(np.float64)
    return np.where(valid[None], bias, -np.inf).astype(np.float32)


def _head_rms(x, gain):
    heads = [_rms(x[:, i:i + HEAD_DIM], gain) for i in range(0, x.shape[1], HEAD_DIM)]
    return jnp.concatenate(heads, axis=1)


def _softmax_sink_pv(logits, sink, v):
    m = jnp.maximum(jnp.max(logits, axis=-1, keepdims=True), sink)
    p = jnp.exp(logits - m)
    denom = jnp.sum(p, axis=-1, keepdims=True) + jnp.exp(sink - m)
    return _dot(p.astype(BF16), v) / denom


def _prompt_attn_kernel(sinks_ref, q_ref, k_ref, v_ref, qg_ref, kg_ref, bias_ref, y_ref, ks_ref, vs_ref,
                        kp_ref, vp_ref, *, n_heads):
    n = pl.program_id(1)

    @pl.when(n == 0)
    def _():
        kp_ref[...] = jnp.zeros_like(kp_ref)
        vp_ref[...] = jnp.zeros_like(vp_ref)

    q_all = q_ref[...]
    k_cur = _head_rms(k_ref[...], kg_ref[...])
    v_cur = v_ref[...]
    k2 = jnp.concatenate([kp_ref[...], k_cur], axis=0).astype(BF16)
    v2 = jnp.concatenate([vp_ref[...], v_cur], axis=0).astype(BF16)
    table = jnp.minimum(n, 1)
    q_per_kv = n_heads // N_KV_HEADS
    outs = []
    for h in range(n_heads):
        kv = h // q_per_kv
        q = (_rms(q_all[:, h * HEAD_DIM:(h + 1) * HEAD_DIM], qg_ref[...]) * QK_SCALE).astype(BF16)
        logits = _dot_nt(q, k2[:, kv * HEAD_DIM:(kv + 1) * HEAD_DIM]) + bias_ref[table, h]
        outs.append(_softmax_sink_pv(logits, sinks_ref[h], v2[:, kv * HEAD_DIM:(kv + 1) * HEAD_DIM]))
    y_ref[...] = jnp.concatenate(outs, axis=1)
    kp_ref[...] = k_cur
    vp_ref[...] = v_cur

    @pl.when(n == pl.num_programs(1) - 1)
    def _():
        ks_ref[0] = k_cur
        vs_ref[0] = v_cur


def _prompt_attn(z, sinks, q_gain, k_gain, bsz, seq, n_tok, col_q, attn_w, kv_w):
    blk = WINDOW
    nb = seq // blk
    n_heads = attn_w // HEAD_DIM
    row = lambda b, n: b * nb + n
    q_blk, k_blk, v_blk = col_q // attn_w, (col_q + attn_w) // kv_w, (col_q + attn_w + kv_w) // kv_w
    pos = np.arange(blk)
    bias = np.stack([_bias_table(pos, np.arange(2 * blk) - blk, n_heads),
                     _bias_table(pos + blk, np.arange(2 * blk), n_heads)])
    return pl.pallas_call(
        functools.partial(_prompt_attn_kernel, n_heads=n_heads),
        grid=(bsz, nb),
        in_specs=[
            pl.BlockSpec(memory_space=pltpu.SMEM),
            pl.BlockSpec((blk, attn_w), lambda b, n: (row(b, n), q_blk)),
            pl.BlockSpec((blk, kv_w), lambda b, n: (row(b, n), k_blk)),
            pl.BlockSpec((blk, kv_w), lambda b, n: (row(b, n), v_blk)),
            pl.BlockSpec((1, HEAD_DIM), lambda b, n: (0, 0)),
            pl.BlockSpec((1, HEAD_DIM), lambda b, n: (0, 0)),
            pl.BlockSpec(bias.shape, lambda b, n: (0, 0, 0, 0)),
        ],
        out_specs=[
            pl.BlockSpec((blk, attn_w), lambda b, n: (row(b, n), 0)),
            pl.BlockSpec((1, blk, kv_w), lambda b, n: (b, 0, 0)),
            pl.BlockSpec((1, blk, kv_w), lambda b, n: (b, 0, 0)),
        ],
        out_shape=[jax.ShapeDtypeStruct((n_tok, attn_w), F32),
                   jax.ShapeDtypeStruct((bsz, blk, kv_w), F32),
                   jax.ShapeDtypeStruct((bsz, blk, kv_w), F32)],
        scratch_shapes=[pltpu.VMEM((blk, kv_w), F32)] * 2,
        compiler_params=_compiler_params("parallel", "arbitrary"),
        name="prompt_attn",
    )(sinks, z, z, z, q_gain.reshape(1, HEAD_DIM), k_gain.reshape(1, HEAD_DIM), jnp.asarray(bias))


def _sample_attn_kernel(q_ref, k_ref, v_ref, ck_ref, cv_ref, qg_ref, kg_ref, bias_ref, sink_ref,
                        y_ref, ks_ref, vs_ref, kbuf_ref, vbuf_ref, *, dec_seq):
    cache_w = ck_ref.shape[1]
    k_new = _head_rms(k_ref[...], kg_ref[...])
    v_new = v_ref[...]
    kbuf_ref[...] = jnp.zeros_like(kbuf_ref)
    vbuf_ref[...] = jnp.zeros_like(vbuf_ref)
    for b in range(q_ref.shape[0]):
        rows = slice(b * dec_seq, (b + 1) * dec_seq)
        kbuf_ref[0:cache_w, :] = ck_ref[b]
        vbuf_ref[0:cache_w, :] = cv_ref[b]
        kbuf_ref[cache_w:cache_w + dec_seq, :] = k_new[rows]
        vbuf_ref[cache_w:cache_w + dec_seq, :] = v_new[rows]
        k2 = kbuf_ref[...].astype(BF16)
        v2 = vbuf_ref[...].astype(BF16)
        for kv in range(q_ref.shape[1]):
            q = (_rms(q_ref[b, kv], qg_ref[...]) * QK_SCALE).astype(BF16)
            logits = _dot_nt(q, k2[:, kv * HEAD_DIM:(kv + 1) * HEAD_DIM]) + bias_ref[kv]
            y_ref[b, kv] = _softmax_sink_pv(logits, sink_ref[kv], v2[:, kv * HEAD_DIM:(kv + 1) * HEAD_DIM])
        ks_ref[b, 0:cache_w - dec_seq, :] = ck_ref[b, dec_seq:cache_w, :]
        vs_ref[b, 0:cache_w - dec_seq, :] = cv_ref[b, dec_seq:cache_w, :]
        ks_ref[b, cache_w - dec_seq:cache_w, :] = k_new[rows]
        vs_ref[b, cache_w - dec_seq:cache_w, :] = v_new[rows]


def _sample_attn(z, cache_k, cache_v, sinks, q_gain, k_gain, row0, dec_seq, col_q, attn_w, kv_w):
    dec_batch, cache_w, _ = cache_k.shape
    n_heads = attn_w // HEAD_DIM
    q_per_kv = n_heads // N_KV_HEADS
    q_rows = q_per_kv * dec_seq
    tile = SAMPLE_BATCH_TILE
    r0 = row0 // (tile * dec_seq)
    k_blk, v_blk = (col_q + attn_w) // kv_w, (col_q + attn_w + kv_w) // kv_w
    keys = 2 * cache_w
    k_pos = np.full(keys, -1)
    k_pos[:cache_w] = PAST_LEN - cache_w + np.arange(cache_w)
    k_pos[cache_w:cache_w + dec_seq] = PAST_LEN + np.arange(dec_seq)
    bias = _bias_table(PAST_LEN + np.arange(dec_seq), k_pos, n_heads).reshape(N_KV_HEADS, q_rows, keys)
    sink_col = jnp.repeat(sinks, dec_seq).reshape(N_KV_HEADS, q_rows, 1)
    q = z[row0:, col_q:col_q + attn_w].reshape(dec_batch, dec_seq, N_KV_HEADS, q_per_kv, HEAD_DIM)
    q = q.transpose(0, 2, 3, 1, 4).reshape(dec_batch, N_KV_HEADS, q_rows, HEAD_DIM)
    y, ks, vs = pl.pallas_call(
        functools.partial(_sample_attn_kernel, dec_seq=dec_seq),
        grid=(dec_batch // tile,),
        in_specs=[
            pl.BlockSpec((tile, N_KV_HEADS, q_rows, HEAD_DIM), lambda i: (i, 0, 0, 0)),
            pl.BlockSpec((tile * dec_seq, kv_w), lambda i: (r0 + i, k_blk)),
            pl.BlockSpec((tile * dec_seq, kv_w), lambda i: (r0 + i, v_blk)),
            pl.BlockSpec((tile, cache_w, kv_w), lambda i: (i, 0, 0)),
            pl.BlockSpec((tile, cache_w, kv_w), lambda i: (i, 0, 0)),
            pl.BlockSpec((1, HEAD_DIM), lambda i: (0, 0)),
            pl.BlockSpec((1, HEAD_DIM), lambda i: (0, 0)),
            pl.BlockSpec(bias.shape, lambda i: (0, 0, 0)),
            pl.BlockSpec(sink_col.shape, lambda i: (0, 0, 0)),
        ],
        out_specs=[
            pl.BlockSpec((tile, N_KV_HEADS, q_rows, HEAD_DIM), lambda i: (i, 0, 0, 0)),
            pl.BlockSpec((tile, cache_w, kv_w), lambda i: (i, 0, 0)),
            pl.BlockSpec((tile, cache_w, kv_w), lambda i: (i, 0, 0)),
        ],
        out_shape=[jax.ShapeDtypeStruct(q.shape, F32),
                   jax.ShapeDtypeStruct(cache_k.shape, F32),
                   jax.ShapeDtypeStruct(cache_v.shape, F32)],
        scratch_shapes=[pltpu.VMEM((keys, kv_w), F32)] * 2,
        compiler_params=_compiler_params("parallel"),
        name="sample_attn",
    )(q, z, z, cache_k, cache_v, q_gain.reshape(1, HEAD_DIM), k_gain.reshape(1, HEAD_DIM), jnp.asarray(bias), sink_col)
    y = y.reshape(dec_batch, N_KV_HEADS, q_per_kv, dec_seq, HEAD_DIM).transpose(0, 3, 1, 2, 4)
    return y.reshape(dec_batch * dec_seq, attn_w), ks, vs


def kernel(x_prompt, x_sample, cache_k, cache_v, state_ssm_re, state_ssm_im, ffn1_norm, ffn1_w_gate, ffn1_w_up, ffn1_w_down, mix_norm, w_in, ssm_A_re, ssm_A_im, ssm_B_re, ssm_B_im, ssm_C_re, ssm_C_im, ssm_D, ssm_log_dt, ssm_w_glu, q_norm, k_norm, sinks, ssm_out_norm, attn_out_norm, w_out, ffn2_norm, ffn2_w_gate, ffn2_w_up, ffn2_w_down):
    bsz, seq, d = x_prompt.shape
    dec_batch, dec_seq, _ = x_sample.shape
    depth = w_in.shape[0]
    ssm_w = ssm_A_re.shape[1] * SSM_GROUP_CH
    attn_w = w_out.shape[1] - ssm_w
    kv_w = N_KV_HEADS * HEAD_DIM
    cache_w = cache_k.shape[2]
    n_prompt = bsz * seq
    n_tok = n_prompt + dec_batch * dec_seq

    x = jnp.concatenate([x_prompt.reshape(n_prompt, d), x_sample.reshape(dec_batch * dec_seq, d)], axis=0)
    zeros_state = jnp.zeros((ssm_w // SSM_GROUP_CH, bsz, SSM_STATE), F32)
    outs = [[] for _ in range(8)]
    bf = lambda w: w.astype(BF16)
    ffn1_w, ffn2_w = [tuple(map(bf, ws)) for ws in ((ffn1_w_gate, ffn1_w_up, ffn1_w_down),
                                                    (ffn2_w_gate, ffn2_w_up, ffn2_w_down))]
    w_in_bf, w_out_bf = bf(w_in), bf(w_out)
    for l in range(depth):
        x = _ffn(x, ffn1_norm[l], *ffn1_w, layer=l)
        z = _norm_matmul(x, mix_norm[l], w_in_bf, layer=l, tn=1280)

        prep = _ssm_prep(ssm_A_re[l], ssm_A_im[l], ssm_log_dt[l], ssm_B_re[l], ssm_B_im[l], ssm_C_re[l], ssm_C_im[l],
                         ssm_D[l], ssm_w_glu[l])
        tab_p = _ssm_tables(prep, PROMPT_CHUNK)
        tab_s = _ssm_tables(prep, dec_seq)
        grp_major = lambda a: a.transpose(1, 0, 2)
        y_ssm, hp_re, hp_im = _ssm_mix(z, None, tab_p, zeros_state, zeros_state, 0, n_prompt, bsz, PROMPT_CHUNK)
        y_ssm, hs_re, hs_im = _ssm_mix(z, y_ssm, tab_s, grp_major(state_ssm_re[l]), grp_major(state_ssm_im[l]),
                                       n_prompt, dec_batch * dec_seq, dec_batch, dec_seq)
        hp_re, hp_im, hs_re, hs_im = map(grp_major, (hp_re, hp_im, hs_re, hs_im))

        y_attn, pk, pv = _prompt_attn(z, sinks[l], q_norm[l], k_norm[l], bsz, seq, n_tok, ssm_w, attn_w, kv_w)
        ya_s, sk, sv = _sample_attn(z, cache_k[l].reshape(dec_batch, cache_w, kv_w),
                                    cache_v[l].reshape(dec_batch, cache_w, kv_w), sinks[l], q_norm[l], k_norm[l],
                                    n_prompt, dec_seq, ssm_w, attn_w, kv_w)
        y_attn = lax.dynamic_update_slice(y_attn, ya_s, (n_prompt, 0))

        x = _out_proj(x, y_ssm, y_attn, ssm_out_norm[l], attn_out_norm[l], w_out_bf, layer=l)
        x = _ffn(x, ffn2_norm[l], *ffn2_w, layer=l)

        kv_shape = (-1, N_KV_HEADS, HEAD_DIM)
        for dst, val in zip(outs, (pk.reshape(bsz, *kv_shape), pv.reshape(bsz, *kv_shape), hp_re, hp_im,
                                   sk.reshape(dec_batch, *kv_shape), sv.reshape(dec_batch, *kv_shape), hs_re, hs_im)):
            dst.append(val)
    y_prompt = x[:n_prompt].reshape(bsz, seq, d)
    y_sample = x[n_prompt:].reshape(dec_batch, dec_seq, d)
    return (y_prompt, y_sample) + tuple(jnp.stack(o) for o in outs)
```

```python
import functools

import jax
import jax.numpy as jnp
import numpy as np
from jax import lax
from jax.experimental import pallas as pl
from jax.experimental.pallas import tpu as pltpu

F32 = jnp.float32
BF16 = jnp.bfloat16

EPS = 1e-6
SSM_GROUP_CH = 16
SSM_STATE = 64
HEAD_DIM = 64
N_KV_HEADS = 4
WINDOW = 128
PAST_LEN = 8192
QK_SCALE = HEAD_DIM ** -0.5
PROMPT_CHUNK = 16

TOKEN_TILE = 512
FFN_TOKEN_TILE = 1088
FFN_VMEM_LIMIT_BYTES = 60 * 1024 * 1024
FF_TILE = 512
SAMPLE_BATCH_TILE = 8
VMEM_LIMIT_BYTES = 48 * 1024 * 1024

HIGHEST = lax.Precision.HIGHEST
SSM_STATE_PRECISION = HIGHEST
SSM_GROUPS_PER_STEP = 8


def _rms(x, gain):
    return x * lax.rsqrt(jnp.mean(x * x, axis=-1, keepdims=True) + EPS) * gain


def _dot(a, b, precision=None):
    return jnp.dot(a, b, preferred_element_type=F32, precision=precision)


def _dot_nt(a, b, precision=None):
    return lax.dot_general(a, b, (((1,), (1,)), ((), ())), preferred_element_type=F32, precision=precision)


def _compiler_params(*semantics):
    return pltpu.CompilerParams(dimension_semantics=semantics, vmem_limit_bytes=VMEM_LIMIT_BYTES)


def _group_spec(*shape):
    return pl.BlockSpec((1,) + shape, lambda i: (i,) + (0,) * len(shape))


def _ffn_kernel(x_ref, g_ref, wg_ref, wu_ref, wd_ref, o_ref, xn_ref):
    j = pl.program_id(1)

    @pl.when(j == 0)
    def _():
        xn_ref[...] = _rms(x_ref[...], g_ref[...]).astype(BF16)
        o_ref[...] = jnp.zeros_like(o_ref)

    xn = xn_ref[...]
    gate = _dot(xn, wg_ref[...])
    up = _dot(xn, wu_ref[...])
    h = (gate * jax.nn.sigmoid(gate) * up).astype(BF16)
    o_ref[...] += _dot(h, wd_ref[...])

    @pl.when(j == pl.num_programs(1) - 1)
    def _():
        o_ref[...] = x_ref[...] + 0.5 * o_ref[...]


def _ffn(x, gain, w_gate, w_up, w_down, layer):
    n_tok, d = x.shape
    d_ff = w_gate.shape[2]
    tm = FFN_TOKEN_TILE if n_tok % FFN_TOKEN_TILE == 0 else TOKEN_TILE
    tf = FF_TILE
    return pl.pallas_call(
        _ffn_kernel,
        grid=(n_tok // tm, d_ff // tf),
        in_specs=[
            pl.BlockSpec((tm, d), lambda i, j: (i, 0)),
            pl.BlockSpec((1, d), lambda i, j: (0, 0)),
            pl.BlockSpec((None, d, tf), lambda i, j: (layer, 0, j)),
            pl.BlockSpec((None, d, tf), lambda i, j: (layer, 0, j)),
            pl.BlockSpec((None, tf, d), lambda i, j: (layer, j, 0)),
        ],
        out_specs=pl.BlockSpec((tm, d), lambda i, j: (i, 0)),
        out_shape=jax.ShapeDtypeStruct((n_tok, d), F32),
        scratch_shapes=[pltpu.VMEM((tm, d), BF16)],
        compiler_params=pltpu.CompilerParams(dimension_semantics=("parallel", "arbitrary"),
                                             vmem_limit_bytes=FFN_VMEM_LIMIT_BYTES),
        name="ffn",
    )(x, gain.reshape(1, d), w_gate, w_up, w_down)


def _norm_matmul_kernel(x_ref, g_ref, w_ref, o_ref, xn_ref):
    @pl.when(pl.program_id(1) == 0)
    def _():
        xn_ref[...] = _rms(x_ref[...], g_ref[...]).astype(BF16)

    o_ref[...] = _dot(xn_ref[...], w_ref[...])


def _norm_matmul(x, gain, w, layer, tn):
    n_tok, d = x.shape
    n_out = w.shape[2]
    tm = TOKEN_TILE
    return pl.pallas_call(
        _norm_matmul_kernel,
        grid=(n_tok // tm, n_out // tn),
        in_specs=[
            pl.BlockSpec((tm, d), lambda i, j: (i, 0)),
            pl.BlockSpec((1, d), lambda i, j: (0, 0)),
            pl.BlockSpec((None, d, tn), lambda i, j: (layer, 0, j)),
        ],
        out_specs=pl.BlockSpec((tm, tn), lambda i, j: (i, j)),
        out_shape=jax.ShapeDtypeStruct((n_tok, n_out), F32),
        scratch_shapes=[pltpu.VMEM((tm, d), BF16)],
        compiler_params=_compiler_params("parallel", "arbitrary"),
        name="in_proj",
    )(x, gain.reshape(1, d), w)


def _out_proj_kernel(x_ref, ys_ref, ya_ref, gs_ref, ga_ref, ws_ref, wa_ref, o_ref):
    ys = _rms(ys_ref[...], gs_ref[...]).astype(BF16)
    ya = _rms(ya_ref[...], ga_ref[...]).astype(BF16)
    o_ref[...] = x_ref[...] + _dot(ys, ws_ref[...]) + _dot(ya, wa_ref[...])


def _out_proj(x, y_ssm, y_attn, g_ssm, g_attn, w_out, layer):
    n_tok, d = x.shape
    ssm_w, attn_w = y_ssm.shape[1], y_attn.shape[1]
    tm = TOKEN_TILE
    return pl.pallas_call(
        _out_proj_kernel,
        grid=(n_tok // tm,),
        in_specs=[
            pl.BlockSpec((tm, d), lambda i: (i, 0)),
            pl.BlockSpec((tm, ssm_w), lambda i: (i, 0)),
            pl.BlockSpec((tm, attn_w), lambda i: (i, 0)),
            pl.BlockSpec((1, ssm_w), lambda i: (0, 0)),
            pl.BlockSpec((1, attn_w), lambda i: (0, 0)),
            pl.BlockSpec((None, ssm_w, d), lambda i: (layer, 0, 0)),
            pl.BlockSpec((None, attn_w, d), lambda i: (layer, ssm_w // attn_w, 0)),
        ],
        out_specs=pl.BlockSpec((tm, d), lambda i: (i, 0)),
        out_shape=jax.ShapeDtypeStruct((n_tok, d), F32),
        compiler_params=_compiler_params("parallel"),
        name="out_proj",
    )(x, y_ssm, y_attn, g_ssm.reshape(1, ssm_w), g_attn.reshape(1, attn_w), w_out, w_out)


def _ssm_prep_kernel(are_ref, aim_ref, ldt_ref, btre_ref, btim_ref, cre_ref, cim_ref, d_ref, wglu_ref,
                     m_ref, wre_ref, wim_ref, ztre_ref, ztim_ref, glu_ref, dt_ref, powre_ref, powim_ref):
    a_re, a_im = are_ref[0], aim_ref[0]
    dt = jnp.exp(ldt_ref[0])
    mag = jnp.exp(a_re * dt)
    lb_re, lb_im = mag * jnp.cos(a_im * dt), mag * jnp.sin(a_im * dt)
    den = a_re * a_re + a_im * a_im
    q_re = ((lb_re - 1.0) * a_re + lb_im * a_im) / den
    q_im = (lb_im * a_re - (lb_re - 1.0) * a_im) / den
    bt_re, bt_im = btre_ref[0], btim_ref[0]
    bb_re = q_re * bt_re - q_im * bt_im
    bb_im = q_re * bt_im + q_im * bt_re
    c_re, c_im = cre_ref[0], cim_ref[0]

    pw_re, pw_im = [jnp.ones_like(lb_re)], [jnp.zeros_like(lb_im)]
    for _ in range(PROMPT_CHUNK):
        pr, pi = pw_re[-1], pw_im[-1]
        pw_re.append(pr * lb_re - pi * lb_im)
        pw_im.append(pr * lb_im + pi * lb_re)

    steps = PROMPT_CHUNK
    c = c_re.shape[0]
    w_re = jnp.concatenate([bb_re * pw_re[steps - 1 - s] - bb_im * pw_im[steps - 1 - s] for s in range(steps)], axis=0)
    w_im = jnp.concatenate([bb_re * pw_im[steps - 1 - s] + bb_im * pw_re[steps - 1 - s] for s in range(steps)], axis=0)
    wre_ref[0] = w_re
    wim_ref[0] = w_im
    ztre_ref[0] = jnp.concatenate([c_re * pw_re[t + 1] - c_im * pw_im[t + 1] for t in range(steps)], axis=0).astype(BF16)
    ztim_ref[0] = jnp.concatenate([-(c_re * pw_im[t + 1] + c_im * pw_re[t + 1]) for t in range(steps)], axis=0).astype(BF16)

    lag = _dot_nt(w_re, c_re, HIGHEST) - _dot_nt(w_im, c_im, HIGHEST)
    cols = []
    for t in range(steps):
        live = lag[(steps - 1 - t) * c:, :]
        cols.append(live if t == steps - 1 else
                    jnp.concatenate([live, jnp.zeros(((steps - 1 - t) * c, c), F32)], axis=0))
    m_ref[0] = jnp.concatenate(cols, axis=1).astype(BF16)

    tiled = jnp.concatenate([jnp.concatenate([wglu_ref[0]] * steps, axis=0)] * steps, axis=1)
    shift = c.bit_length() - 1
    row_step = lax.shift_right_logical(lax.broadcasted_iota(jnp.int32, tiled.shape, 0), shift)
    col_step = lax.shift_right_logical(lax.broadcasted_iota(jnp.int32, tiled.shape, 1), shift)
    glu_ref[0] = jnp.where(row_step == col_step, tiled, 0.0).astype(BF16)
    dt_ref[0] = jnp.concatenate([d_ref[0]] * steps, axis=1)

    pad = jnp.zeros((powre_ref.shape[1] - (steps + 1), a_re.shape[1]), F32)
    powre_ref[0] = jnp.concatenate(pw_re + [pad], axis=0)
    powim_ref[0] = jnp.concatenate(pw_im + [pad], axis=0)


def _ssm_prep(a_re, a_im, log_dt, b_re, b_im, c_re, c_im, d_skip, w_glu):
    g, p = a_re.shape
    c = c_re.shape[1]
    assert c & (c - 1) == 0
    tc = PROMPT_CHUNK * c
    pow_rows = 24
    sds = lambda dtype, *shape: jax.ShapeDtypeStruct((g,) + shape, dtype)
    gs = _group_spec
    names = ("m", "w_re", "w_im", "zt_re", "zt_im", "glu", "d", "pow_re", "pow_im")
    outs = pl.pallas_call(
        _ssm_prep_kernel,
        grid=(g,),
        in_specs=[gs(1, p), gs(1, p), gs(1, 1), gs(c, p), gs(c, p), gs(c, p), gs(c, p), gs(1, c), gs(c, c)],
        out_specs=[gs(tc, tc), gs(tc, p), gs(tc, p), gs(tc, p), gs(tc, p), gs(tc, tc), gs(1, tc),
                   gs(pow_rows, p), gs(pow_rows, p)],
        out_shape=[sds(BF16, tc, tc), sds(F32, tc, p), sds(F32, tc, p), sds(BF16, tc, p), sds(BF16, tc, p),
                   sds(BF16, tc, tc), sds(F32, 1, tc), sds(F32, pow_rows, p), sds(F32, pow_rows, p)],
        compiler_params=_compiler_params("parallel"),
        name="ssm_prep",
    )(a_re.reshape(g, 1, p), a_im.reshape(g, 1, p), log_dt.reshape(g, 1, 1),
      jnp.swapaxes(b_re, 1, 2), jnp.swapaxes(b_im, 1, 2), c_re, c_im, d_skip.reshape(g, 1, c), w_glu)
    return dict(zip(names, outs))


def _ssm_tables(prep, t):
    n = t * SSM_GROUP_CH
    full = prep["m"].shape[1]
    return dict(m=prep["m"][:, :n, :n], w_re=prep["w_re"][:, full - n:], w_im=prep["w_im"][:, full - n:],
                zt_re=prep["zt_re"][:, :n], zt_im=prep["zt_im"][:, :n], glu=prep["glu"][:, :n, :n],
                d=prep["d"][:, :, :n], a_re=prep["pow_re"][:, t:t + 1], a_im=prep["pow_im"][:, t:t + 1])


def _ssm_mix_kernel(*refs, bsz, t, aliased):
    (u_ref, m_ref, wre_ref, wim_ref, ztre_ref, ztim_ref, are_ref, aim_ref, d_ref, glu_ref, h0re_ref, h0im_ref) = refs[:12]
    y_ref, hre_ref, him_ref, xs_ref, ug_ref, vre_ref, vim_ref, pre_ref, pim_ref = refs[12 + aliased:]
    rows = u_ref.shape[0] // t
    chunks = rows // bsz
    c = SSM_GROUP_CH
    n_grp = m_ref.shape[0]
    for s in range(t):
        xs_ref[s] = u_ref[pl.ds(s, rows, stride=t), :]
    for gi in range(n_grp):
        lanes = slice(gi * c, (gi + 1) * c)
        u = jnp.concatenate([xs_ref[s, :, lanes] for s in range(t)], axis=1)
        ug_ref[gi] = u
        vre_ref[gi] = _dot(u, wre_ref[gi], SSM_STATE_PRECISION)
        vim_ref[gi] = _dot(u, wim_ref[gi], SSM_STATE_PRECISION)

    def chunk_step(k, h):
        sel = pl.ds(k, bsz, stride=chunks)
        nxt = []
        for gi in range(n_grp):
            h_re, h_im = h[2 * gi], h[2 * gi + 1]
            pre_ref[gi, sel, :] = h_re
            pim_ref[gi, sel, :] = h_im
            a_re, a_im = are_ref[gi], aim_ref[gi]
            nxt.append(a_re * h_re - a_im * h_im + vre_ref[gi, sel, :])
            nxt.append(a_re * h_im + a_im * h_re + vim_ref[gi, sel, :])
        return tuple(nxt)

    h = tuple(r[gi] for gi in range(n_grp) for r in (h0re_ref, h0im_ref))
    h = lax.fori_loop(0, chunks, chunk_step, h)
    for gi in range(n_grp):
        hre_ref[gi] = h[2 * gi]
        him_ref[gi] = h[2 * gi + 1]

    for gi in range(n_grp):
        u = ug_ref[gi]
        y = (_dot(u.astype(BF16), m_ref[gi]) + _dot_nt(pre_ref[gi].astype(BF16), ztre_ref[gi])
             + _dot_nt(pim_ref[gi].astype(BF16), ztim_ref[gi]) + d_ref[gi] * u)
        y = jax.nn.gelu(y)
        y = y * jax.nn.sigmoid(_dot(y.astype(BF16), glu_ref[gi]))
        for s in range(t):
            xs_ref[s, :, gi * c:(gi + 1) * c] = y[:, s * c:(s + 1) * c]
    for s in range(t):
        y_ref[pl.ds(s, rows, stride=t), :] = xs_ref[s]


def _ssm_mix(z, y_prev, tab, h0_re, h0_im, row0, n_rows, bsz, t):
    g, tc, _ = tab["m"].shape
    p = h0_re.shape[2]
    n_tok = z.shape[0]
    width = g * SSM_GROUP_CH
    gps = SSM_GROUPS_PER_STEP
    lanes = gps * SSM_GROUP_CH
    rows = n_rows // t
    blk = lambda *shape: pl.BlockSpec((gps,) + shape, lambda i: (i,) + (0,) * len(shape))
    tok = pl.BlockSpec((n_rows, lanes), lambda i: (row0 // n_rows, i))
    aliased = y_prev is not None
    in_specs = [tok, blk(tc, tc), blk(tc, p), blk(tc, p), blk(tc, p), blk(tc, p), blk(1, p), blk(1, p), blk(1, tc),
                blk(tc, tc), blk(bsz, p), blk(bsz, p)]
    args = [z, tab["m"], tab["w_re"], tab["w_im"], tab["zt_re"], tab["zt_im"], tab["a_re"], tab["a_im"], tab["d"],
            tab["glu"], h0_re, h0_im]
    if aliased:
        in_specs.append(pl.BlockSpec(memory_space=pl.ANY))
        args.append(y_prev)
    return pl.pallas_call(
        functools.partial(_ssm_mix_kernel, bsz=bsz, t=t, aliased=aliased),
        grid=(g // gps,),
        in_specs=in_specs,
        out_specs=[tok, blk(bsz, p), blk(bsz, p)],
        out_shape=[jax.ShapeDtypeStruct((n_tok, width), F32), jax.ShapeDtypeStruct((g, bsz, p), F32),
                   jax.ShapeDtypeStruct((g, bsz, p), F32)],
        scratch_shapes=[pltpu.VMEM((t, rows, lanes), F32), pltpu.VMEM((gps, rows, tc), F32)]
        + [pltpu.VMEM((gps, rows, p), F32)] * 4,
        input_output_aliases={len(args) - 1: 0} if aliased else {},
        compiler_params=_compiler_params("parallel"),
        name="ssm_mix",
    )(*args)


def _bias_table(q_pos, k_pos, n_heads):
    slopes = np.exp2(-8.0 * np.arange(1, n_heads + 1, dtype=np.float64) / n_heads)
    dist = q_pos[:, None] - k_pos[None, :]
    valid = (dist >= 0) & (dist < WINDOW) & (k_pos[None, :] >= 0)
    bias = -slopes[:, None, None] * dist[None].astype(np.float64)
    return np.where(valid[None], bias, -np.inf).astype(np.float32)


def _head_rms(x, gain):
    heads = [_rms(x[:, i:i + HEAD_DIM], gain) for i in range(0, x.shape[1], HEAD_DIM)]
    return jnp.concatenate(heads, axis=1)


def _softmax_sink_pv(logits, sink, v):
    m = jnp.maximum(jnp.max(logits, axis=-1, keepdims=True), sink)
    p = jnp.exp(logits - m)
    denom = jnp.sum(p, axis=-1, keepdims=True) + jnp.exp(sink - m)
    return _dot(p.astype(BF16), v) / denom


def _prompt_attn_kernel(sinks_ref, q_ref, k_ref, v_ref, qg_ref, kg_ref, bias_ref, y_ref, ks_ref, vs_ref,
                        kp_ref, vp_ref, *, n_heads):
    n = pl.program_id(1)

    @pl.when(n == 0)
    def _():
        kp_ref[...] = jnp.zeros_like(kp_ref)
        vp_ref[...] = jnp.zeros_like(vp_ref)

    q_all = q_ref[...]
    blk = q_all.shape[0]
    k_cur = _head_rms(k_ref[...], kg_ref[...])
    v_cur = v_ref[...]
    k2 = jnp.concatenate([kp_ref[...], k_cur], axis=0).astype(BF16)
    v2 = jnp.concatenate([vp_ref[...], v_cur], axis=0).astype(BF16)
    table = jnp.minimum(n, 1)
    from_prev = (lax.broadcasted_iota(jnp.int32, (blk, blk), 1) > lax.broadcasted_iota(jnp.int32, (blk, blk), 0))
    q_per_kv = n_heads // N_KV_HEADS
    outs = []
    for kv in range(N_KV_HEADS):
        heads = range(kv * q_per_kv, (kv + 1) * q_per_kv)
        kv_lanes = slice(kv * HEAD_DIM, (kv + 1) * HEAD_DIM)
        q = jnp.concatenate([_rms(q_all[:, h * HEAD_DIM:(h + 1) * HEAD_DIM], qg_ref[...]) * QK_SCALE for h in heads],
                            axis=0).astype(BF16)
        s = _dot_nt(q, k2[:, kv_lanes])
        probs, inv_denoms = [], []
        for i, h in enumerate(heads):
            s_h = s[i * blk:(i + 1) * blk]
            logits = jnp.where(from_prev, s_h[:, :blk], s_h[:, blk:]) + bias_ref[table, h]
            sink = sinks_ref[h]
            m = jnp.maximum(jnp.max(logits, axis=-1, keepdims=True), sink)
            p = jnp.exp(logits - m)
            inv_denoms.append(1.0 / (jnp.sum(p, axis=-1, keepdims=True) + jnp.exp(sink - m)))
            probs.append(jnp.concatenate([jnp.where(from_prev, p, 0.0), jnp.where(from_prev, 0.0, p)], axis=1))
        o = _dot(jnp.concatenate(probs, axis=0).astype(BF16), v2[:, kv_lanes])
        outs += [o[i * blk:(i + 1) * blk] * inv_denoms[i] for i in range(q_per_kv)]
    y_ref[...] = jnp.concatenate(outs, axis=1)
    kp_ref[...] = k_cur
    vp_ref[...] = v_cur

    @pl.when(n == pl.num_programs(1) - 1)
    def _():
        ks_ref[0] = k_cur
        vs_ref[0] = v_cur


def _prompt_attn(z, sinks, q_gain, k_gain, bsz, seq, n_tok, col_q, attn_w, kv_w):
    blk = WINDOW
    nb = seq // blk
    n_heads = attn_w // HEAD_DIM
    row = lambda b, n: b * nb + n
    q_blk, k_blk, v_blk = col_q // attn_w, (col_q + attn_w) // kv_w, (col_q + attn_w + kv_w) // kv_w
    qi, kj = np.arange(blk)[:, None], np.arange(blk)[None, :]
    first, later = [np.take_along_axis(_bias_table(first_pos + np.arange(blk), first_pos - blk + np.arange(2 * blk), n_heads),
                                       np.broadcast_to(np.where(kj > qi, kj, kj + blk), (n_heads, blk, blk)), axis=2)
                    for first_pos in (0, blk)]
    bias = np.stack([first, later])
    return pl.pallas_call(
        functools.partial(_prompt_attn_kernel, n_heads=n_heads),
        grid=(bsz, nb),
        in_specs=[
            pl.BlockSpec(memory_space=pltpu.SMEM),
            pl.BlockSpec((blk, attn_w), lambda b, n: (row(b, n), q_blk)),
            pl.BlockSpec((blk, kv_w), lambda b, n: (row(b, n), k_blk)),
            pl.BlockSpec((blk, kv_w), lambda b, n: (row(b, n), v_blk)),
            pl.BlockSpec((1, HEAD_DIM), lambda b, n: (0, 0)),
            pl.BlockSpec((1, HEAD_DIM), lambda b, n: (0, 0)),
            pl.BlockSpec(bias.shape, lambda b, n: (0, 0, 0, 0)),
        ],
        out_specs=[
            pl.BlockSpec((blk, attn_w), lambda b, n: (row(b, n), 0)),
            pl.BlockSpec((1, blk, kv_w), lambda b, n: (b, 0, 0)),
            pl.BlockSpec((1, blk, kv_w), lambda b, n: (b, 0, 0)),
        ],
        out_shape=[jax.ShapeDtypeStruct((n_tok, attn_w), F32),
                   jax.ShapeDtypeStruct((bsz, blk, kv_w), F32),
                   jax.ShapeDtypeStruct((bsz, blk, kv_w), F32)],
        scratch_shapes=[pltpu.VMEM((blk, kv_w), F32)] * 2,
        compiler_params=_compiler_params("parallel", "arbitrary"),
        name="prompt_attn",
    )(sinks, z, z, z, q_gain.reshape(1, HEAD_DIM), k_gain.reshape(1, HEAD_DIM), jnp.asarray(bias))


def _sample_attn_kernel(q_ref, k_ref, v_ref, ck_ref, cv_ref, qg_ref, kg_ref, bias_ref, sink_ref,
                        y_ref, ks_ref, vs_ref, kbuf_ref, vbuf_ref, *, dec_seq):
    cache_w = ck_ref.shape[1]
    k_new = _head_rms(k_ref[...], kg_ref[...])
    v_new = v_ref[...]
    pad = jnp.zeros((kbuf_ref.shape[1] - cache_w, kbuf_ref.shape[2]), F32)
    for b in range(q_ref.shape[0]):
        rows = slice(b * dec_seq, (b + 1) * dec_seq)
        kbuf_ref[b, 0:cache_w, :] = ck_ref[b]
        vbuf_ref[b, 0:cache_w, :] = cv_ref[b]
        kbuf_ref[b, cache_w:, :] = pad
        vbuf_ref[b, cache_w:, :] = pad
        kbuf_ref[b, cache_w:cache_w + dec_seq, :] = k_new[rows]
        vbuf_ref[b, cache_w:cache_w + dec_seq, :] = v_new[rows]
        q = q_ref[b]
        q = q * lax.rsqrt(jnp.sum(q * q, axis=-1, keepdims=True) * (1.0 / HEAD_DIM) + EPS) * qg_ref[...]
        logits = _dot_nt((q * QK_SCALE).astype(BF16), kbuf_ref[b].astype(BF16)) + bias_ref[...]
        y_ref[b] = _softmax_sink_pv(logits, sink_ref[...], vbuf_ref[b].astype(BF16))
        ks_ref[b, 0:cache_w - dec_seq, :] = ck_ref[b, dec_seq:cache_w, :]
        vs_ref[b, 0:cache_w - dec_seq, :] = cv_ref[b, dec_seq:cache_w, :]
        ks_ref[b, cache_w - dec_seq:cache_w, :] = k_new[rows]
        vs_ref[b, cache_w - dec_seq:cache_w, :] = v_new[rows]


def _sample_attn(z, cache_k, cache_v, sinks, q_gain, k_gain, row0, dec_seq, col_q, attn_w, kv_w):
    dec_batch, cache_w, _ = cache_k.shape
    n_heads = attn_w // HEAD_DIM
    q_per_kv = n_heads // N_KV_HEADS
    q_rows = q_per_kv * dec_seq
    tile = SAMPLE_BATCH_TILE
    r0 = row0 // (tile * dec_seq)
    k_blk, v_blk = (col_q + attn_w) // kv_w, (col_q + attn_w + kv_w) // kv_w
    keys = 2 * cache_w
    k_pos = np.full(keys, -1)
    k_pos[:cache_w] = PAST_LEN - cache_w + np.arange(cache_w)
    k_pos[cache_w:cache_w + dec_seq] = PAST_LEN + np.arange(dec_seq)
    bias = _bias_table(PAST_LEN + np.arange(dec_seq), k_pos, n_heads).reshape(n_heads * dec_seq, keys)
    sink_col = jnp.repeat(sinks, dec_seq).reshape(n_heads * dec_seq, 1)
    q = z[row0:, col_q:col_q + attn_w].reshape(dec_batch, dec_seq, N_KV_HEADS, q_per_kv, 1, HEAD_DIM)
    own_kv = np.eye(N_KV_HEADS, dtype=np.float32)[None, :, None, None, :, None]
    q = (q.transpose(0, 2, 3, 1, 4, 5) * own_kv).reshape(dec_batch, n_heads * dec_seq, kv_w)
    y, ks, vs = pl.pallas_call(
        functools.partial(_sample_attn_kernel, dec_seq=dec_seq),
        grid=(dec_batch // tile,),
        in_specs=[
            pl.BlockSpec((tile, n_heads * dec_seq, kv_w), lambda i: (i, 0, 0)),
            pl.BlockSpec((tile * dec_seq, kv_w), lambda i: (r0 + i, k_blk)),
            pl.BlockSpec((tile * dec_seq, kv_w), lambda i: (r0 + i, v_blk)),
            pl.BlockSpec((tile, cache_w, kv_w), lambda i: (i, 0, 0)),
            pl.BlockSpec((tile, cache_w, kv_w), lambda i: (i, 0, 0)),
            pl.BlockSpec((1, kv_w), lambda i: (0, 0)),
            pl.BlockSpec((1, HEAD_DIM), lambda i: (0, 0)),
            pl.BlockSpec(bias.shape, lambda i: (0, 0)),
            pl.BlockSpec(sink_col.shape, lambda i: (0, 0)),
        ],
        out_specs=[
            pl.BlockSpec((tile, n_heads * dec_seq, kv_w), lambda i: (i, 0, 0)),
            pl.BlockSpec((tile, cache_w, kv_w), lambda i: (i, 0, 0)),
            pl.BlockSpec((tile, cache_w, kv_w), lambda i: (i, 0, 0)),
        ],
        out_shape=[jax.ShapeDtypeStruct(q.shape, F32),
                   jax.ShapeDtypeStruct(cache_k.shape, F32),
                   jax.ShapeDtypeStruct(cache_v.shape, F32)],
        scratch_shapes=[pltpu.VMEM((tile, keys, kv_w), F32)] * 2,
        compiler_params=_compiler_params("parallel"),
        name="sample_attn",
    )(q, z, z, cache_k, cache_v, jnp.tile(q_gain, N_KV_HEADS).reshape(1, kv_w), k_gain.reshape(1, HEAD_DIM),
      jnp.asarray(bias), sink_col)
    y = y.reshape(dec_batch, N_KV_HEADS, q_per_kv, dec_seq, N_KV_HEADS, HEAD_DIM)
    y = jnp.stack([y[:, kv, :, :, kv] for kv in range(N_KV_HEADS)], axis=1)
    return y.transpose(0, 3, 1, 2, 4).reshape(dec_batch * dec_seq, attn_w), ks, vs


def kernel(x_prompt, x_sample, cache_k, cache_v, state_ssm_re, state_ssm_im, ffn1_norm, ffn1_w_gate, ffn1_w_up, ffn1_w_down, mix_norm, w_in, ssm_A_re, ssm_A_im, ssm_B_re, ssm_B_im, ssm_C_re, ssm_C_im, ssm_D, ssm_log_dt, ssm_w_glu, q_norm, k_norm, sinks, ssm_out_norm, attn_out_norm, w_out, ffn2_norm, ffn2_w_gate, ffn2_w_up, ffn2_w_down):
    bsz, seq, d = x_prompt.shape
    dec_batch, dec_seq, _ = x_sample.shape
    depth = w_in.shape[0]
    ssm_w = ssm_A_re.shape[1] * SSM_GROUP_CH
    attn_w = w_out.shape[1] - ssm_w
    kv_w = N_KV_HEADS * HEAD_DIM
    cache_w = cache_k.shape[2]
    n_prompt = bsz * seq
    n_tok = n_prompt + dec_batch * dec_seq

    x = jnp.concatenate([x_prompt.reshape(n_prompt, d), x_sample.reshape(dec_batch * dec_seq, d)], axis=0)
    zeros_state = jnp.zeros((ssm_w // SSM_GROUP_CH, bsz, SSM_STATE), F32)
    outs = [[] for _ in range(8)]
    bf = lambda w: w.astype(BF16)
    ffn1_w, ffn2_w = [tuple(map(bf, ws)) for ws in ((ffn1_w_gate, ffn1_w_up, ffn1_w_down),
                                                    (ffn2_w_gate, ffn2_w_up, ffn2_w_down))]
    w_in_bf, w_out_bf = bf(w_in), bf(w_out)
    for l in range(depth):
        x = _ffn(x, ffn1_norm[l], *ffn1_w, layer=l)
        z = _norm_matmul(x, mix_norm[l], w_in_bf, layer=l, tn=1280)

        prep = _ssm_prep(ssm_A_re[l], ssm_A_im[l], ssm_log_dt[l], ssm_B_re[l], ssm_B_im[l], ssm_C_re[l], ssm_C_im[l],
                         ssm_D[l], ssm_w_glu[l])
        tab_p = _ssm_tables(prep, PROMPT_CHUNK)
        tab_s = _ssm_tables(prep, dec_seq)
        grp_major = lambda a: a.transpose(1, 0, 2)
        y_ssm, hp_re, hp_im = _ssm_mix(z, None, tab_p, zeros_state, zeros_state, 0, n_prompt, bsz, PROMPT_CHUNK)
        y_ssm, hs_re, hs_im = _ssm_mix(z, y_ssm, tab_s, grp_major(state_ssm_re[l]), grp_major(state_ssm_im[l]),
                                       n_prompt, dec_batch * dec_seq, dec_batch, dec_seq)
        hp_re, hp_im, hs_re, hs_im = map(grp_major, (hp_re, hp_im, hs_re, hs_im))

        y_attn, pk, pv = _prompt_attn(z, sinks[l], q_norm[l], k_norm[l], bsz, seq, n_tok, ssm_w, attn_w, kv_w)
        ya_s, sk, sv = _sample_attn(z, cache_k[l].reshape(dec_batch, cache_w, kv_w),
                                    cache_v[l].reshape(dec_batch, cache_w, kv_w), sinks[l], q_norm[l], k_norm[l],
                                    n_prompt, dec_seq, ssm_w, attn_w, kv_w)
        y_attn = lax.dynamic_update_slice(y_attn, ya_s, (n_prompt, 0))

        x = _out_proj(x, y_ssm, y_attn, ssm_out_norm[l], attn_out_norm[l], w_out_bf, layer=l)
        x = _ffn(x, ffn2_norm[l], *ffn2_w, layer=l)

        kv_shape = (-1, N_KV_HEADS, HEAD_DIM)
        for dst, val in zip(outs, (pk.reshape(bsz, *kv_shape), pv.reshape(bsz, *kv_shape), hp_re, hp_im,
                                   sk.reshape(dec_batch, *kv_shape), sv.reshape(dec_batch, *kv_shape), hs_re, hs_im)):
            dst.append(val)
    y_prompt = x[:n_prompt].reshape(bsz, seq, d)
    y_sample = x[n_prompt:].reshape(dec_batch, dec_seq, d)
    return (y_prompt, y_sample) + tuple(jnp.stack(o) for o in outs)
```

```python
import functools

import jax
import jax.numpy as jnp
import numpy as np
from jax import lax
from jax.experimental import pallas as pl
from jax.experimental.pallas import tpu as pltpu

F32 = jnp.float32
BF16 = jnp.bfloat16

EPS = 1e-6
SSM_GROUP_CH = 16
SSM_STATE = 64
HEAD_DIM = 64
N_KV_HEADS = 4
WINDOW = 128
PAST_LEN = 8192
QK_SCALE = HEAD_DIM ** -0.5
PROMPT_CHUNK = 16

TOKEN_TILE = 512
FFN_TOKEN_TILE = 1088
FFN_VMEM_LIMIT_BYTES = 60 * 1024 * 1024
FF_TILE = 512
SAMPLE_BATCH_TILE = 8
VMEM_LIMIT_BYTES = 48 * 1024 * 1024

HIGHEST = lax.Precision.HIGHEST
SSM_STATE_PRECISION = HIGHEST
SSM_GROUPS_PER_STEP = 8


def _rms(x, gain):
    return x * lax.rsqrt(jnp.mean(x * x, axis=-1, keepdims=True) + EPS) * gain


def _dot(a, b, precision=None):
    return jnp.dot(a, b, preferred_element_type=F32, precision=precision)


def _dot_nt(a, b, precision=None):
    return lax.dot_general(a, b, (((1,), (1,)), ((), ())), preferred_element_type=F32, precision=precision)


def _compiler_params(*semantics):
    return pltpu.CompilerParams(dimension_semantics=semantics, vmem_limit_bytes=VMEM_LIMIT_BYTES)


def _group_spec(*shape):
    return pl.BlockSpec((1,) + shape, lambda i: (i,) + (0,) * len(shape))


def _ffn_kernel(x_ref, g_ref, wg_ref, wu_ref, wd_ref, o_ref, xn_ref):
    j = pl.program_id(1)

    @pl.when(j == 0)
    def _():
        xn_ref[...] = _rms(x_ref[...], g_ref[...]).astype(BF16)
        o_ref[...] = jnp.zeros_like(o_ref)

    xn = xn_ref[...]
    gate = _dot(xn, wg_ref[...])
    up = _dot(xn, wu_ref[...])
    h = (gate * jax.nn.sigmoid(gate) * up).astype(BF16)
    o_ref[...] += _dot(h, wd_ref[...])

    @pl.when(j == pl.num_programs(1) - 1)
    def _():
        o_ref[...] = x_ref[...] + 0.5 * o_ref[...]


def _ffn(x, gain, w_gate, w_up, w_down, layer):
    n_tok, d = x.shape
    d_ff = w_gate.shape[2]
    tm = FFN_TOKEN_TILE if n_tok % FFN_TOKEN_TILE == 0 else TOKEN_TILE
    tf = FF_TILE
    return pl.pallas_call(
        _ffn_kernel,
        grid=(n_tok // tm, d_ff // tf),
        in_specs=[
            pl.BlockSpec((tm, d), lambda i, j: (i, 0)),
            pl.BlockSpec((1, d), lambda i, j: (0, 0)),
            pl.BlockSpec((None, d, tf), lambda i, j: (layer, 0, j)),
            pl.BlockSpec((None, d, tf), lambda i, j: (layer, 0, j)),
            pl.BlockSpec((None, tf, d), lambda i, j: (layer, j, 0)),
        ],
        out_specs=pl.BlockSpec((tm, d), lambda i, j: (i, 0)),
        out_shape=jax.ShapeDtypeStruct((n_tok, d), F32),
        scratch_shapes=[pltpu.VMEM((tm, d), BF16)],
        compiler_params=pltpu.CompilerParams(dimension_semantics=("parallel", "arbitrary"),
                                             vmem_limit_bytes=FFN_VMEM_LIMIT_BYTES),
        name="ffn",
    )(x, gain.reshape(1, d), w_gate, w_up, w_down)


def _norm_matmul_kernel(x_ref, g_ref, w_ref, o_ref, xn_ref):
    @pl.when(pl.program_id(1) == 0)
    def _():
        xn_ref[...] = _rms(x_ref[...], g_ref[...]).astype(BF16)

    o_ref[...] = _dot(xn_ref[...], w_ref[...])


def _norm_matmul(x, gain, w, layer, tn):
    n_tok, d = x.shape
    n_out = w.shape[2]
    tm = TOKEN_TILE
    return pl.pallas_call(
        _norm_matmul_kernel,
        grid=(n_tok // tm, n_out // tn),
        in_specs=[
            pl.BlockSpec((tm, d), lambda i, j: (i, 0)),
            pl.BlockSpec((1, d), lambda i, j: (0, 0)),
            pl.BlockSpec((None, d, tn), lambda i, j: (layer, 0, j)),
        ],
        out_specs=pl.BlockSpec((tm, tn), lambda i, j: (i, j)),
        out_shape=jax.ShapeDtypeStruct((n_tok, n_out), F32),
        scratch_shapes=[pltpu.VMEM((tm, d), BF16)],
        compiler_params=_compiler_params("parallel", "arbitrary"),
        name="in_proj",
    )(x, gain.reshape(1, d), w)


def _out_proj_kernel(x_ref, ys_ref, ya_ref, gs_ref, ga_ref, ws_ref, wa_ref, o_ref):
    ys = _rms(ys_ref[...], gs_ref[...]).astype(BF16)
    ya = _rms(ya_ref[...], ga_ref[...]).astype(BF16)
    o_ref[...] = x_ref[...] + _dot(ys, ws_ref[...]) + _dot(ya, wa_ref[...])


def _out_proj(x, y_ssm, y_attn, g_ssm, g_attn, w_out, layer):
    n_tok, d = x.shape
    ssm_w, attn_w = y_ssm.shape[1], y_attn.shape[1]
    tm = TOKEN_TILE
    return pl.pallas_call(
        _out_proj_kernel,
        grid=(n_tok // tm,),
        in_specs=[
            pl.BlockSpec((tm, d), lambda i: (i, 0)),
            pl.BlockSpec((tm, ssm_w), lambda i: (i, 0)),
            pl.BlockSpec((tm, attn_w), lambda i: (i, 0)),
            pl.BlockSpec((1, ssm_w), lambda i: (0, 0)),
            pl.BlockSpec((1, attn_w), lambda i: (0, 0)),
            pl.BlockSpec((None, ssm_w, d), lambda i: (layer, 0, 0)),
            pl.BlockSpec((None, attn_w, d), lambda i: (layer, ssm_w // attn_w, 0)),
        ],
        out_specs=pl.BlockSpec((tm, d), lambda i: (i, 0)),
        out_shape=jax.ShapeDtypeStruct((n_tok, d), F32),
        compiler_params=_compiler_params("parallel"),
        name="out_proj",
    )(x, y_ssm, y_attn, g_ssm.reshape(1, ssm_w), g_attn.reshape(1, attn_w), w_out, w_out)


def _ssm_prep_kernel(are_ref, aim_ref, ldt_ref, btre_ref, btim_ref, cre_ref, cim_ref, d_ref, wglu_ref,
                     m_ref, wre_ref, wim_ref, ztre_ref, ztim_ref, glu_ref, dt_ref, powre_ref, powim_ref):
    a_re, a_im = are_ref[0], aim_ref[0]
    dt = jnp.exp(ldt_ref[0])
    mag = jnp.exp(a_re * dt)
    lb_re, lb_im = mag * jnp.cos(a_im * dt), mag * jnp.sin(a_im * dt)
    den = a_re * a_re + a_im * a_im
    q_re = ((lb_re - 1.0) * a_re + lb_im * a_im) / den
    q_im = (lb_im * a_re - (lb_re - 1.0) * a_im) / den
    bt_re, bt_im = btre_ref[0], btim_ref[0]
    bb_re = q_re * bt_re - q_im * bt_im
    bb_im = q_re * bt_im + q_im * bt_re
    c_re, c_im = cre_ref[0], cim_ref[0]

    pw_re, pw_im = [jnp.ones_like(lb_re)], [jnp.zeros_like(lb_im)]
    for _ in range(PROMPT_CHUNK):
        pr, pi = pw_re[-1], pw_im[-1]
        pw_re.append(pr * lb_re - pi * lb_im)
        pw_im.append(pr * lb_im + pi * lb_re)

    steps = PROMPT_CHUNK
    c = c_re.shape[0]
    w_re = jnp.concatenate([bb_re * pw_re[steps - 1 - s] - bb_im * pw_im[steps - 1 - s] for s in range(steps)], axis=0)
    w_im = jnp.concatenate([bb_re * pw_im[steps - 1 - s] + bb_im * pw_re[steps - 1 - s] for s in range(steps)], axis=0)
    wre_ref[0] = w_re
    wim_ref[0] = w_im
    ztre_ref[0] = jnp.concatenate([c_re * pw_re[t + 1] - c_im * pw_im[t + 1] for t in range(steps)], axis=0).astype(BF16)
    ztim_ref[0] = jnp.concatenate([-(c_re * pw_im[t + 1] + c_im * pw_re[t + 1]) for t in range(steps)], axis=0).astype(BF16)

    lag = _dot_nt(w_re, c_re, HIGHEST) - _dot_nt(w_im, c_im, HIGHEST)
    cols = []
    for t in range(steps):
        live = lag[(steps - 1 - t) * c:, :]
        cols.append(live if t == steps - 1 else
                    jnp.concatenate([live, jnp.zeros(((steps - 1 - t) * c, c), F32)], axis=0))
    m_ref[0] = jnp.concatenate(cols, axis=1).astype(BF16)

    tiled = jnp.concatenate([jnp.concatenate([wglu_ref[0]] * steps, axis=0)] * steps, axis=1)
    shift = c.bit_length() - 1
    row_step = lax.shift_right_logical(lax.broadcasted_iota(jnp.int32, tiled.shape, 0), shift)
    col_step = lax.shift_right_logical(lax.broadcasted_iota(jnp.int32, tiled.shape, 1), shift)
    glu_ref[0] = jnp.where(row_step == col_step, tiled, 0.0).astype(BF16)
    dt_ref[0] = jnp.concatenate([d_ref[0]] * steps, axis=1)

    pad = jnp.zeros((powre_ref.shape[1] - (steps + 1), a_re.shape[1]), F32)
    powre_ref[0] = jnp.concatenate(pw_re + [pad], axis=0)
    powim_ref[0] = jnp.concatenate(pw_im + [pad], axis=0)


def _ssm_prep(a_re, a_im, log_dt, b_re, b_im, c_re, c_im, d_skip, w_glu):
    g, p = a_re.shape
    c = c_re.shape[1]
    assert c & (c - 1) == 0
    tc = PROMPT_CHUNK * c
    pow_rows = 24
    sds = lambda dtype, *shape: jax.ShapeDtypeStruct((g,) + shape, dtype)
    gs = _group_spec
    names = ("m", "w_re", "w_im", "zt_re", "zt_im", "glu", "d", "pow_re", "pow_im")
    outs = pl.pallas_call(
        _ssm_prep_kernel,
        grid=(g,),
        in_specs=[gs(1, p), gs(1, p), gs(1, 1), gs(c, p), gs(c, p), gs(c, p), gs(c, p), gs(1, c), gs(c, c)],
        out_specs=[gs(tc, tc), gs(tc, p), gs(tc, p), gs(tc, p), gs(tc, p), gs(tc, tc), gs(1, tc),
                   gs(pow_rows, p), gs(pow_rows, p)],
        out_shape=[sds(BF16, tc, tc), sds(F32, tc, p), sds(F32, tc, p), sds(BF16, tc, p), sds(BF16, tc, p),
                   sds(BF16, tc, tc), sds(F32, 1, tc), sds(F32, pow_rows, p), sds(F32, pow_rows, p)],
        compiler_params=_compiler_params("parallel"),
        name="ssm_prep",
    )(a_re.reshape(g, 1, p), a_im.reshape(g, 1, p), log_dt.reshape(g, 1, 1),
      jnp.swapaxes(b_re, 1, 2), jnp.swapaxes(b_im, 1, 2), c_re, c_im, d_skip.reshape(g, 1, c), w_glu)
    return dict(zip(names, outs))


def _ssm_tables(prep, t):
    n = t * SSM_GROUP_CH
    full = prep["m"].shape[1]
    return dict(m=prep["m"][:, :n, :n], w_re=prep["w_re"][:, full - n:], w_im=prep["w_im"][:, full - n:],
                zt_re=prep["zt_re"][:, :n], zt_im=prep["zt_im"][:, :n], glu=prep["glu"][:, :n, :n],
                d=prep["d"][:, :, :n], a_re=prep["pow_re"][:, t:t + 1], a_im=prep["pow_im"][:, t:t + 1])


def _ssm_mix_kernel(*refs, bsz, t, aliased):
    (u_ref, m_ref, wre_ref, wim_ref, ztre_ref, ztim_ref, are_ref, aim_ref, d_ref, glu_ref, h0re_ref, h0im_ref) = refs[:12]
    y_ref, hre_ref, him_ref, xs_ref, ug_ref, vre_ref, vim_ref, pre_ref, pim_ref = refs[12 + aliased:]
    rows = u_ref.shape[0] // t
    chunks = rows // bsz
    c = SSM_GROUP_CH
    n_grp = m_ref.shape[0]
    for s in range(t):
        xs_ref[s] = u_ref[pl.ds(s, rows, stride=t), :]
    for gi in range(n_grp):
        lanes = slice(gi * c, (gi + 1) * c)
        u = jnp.concatenate([xs_ref[s, :, lanes] for s in range(t)], axis=1)
        ug_ref[gi] = u
        vre_ref[gi] = _dot(u, wre_ref[gi], SSM_STATE_PRECISION)
        vim_ref[gi] = _dot(u, wim_ref[gi], SSM_STATE_PRECISION)

    def chunk_step(k, h):
        sel = pl.ds(k, bsz, stride=chunks)
        nxt = []
        for gi in range(n_grp):
            h_re, h_im = h[2 * gi], h[2 * gi + 1]
            pre_ref[gi, sel, :] = h_re
            pim_ref[gi, sel, :] = h_im
            a_re, a_im = are_ref[gi], aim_ref[gi]
            nxt.append(a_re * h_re - a_im * h_im + vre_ref[gi, sel, :])
            nxt.append(a_re * h_im + a_im * h_re + vim_ref[gi, sel, :])
        return tuple(nxt)

    h = tuple(r[gi] for gi in range(n_grp) for r in (h0re_ref, h0im_ref))
    h = lax.fori_loop(0, chunks, chunk_step, h)
    for gi in range(n_grp):
        hre_ref[gi] = h[2 * gi]
        him_ref[gi] = h[2 * gi + 1]

    for gi in range(n_grp):
        u = ug_ref[gi]
        y = (_dot(u.astype(BF16), m_ref[gi]) + _dot_nt(pre_ref[gi].astype(BF16), ztre_ref[gi])
             + _dot_nt(pim_ref[gi].astype(BF16), ztim_ref[gi]) + d_ref[gi] * u)
        y = jax.nn.gelu(y)
        y = y * jax.nn.sigmoid(_dot(y.astype(BF16), glu_ref[gi]))
        for s in range(t):
            xs_ref[s, :, gi * c:(gi + 1) * c] = y[:, s * c:(s + 1) * c]
    for s in range(t):
        y_ref[pl.ds(s, rows, stride=t), :] = xs_ref[s]


def _ssm_mix(z, y_prev, tab, h0_re, h0_im, row0, n_rows, bsz, t):
    g, tc, _ = tab["m"].shape
    p = h0_re.shape[2]
    n_tok = z.shape[0]
    width = g * SSM_GROUP_CH
    gps = SSM_GROUPS_PER_STEP
    lanes = gps * SSM_GROUP_CH
    rows = n_rows // t
    blk = lambda *shape: pl.BlockSpec((gps,) + shape, lambda i: (i,) + (0,) * len(shape))
    tok = pl.BlockSpec((n_rows, lanes), lambda i: (row0 // n_rows, i))
    aliased = y_prev is not None
    in_specs = [tok, blk(tc, tc), blk(tc, p), blk(tc, p), blk(tc, p), blk(tc, p), blk(1, p), blk(1, p), blk(1, tc),
                blk(tc, tc), blk(bsz, p), blk(bsz, p)]
    args = [z, tab["m"], tab["w_re"], tab["w_im"], tab["zt_re"], tab["zt_im"], tab["a_re"], tab["a_im"], tab["d"],
            tab["glu"], h0_re, h0_im]
    if aliased:
        in_specs.append(pl.BlockSpec(memory_space=pl.ANY))
        args.append(y_prev)
    return pl.pallas_call(
        functools.partial(_ssm_mix_kernel, bsz=bsz, t=t, aliased=aliased),
        grid=(g // gps,),
        in_specs=in_specs,
        out_specs=[tok, blk(bsz, p), blk(bsz, p)],
        out_shape=[jax.ShapeDtypeStruct((n_tok, width), F32), jax.ShapeDtypeStruct((g, bsz, p), F32),
                   jax.ShapeDtypeStruct((g, bsz, p), F32)],
        scratch_shapes=[pltpu.VMEM((t, rows, lanes), F32), pltpu.VMEM((gps, rows, tc), F32)]
        + [pltpu.VMEM((gps, rows, p), F32)] * 4,
        input_output_aliases={len(args) - 1: 0} if aliased else {},
        compiler_params=_compiler_params("parallel"),
        name="ssm_mix",
    )(*args)


def _bias_table(q_pos, k_pos, n_heads):
    slopes = np.exp2(-8.0 * np.arange(1, n_heads + 1, dtype=np.float64) / n_heads)
    dist = q_pos[:, None] - k_pos[None, :]
    valid = (dist >= 0) & (dist < WINDOW) & (k_pos[None, :] >= 0)
    bias = -slopes[:, None, None] * dist[None].astype(np.float64)
    return np.where(valid[None], bias, -np.inf).astype(np.float32)


def _head_rms(x, gain):
    heads = [_rms(x[:, i:i + HEAD_DIM], gain) for i in range(0, x.shape[1], HEAD_DIM)]
    return jnp.concatenate(heads, axis=1)


def _softmax_sink_pv(logits, sink, v):
    m = jnp.maximum(jnp.max(logits, axis=-1, keepdims=True), sink)
    p = jnp.exp(logits - m)
    denom = jnp.sum(p, axis=-1, keepdims=True) + jnp.exp(sink - m)
    return _dot(p.astype(BF16), v) / denom


def _prompt_attn_kernel(sinks_ref, q_ref, k_ref, v_ref, qg_ref, kg_ref, bias_ref, y_ref, ks_ref, vs_ref,
                        kp_ref, vp_ref, *, n_heads):
    n = pl.program_id(1)

    @pl.when(n == 0)
    def _():
        kp_ref[...] = jnp.zeros_like(kp_ref)
        vp_ref[...] = jnp.zeros_like(vp_ref)

    q_all = q_ref[...]
    blk = q_all.shape[0]
    half = 2 * HEAD_DIM
    assert n_heads // N_KV_HEADS == 4 and half == 128
    k_cur = _head_rms(k_ref[...], kg_ref[...])
    v_cur = v_ref[...]
    k2 = jnp.concatenate([kp_ref[...], k_cur], axis=0)
    v2t = jnp.concatenate([vp_ref[...], v_cur], axis=0).T.astype(BF16)
    qs = (q_all * qg_ref[...] * QK_SCALE).astype(BF16)
    qsq = q_all * q_all
    table = jnp.minimum(n, 1)
    from_prev = (lax.broadcasted_iota(jnp.int32, (blk, blk), 0) > lax.broadcasted_iota(jnp.int32, (blk, blk), 1))
    low_lanes = lax.broadcasted_iota(jnp.int32, (2 * blk, half), 1) < HEAD_DIM
    low8 = lax.broadcasted_iota(jnp.int32, (8, half), 1) < HEAD_DIM
    pick = jnp.concatenate([jnp.where(low8, 1.0, 0.0), jnp.where(low8, 0.0, 1.0)], axis=0)
    for kv in range(N_KV_HEADS):
        k_tile = k2[:, (kv // 2) * half:(kv // 2 + 1) * half]
        k_own = jnp.where(low_lanes if kv % 2 == 0 else ~low_lanes, k_tile, 0.0)
        k_swap = pltpu.roll(k_own, HEAD_DIM, axis=1)
        k_lhs = jnp.concatenate([k_own, k_swap] if kv % 2 == 0 else [k_swap, k_own], axis=0).astype(BF16)
        tiles = slice(2 * kv * half, (2 * kv + 1) * half), slice((2 * kv + 1) * half, (2 * kv + 2) * half)
        q_rows = jnp.concatenate([qs[:, t] for t in tiles], axis=0)
        qsq_rows = jnp.concatenate([qsq[:, t] for t in tiles], axis=0)
        s = _dot_nt(k_lhs, q_rows)
        ssq = _dot_nt(pick, qsq_rows, HIGHEST)
        probs, inv_denoms = [], []
        for i in range(4):
            odd, tile = i % 2, i // 2
            lanes = slice(tile * blk, (tile + 1) * blk)
            s_h = s[odd * 2 * blk:(odd + 1) * 2 * blk, lanes]
            q_rms = lax.rsqrt(ssq[odd * 8:odd * 8 + 1, lanes] * (1.0 / HEAD_DIM) + EPS)
            logits = jnp.where(from_prev, s_h[:blk], s_h[blk:]) * q_rms + bias_ref[table, 4 * kv + i]
            sink = sinks_ref[4 * kv + i]
            m = jnp.maximum(jnp.max(logits, axis=0, keepdims=True), sink)
            p = jnp.exp(logits - m)
            inv_denoms.append(1.0 / (jnp.sum(p, axis=0, keepdims=True) + jnp.exp(sink - m)))
            probs.append(jnp.concatenate([jnp.where(from_prev, p, 0.0), jnp.where(from_prev, 0.0, p)], axis=0))
        o = _dot(v2t[kv * HEAD_DIM:(kv + 1) * HEAD_DIM], jnp.concatenate(probs, axis=1).astype(BF16))
        for tile in range(2):
            pair = [o[:, i * blk:(i + 1) * blk] * inv_denoms[i] for i in (2 * tile, 2 * tile + 1)]
            y_ref[:, tiles[tile]] = jnp.concatenate(pair, axis=0).T
    kp_ref[...] = k_cur
    vp_ref[...] = v_cur

    @pl.when(n == pl.num_programs(1) - 1)
    def _():
        ks_ref[0] = k_cur
        vs_ref[0] = v_cur


def _prompt_attn(z, sinks, q_gain, k_gain, bsz, seq, n_tok, col_q, attn_w, kv_w):
    blk = WINDOW
    nb = seq // blk
    n_heads = attn_w // HEAD_DIM
    row = lambda b, n: b * nb + n
    q_blk, k_blk, v_blk = col_q // attn_w, (col_q + attn_w) // kv_w, (col_q + attn_w + kv_w) // kv_w
    qi, kj = np.arange(blk)[:, None], np.arange(blk)[None, :]
    first, later = [np.take_along_axis(_bias_table(first_pos + np.arange(blk), first_pos - blk + np.arange(2 * blk), n_heads),
                                       np.broadcast_to(np.where(kj > qi, kj, kj + blk), (n_heads, blk, blk)), axis=2)
                    for first_pos in (0, blk)]
    bias = np.stack([first, later]).swapaxes(2, 3)
    return pl.pallas_call(
        functools.partial(_prompt_attn_kernel, n_heads=n_heads),
        grid=(bsz, nb),
        in_specs=[
            pl.BlockSpec(memory_space=pltpu.SMEM),
            pl.BlockSpec((blk, attn_w), lambda b, n: (row(b, n), q_blk)),
            pl.BlockSpec((blk, kv_w), lambda b, n: (row(b, n), k_blk)),
            pl.BlockSpec((blk, kv_w), lambda b, n: (row(b, n), v_blk)),
            pl.BlockSpec((1, attn_w), lambda b, n: (0, 0)),
            pl.BlockSpec((1, HEAD_DIM), lambda b, n: (0, 0)),
            pl.BlockSpec(bias.shape, lambda b, n: (0, 0, 0, 0)),
        ],
        out_specs=[
            pl.BlockSpec((blk, attn_w), lambda b, n: (row(b, n), 0)),
            pl.BlockSpec((1, blk, kv_w), lambda b, n: (b, 0, 0)),
            pl.BlockSpec((1, blk, kv_w), lambda b, n: (b, 0, 0)),
        ],
        out_shape=[jax.ShapeDtypeStruct((n_tok, attn_w), F32),
                   jax.ShapeDtypeStruct((bsz, blk, kv_w), F32),
                   jax.ShapeDtypeStruct((bsz, blk, kv_w), F32)],
        scratch_shapes=[pltpu.VMEM((blk, kv_w), F32)] * 2,
        compiler_params=_compiler_params("parallel", "arbitrary"),
        name="prompt_attn",
    )(sinks, z, z, z, jnp.tile(q_gain, n_heads).reshape(1, attn_w), k_gain.reshape(1, HEAD_DIM), jnp.asarray(bias))


def _sample_attn_kernel(q_ref, k_ref, v_ref, ck_ref, cv_ref, qg_ref, kg_ref, bias_ref, sink_ref,
                        y_ref, ks_ref, vs_ref, kbuf_ref, vbuf_ref, *, dec_seq):
    cache_w = ck_ref.shape[1]
    k_new = _head_rms(k_ref[...], kg_ref[...])
    v_new = v_ref[...]
    pad = jnp.zeros((kbuf_ref.shape[1] - cache_w, kbuf_ref.shape[2]), F32)
    for b in range(q_ref.shape[0]):
        rows = slice(b * dec_seq, (b + 1) * dec_seq)
        kbuf_ref[b, 0:cache_w, :] = ck_ref[b]
        vbuf_ref[b, 0:cache_w, :] = cv_ref[b]
        kbuf_ref[b, cache_w:, :] = pad
        vbuf_ref[b, cache_w:, :] = pad
        kbuf_ref[b, cache_w:cache_w + dec_seq, :] = k_new[rows]
        vbuf_ref[b, cache_w:cache_w + dec_seq, :] = v_new[rows]
        q = q_ref[b]
        q = q * lax.rsqrt(jnp.sum(q * q, axis=-1, keepdims=True) * (1.0 / HEAD_DIM) + EPS) * qg_ref[...]
        logits = _dot_nt((q * QK_SCALE).astype(BF16), kbuf_ref[b].astype(BF16)) + bias_ref[...]
        y_ref[b] = _softmax_sink_pv(logits, sink_ref[...], vbuf_ref[b].astype(BF16))
        ks_ref[b, 0:cache_w - dec_seq, :] = ck_ref[b, dec_seq:cache_w, :]
        vs_ref[b, 0:cache_w - dec_seq, :] = cv_ref[b, dec_seq:cache_w, :]
        ks_ref[b, cache_w - dec_seq:cache_w, :] = k_new[rows]
        vs_ref[b, cache_w - dec_seq:cache_w, :] = v_new[rows]


def _sample_attn(z, cache_k, cache_v, sinks, q_gain, k_gain, row0, dec_seq, col_q, attn_w, kv_w):
    dec_batch, cache_w, _ = cache_k.shape
    n_heads = attn_w // HEAD_DIM
    q_per_kv = n_heads // N_KV_HEADS
    q_rows = q_per_kv * dec_seq
    tile = SAMPLE_BATCH_TILE
    r0 = row0 // (tile * dec_seq)
    k_blk, v_blk = (col_q + attn_w) // kv_w, (col_q + attn_w + kv_w) // kv_w
    keys = 2 * cache_w
    k_pos = np.full(keys, -1)
    k_pos[:cache_w] = PAST_LEN - cache_w + np.arange(cache_w)
    k_pos[cache_w:cache_w + dec_seq] = PAST_LEN + np.arange(dec_seq)
    bias = _bias_table(PAST_LEN + np.arange(dec_seq), k_pos, n_heads).reshape(n_heads * dec_seq, keys)
    sink_col = jnp.repeat(sinks, dec_seq).reshape(n_heads * dec_seq, 1)
    q = z[row0:, col_q:col_q + attn_w].reshape(dec_batch, dec_seq, N_KV_HEADS, q_per_kv, 1, HEAD_DIM)
    own_kv = np.eye(N_KV_HEADS, dtype=np.float32)[None, :, None, None, :, None]
    q = (q.transpose(0, 2, 3, 1, 4, 5) * own_kv).reshape(dec_batch, n_heads * dec_seq, kv_w)
    y, ks, vs = pl.pallas_call(
        functools.partial(_sample_attn_kernel, dec_seq=dec_seq),
        grid=(dec_batch // tile,),
        in_specs=[
            pl.BlockSpec((tile, n_heads * dec_seq, kv_w), lambda i: (i, 0, 0)),
            pl.BlockSpec((tile * dec_seq, kv_w), lambda i: (r0 + i, k_blk)),
            pl.BlockSpec((tile * dec_seq, kv_w), lambda i: (r0 + i, v_blk)),
            pl.BlockSpec((tile, cache_w, kv_w), lambda i: (i, 0, 0)),
            pl.BlockSpec((tile, cache_w, kv_w), lambda i: (i, 0, 0)),
            pl.BlockSpec((1, kv_w), lambda i: (0, 0)),
            pl.BlockSpec((1, HEAD_DIM), lambda i: (0, 0)),
            pl.BlockSpec(bias.shape, lambda i: (0, 0)),
            pl.BlockSpec(sink_col.shape, lambda i: (0, 0)),
        ],
        out_specs=[
            pl.BlockSpec((tile, n_heads * dec_seq, kv_w), lambda i: (i, 0, 0)),
            pl.BlockSpec((tile, cache_w, kv_w), lambda i: (i, 0, 0)),
            pl.BlockSpec((tile, cache_w, kv_w), lambda i: (i, 0, 0)),
        ],
        out_shape=[jax.ShapeDtypeStruct(q.shape, F32),
                   jax.ShapeDtypeStruct(cache_k.shape, F32),
                   jax.ShapeDtypeStruct(cache_v.shape, F32)],
        scratch_shapes=[pltpu.VMEM((tile, keys, kv_w), F32)] * 2,
        compiler_params=_compiler_params("parallel"),
        name="sample_attn",
    )(q, z, z, cache_k, cache_v, jnp.tile(q_gain, N_KV_HEADS).reshape(1, kv_w), k_gain.reshape(1, HEAD_DIM),
      jnp.asarray(bias), sink_col)
    y = y.reshape(dec_batch, N_KV_HEADS, q_per_kv, dec_seq, N_KV_HEADS, HEAD_DIM)
    y = jnp.stack([y[:, kv, :, :, kv] for kv in range(N_KV_HEADS)], axis=1)
    return y.transpose(0, 3, 1, 2, 4).reshape(dec_batch * dec_seq, attn_w), ks, vs


def kernel(x_prompt, x_sample, cache_k, cache_v, state_ssm_re, state_ssm_im, ffn1_norm, ffn1_w_gate, ffn1_w_up, ffn1_w_down, mix_norm, w_in, ssm_A_re, ssm_A_im, ssm_B_re, ssm_B_im, ssm_C_re, ssm_C_im, ssm_D, ssm_log_dt, ssm_w_glu, q_norm, k_norm, sinks, ssm_out_norm, attn_out_norm, w_out, ffn2_norm, ffn2_w_gate, ffn2_w_up, ffn2_w_down):
    bsz, seq, d = x_prompt.shape
    dec_batch, dec_seq, _ = x_sample.shape
    depth = w_in.shape[0]
    ssm_w = ssm_A_re.shape[1] * SSM_GROUP_CH
    attn_w = w_out.shape[1] - ssm_w
    kv_w = N_KV_HEADS * HEAD_DIM
    cache_w = cache_k.shape[2]
    n_prompt = bsz * seq
    n_tok = n_prompt + dec_batch * dec_seq

    x = jnp.concatenate([x_prompt.reshape(n_prompt, d), x_sample.reshape(dec_batch * dec_seq, d)], axis=0)
    zeros_state = jnp.zeros((ssm_w // SSM_GROUP_CH, bsz, SSM_STATE), F32)
    outs = [[] for _ in range(8)]
    bf = lambda w: w.astype(BF16)
    ffn1_w, ffn2_w = [tuple(map(bf, ws)) for ws in ((ffn1_w_gate, ffn1_w_up, ffn1_w_down),
                                                    (ffn2_w_gate, ffn2_w_up, ffn2_w_down))]
    w_in_bf, w_out_bf = bf(w_in), bf(w_out)
    for l in range(depth):
        x = _ffn(x, ffn1_norm[l], *ffn1_w, layer=l)
        z = _norm_matmul(x, mix_norm[l], w_in_bf, layer=l, tn=w_in.shape[2])

        prep = _ssm_prep(ssm_A_re[l], ssm_A_im[l], ssm_log_dt[l], ssm_B_re[l], ssm_B_im[l], ssm_C_re[l], ssm_C_im[l],
                         ssm_D[l], ssm_w_glu[l])
        tab_p = _ssm_tables(prep, PROMPT_CHUNK)
        tab_s = _ssm_tables(prep, dec_seq)
        grp_major = lambda a: a.transpose(1, 0, 2)
        y_ssm, hp_re, hp_im = _ssm_mix(z, None, tab_p, zeros_state, zeros_state, 0, n_prompt, bsz, PROMPT_CHUNK)
        y_ssm, hs_re, hs_im = _ssm_mix(z, y_ssm, tab_s, grp_major(state_ssm_re[l]), grp_major(state_ssm_im[l]),
                                       n_prompt, dec_batch * dec_seq, dec_batch, dec_seq)
        hp_re, hp_im, hs_re, hs_im = map(grp_major, (hp_re, hp_im, hs_re, hs_im))

        y_attn, pk, pv = _prompt_attn(z, sinks[l], q_norm[l], k_norm[l], bsz, seq, n_tok, ssm_w, attn_w, kv_w)
        ya_s, sk, sv = _sample_attn(z, cache_k[l].reshape(dec_batch, cache_w, kv_w),
                                    cache_v[l].reshape(dec_batch, cache_w, kv_w), sinks[l], q_norm[l], k_norm[l],
                                    n_prompt, dec_seq, ssm_w, attn_w, kv_w)
        y_attn = lax.dynamic_update_slice(y_attn, ya_s, (n_prompt, 0))

        x = _out_proj(x, y_ssm, y_attn, ssm_out_norm[l], attn_out_norm[l], w_out_bf, layer=l)
        x = _ffn(x, ffn2_norm[l], *ffn2_w, layer=l)

        kv_shape = (-1, N_KV_HEADS, HEAD_DIM)
        for dst, val in zip(outs, (pk.reshape(bsz, *kv_shape), pv.reshape(bsz, *kv_shape), hp_re, hp_im,
                                   sk.reshape(dec_batch, *kv_shape), sv.reshape(dec_batch, *kv_shape), hs_re, hs_im)):
            dst.append(val)
    y_prompt = x[:n_prompt].reshape(bsz, seq, d)
    y_sample = x[n_prompt:].reshape(dec_batch, dec_seq, d)
    return (y_prompt, y_sample) + tuple(jnp.stack(o) for o in outs)
```

```python
import functools

import jax
import jax.numpy as jnp
import numpy as np
from jax import lax
from jax.experimental import pallas as pl
from jax.experimental.pallas import tpu as pltpu

F32 = jnp.float32
BF16 = jnp.bfloat16

EPS = 1e-6
SSM_GROUP_CH = 16
SSM_STATE = 64
HEAD_DIM = 64
N_KV_HEADS = 4
WINDOW = 128
PAST_LEN = 8192
QK_SCALE = HEAD_DIM ** -0.5
PROMPT_CHUNK = 16

TOKEN_TILE = 512
FFN_TOKEN_TILE = 1088
FFN_VMEM_LIMIT_BYTES = 60 * 1024 * 1024
FF_TILE = 512
SAMPLE_BATCH_TILE = 8
VMEM_LIMIT_BYTES = 48 * 1024 * 1024

HIGHEST = lax.Precision.HIGHEST
SSM_STATE_PRECISION = HIGHEST
SSM_GROUPS_PER_STEP = 8


def _rms(x, gain):
    return x * lax.rsqrt(jnp.mean(x * x, axis=-1, keepdims=True) + EPS) * gain


def _dot(a, b, precision=None):
    return jnp.dot(a, b, preferred_element_type=F32, precision=precision)


def _dot_nt(a, b, precision=None):
    return lax.dot_general(a, b, (((1,), (1,)), ((), ())), preferred_element_type=F32, precision=precision)


def _compiler_params(*semantics):
    return pltpu.CompilerParams(dimension_semantics=semantics, vmem_limit_bytes=VMEM_LIMIT_BYTES)


def _ffn_kernel(x_ref, g_ref, wg_ref, wu_ref, wd_ref, o_ref, xn_ref):
    j = pl.program_id(1)

    @pl.when(j == 0)
    def _():
        xn_ref[...] = _rms(x_ref[...], g_ref[...]).astype(BF16)
        o_ref[...] = jnp.zeros_like(o_ref)

    xn = xn_ref[...]
    gate = _dot(xn, wg_ref[...])
    up = _dot(xn, wu_ref[...])
    h = (gate * jax.nn.sigmoid(gate) * up).astype(BF16)
    o_ref[...] += _dot(h, wd_ref[...])

    @pl.when(j == pl.num_programs(1) - 1)
    def _():
        o_ref[...] = x_ref[...] + 0.5 * o_ref[...]


def _ffn(x, gain, w_gate, w_up, w_down, layer):
    n_tok, d = x.shape
    d_ff = w_gate.shape[2]
    tm = FFN_TOKEN_TILE if n_tok % FFN_TOKEN_TILE == 0 else TOKEN_TILE
    tf = FF_TILE
    return pl.pallas_call(
        _ffn_kernel,
        grid=(n_tok // tm, d_ff // tf),
        in_specs=[
            pl.BlockSpec((tm, d), lambda i, j: (i, 0)),
            pl.BlockSpec((1, d), lambda i, j: (0, 0)),
            pl.BlockSpec((None, d, tf), lambda i, j: (layer, 0, j)),
            pl.BlockSpec((None, d, tf), lambda i, j: (layer, 0, j)),
            pl.BlockSpec((None, tf, d), lambda i, j: (layer, j, 0)),
        ],
        out_specs=pl.BlockSpec((tm, d), lambda i, j: (i, 0)),
        out_shape=jax.ShapeDtypeStruct((n_tok, d), F32),
        scratch_shapes=[pltpu.VMEM((tm, d), BF16)],
        compiler_params=pltpu.CompilerParams(dimension_semantics=("parallel", "arbitrary"),
                                             vmem_limit_bytes=FFN_VMEM_LIMIT_BYTES),
        name="ffn",
    )(x, gain.reshape(1, d), w_gate, w_up, w_down)


def _norm_matmul_kernel(x_ref, g_ref, w_ref, o_ref, xn_ref):
    @pl.when(pl.program_id(1) == 0)
    def _():
        xn_ref[...] = _rms(x_ref[...], g_ref[...]).astype(BF16)

    o_ref[...] = _dot(xn_ref[...], w_ref[...])


def _norm_matmul(x, gain, w, layer, tn):
    n_tok, d = x.shape
    n_out = w.shape[2]
    tm = TOKEN_TILE
    return pl.pallas_call(
        _norm_matmul_kernel,
        grid=(n_tok // tm, n_out // tn),
        in_specs=[
            pl.BlockSpec((tm, d), lambda i, j: (i, 0)),
            pl.BlockSpec((1, d), lambda i, j: (0, 0)),
            pl.BlockSpec((None, d, tn), lambda i, j: (layer, 0, j)),
        ],
        out_specs=pl.BlockSpec((tm, tn), lambda i, j: (i, j)),
        out_shape=jax.ShapeDtypeStruct((n_tok, n_out), F32),
        scratch_shapes=[pltpu.VMEM((tm, d), BF16)],
        compiler_params=_compiler_params("parallel", "arbitrary"),
        name="in_proj",
    )(x, gain.reshape(1, d), w)


def _out_proj_kernel(x_ref, ys_ref, ya_ref, gs_ref, ga_ref, ws_ref, wa_ref, o_ref):
    ys = _rms(ys_ref[...], gs_ref[...]).astype(BF16)
    ya = _rms(ya_ref[...], ga_ref[...]).astype(BF16)
    o_ref[...] = x_ref[...] + _dot(ys, ws_ref[...]) + _dot(ya, wa_ref[...])


def _out_proj(x, y_ssm, y_attn, g_ssm, g_attn, w_out, layer):
    n_tok, d = x.shape
    ssm_w, attn_w = y_ssm.shape[1], y_attn.shape[1]
    tm = TOKEN_TILE
    return pl.pallas_call(
        _out_proj_kernel,
        grid=(n_tok // tm,),
        in_specs=[
            pl.BlockSpec((tm, d), lambda i: (i, 0)),
            pl.BlockSpec((tm, ssm_w), lambda i: (i, 0)),
            pl.BlockSpec((tm, attn_w), lambda i: (i, 0)),
            pl.BlockSpec((1, ssm_w), lambda i: (0, 0)),
            pl.BlockSpec((1, attn_w), lambda i: (0, 0)),
            pl.BlockSpec((None, ssm_w, d), lambda i: (layer, 0, 0)),
            pl.BlockSpec((None, attn_w, d), lambda i: (layer, ssm_w // attn_w, 0)),
        ],
        out_specs=pl.BlockSpec((tm, d), lambda i: (i, 0)),
        out_shape=jax.ShapeDtypeStruct((n_tok, d), F32),
        compiler_params=_compiler_params("parallel"),
        name="out_proj",
    )(x, y_ssm, y_attn, g_ssm.reshape(1, ssm_w), g_attn.reshape(1, attn_w), w_out, w_out)


def _ssm_prep_kernel(*refs):
    for gi in range(refs[0].shape[0]):
        _ssm_prep_group(*[r.at[pl.ds(gi, 1)] for r in refs])


def _ssm_prep_group(are_ref, aim_ref, ldt_ref, btre_ref, btim_ref, cre_ref, cim_ref, d_ref, wglu_ref,
                    m_ref, wre_ref, wim_ref, ztre_ref, ztim_ref, glu_ref, dt_ref, powre_ref, powim_ref):
    a_re, a_im = are_ref[0], aim_ref[0]
    dt = jnp.exp(ldt_ref[0])
    mag = jnp.exp(a_re * dt)
    lb_re, lb_im = mag * jnp.cos(a_im * dt), mag * jnp.sin(a_im * dt)
    den = a_re * a_re + a_im * a_im
    q_re = ((lb_re - 1.0) * a_re + lb_im * a_im) / den
    q_im = (lb_im * a_re - (lb_re - 1.0) * a_im) / den
    bt_re, bt_im = btre_ref[0], btim_ref[0]
    bb_re = q_re * bt_re - q_im * bt_im
    bb_im = q_re * bt_im + q_im * bt_re
    c_re, c_im = cre_ref[0], cim_ref[0]

    pw_re, pw_im = [jnp.ones_like(lb_re)], [jnp.zeros_like(lb_im)]
    for _ in range(PROMPT_CHUNK):
        pr, pi = pw_re[-1], pw_im[-1]
        pw_re.append(pr * lb_re - pi * lb_im)
        pw_im.append(pr * lb_im + pi * lb_re)

    steps = PROMPT_CHUNK
    c = c_re.shape[0]
    w_re = jnp.concatenate([bb_re * pw_re[steps - 1 - s] - bb_im * pw_im[steps - 1 - s] for s in range(steps)], axis=0)
    w_im = jnp.concatenate([bb_re * pw_im[steps - 1 - s] + bb_im * pw_re[steps - 1 - s] for s in range(steps)], axis=0)
    wre_ref[0] = w_re
    wim_ref[0] = w_im
    ztre_ref[0] = jnp.concatenate([c_re * pw_re[t + 1] - c_im * pw_im[t + 1] for t in range(steps)], axis=0).astype(BF16)
    ztim_ref[0] = jnp.concatenate([-(c_re * pw_im[t + 1] + c_im * pw_re[t + 1]) for t in range(steps)], axis=0).astype(BF16)

    lag = _dot_nt(w_re, c_re, HIGHEST) - _dot_nt(w_im, c_im, HIGHEST)
    cols = []
    for t in range(steps):
        live = lag[(steps - 1 - t) * c:, :]
        cols.append(live if t == steps - 1 else
                    jnp.concatenate([live, jnp.zeros(((steps - 1 - t) * c, c), F32)], axis=0))
    m_ref[0] = jnp.concatenate(cols, axis=1).astype(BF16)

    tiled = jnp.concatenate([jnp.concatenate([wglu_ref[0]] * steps, axis=0)] * steps, axis=1)
    shift = c.bit_length() - 1
    row_step = lax.shift_right_logical(lax.broadcasted_iota(jnp.int32, tiled.shape, 0), shift)
    col_step = lax.shift_right_logical(lax.broadcasted_iota(jnp.int32, tiled.shape, 1), shift)
    glu_ref[0] = jnp.where(row_step == col_step, tiled, 0.0).astype(BF16)
    dt_ref[0] = jnp.concatenate([d_ref[0]] * steps, axis=1)

    pad = jnp.zeros((powre_ref.shape[1] - (steps + 1), a_re.shape[1]), F32)
    powre_ref[0] = jnp.concatenate(pw_re + [pad], axis=0)
    powim_ref[0] = jnp.concatenate(pw_im + [pad], axis=0)


def _ssm_prep(a_re, a_im, log_dt, b_re, b_im, c_re, c_im, d_skip, w_glu):
    g, p = a_re.shape
    c = c_re.shape[1]
    assert c & (c - 1) == 0
    tc = PROMPT_CHUNK * c
    pow_rows = 24
    sds = lambda dtype, *shape: jax.ShapeDtypeStruct((g,) + shape, dtype)
    gps = SSM_GROUPS_PER_STEP
    gs = lambda *shape: pl.BlockSpec((gps,) + shape, lambda i: (i,) + (0,) * len(shape))
    names = ("m", "w_re", "w_im", "zt_re", "zt_im", "glu", "d", "pow_re", "pow_im")
    outs = pl.pallas_call(
        _ssm_prep_kernel,
        grid=(g // gps,),
        in_specs=[gs(1, p), gs(1, p), gs(1, 1), gs(c, p), gs(c, p), gs(c, p), gs(c, p), gs(1, c), gs(c, c)],
        out_specs=[gs(tc, tc), gs(tc, p), gs(tc, p), gs(tc, p), gs(tc, p), gs(tc, tc), gs(1, tc),
                   gs(pow_rows, p), gs(pow_rows, p)],
        out_shape=[sds(BF16, tc, tc), sds(F32, tc, p), sds(F32, tc, p), sds(BF16, tc, p), sds(BF16, tc, p),
                   sds(BF16, tc, tc), sds(F32, 1, tc), sds(F32, pow_rows, p), sds(F32, pow_rows, p)],
        compiler_params=_compiler_params("parallel"),
        name="ssm_prep",
    )(a_re.reshape(g, 1, p), a_im.reshape(g, 1, p), log_dt.reshape(g, 1, 1),
      jnp.swapaxes(b_re, 1, 2), jnp.swapaxes(b_im, 1, 2), c_re, c_im, d_skip.reshape(g, 1, c), w_glu)
    return dict(zip(names, outs))


def _ssm_tables(prep, t):
    n = t * SSM_GROUP_CH
    full = prep["m"].shape[1]
    return dict(m=prep["m"][:, :n, :n], w_re=prep["w_re"][:, full - n:], w_im=prep["w_im"][:, full - n:],
                zt_re=prep["zt_re"][:, :n], zt_im=prep["zt_im"][:, :n], glu=prep["glu"][:, :n, :n],
                d=prep["d"][:, :, :n], a_re=prep["pow_re"][:, t:t + 1], a_im=prep["pow_im"][:, t:t + 1])


def _ssm_mix_kernel(*refs, bsz, t, aliased):
    (u_ref, m_ref, wre_ref, wim_ref, ztre_ref, ztim_ref, are_ref, aim_ref, d_ref, glu_ref, h0re_ref, h0im_ref) = refs[:12]
    y_ref, hre_ref, him_ref, xs_ref, ug_ref, vre_ref, vim_ref, pre_ref, pim_ref = refs[12 + aliased:]
    rows = u_ref.shape[0] // t
    chunks = rows // bsz
    c = SSM_GROUP_CH
    n_grp = m_ref.shape[0]
    for s in range(t):
        xs_ref[s] = u_ref[pl.ds(s, rows, stride=t), :]
    for gi in range(n_grp):
        lanes = slice(gi * c, (gi + 1) * c)
        u = jnp.concatenate([xs_ref[s, :, lanes] for s in range(t)], axis=1)
        ug_ref[gi] = u
        vre_ref[gi] = _dot(u, wre_ref[gi], SSM_STATE_PRECISION)
        vim_ref[gi] = _dot(u, wim_ref[gi], SSM_STATE_PRECISION)

    def chunk_step(k, h):
        sel = pl.ds(k, bsz, stride=chunks)
        nxt = []
        for gi in range(n_grp):
            h_re, h_im = h[2 * gi], h[2 * gi + 1]
            pre_ref[gi, sel, :] = h_re
            pim_ref[gi, sel, :] = h_im
            a_re, a_im = are_ref[gi], aim_ref[gi]
            nxt.append(a_re * h_re - a_im * h_im + vre_ref[gi, sel, :])
            nxt.append(a_re * h_im + a_im * h_re + vim_ref[gi, sel, :])
        return tuple(nxt)

    h = tuple(r[gi] for gi in range(n_grp) for r in (h0re_ref, h0im_ref))
    h = lax.fori_loop(0, chunks, chunk_step, h, unroll=min(chunks, 4))
    for gi in range(n_grp):
        hre_ref[gi] = h[2 * gi]
        him_ref[gi] = h[2 * gi + 1]

    for gi in range(n_grp):
        u = ug_ref[gi]
        y = (_dot(u.astype(BF16), m_ref[gi]) + _dot_nt(pre_ref[gi].astype(BF16), ztre_ref[gi])
             + _dot_nt(pim_ref[gi].astype(BF16), ztim_ref[gi]) + d_ref[gi] * u)
        y = jax.nn.gelu(y)
        y = y * jax.nn.sigmoid(_dot(y.astype(BF16), glu_ref[gi]))
        for s in range(t):
            xs_ref[s, :, gi * c:(gi + 1) * c] = y[:, s * c:(s + 1) * c]
    for s in range(t):
        y_ref[pl.ds(s, rows, stride=t), :] = xs_ref[s]


def _ssm_mix(z, y_prev, tab, h0_re, h0_im, row0, n_rows, bsz, t):
    g, tc, _ = tab["m"].shape
    p = h0_re.shape[2]
    n_tok = z.shape[0]
    width = g * SSM_GROUP_CH
    gps = SSM_GROUPS_PER_STEP
    lanes = gps * SSM_GROUP_CH
    rows = n_rows // t
    blk = lambda *shape: pl.BlockSpec((gps,) + shape, lambda i: (i,) + (0,) * len(shape))
    tok = pl.BlockSpec((n_rows, lanes), lambda i: (row0 // n_rows, i))
    aliased = y_prev is not None
    in_specs = [tok, blk(tc, tc), blk(tc, p), blk(tc, p), blk(tc, p), blk(tc, p), blk(1, p), blk(1, p), blk(1, tc),
                blk(tc, tc), blk(bsz, p), blk(bsz, p)]
    args = [z, tab["m"], tab["w_re"], tab["w_im"], tab["zt_re"], tab["zt_im"], tab["a_re"], tab["a_im"], tab["d"],
            tab["glu"], h0_re, h0_im]
    if aliased:
        in_specs.append(pl.BlockSpec(memory_space=pl.ANY))
        args.append(y_prev)
    return pl.pallas_call(
        functools.partial(_ssm_mix_kernel, bsz=bsz, t=t, aliased=aliased),
        grid=(g // gps,),
        in_specs=in_specs,
        out_specs=[tok, blk(bsz, p), blk(bsz, p)],
        out_shape=[jax.ShapeDtypeStruct((n_tok, width), F32), jax.ShapeDtypeStruct((g, bsz, p), F32),
                   jax.ShapeDtypeStruct((g, bsz, p), F32)],
        scratch_shapes=[pltpu.VMEM((t, rows, lanes), F32), pltpu.VMEM((gps, rows, tc), F32)]
        + [pltpu.VMEM((gps, rows, p), F32)] * 4,
        input_output_aliases={len(args) - 1: 0} if aliased else {},
        compiler_params=_compiler_params("parallel"),
        name="ssm_mix",
    )(*args)


def _bias_table(q_pos, k_pos, n_heads):
    slopes = np.exp2(-8.0 * np.arange(1, n_heads + 1, dtype=np.float64) / n_heads)
    dist = q_pos[:, None] - k_pos[None, :]
    valid = (dist >= 0) & (dist < WINDOW) & (k_pos[None, :] >= 0)
    bias = -slopes[:, None, None] * dist[None].astype(np.float64)
    return np.where(valid[None], bias, -np.inf).astype(np.float32)


def _head_rms(x, gain):
    heads = [_rms(x[:, i:i + HEAD_DIM], gain) for i in range(0, x.shape[1], HEAD_DIM)]
    return jnp.concatenate(heads, axis=1)


def _softmax_sink_pv(logits, sink, v):
    m = jnp.maximum(jnp.max(logits, axis=-1, keepdims=True), sink)
    p = jnp.exp(logits - m)
    denom = jnp.sum(p, axis=-1, keepdims=True) + jnp.exp(sink - m)
    return _dot(p.astype(BF16), v) / denom


def _prompt_attn_kernel(sinks_ref, q_ref, k_ref, v_ref, qg_ref, kg_ref, bias_ref, y_ref, ks_ref, vs_ref,
                        kp_ref, vp_ref, *, n_heads):
    n = pl.program_id(1)

    @pl.when(n == 0)
    def _():
        kp_ref[...] = jnp.zeros_like(kp_ref)
        vp_ref[...] = jnp.zeros_like(vp_ref)

    q_all = q_ref[...]
    blk = q_all.shape[0]
    half = 2 * HEAD_DIM
    assert n_heads // N_KV_HEADS == 4 and half == 128
    k_cur = _head_rms(k_ref[...], kg_ref[...])
    v_cur = v_ref[...]
    k2 = jnp.concatenate([kp_ref[...], k_cur], axis=0)
    v2t = jnp.concatenate([vp_ref[...], v_cur], axis=0).T.astype(BF16)
    qs = (q_all * qg_ref[...] * QK_SCALE).astype(BF16)
    qsq = q_all * q_all
    table = jnp.minimum(n, 1)
    from_prev = (lax.broadcasted_iota(jnp.int32, (blk, blk), 0) > lax.broadcasted_iota(jnp.int32, (blk, blk), 1))
    low_lanes = lax.broadcasted_iota(jnp.int32, (2 * blk, half), 1) < HEAD_DIM
    low8 = lax.broadcasted_iota(jnp.int32, (8, half), 1) < HEAD_DIM
    pick = jnp.concatenate([jnp.where(low8, 1.0, 0.0), jnp.where(low8, 0.0, 1.0)], axis=0)
    for kv in range(N_KV_HEADS):
        k_tile = k2[:, (kv // 2) * half:(kv // 2 + 1) * half]
        k_own = jnp.where(low_lanes if kv % 2 == 0 else ~low_lanes, k_tile, 0.0)
        k_swap = pltpu.roll(k_own, HEAD_DIM, axis=1)
        k_lhs = jnp.concatenate([k_own, k_swap] if kv % 2 == 0 else [k_swap, k_own], axis=0).astype(BF16)
        tiles = slice(2 * kv * half, (2 * kv + 1) * half), slice((2 * kv + 1) * half, (2 * kv + 2) * half)
        q_rows = jnp.concatenate([qs[:, t] for t in tiles], axis=0)
        qsq_rows = jnp.concatenate([qsq[:, t] for t in tiles], axis=0)
        s = _dot_nt(k_lhs, q_rows)
        ssq = _dot_nt(pick, qsq_rows, HIGHEST)
        probs, inv_denoms = [], []
        for i in range(4):
            odd, tile = i % 2, i // 2
            lanes = slice(tile * blk, (tile + 1) * blk)
            s_h = s[odd * 2 * blk:(odd + 1) * 2 * blk, lanes]
            q_rms = lax.rsqrt(ssq[odd * 8:odd * 8 + 1, lanes] * (1.0 / HEAD_DIM) + EPS)
            logits = jnp.where(from_prev, s_h[:blk], s_h[blk:]) * q_rms + bias_ref[table, 4 * kv + i]
            sink = sinks_ref[4 * kv + i]
            m = jnp.maximum(jnp.max(logits, axis=0, keepdims=True), sink)
            p = jnp.exp(logits - m)
            inv_denoms.append(1.0 / (jnp.sum(p, axis=0, keepdims=True) + jnp.exp(sink - m)))
            probs.append(jnp.concatenate([jnp.where(from_prev, p, 0.0), jnp.where(from_prev, 0.0, p)], axis=0))
        o = _dot(v2t[kv * HEAD_DIM:(kv + 1) * HEAD_DIM], jnp.concatenate(probs, axis=1).astype(BF16))
        for tile in range(2):
            pair = [o[:, i * blk:(i + 1) * blk] * inv_denoms[i] for i in (2 * tile, 2 * tile + 1)]
            y_ref[:, tiles[tile]] = jnp.concatenate(pair, axis=0).T
    kp_ref[...] = k_cur
    vp_ref[...] = v_cur

    @pl.when(n == pl.num_programs(1) - 1)
    def _():
        ks_ref[0] = k_cur
        vs_ref[0] = v_cur


def _prompt_attn(z, sinks, q_gain, k_gain, bsz, seq, n_tok, col_q, attn_w, kv_w):
    blk = WINDOW
    nb = seq // blk
    n_heads = attn_w // HEAD_DIM
    row = lambda b, n: b * nb + n
    q_blk, k_blk, v_blk = col_q // attn_w, (col_q + attn_w) // kv_w, (col_q + attn_w + kv_w) // kv_w
    qi, kj = np.arange(blk)[:, None], np.arange(blk)[None, :]
    first, later = [np.take_along_axis(_bias_table(first_pos + np.arange(blk), first_pos - blk + np.arange(2 * blk), n_heads),
                                       np.broadcast_to(np.where(kj > qi, kj, kj + blk), (n_heads, blk, blk)), axis=2)
                    for first_pos in (0, blk)]
    bias = np.stack([first, later]).swapaxes(2, 3)
    return pl.pallas_call(
        functools.partial(_prompt_attn_kernel, n_heads=n_heads),
        grid=(bsz, nb),
        in_specs=[
            pl.BlockSpec(memory_space=pltpu.SMEM),
            pl.BlockSpec((blk, attn_w), lambda b, n: (row(b, n), q_blk)),
            pl.BlockSpec((blk, kv_w), lambda b, n: (row(b, n), k_blk)),
            pl.BlockSpec((blk, kv_w), lambda b, n: (row(b, n), v_blk)),
            pl.BlockSpec((1, attn_w), lambda b, n: (0, 0)),
            pl.BlockSpec((1, HEAD_DIM), lambda b, n: (0, 0)),
            pl.BlockSpec(bias.shape, lambda b, n: (0, 0, 0, 0)),
        ],
        out_specs=[
            pl.BlockSpec((blk, attn_w), lambda b, n: (row(b, n), 0)),
            pl.BlockSpec((1, blk, kv_w), lambda b, n: (b, 0, 0)),
            pl.BlockSpec((1, blk, kv_w), lambda b, n: (b, 0, 0)),
        ],
        out_shape=[jax.ShapeDtypeStruct((n_tok, attn_w), F32),
                   jax.ShapeDtypeStruct((bsz, blk, kv_w), F32),
                   jax.ShapeDtypeStruct((bsz, blk, kv_w), F32)],
        scratch_shapes=[pltpu.VMEM((blk, kv_w), F32)] * 2,
        compiler_params=_compiler_params("parallel", "arbitrary"),
        name="prompt_attn",
    )(sinks, z, z, z, jnp.tile(q_gain, n_heads).reshape(1, attn_w), k_gain.reshape(1, HEAD_DIM), jnp.asarray(bias))


def _sample_attn_kernel(*refs, dec_seq, aliased):
    q_ref, k_ref, v_ref, ck_ref, cv_ref, qg_ref, kg_ref, bias_ref, sink_ref = refs[:9]
    y_ref, ks_ref, vs_ref, kbuf_ref, vbuf_ref = refs[9 + 2 * aliased:]
    cache_w = ck_ref.shape[1]
    k_new = _head_rms(k_ref[...], kg_ref[...])
    v_new = v_ref[...]
    pad = jnp.zeros((kbuf_ref.shape[1] - cache_w, kbuf_ref.shape[2]), F32)
    for b in range(q_ref.shape[0]):
        rows = slice(b * dec_seq, (b + 1) * dec_seq)
        kbuf_ref[b, 0:cache_w, :] = ck_ref[b]
        vbuf_ref[b, 0:cache_w, :] = cv_ref[b]
        kbuf_ref[b, cache_w:, :] = pad
        vbuf_ref[b, cache_w:, :] = pad
        kbuf_ref[b, cache_w:cache_w + dec_seq, :] = k_new[rows]
        vbuf_ref[b, cache_w:cache_w + dec_seq, :] = v_new[rows]
        q = q_ref[b]
        q = q * lax.rsqrt(jnp.sum(q * q, axis=-1, keepdims=True) * (1.0 / HEAD_DIM) + EPS) * qg_ref[...]
        logits = _dot_nt((q * QK_SCALE).astype(BF16), kbuf_ref[b].astype(BF16)) + bias_ref[...]
        y_ref[b] = _softmax_sink_pv(logits, sink_ref[...], vbuf_ref[b].astype(BF16))
        ks_ref[b, 0:cache_w - dec_seq, :] = ck_ref[b, dec_seq:cache_w, :]
        vs_ref[b, 0:cache_w - dec_seq, :] = cv_ref[b, dec_seq:cache_w, :]
        ks_ref[b, cache_w - dec_seq:cache_w, :] = k_new[rows]
        vs_ref[b, cache_w - dec_seq:cache_w, :] = v_new[rows]


def _sample_attn(z, cache_k, cache_v, windows, layer, sinks, q_gain, k_gain, row0, dec_seq, col_q, attn_w, kv_w):
    _, dec_batch, cache_w, _ = cache_k.shape
    n_heads = attn_w // HEAD_DIM
    q_per_kv = n_heads // N_KV_HEADS
    tile = SAMPLE_BATCH_TILE
    r0 = row0 // (tile * dec_seq)
    k_blk, v_blk = (col_q + attn_w) // kv_w, (col_q + attn_w + kv_w) // kv_w
    keys = 2 * cache_w
    k_pos = np.full(keys, -1)
    k_pos[:cache_w] = PAST_LEN - cache_w + np.arange(cache_w)
    k_pos[cache_w:cache_w + dec_seq] = PAST_LEN + np.arange(dec_seq)
    bias = _bias_table(PAST_LEN + np.arange(dec_seq), k_pos, n_heads).reshape(n_heads * dec_seq, keys)
    sink_col = jnp.repeat(sinks, dec_seq).reshape(n_heads * dec_seq, 1)
    q = z[row0:, col_q:col_q + attn_w].reshape(dec_batch, dec_seq, N_KV_HEADS, q_per_kv, 1, HEAD_DIM)
    own_kv = np.eye(N_KV_HEADS, dtype=np.float32)[None, :, None, None, :, None]
    q = (q.transpose(0, 2, 3, 1, 4, 5) * own_kv).reshape(dec_batch, n_heads * dec_seq, kv_w)
    window = pl.BlockSpec((None, tile, cache_w, kv_w), lambda i: (layer, i, 0, 0))
    aliased = windows is not None
    in_specs = [
        pl.BlockSpec((tile, n_heads * dec_seq, kv_w), lambda i: (i, 0, 0)),
        pl.BlockSpec((tile * dec_seq, kv_w), lambda i: (r0 + i, k_blk)),
        pl.BlockSpec((tile * dec_seq, kv_w), lambda i: (r0 + i, v_blk)),
        window,
        window,
        pl.BlockSpec((1, kv_w), lambda i: (0, 0)),
        pl.BlockSpec((1, HEAD_DIM), lambda i: (0, 0)),
        pl.BlockSpec(bias.shape, lambda i: (0, 0)),
        pl.BlockSpec(sink_col.shape, lambda i: (0, 0)),
    ]
    args = [q, z, z, cache_k, cache_v, jnp.tile(q_gain, N_KV_HEADS).reshape(1, kv_w), k_gain.reshape(1, HEAD_DIM),
            jnp.asarray(bias), sink_col]
    if aliased:
        in_specs += [pl.BlockSpec(memory_space=pl.ANY)] * 2
        args += list(windows)
    y, ks, vs = pl.pallas_call(
        functools.partial(_sample_attn_kernel, dec_seq=dec_seq, aliased=aliased),
        grid=(dec_batch // tile,),
        in_specs=in_specs,
        out_specs=[pl.BlockSpec((tile, n_heads * dec_seq, kv_w), lambda i: (i, 0, 0)), window, window],
        out_shape=[jax.ShapeDtypeStruct(q.shape, F32),
                   jax.ShapeDtypeStruct(cache_k.shape, F32),
                   jax.ShapeDtypeStruct(cache_v.shape, F32)],
        scratch_shapes=[pltpu.VMEM((tile, keys, kv_w), F32)] * 2,
        input_output_aliases={len(args) - 2: 1, len(args) - 1: 2} if aliased else {},
        compiler_params=_compiler_params("parallel"),
        name="sample_attn",
    )(*args)
    y = y.reshape(dec_batch, N_KV_HEADS, q_per_kv, dec_seq, N_KV_HEADS, HEAD_DIM)
    y = jnp.stack([y[:, kv, :, :, kv] for kv in range(N_KV_HEADS)], axis=1)
    return y.transpose(0, 3, 1, 2, 4).reshape(dec_batch * dec_seq, attn_w), ks, vs


def kernel(x_prompt, x_sample, cache_k, cache_v, state_ssm_re, state_ssm_im, ffn1_norm, ffn1_w_gate, ffn1_w_up, ffn1_w_down, mix_norm, w_in, ssm_A_re, ssm_A_im, ssm_B_re, ssm_B_im, ssm_C_re, ssm_C_im, ssm_D, ssm_log_dt, ssm_w_glu, q_norm, k_norm, sinks, ssm_out_norm, attn_out_norm, w_out, ffn2_norm, ffn2_w_gate, ffn2_w_up, ffn2_w_down):
    bsz, seq, d = x_prompt.shape
    dec_batch, dec_seq, _ = x_sample.shape
    depth = w_in.shape[0]
    ssm_w = ssm_A_re.shape[1] * SSM_GROUP_CH
    attn_w = w_out.shape[1] - ssm_w
    kv_w = N_KV_HEADS * HEAD_DIM
    cache_w = cache_k.shape[2]
    n_prompt = bsz * seq
    n_tok = n_prompt + dec_batch * dec_seq

    x = jnp.concatenate([x_prompt.reshape(n_prompt, d), x_sample.reshape(dec_batch * dec_seq, d)], axis=0)
    zeros_state = jnp.zeros((ssm_w // SSM_GROUP_CH, bsz, SSM_STATE), F32)
    outs = [[] for _ in range(6)]
    cache_k_flat = cache_k.reshape(depth, dec_batch, cache_w, kv_w)
    cache_v_flat = cache_v.reshape(depth, dec_batch, cache_w, kv_w)
    windows = None
    bf = lambda w: w.astype(BF16)
    ffn1_w, ffn2_w = [tuple(map(bf, ws)) for ws in ((ffn1_w_gate, ffn1_w_up, ffn1_w_down),
                                                    (ffn2_w_gate, ffn2_w_up, ffn2_w_down))]
    w_in_bf, w_out_bf = bf(w_in), bf(w_out)
    for l in range(depth):
        x = _ffn(x, ffn1_norm[l], *ffn1_w, layer=l)
        z = _norm_matmul(x, mix_norm[l], w_in_bf, layer=l, tn=w_in.shape[2])

        prep = _ssm_prep(ssm_A_re[l], ssm_A_im[l], ssm_log_dt[l], ssm_B_re[l], ssm_B_im[l], ssm_C_re[l], ssm_C_im[l],
                         ssm_D[l], ssm_w_glu[l])
        tab_p = _ssm_tables(prep, PROMPT_CHUNK)
        tab_s = _ssm_tables(prep, dec_seq)
        grp_major = lambda a: a.transpose(1, 0, 2)
        y_ssm, hp_re, hp_im = _ssm_mix(z, None, tab_p, zeros_state, zeros_state, 0, n_prompt, bsz, PROMPT_CHUNK)
        y_ssm, hs_re, hs_im = _ssm_mix(z, y_ssm, tab_s, grp_major(state_ssm_re[l]), grp_major(state_ssm_im[l]),
                                       n_prompt, dec_batch * dec_seq, dec_batch, dec_seq)
        hp_re, hp_im, hs_re, hs_im = map(grp_major, (hp_re, hp_im, hs_re, hs_im))

        y_attn, pk, pv = _prompt_attn(z, sinks[l], q_norm[l], k_norm[l], bsz, seq, n_tok, ssm_w, attn_w, kv_w)
        ya_s, *windows = _sample_attn(z, cache_k_flat, cache_v_flat, windows, l, sinks[l], q_norm[l], k_norm[l],
                                      n_prompt, dec_seq, ssm_w, attn_w, kv_w)
        y_attn = lax.dynamic_update_slice(y_attn, ya_s, (n_prompt, 0))

        x = _out_proj(x, y_ssm, y_attn, ssm_out_norm[l], attn_out_norm[l], w_out_bf, layer=l)
        x = _ffn(x, ffn2_norm[l], *ffn2_w, layer=l)

        kv_shape = (-1, N_KV_HEADS, HEAD_DIM)
        for dst, val in zip(outs, (pk.reshape(bsz, *kv_shape), pv.reshape(bsz, *kv_shape), hp_re, hp_im, hs_re, hs_im)):
            dst.append(val)
    y_prompt = x[:n_prompt].reshape(bsz, seq, d)
    y_sample = x[n_prompt:].reshape(dec_batch, dec_seq, d)
    pk, pv, hp_re, hp_im, hs_re, hs_im = (jnp.stack(o) for o in outs)
    sk, sv = (w.reshape(cache_k.shape) for w in windows)
    return (y_prompt, y_sample, pk, pv, hp_re, hp_im, sk, sv, hs_re, hs_im)
```

```python
import functools

import jax
import jax.numpy as jnp
import numpy as np
from jax import lax
from jax.experimental import pallas as pl
from jax.experimental.pallas import tpu as pltpu

F32 = jnp.float32
BF16 = jnp.bfloat16

EPS = 1e-6
SSM_GROUP_CH = 16
SSM_STATE = 64
HEAD_DIM = 64
N_KV_HEADS = 4
WINDOW = 128
PAST_LEN = 8192
QK_SCALE = HEAD_DIM ** -0.5
PROMPT_CHUNK = 16

TOKEN_TILE = 512
FFN_TOKEN_TILE = 1088
FFN_VMEM_LIMIT_BYTES = 60 * 1024 * 1024
FF_TILE = 512
SAMPLE_BATCH_TILE = 8
VMEM_LIMIT_BYTES = 48 * 1024 * 1024

HIGHEST = lax.Precision.HIGHEST
LANES = 128
SSM_MERGE = 2


def _rms(x, gain):
    return x * lax.rsqrt(jnp.mean(x * x, axis=-1, keepdims=True) + EPS) * gain


def _dot(a, b, precision=None):
    return jnp.dot(a, b, preferred_element_type=F32, precision=precision)


def _dot_nt(a, b, precision=None):
    return lax.dot_general(a, b, (((1,), (1,)), ((), ())), preferred_element_type=F32, precision=precision)


def _compiler_params(*semantics):
    return pltpu.CompilerParams(dimension_semantics=semantics, vmem_limit_bytes=VMEM_LIMIT_BYTES)


def _ffn_kernel(x_ref, g_ref, wg_ref, wu_ref, wd_ref, o_ref, xn_ref):
    j = pl.program_id(1)

    @pl.when(j == 0)
    def _():
        xn_ref[...] = _rms(x_ref[...], g_ref[...]).astype(BF16)
        o_ref[...] = jnp.zeros_like(o_ref)

    xn = xn_ref[...]
    gate = _dot(xn, wg_ref[...])
    up = _dot(xn, wu_ref[...])
    h = (gate * jax.nn.sigmoid(gate) * up).astype(BF16)
    o_ref[...] += _dot(h, wd_ref[...])

    @pl.when(j == pl.num_programs(1) - 1)
    def _():
        o_ref[...] = x_ref[...] + 0.5 * o_ref[...]


def _ffn(x, gain, w_gate, w_up, w_down, layer):
    n_tok, d = x.shape
    d_ff = w_gate.shape[2]
    tm = FFN_TOKEN_TILE if n_tok % FFN_TOKEN_TILE == 0 else TOKEN_TILE
    tf = FF_TILE
    return pl.pallas_call(
        _ffn_kernel,
        grid=(n_tok // tm, d_ff // tf),
        in_specs=[
            pl.BlockSpec((tm, d), lambda i, j: (i, 0)),
            pl.BlockSpec((1, d), lambda i, j: (0, 0)),
            pl.BlockSpec((None, d, tf), lambda i, j: (layer, 0, j)),
            pl.BlockSpec((None, d, tf), lambda i, j: (layer, 0, j)),
            pl.BlockSpec((None, tf, d), lambda i, j: (layer, j, 0)),
        ],
        out_specs=pl.BlockSpec((tm, d), lambda i, j: (i, 0)),
        out_shape=jax.ShapeDtypeStruct((n_tok, d), F32),
        scratch_shapes=[pltpu.VMEM((tm, d), BF16)],
        compiler_params=pltpu.CompilerParams(dimension_semantics=("parallel", "arbitrary"),
                                             vmem_limit_bytes=FFN_VMEM_LIMIT_BYTES),
        name="ffn",
    )(x, gain.reshape(1, d), w_gate, w_up, w_down)


def _norm_matmul_kernel(x_ref, g_ref, w_ref, o_ref, xn_ref):
    @pl.when(pl.program_id(1) == 0)
    def _():
        xn_ref[...] = _rms(x_ref[...], g_ref[...]).astype(BF16)

    o_ref[...] = _dot(xn_ref[...], w_ref[...])


def _norm_matmul(x, gain, w, layer, tn):
    n_tok, d = x.shape
    n_out = w.shape[2]
    tm = TOKEN_TILE
    return pl.pallas_call(
        _norm_matmul_kernel,
        grid=(n_tok // tm, n_out // tn),
        in_specs=[
            pl.BlockSpec((tm, d), lambda i, j: (i, 0)),
            pl.BlockSpec((1, d), lambda i, j: (0, 0)),
            pl.BlockSpec((None, d, tn), lambda i, j: (layer, 0, j)),
        ],
        out_specs=pl.BlockSpec((tm, tn), lambda i, j: (i, j)),
        out_shape=jax.ShapeDtypeStruct((n_tok, n_out), F32),
        scratch_shapes=[pltpu.VMEM((tm, d), BF16)],
        compiler_params=_compiler_params("parallel", "arbitrary"),
        name="in_proj",
    )(x, gain.reshape(1, d), w)


def _out_proj_kernel(x_ref, ys_ref, ya_ref, gs_ref, ga_ref, ws_ref, wa_ref, o_ref):
    ys = _rms(ys_ref[...], gs_ref[...]).astype(BF16)
    ya = _rms(ya_ref[...], ga_ref[...]).astype(BF16)
    o_ref[...] = x_ref[...] + _dot(ys, ws_ref[...]) + _dot(ya, wa_ref[...])


def _out_proj(x, y_ssm, y_attn, g_ssm, g_attn, w_out, layer):
    n_tok, d = x.shape
    ssm_w, attn_w = y_ssm.shape[1], y_attn.shape[1]
    tm = TOKEN_TILE
    return pl.pallas_call(
        _out_proj_kernel,
        grid=(n_tok // tm,),
        in_specs=[
            pl.BlockSpec((tm, d), lambda i: (i, 0)),
            pl.BlockSpec((tm, ssm_w), lambda i: (i, 0)),
            pl.BlockSpec((tm, attn_w), lambda i: (i, 0)),
            pl.BlockSpec((1, ssm_w), lambda i: (0, 0)),
            pl.BlockSpec((1, attn_w), lambda i: (0, 0)),
            pl.BlockSpec((None, ssm_w, d), lambda i: (layer, 0, 0)),
            pl.BlockSpec((None, attn_w, d), lambda i: (layer, ssm_w // attn_w, 0)),
        ],
        out_specs=pl.BlockSpec((tm, d), lambda i: (i, 0)),
        out_shape=jax.ShapeDtypeStruct((n_tok, d), F32),
        compiler_params=_compiler_params("parallel"),
        name="out_proj",
    )(x, y_ssm, y_attn, g_ssm.reshape(1, ssm_w), g_attn.reshape(1, attn_w), w_out, w_out)


def _ssm_prep_kernel(*refs):
    for gi in range(refs[0].shape[0]):
        _ssm_prep_group(*[r.at[pl.ds(gi, 1)] for r in refs])


def _ssm_prep_group(are_ref, aim_ref, ldt_ref, btre_ref, btim_ref, cre_ref, cim_ref, d_ref, wglu_ref,
                    m_ref, wre_ref, wim_ref, ztre_ref, ztim_ref, glu_ref, dt_ref, powre_ref, powim_ref):
    a_re, a_im = are_ref[0], aim_ref[0]
    dt = jnp.exp(ldt_ref[0])
    mag = jnp.exp(a_re * dt)
    lb_re, lb_im = mag * jnp.cos(a_im * dt), mag * jnp.sin(a_im * dt)
    den = a_re * a_re + a_im * a_im
    q_re = ((lb_re - 1.0) * a_re + lb_im * a_im) / den
    q_im = (lb_im * a_re - (lb_re - 1.0) * a_im) / den
    bt_re, bt_im = btre_ref[0], btim_ref[0]
    bb_re = q_re * bt_re - q_im * bt_im
    bb_im = q_re * bt_im + q_im * bt_re
    c_re, c_im = cre_ref[0], cim_ref[0]

    pw_re, pw_im = [jnp.ones_like(lb_re)], [jnp.zeros_like(lb_im)]
    for _ in range(PROMPT_CHUNK):
        pr, pi = pw_re[-1], pw_im[-1]
        pw_re.append(pr * lb_re - pi * lb_im)
        pw_im.append(pr * lb_im + pi * lb_re)

    steps = PROMPT_CHUNK
    c = c_re.shape[0]
    w_re = jnp.concatenate([bb_re * pw_re[steps - 1 - s] - bb_im * pw_im[steps - 1 - s] for s in range(steps)], axis=0)
    w_im = jnp.concatenate([bb_re * pw_im[steps - 1 - s] + bb_im * pw_re[steps - 1 - s] for s in range(steps)], axis=0)
    wre_ref[0] = w_re.astype(BF16)
    wim_ref[0] = w_im.astype(BF16)
    ztre_ref[0] = jnp.concatenate([c_re * pw_re[t + 1] - c_im * pw_im[t + 1] for t in range(steps)], axis=0).astype(BF16)
    ztim_ref[0] = jnp.concatenate([-(c_re * pw_im[t + 1] + c_im * pw_re[t + 1]) for t in range(steps)], axis=0).astype(BF16)

    lag = _dot_nt(w_re, c_re, HIGHEST) - _dot_nt(w_im, c_im, HIGHEST)
    cols = []
    for t in range(steps):
        live = lag[(steps - 1 - t) * c:, :]
        cols.append(live if t == steps - 1 else
                    jnp.concatenate([live, jnp.zeros(((steps - 1 - t) * c, c), F32)], axis=0))
    m_ref[0] = jnp.concatenate(cols, axis=1).astype(BF16)

    tiled = jnp.concatenate([jnp.concatenate([wglu_ref[0]] * steps, axis=0)] * steps, axis=1)
    shift = c.bit_length() - 1
    row_step = lax.shift_right_logical(lax.broadcasted_iota(jnp.int32, tiled.shape, 0), shift)
    col_step = lax.shift_right_logical(lax.broadcasted_iota(jnp.int32, tiled.shape, 1), shift)
    glu_ref[0] = jnp.where(row_step == col_step, tiled, 0.0).astype(BF16)
    dt_ref[0] = jnp.concatenate([d_ref[0]] * steps, axis=1)

    pad = jnp.zeros((powre_ref.shape[1] - (steps + 1), a_re.shape[1]), F32)
    powre_ref[0] = jnp.concatenate(pw_re + [pad], axis=0)
    powim_ref[0] = jnp.concatenate(pw_im + [pad], axis=0)


def _merge_groups(a_re, a_im, log_dt, b_re, b_im, c_re, c_im, d_skip, w_glu, n):
    g, p = a_re.shape
    gm = g // n
    own = np.eye(n, dtype=np.float32)[None, :, :, None, None]

    def block_diag(x):
        r, q = x.shape[1:]
        return (x.reshape(gm, n, 1, r, q) * own).transpose(0, 1, 3, 2, 4).reshape(gm, n * r, n * q)

    lanes = lambda x: x.reshape(gm, 1, -1)
    return (lanes(a_re), lanes(a_im), lanes(jnp.repeat(log_dt, p)),
            block_diag(jnp.swapaxes(b_re, 1, 2)), block_diag(jnp.swapaxes(b_im, 1, 2)),
            block_diag(c_re), block_diag(c_im), lanes(d_skip), block_diag(w_glu))


def _ssm_prep(a_re, a_im, log_dt, bt_re, bt_im, c_re, c_im, d_skip, w_glu):
    g, c, p = c_re.shape
    assert c & (c - 1) == 0
    tc = PROMPT_CHUNK * c
    pow_rows = 24
    sds = lambda dtype, *shape: jax.ShapeDtypeStruct((g,) + shape, dtype)
    gps = LANES // c
    gs = lambda *shape: pl.BlockSpec((gps,) + shape, lambda i: (i,) + (0,) * len(shape))
    names = ("m", "w_re", "w_im", "zt_re", "zt_im", "glu", "d", "pow_re", "pow_im")
    outs = pl.pallas_call(
        _ssm_prep_kernel,
        grid=(g // gps,),
        in_specs=[gs(1, p), gs(1, p), gs(1, p), gs(c, p), gs(c, p), gs(c, p), gs(c, p), gs(1, c), gs(c, c)],
        out_specs=[gs(tc, tc), gs(tc, p), gs(tc, p), gs(tc, p), gs(tc, p), gs(tc, tc), gs(1, tc),
                   gs(pow_rows, p), gs(pow_rows, p)],
        out_shape=[sds(BF16, tc, tc), sds(BF16, tc, p), sds(BF16, tc, p), sds(BF16, tc, p), sds(BF16, tc, p),
                   sds(BF16, tc, tc), sds(F32, 1, tc), sds(F32, pow_rows, p), sds(F32, pow_rows, p)],
        compiler_params=_compiler_params("parallel"),
        name="ssm_prep",
    )(a_re, a_im, log_dt, bt_re, bt_im, c_re, c_im, d_skip, w_glu)
    return dict(zip(names, outs))


def _ssm_tables(prep, t):
    n = t * (prep["d"].shape[2] // PROMPT_CHUNK)
    full = prep["m"].shape[1]
    return dict(m=prep["m"][:, :n, :n], w_re=prep["w_re"][:, full - n:], w_im=prep["w_im"][:, full - n:],
                zt_re=prep["zt_re"][:, :n], zt_im=prep["zt_im"][:, :n], glu=prep["glu"][:, :n, :n],
                d=prep["d"][:, :, :n], a_re=prep["pow_re"][:, t:t + 1], a_im=prep["pow_im"][:, t:t + 1])


def _ssm_mix_kernel(*refs, bsz, t, aliased):
    (u_ref, m_ref, wre_ref, wim_ref, ztre_ref, ztim_ref, are_ref, aim_ref, d_ref, glu_ref, h0re_ref, h0im_ref) = refs[:12]
    y_ref, hre_ref, him_ref, xs_ref, ug_ref, vre_ref, vim_ref, pre_ref, pim_ref = refs[12 + aliased:]
    rows = u_ref.shape[0] // t
    chunks = rows // bsz
    n_grp, tc = m_ref.shape[0], m_ref.shape[1]
    c = tc // t

    def token_rows(b, s):
        return pl.ds(b * chunks * t + s, chunks, stride=t)

    def chunk_rows(b):
        return pl.ds(b, chunks, stride=bsz)

    for s in range(t):
        if chunks == 1:
            xs_ref[s] = u_ref[pl.ds(s, rows, stride=t), :]
        else:
            for b in range(bsz):
                xs_ref[s, chunk_rows(b), :] = u_ref[token_rows(b, s), :]
    for gi in range(n_grp):
        lanes = slice(gi * c, (gi + 1) * c)
        u = jnp.concatenate([xs_ref[s, :, lanes] for s in range(t)], axis=1)
        ug_ref[gi] = u
        vre_ref[gi] = _dot(u.astype(BF16), wre_ref[gi])
        vim_ref[gi] = _dot(u.astype(BF16), wim_ref[gi])

    per_tile = max(1, 8 // bsz)
    tile_rows = per_tile * bsz

    def chunk_tile(i, h):
        sel = pl.ds(pl.multiple_of(i * tile_rows, tile_rows), tile_rows)
        nxt = []
        for gi in range(n_grp):
            h_re, h_im = h[2 * gi], h[2 * gi + 1]
            v_re, v_im = vre_ref[gi, sel, :], vim_ref[gi, sel, :]
            a_re, a_im = are_ref[gi], aim_ref[gi]
            ent_re, ent_im = [], []
            for j in range(per_tile):
                ent_re.append(h_re)
                ent_im.append(h_im)
                part = slice(j * bsz, (j + 1) * bsz)
                h_re, h_im = a_re * h_re - a_im * h_im + v_re[part], a_re * h_im + a_im * h_re + v_im[part]
            pre_ref[gi, sel, :] = jnp.concatenate(ent_re, axis=0)
            pim_ref[gi, sel, :] = jnp.concatenate(ent_im, axis=0)
            nxt += [h_re, h_im]
        return tuple(nxt)

    h = tuple(r[gi] for gi in range(n_grp) for r in (h0re_ref, h0im_ref))
    n_tiles = chunks // per_tile
    h = lax.fori_loop(0, n_tiles, chunk_tile, h, unroll=min(n_tiles, 4))
    for gi in range(n_grp):
        hre_ref[gi] = h[2 * gi]
        him_ref[gi] = h[2 * gi + 1]

    for gi in range(n_grp):
        u = ug_ref[gi]
        y = (_dot(u.astype(BF16), m_ref[gi]) + _dot_nt(pre_ref[gi].astype(BF16), ztre_ref[gi])
             + _dot_nt(pim_ref[gi].astype(BF16), ztim_ref[gi]) + d_ref[gi] * u)
        y = jax.nn.gelu(y)
        y = y * jax.nn.sigmoid(_dot(y.astype(BF16), glu_ref[gi]))
        for s in range(t):
            xs_ref[s, :, gi * c:(gi + 1) * c] = y[:, s * c:(s + 1) * c]
    for s in range(t):
        if chunks == 1:
            y_ref[pl.ds(s, rows, stride=t), :] = xs_ref[s]
        else:
            for b in range(bsz):
                y_ref[token_rows(b, s), :] = xs_ref[s, chunk_rows(b), :]


def _ssm_mix(z, y_prev, tab, h0_re, h0_im, row0, n_rows, bsz, t):
    g, tc, _ = tab["m"].shape
    p = h0_re.shape[2]
    n_tok = z.shape[0]
    c = tc // t
    width = g * c
    lanes = LANES
    gps = lanes // c
    rows = n_rows // t
    blk = lambda *shape: pl.BlockSpec((gps,) + shape, lambda i: (i,) + (0,) * len(shape))
    tok = pl.BlockSpec((n_rows, lanes), lambda i: (row0 // n_rows, i))
    aliased = y_prev is not None
    in_specs = [tok, blk(tc, tc), blk(tc, p), blk(tc, p), blk(tc, p), blk(tc, p), blk(1, p), blk(1, p), blk(1, tc),
                blk(tc, tc), blk(bsz, p), blk(bsz, p)]
    args = [z, tab["m"], tab["w_re"], tab["w_im"], tab["zt_re"], tab["zt_im"], tab["a_re"], tab["a_im"], tab["d"],
            tab["glu"], h0_re, h0_im]
    if aliased:
        in_specs.append(pl.BlockSpec(memory_space=pl.ANY))
        args.append(y_prev)
    return pl.pallas_call(
        functools.partial(_ssm_mix_kernel, bsz=bsz, t=t, aliased=aliased),
        grid=(g // gps,),
        in_specs=in_specs,
        out_specs=[tok, blk(bsz, p), blk(bsz, p)],
        out_shape=[jax.ShapeDtypeStruct((n_tok, width), F32), jax.ShapeDtypeStruct((g, bsz, p), F32),
                   jax.ShapeDtypeStruct((g, bsz, p), F32)],
        scratch_shapes=[pltpu.VMEM((t, rows, lanes), F32), pltpu.VMEM((gps, rows, tc), F32)]
        + [pltpu.VMEM((gps, rows, p), F32)] * 4,
        input_output_aliases={len(args) - 1: 0} if aliased else {},
        compiler_params=_compiler_params("parallel"),
        name="ssm_mix",
    )(*args)


def _bias_table(q_pos, k_pos, n_heads):
    slopes = np.exp2(-8.0 * np.arange(1, n_heads + 1, dtype=np.float64) / n_heads)
    dist = q_pos[:, None] - k_pos[None, :]
    valid = (dist >= 0) & (dist < WINDOW) & (k_pos[None, :] >= 0)
    bias = -slopes[:, None, None] * dist[None].astype(np.float64)
    return np.where(valid[None], bias, -np.inf).astype(np.float32)


def _head_rms(x, gain):
    heads = [_rms(x[:, i:i + HEAD_DIM], gain) for i in range(0, x.shape[1], HEAD_DIM)]
    return jnp.concatenate(heads, axis=1)


def _softmax_sink_pv(logits, sink, v):
    m = jnp.maximum(jnp.max(logits, axis=-1, keepdims=True), sink)
    p = jnp.exp(logits - m)
    denom = jnp.sum(p, axis=-1, keepdims=True) + jnp.exp(sink - m)
    return _dot(p.astype(BF16), v) / denom


def _prompt_attn_kernel(sinks_ref, q_ref, k_ref, v_ref, qg_ref, kg_ref, bias_ref, y_ref, ks_ref, vs_ref,
                        kp_ref, vp_ref, *, n_heads):
    n = pl.program_id(1)

    @pl.when(n == 0)
    def _():
        kp_ref[...] = jnp.zeros_like(kp_ref)
        vp_ref[...] = jnp.zeros_like(vp_ref)

    q_all = q_ref[...]
    blk = q_all.shape[0]
    half = 2 * HEAD_DIM
    assert n_heads // N_KV_HEADS == 4 and half == 128
    k_cur = _head_rms(k_ref[...], kg_ref[...])
    v_cur = v_ref[...]
    k2 = jnp.concatenate([kp_ref[...], k_cur], axis=0)
    v2t = jnp.concatenate([vp_ref[...], v_cur], axis=0).T.astype(BF16)
    qs = (q_all * qg_ref[...] * QK_SCALE).astype(BF16)
    qsq = q_all * q_all
    table = jnp.minimum(n, 1)
    from_prev = (lax.broadcasted_iota(jnp.int32, (blk, blk), 0) > lax.broadcasted_iota(jnp.int32, (blk, blk), 1))
    low_lanes = lax.broadcasted_iota(jnp.int32, (2 * blk, half), 1) < HEAD_DIM
    low8 = lax.broadcasted_iota(jnp.int32, (8, half), 1) < HEAD_DIM
    pick = jnp.concatenate([jnp.where(low8, 1.0, 0.0), jnp.where(low8, 0.0, 1.0)], axis=0)
    for kv in range(N_KV_HEADS):
        k_tile = k2[:, (kv // 2) * half:(kv // 2 + 1) * half]
        k_own = jnp.where(low_lanes if kv % 2 == 0 else ~low_lanes, k_tile, 0.0)
        k_swap = pltpu.roll(k_own, HEAD_DIM, axis=1)
        k_lhs = jnp.concatenate([k_own, k_swap] if kv % 2 == 0 else [k_swap, k_own], axis=0).astype(BF16)
        tiles = slice(2 * kv * half, (2 * kv + 1) * half), slice((2 * kv + 1) * half, (2 * kv + 2) * half)
        q_rows = jnp.concatenate([qs[:, t] for t in tiles], axis=0)
        qsq_rows = jnp.concatenate([qsq[:, t] for t in tiles], axis=0)
        s = _dot_nt(k_lhs, q_rows)
        ssq = _dot_nt(pick, qsq_rows, HIGHEST)
        probs, inv_denoms = [], []
        for i in range(4):
            odd, tile = i % 2, i // 2
            lanes = slice(tile * blk, (tile + 1) * blk)
            s_h = s[odd * 2 * blk:(odd + 1) * 2 * blk, lanes]
            q_rms = lax.rsqrt(ssq[odd * 8:odd * 8 + 1, lanes] * (1.0 / HEAD_DIM) + EPS)
            logits = jnp.where(from_prev, s_h[:blk], s_h[blk:]) * q_rms + bias_ref[table, 4 * kv + i]
            sink = sinks_ref[4 * kv + i]
            m = jnp.maximum(jnp.max(logits, axis=0, keepdims=True), sink)
            p = jnp.exp(logits - m)
            inv_denoms.append(1.0 / (jnp.sum(p, axis=0, keepdims=True) + jnp.exp(sink - m)))
            probs.append(jnp.concatenate([jnp.where(from_prev, p, 0.0), jnp.where(from_prev, 0.0, p)], axis=0))
        o = _dot(v2t[kv * HEAD_DIM:(kv + 1) * HEAD_DIM], jnp.concatenate(probs, axis=1).astype(BF16))
        for tile in range(2):
            pair = [o[:, i * blk:(i + 1) * blk] * inv_denoms[i] for i in (2 * tile, 2 * tile + 1)]
            y_ref[:, tiles[tile]] = jnp.concatenate(pair, axis=0).T
    kp_ref[...] = k_cur
    vp_ref[...] = v_cur

    @pl.when(n == pl.num_programs(1) - 1)
    def _():
        ks_ref[0] = k_cur
        vs_ref[0] = v_cur


def _prompt_attn(z, sinks, q_gain, k_gain, bsz, seq, n_tok, col_q, attn_w, kv_w):
    blk = WINDOW
    nb = seq // blk
    n_heads = attn_w // HEAD_DIM
    row = lambda b, n: b * nb + n
    q_blk, k_blk, v_blk = col_q // attn_w, (col_q + attn_w) // kv_w, (col_q + attn_w + kv_w) // kv_w
    qi, kj = np.arange(blk)[:, None], np.arange(blk)[None, :]
    first, later = [np.take_along_axis(_bias_table(first_pos + np.arange(blk), first_pos - blk + np.arange(2 * blk), n_heads),
                                       np.broadcast_to(np.where(kj > qi, kj, kj + blk), (n_heads, blk, blk)), axis=2)
                    for first_pos in (0, blk)]
    bias = np.stack([first, later]).swapaxes(2, 3)
    return pl.pallas_call(
        functools.partial(_prompt_attn_kernel, n_heads=n_heads),
        grid=(bsz, nb),
        in_specs=[
            pl.BlockSpec(memory_space=pltpu.SMEM),
            pl.BlockSpec((blk, attn_w), lambda b, n: (row(b, n), q_blk)),
            pl.BlockSpec((blk, kv_w), lambda b, n: (row(b, n), k_blk)),
            pl.BlockSpec((blk, kv_w), lambda b, n: (row(b, n), v_blk)),
            pl.BlockSpec((1, attn_w), lambda b, n: (0, 0)),
            pl.BlockSpec((1, HEAD_DIM), lambda b, n: (0, 0)),
            pl.BlockSpec(bias.shape, lambda b, n: (0, 0, 0, 0)),
        ],
        out_specs=[
            pl.BlockSpec((blk, attn_w), lambda b, n: (row(b, n), 0)),
            pl.BlockSpec((1, blk, kv_w), lambda b, n: (b, 0, 0)),
            pl.BlockSpec((1, blk, kv_w), lambda b, n: (b, 0, 0)),
        ],
        out_shape=[jax.ShapeDtypeStruct((n_tok, attn_w), F32),
                   jax.ShapeDtypeStruct((bsz, blk, kv_w), F32),
                   jax.ShapeDtypeStruct((bsz, blk, kv_w), F32)],
        scratch_shapes=[pltpu.VMEM((blk, kv_w), F32)] * 2,
        compiler_params=_compiler_params("parallel", "arbitrary"),
        name="prompt_attn",
    )(sinks, z, z, z, jnp.tile(q_gain, n_heads).reshape(1, attn_w), k_gain.reshape(1, HEAD_DIM), jnp.asarray(bias))


def _sample_attn_kernel(*refs, dec_seq, aliased):
    q_ref, k_ref, v_ref, ck_ref, cv_ref, qg_ref, kg_ref, bias_ref, sink_ref = refs[:9]
    y_ref, ks_ref, vs_ref, kbuf_ref, vbuf_ref = refs[9 + 2 * aliased:]
    cache_w = ck_ref.shape[1]
    k_new = _head_rms(k_ref[...], kg_ref[...])
    v_new = v_ref[...]
    pad = jnp.zeros((kbuf_ref.shape[1] - cache_w, kbuf_ref.shape[2]), F32)
    for b in range(q_ref.shape[0]):
        rows = slice(b * dec_seq, (b + 1) * dec_seq)
        kbuf_ref[b, 0:cache_w, :] = ck_ref[b]
        vbuf_ref[b, 0:cache_w, :] = cv_ref[b]
        kbuf_ref[b, cache_w:, :] = pad
        vbuf_ref[b, cache_w:, :] = pad
        kbuf_ref[b, cache_w:cache_w + dec_seq, :] = k_new[rows]
        vbuf_ref[b, cache_w:cache_w + dec_seq, :] = v_new[rows]
        q = q_ref[b]
        q = q * lax.rsqrt(jnp.sum(q * q, axis=-1, keepdims=True) * (1.0 / HEAD_DIM) + EPS) * qg_ref[...]
        logits = _dot_nt((q * QK_SCALE).astype(BF16), kbuf_ref[b].astype(BF16)) + bias_ref[...]
        y_ref[b] = _softmax_sink_pv(logits, sink_ref[...], vbuf_ref[b].astype(BF16))
        ks_ref[b, 0:cache_w - dec_seq, :] = ck_ref[b, dec_seq:cache_w, :]
        vs_ref[b, 0:cache_w - dec_seq, :] = cv_ref[b, dec_seq:cache_w, :]
        ks_ref[b, cache_w - dec_seq:cache_w, :] = k_new[rows]
        vs_ref[b, cache_w - dec_seq:cache_w, :] = v_new[rows]


def _sample_attn(z, cache_k, cache_v, windows, layer, sinks, q_gain, k_gain, row0, dec_seq, col_q, attn_w, kv_w):
    _, dec_batch, cache_w, _ = cache_k.shape
    n_heads = attn_w // HEAD_DIM
    q_per_kv = n_heads // N_KV_HEADS
    tile = SAMPLE_BATCH_TILE
    r0 = row0 // (tile * dec_seq)
    k_blk, v_blk = (col_q + attn_w) // kv_w, (col_q + attn_w + kv_w) // kv_w
    keys = 2 * cache_w
    k_pos = np.full(keys, -1)
    k_pos[:cache_w] = PAST_LEN - cache_w + np.arange(cache_w)
    k_pos[cache_w:cache_w + dec_seq] = PAST_LEN + np.arange(dec_seq)
    bias = _bias_table(PAST_LEN + np.arange(dec_seq), k_pos, n_heads).reshape(n_heads * dec_seq, keys)
    sink_col = jnp.repeat(sinks, dec_seq).reshape(n_heads * dec_seq, 1)
    q = z[row0:, col_q:col_q + attn_w].reshape(dec_batch, dec_seq, N_KV_HEADS, q_per_kv, 1, HEAD_DIM)
    own_kv = np.eye(N_KV_HEADS, dtype=np.float32)[None, :, None, None, :, None]
    q = (q.transpose(0, 2, 3, 1, 4, 5) * own_kv).reshape(dec_batch, n_heads * dec_seq, kv_w)
    window = pl.BlockSpec((None, tile, cache_w, kv_w), lambda i: (layer, i, 0, 0))
    aliased = windows is not None
    in_specs = [
        pl.BlockSpec((tile, n_heads * dec_seq, kv_w), lambda i: (i, 0, 0)),
        pl.BlockSpec((tile * dec_seq, kv_w), lambda i: (r0 + i, k_blk)),
        pl.BlockSpec((tile * dec_seq, kv_w), lambda i: (r0 + i, v_blk)),
        window,
        window,
        pl.BlockSpec((1, kv_w), lambda i: (0, 0)),
        pl.BlockSpec((1, HEAD_DIM), lambda i: (0, 0)),
        pl.BlockSpec(bias.shape, lambda i: (0, 0)),
        pl.BlockSpec(sink_col.shape, lambda i: (0, 0)),
    ]
    args = [q, z, z, cache_k, cache_v, jnp.tile(q_gain, N_KV_HEADS).reshape(1, kv_w), k_gain.reshape(1, HEAD_DIM),
            jnp.asarray(bias), sink_col]
    if aliased:
        in_specs += [pl.BlockSpec(memory_space=pl.ANY)] * 2
        args += list(windows)
    y, ks, vs = pl.pallas_call(
        functools.partial(_sample_attn_kernel, dec_seq=dec_seq, aliased=aliased),
        grid=(dec_batch // tile,),
        in_specs=in_specs,
        out_specs=[pl.BlockSpec((tile, n_heads * dec_seq, kv_w), lambda i: (i, 0, 0)), window, window],
        out_shape=[jax.ShapeDtypeStruct(q.shape, F32),
                   jax.ShapeDtypeStruct(cache_k.shape, F32),
                   jax.ShapeDtypeStruct(cache_v.shape, F32)],
        scratch_shapes=[pltpu.VMEM((tile, keys, kv_w), F32)] * 2,
        input_output_aliases={len(args) - 2: 1, len(args) - 1: 2} if aliased else {},
        compiler_params=_compiler_params("parallel"),
        name="sample_attn",
    )(*args)
    y = y.reshape(dec_batch, N_KV_HEADS, q_per_kv, dec_seq, N_KV_HEADS, HEAD_DIM)
    y = jnp.stack([y[:, kv, :, :, kv] for kv in range(N_KV_HEADS)], axis=1)
    return y.transpose(0, 3, 1, 2, 4).reshape(dec_batch * dec_seq, attn_w), ks, vs


def kernel(x_prompt, x_sample, cache_k, cache_v, state_ssm_re, state_ssm_im, ffn1_norm, ffn1_w_gate, ffn1_w_up, ffn1_w_down, mix_norm, w_in, ssm_A_re, ssm_A_im, ssm_B_re, ssm_B_im, ssm_C_re, ssm_C_im, ssm_D, ssm_log_dt, ssm_w_glu, q_norm, k_norm, sinks, ssm_out_norm, attn_out_norm, w_out, ffn2_norm, ffn2_w_gate, ffn2_w_up, ffn2_w_down):
    bsz, seq, d = x_prompt.shape
    dec_batch, dec_seq, _ = x_sample.shape
    depth = w_in.shape[0]
    ssm_w = ssm_A_re.shape[1] * SSM_GROUP_CH
    attn_w = w_out.shape[1] - ssm_w
    kv_w = N_KV_HEADS * HEAD_DIM
    cache_w = cache_k.shape[2]
    n_prompt = bsz * seq
    n_tok = n_prompt + dec_batch * dec_seq

    x = jnp.concatenate([x_prompt.reshape(n_prompt, d), x_sample.reshape(dec_batch * dec_seq, d)], axis=0)
    n_groups = ssm_w // SSM_GROUP_CH
    n_merged = n_groups // SSM_MERGE
    zeros_state = jnp.zeros((n_merged, bsz, SSM_MERGE * SSM_STATE), F32)
    outs = [[] for _ in range(6)]
    cache_k_flat = cache_k.reshape(depth, dec_batch, cache_w, kv_w)
    cache_v_flat = cache_v.reshape(depth, dec_batch, cache_w, kv_w)
    windows = None
    bf = lambda w: w.astype(BF16)
    ffn1_w, ffn2_w = [tuple(map(bf, ws)) for ws in ((ffn1_w_gate, ffn1_w_up, ffn1_w_down),
                                                    (ffn2_w_gate, ffn2_w_up, ffn2_w_down))]
    w_in_bf, w_out_bf = bf(w_in), bf(w_out)
    for l in range(depth):
        x = _ffn(x, ffn1_norm[l], *ffn1_w, layer=l)
        z = _norm_matmul(x, mix_norm[l], w_in_bf, layer=l, tn=w_in.shape[2])

        prep = _ssm_prep(*_merge_groups(ssm_A_re[l], ssm_A_im[l], ssm_log_dt[l], ssm_B_re[l], ssm_B_im[l],
                                        ssm_C_re[l], ssm_C_im[l], ssm_D[l], ssm_w_glu[l], SSM_MERGE))
        tab_p = _ssm_tables(prep, PROMPT_CHUNK)
        tab_s = _ssm_tables(prep, dec_seq)
        grp_major = lambda a: a.reshape(a.shape[0], n_merged, SSM_MERGE * SSM_STATE).transpose(1, 0, 2)
        seq_major = lambda a: a.transpose(1, 0, 2).reshape(a.shape[1], n_groups, SSM_STATE)
        y_ssm, hp_re, hp_im = _ssm_mix(z, None, tab_p, zeros_state, zeros_state, 0, n_prompt, bsz, PROMPT_CHUNK)
        y_ssm, hs_re, hs_im = _ssm_mix(z, y_ssm, tab_s, grp_major(state_ssm_re[l]), grp_major(state_ssm_im[l]),
                                       n_prompt, dec_batch * dec_seq, dec_batch, dec_seq)
        hp_re, hp_im, hs_re, hs_im = map(seq_major, (hp_re, hp_im, hs_re, hs_im))

        y_attn, pk, pv = _prompt_attn(z, sinks[l], q_norm[l], k_norm[l], bsz, seq, n_tok, ssm_w, attn_w, kv_w)
        ya_s, *windows = _sample_attn(z, cache_k_flat, cache_v_flat, windows, l, sinks[l], q_norm[l], k_norm[l],
                                      n_prompt, dec_seq, ssm_w, attn_w, kv_w)
        y_attn = lax.dynamic_update_slice(y_attn, ya_s, (n_prompt, 0))

        x = _out_proj(x, y_ssm, y_attn, ssm_out_norm[l], attn_out_norm[l], w_out_bf, layer=l)
        x = _ffn(x, ffn2_norm[l], *ffn2_w, layer=l)

        kv_shape = (-1, N_KV_HEADS, HEAD_DIM)
        for dst, val in zip(outs, (pk.reshape(bsz, *kv_shape), pv.reshape(bsz, *kv_shape), hp_re, hp_im, hs_re, hs_im)):
            dst.append(val)
    y_prompt = x[:n_prompt].reshape(bsz, seq, d)
    y_sample = x[n_prompt:].reshape(dec_batch, dec_seq, d)
    pk, pv, hp_re, hp_im, hs_re, hs_im = (jnp.stack(o) for o in outs)
    sk, sv = (w.reshape(cache_k.shape) for w in windows)
    return (y_prompt, y_sample, pk, pv, hp_re, hp_im, sk, sv, hs_re, hs_im)
```

```python
import functools

import jax
import jax.numpy as jnp
import numpy as np
from jax import lax
from jax.experimental import pallas as pl
from jax.experimental.pallas import tpu as pltpu

F32 = jnp.float32
BF16 = jnp.bfloat16

EPS = 1e-6
SSM_GROUP_CH = 16
SSM_STATE = 64
HEAD_DIM = 64
N_KV_HEADS = 4
WINDOW = 128
PAST_LEN = 8192
QK_SCALE = HEAD_DIM ** -0.5
PROMPT_CHUNK = 16

TOKEN_TILE = 512
FFN_TOKEN_TILE = 1088
FFN_VMEM_LIMIT_BYTES = 60 * 1024 * 1024
FF_TILE = 256
SAMPLE_BATCH_TILE = 8
VMEM_LIMIT_BYTES = 48 * 1024 * 1024

HIGHEST = lax.Precision.HIGHEST
LANES = 128
SSM_MERGE = 2


def _rms(x, gain):
    return x * lax.rsqrt(jnp.mean(x * x, axis=-1, keepdims=True) + EPS) * gain


def _dot(a, b, precision=None):
    return jnp.dot(a, b, preferred_element_type=F32, precision=precision)


def _dot_nt(a, b, precision=None):
    return lax.dot_general(a, b, (((1,), (1,)), ((), ())), preferred_element_type=F32, precision=precision)


def _compiler_params(*semantics):
    return pltpu.CompilerParams(dimension_semantics=semantics, vmem_limit_bytes=VMEM_LIMIT_BYTES)


def _ffn_kernel(x_ref, g_ref, wg_ref, wu_ref, wd_ref, o_ref, xn_ref):
    j = pl.program_id(1)

    @pl.when(j == 0)
    def _():
        xn_ref[...] = _rms(x_ref[...], g_ref[...]).astype(BF16)
        o_ref[...] = jnp.zeros_like(o_ref)

    xn = xn_ref[...]
    gate = _dot(xn, wg_ref[...].astype(BF16))
    up = _dot(xn, wu_ref[...].astype(BF16))
    h = (gate * jax.nn.sigmoid(gate) * up).astype(BF16)
    o_ref[...] += _dot(h, wd_ref[...].astype(BF16))

    @pl.when(j == pl.num_programs(1) - 1)
    def _():
        o_ref[...] = x_ref[...] + 0.5 * o_ref[...]


def _ffn(x, gain, w_gate, w_up, w_down, layer):
    n_tok, d = x.shape
    d_ff = w_gate.shape[2]
    tm = FFN_TOKEN_TILE if n_tok % FFN_TOKEN_TILE == 0 else TOKEN_TILE
    tf = FF_TILE
    return pl.pallas_call(
        _ffn_kernel,
        grid=(n_tok // tm, d_ff // tf),
        in_specs=[
            pl.BlockSpec((tm, d), lambda i, j: (i, 0)),
            pl.BlockSpec((1, d), lambda i, j: (0, 0)),
            pl.BlockSpec((None, d, tf), lambda i, j: (layer, 0, j)),
            pl.BlockSpec((None, d, tf), lambda i, j: (layer, 0, j)),
            pl.BlockSpec((None, tf, d), lambda i, j: (layer, j, 0)),
        ],
        out_specs=pl.BlockSpec((tm, d), lambda i, j: (i, 0)),
        out_shape=jax.ShapeDtypeStruct((n_tok, d), F32),
        scratch_shapes=[pltpu.VMEM((tm, d), BF16)],
        compiler_params=pltpu.CompilerParams(dimension_semantics=("parallel", "arbitrary"),
                                             vmem_limit_bytes=FFN_VMEM_LIMIT_BYTES),
        name="ffn",
    )(x, gain.reshape(1, d), w_gate, w_up, w_down)


def _norm_matmul_kernel(x_ref, g_ref, w_ref, o_ref, xn_ref):
    @pl.when(pl.program_id(1) == 0)
    def _():
        xn_ref[...] = _rms(x_ref[...], g_ref[...]).astype(BF16)

    o_ref[...] = _dot(xn_ref[...], w_ref[...])


def _norm_matmul(x, gain, w, layer, tn):
    n_tok, d = x.shape
    n_out = w.shape[2]
    tm = TOKEN_TILE
    return pl.pallas_call(
        _norm_matmul_kernel,
        grid=(n_tok // tm, n_out // tn),
        in_specs=[
            pl.BlockSpec((tm, d), lambda i, j: (i, 0)),
            pl.BlockSpec((1, d), lambda i, j: (0, 0)),
            pl.BlockSpec((None, d, tn), lambda i, j: (layer, 0, j)),
        ],
        out_specs=pl.BlockSpec((tm, tn), lambda i, j: (i, j)),
        out_shape=jax.ShapeDtypeStruct((n_tok, n_out), F32),
        scratch_shapes=[pltpu.VMEM((tm, d), BF16)],
        compiler_params=_compiler_params("parallel", "arbitrary"),
        name="in_proj",
    )(x, gain.reshape(1, d), w)


def _out_proj_kernel(x_ref, ys_ref, ya_ref, gs_ref, ga_ref, ws_ref, wa_ref, o_ref):
    ys = _rms(ys_ref[...], gs_ref[...]).astype(BF16)
    ya = _rms(ya_ref[...], ga_ref[...]).astype(BF16)
    o_ref[...] = x_ref[...] + _dot(ys, ws_ref[...]) + _dot(ya, wa_ref[...])


def _out_proj(x, y_ssm, y_attn, g_ssm, g_attn, w_out, layer):
    n_tok, d = x.shape
    ssm_w, attn_w = y_ssm.shape[1], y_attn.shape[1]
    tm = TOKEN_TILE
    return pl.pallas_call(
        _out_proj_kernel,
        grid=(n_tok // tm,),
        in_specs=[
            pl.BlockSpec((tm, d), lambda i: (i, 0)),
            pl.BlockSpec((tm, ssm_w), lambda i: (i, 0)),
            pl.BlockSpec((tm, attn_w), lambda i: (i, 0)),
            pl.BlockSpec((1, ssm_w), lambda i: (0, 0)),
            pl.BlockSpec((1, attn_w), lambda i: (0, 0)),
            pl.BlockSpec((None, ssm_w, d), lambda i: (layer, 0, 0)),
            pl.BlockSpec((None, attn_w, d), lambda i: (layer, ssm_w // attn_w, 0)),
        ],
        out_specs=pl.BlockSpec((tm, d), lambda i: (i, 0)),
        out_shape=jax.ShapeDtypeStruct((n_tok, d), F32),
        compiler_params=_compiler_params("parallel"),
        name="out_proj",
    )(x, y_ssm, y_attn, g_ssm.reshape(1, ssm_w), g_attn.reshape(1, attn_w), w_out, w_out)


def _ssm_prep_kernel(*refs):
    for gi in range(refs[0].shape[0]):
        _ssm_prep_group(*[r.at[pl.ds(gi, 1)] for r in refs])


def _ssm_prep_group(are_ref, aim_ref, ldt_ref, btre_ref, btim_ref, cre_ref, cim_ref, d_ref, wglu_ref,
                    m_ref, wre_ref, wim_ref, ztre_ref, ztim_ref, glu_ref, dt_ref, powre_ref, powim_ref):
    a_re, a_im = are_ref[0], aim_ref[0]
    dt = jnp.exp(ldt_ref[0])
    mag = jnp.exp(a_re * dt)
    lb_re, lb_im = mag * jnp.cos(a_im * dt), mag * jnp.sin(a_im * dt)
    den = a_re * a_re + a_im * a_im
    q_re = ((lb_re - 1.0) * a_re + lb_im * a_im) / den
    q_im = (lb_im * a_re - (lb_re - 1.0) * a_im) / den
    bt_re, bt_im = btre_ref[0], btim_ref[0]
    bb_re = q_re * bt_re - q_im * bt_im
    bb_im = q_re * bt_im + q_im * bt_re
    c_re, c_im = cre_ref[0], cim_ref[0]

    pw_re, pw_im = [jnp.ones_like(lb_re)], [jnp.zeros_like(lb_im)]
    for _ in range(PROMPT_CHUNK):
        pr, pi = pw_re[-1], pw_im[-1]
        pw_re.append(pr * lb_re - pi * lb_im)
        pw_im.append(pr * lb_im + pi * lb_re)

    steps = PROMPT_CHUNK
    c = c_re.shape[0]
    w_re = jnp.concatenate([bb_re * pw_re[steps - 1 - s] - bb_im * pw_im[steps - 1 - s] for s in range(steps)], axis=0)
    w_im = jnp.concatenate([bb_re * pw_im[steps - 1 - s] + bb_im * pw_re[steps - 1 - s] for s in range(steps)], axis=0)
    wre_ref[0] = w_re.astype(BF16)
    wim_ref[0] = w_im.astype(BF16)
    ztre_ref[0] = jnp.concatenate([c_re * pw_re[t + 1] - c_im * pw_im[t + 1] for t in range(steps)], axis=0).astype(BF16)
    ztim_ref[0] = jnp.concatenate([-(c_re * pw_im[t + 1] + c_im * pw_re[t + 1]) for t in range(steps)], axis=0).astype(BF16)

    lag = _dot_nt(w_re, c_re, HIGHEST) - _dot_nt(w_im, c_im, HIGHEST)
    cols = []
    for t in range(steps):
        live = lag[(steps - 1 - t) * c:, :]
        cols.append(live if t == steps - 1 else
                    jnp.concatenate([live, jnp.zeros(((steps - 1 - t) * c, c), F32)], axis=0))
    m_ref[0] = jnp.concatenate(cols, axis=1).astype(BF16)

    tiled = jnp.concatenate([jnp.concatenate([wglu_ref[0]] * steps, axis=0)] * steps, axis=1)
    shift = c.bit_length() - 1
    row_step = lax.shift_right_logical(lax.broadcasted_iota(jnp.int32, tiled.shape, 0), shift)
    col_step = lax.shift_right_logical(lax.broadcasted_iota(jnp.int32, tiled.shape, 1), shift)
    glu_ref[0] = jnp.where(row_step == col_step, tiled, 0.0).astype(BF16)
    dt_ref[0] = jnp.concatenate([d_ref[0]] * steps, axis=1)

    pad = jnp.zeros((powre_ref.shape[1] - (steps + 1), a_re.shape[1]), F32)
    powre_ref[0] = jnp.concatenate(pw_re + [pad], axis=0)
    powim_ref[0] = jnp.concatenate(pw_im + [pad], axis=0)


def _merge_groups(a_re, a_im, log_dt, b_re, b_im, c_re, c_im, d_skip, w_glu, n):
    g, p = a_re.shape
    gm = g // n
    own = np.eye(n, dtype=np.float32)[None, :, :, None, None]

    def block_diag(x):
        r, q = x.shape[1:]
        return (x.reshape(gm, n, 1, r, q) * own).transpose(0, 1, 3, 2, 4).reshape(gm, n * r, n * q)

    lanes = lambda x: x.reshape(gm, 1, -1)
    return (lanes(a_re), lanes(a_im), lanes(jnp.repeat(log_dt, p)),
            block_diag(jnp.swapaxes(b_re, 1, 2)), block_diag(jnp.swapaxes(b_im, 1, 2)),
            block_diag(c_re), block_diag(c_im), lanes(d_skip), block_diag(w_glu))


def _ssm_prep(a_re, a_im, log_dt, bt_re, bt_im, c_re, c_im, d_skip, w_glu):
    g, c, p = c_re.shape
    assert c & (c - 1) == 0
    tc = PROMPT_CHUNK * c
    pow_rows = 24
    sds = lambda dtype, *shape: jax.ShapeDtypeStruct((g,) + shape, dtype)
    gps = LANES // c
    gs = lambda *shape: pl.BlockSpec((gps,) + shape, lambda i: (i,) + (0,) * len(shape))
    names = ("m", "w_re", "w_im", "zt_re", "zt_im", "glu", "d", "pow_re", "pow_im")
    outs = pl.pallas_call(
        _ssm_prep_kernel,
        grid=(g // gps,),
        in_specs=[gs(1, p), gs(1, p), gs(1, p), gs(c, p), gs(c, p), gs(c, p), gs(c, p), gs(1, c), gs(c, c)],
        out_specs=[gs(tc, tc), gs(tc, p), gs(tc, p), gs(tc, p), gs(tc, p), gs(tc, tc), gs(1, tc),
                   gs(pow_rows, p), gs(pow_rows, p)],
        out_shape=[sds(BF16, tc, tc), sds(BF16, tc, p), sds(BF16, tc, p), sds(BF16, tc, p), sds(BF16, tc, p),
                   sds(BF16, tc, tc), sds(F32, 1, tc), sds(F32, pow_rows, p), sds(F32, pow_rows, p)],
        compiler_params=_compiler_params("parallel"),
        name="ssm_prep",
    )(a_re, a_im, log_dt, bt_re, bt_im, c_re, c_im, d_skip, w_glu)
    return dict(zip(names, outs))


def _ssm_tables(prep, t):
    n = t * (prep["d"].shape[2] // PROMPT_CHUNK)
    full = prep["m"].shape[1]
    return dict(m=prep["m"][:, :n, :n], w_re=prep["w_re"][:, full - n:], w_im=prep["w_im"][:, full - n:],
                zt_re=prep["zt_re"][:, :n], zt_im=prep["zt_im"][:, :n], glu=prep["glu"][:, :n, :n],
                d=prep["d"][:, :, :n], a_re=prep["pow_re"][:, t:t + 1], a_im=prep["pow_im"][:, t:t + 1])


def _ssm_mix_kernel(*refs, bsz, t, aliased):
    (u_ref, m_ref, wre_ref, wim_ref, ztre_ref, ztim_ref, are_ref, aim_ref, d_ref, glu_ref, h0re_ref, h0im_ref) = refs[:12]
    y_ref, hre_ref, him_ref, xs_ref, ug_ref, vre_ref, vim_ref, pre_ref, pim_ref = refs[12 + aliased:]
    rows = u_ref.shape[0] // t
    chunks = rows // bsz
    n_grp, tc = m_ref.shape[0], m_ref.shape[1]
    c = tc // t

    def token_rows(b, s):
        return pl.ds(b * chunks * t + s, chunks, stride=t)

    def chunk_rows(b):
        return pl.ds(b, chunks, stride=bsz)

    for s in range(t):
        if chunks == 1:
            xs_ref[s] = u_ref[pl.ds(s, rows, stride=t), :]
        else:
            for b in range(bsz):
                xs_ref[s, chunk_rows(b), :] = u_ref[token_rows(b, s), :]
    for gi in range(n_grp):
        lanes = slice(gi * c, (gi + 1) * c)
        u = jnp.concatenate([xs_ref[s, :, lanes] for s in range(t)], axis=1)
        ug_ref[gi] = u
        vre_ref[gi] = _dot(u.astype(BF16), wre_ref[gi])
        vim_ref[gi] = _dot(u.astype(BF16), wim_ref[gi])

    per_tile = max(1, 8 // bsz)
    tile_rows = per_tile * bsz

    def chunk_tile(i, h):
        sel = pl.ds(pl.multiple_of(i * tile_rows, tile_rows), tile_rows)
        nxt = []
        for gi in range(n_grp):
            h_re, h_im = h[2 * gi], h[2 * gi + 1]
            v_re, v_im = vre_ref[gi, sel, :], vim_ref[gi, sel, :]
            a_re, a_im = are_ref[gi], aim_ref[gi]
            ent_re, ent_im = [], []
            for j in range(per_tile):
                ent_re.append(h_re)
                ent_im.append(h_im)
                part = slice(j * bsz, (j + 1) * bsz)
                h_re, h_im = a_re * h_re - a_im * h_im + v_re[part], a_re * h_im + a_im * h_re + v_im[part]
            pre_ref[gi, sel, :] = jnp.concatenate(ent_re, axis=0)
            pim_ref[gi, sel, :] = jnp.concatenate(ent_im, axis=0)
            nxt += [h_re, h_im]
        return tuple(nxt)

    h = tuple(r[gi] for gi in range(n_grp) for r in (h0re_ref, h0im_ref))
    n_tiles = chunks // per_tile
    h = lax.fori_loop(0, n_tiles, chunk_tile, h, unroll=min(n_tiles, 4))
    for gi in range(n_grp):
        hre_ref[gi] = h[2 * gi]
        him_ref[gi] = h[2 * gi + 1]

    for gi in range(n_grp):
        u = ug_ref[gi]
        y = (_dot(u.astype(BF16), m_ref[gi]) + _dot_nt(pre_ref[gi].astype(BF16), ztre_ref[gi])
             + _dot_nt(pim_ref[gi].astype(BF16), ztim_ref[gi]) + d_ref[gi] * u)
        y = jax.nn.gelu(y)
        y = y * jax.nn.sigmoid(_dot(y.astype(BF16), glu_ref[gi]))
        for s in range(t):
            xs_ref[s, :, gi * c:(gi + 1) * c] = y[:, s * c:(s + 1) * c]
    for s in range(t):
        if chunks == 1:
            y_ref[pl.ds(s, rows, stride=t), :] = xs_ref[s]
        else:
            for b in range(bsz):
                y_ref[token_rows(b, s), :] = xs_ref[s, chunk_rows(b), :]


def _ssm_mix(z, y_prev, tab, h0_re, h0_im, row0, n_rows, bsz, t):
    g, tc, _ = tab["m"].shape
    p = h0_re.shape[2]
    n_tok = z.shape[0]
    c = tc // t
    width = g * c
    lanes = LANES
    gps = lanes // c
    rows = n_rows // t
    blk = lambda *shape: pl.BlockSpec((gps,) + shape, lambda i: (i,) + (0,) * len(shape))
    tok = pl.BlockSpec((n_rows, lanes), lambda i: (row0 // n_rows, i))
    aliased = y_prev is not None
    in_specs = [tok, blk(tc, tc), blk(tc, p), blk(tc, p), blk(tc, p), blk(tc, p), blk(1, p), blk(1, p), blk(1, tc),
                blk(tc, tc), blk(bsz, p), blk(bsz, p)]
    args = [z, tab["m"], tab["w_re"], tab["w_im"], tab["zt_re"], tab["zt_im"], tab["a_re"], tab["a_im"], tab["d"],
            tab["glu"], h0_re, h0_im]
    if aliased:
        in_specs.append(pl.BlockSpec(memory_space=pl.ANY))
        args.append(y_prev)
    return pl.pallas_call(
        functools.partial(_ssm_mix_kernel, bsz=bsz, t=t, aliased=aliased),
        grid=(g // gps,),
        in_specs=in_specs,
        out_specs=[tok, blk(bsz, p), blk(bsz, p)],
        out_shape=[jax.ShapeDtypeStruct((n_tok, width), F32), jax.ShapeDtypeStruct((g, bsz, p), F32),
                   jax.ShapeDtypeStruct((g, bsz, p), F32)],
        scratch_shapes=[pltpu.VMEM((t, rows, lanes), F32), pltpu.VMEM((gps, rows, tc), F32)]
        + [pltpu.VMEM((gps, rows, p), F32)] * 4,
        input_output_aliases={len(args) - 1: 0} if aliased else {},
        compiler_params=_compiler_params("parallel"),
        name="ssm_mix",
    )(*args)


def _bias_table(q_pos, k_pos, n_heads):
    slopes = np.exp2(-8.0 * np.arange(1, n_heads + 1, dtype=np.float64) / n_heads)
    dist = q_pos[:, None] - k_pos[None, :]
    valid = (dist >= 0) & (dist < WINDOW) & (k_pos[None, :] >= 0)
    bias = -slopes[:, None, None] * dist[None].astype(np.float64)
    return np.where(valid[None], bias, -np.inf).astype(np.float32)


def _head_rms(x, gain):
    heads = [_rms(x[:, i:i + HEAD_DIM], gain) for i in range(0, x.shape[1], HEAD_DIM)]
    return jnp.concatenate(heads, axis=1)


def _softmax_sink_pv(logits, sink, v):
    m = jnp.maximum(jnp.max(logits, axis=-1, keepdims=True), sink)
    p = jnp.exp(logits - m)
    denom = jnp.sum(p, axis=-1, keepdims=True) + jnp.exp(sink - m)
    return _dot(p.astype(BF16), v) / denom


def _prompt_attn_kernel(sinks_ref, q_ref, k_ref, v_ref, qg_ref, kg_ref, bias_ref, y_ref, ks_ref, vs_ref,
                        kp_ref, vp_ref, *, n_heads):
    n = pl.program_id(1)

    @pl.when(n == 0)
    def _():
        kp_ref[...] = jnp.zeros_like(kp_ref)
        vp_ref[...] = jnp.zeros_like(vp_ref)

    q_all = q_ref[...]
    blk = q_all.shape[0]
    half = 2 * HEAD_DIM
    assert n_heads // N_KV_HEADS == 4 and half == 128
    k_cur = _head_rms(k_ref[...], kg_ref[...])
    v_cur = v_ref[...]
    k2 = jnp.concatenate([kp_ref[...], k_cur], axis=0)
    v2t = jnp.concatenate([vp_ref[...], v_cur], axis=0).T.astype(BF16)
    qs = (q_all * qg_ref[...] * QK_SCALE).astype(BF16)
    qsq = q_all * q_all
    table = jnp.minimum(n, 1)
    from_prev = (lax.broadcasted_iota(jnp.int32, (blk, blk), 0) > lax.broadcasted_iota(jnp.int32, (blk, blk), 1))
    low_lanes = lax.broadcasted_iota(jnp.int32, (2 * blk, half), 1) < HEAD_DIM
    low8 = lax.broadcasted_iota(jnp.int32, (8, half), 1) < HEAD_DIM
    pick = jnp.concatenate([jnp.where(low8, 1.0, 0.0), jnp.where(low8, 0.0, 1.0)], axis=0)
    for kv in range(N_KV_HEADS):
        k_tile = k2[:, (kv // 2) * half:(kv // 2 + 1) * half]
        k_own = jnp.where(low_lanes if kv % 2 == 0 else ~low_lanes, k_tile, 0.0)
        k_swap = pltpu.roll(k_own, HEAD_DIM, axis=1)
        k_lhs = jnp.concatenate([k_own, k_swap] if kv % 2 == 0 else [k_swap, k_own], axis=0).astype(BF16)
        tiles = slice(2 * kv * half, (2 * kv + 1) * half), slice((2 * kv + 1) * half, (2 * kv + 2) * half)
        q_rows = jnp.concatenate([qs[:, t] for t in tiles], axis=0)
        qsq_rows = jnp.concatenate([qsq[:, t] for t in tiles], axis=0)
        s = _dot_nt(k_lhs, q_rows)
        ssq = _dot_nt(pick, qsq_rows, HIGHEST)
        probs, inv_denoms = [], []
        for i in range(4):
            odd, tile = i % 2, i // 2
            lanes = slice(tile * blk, (tile + 1) * blk)
            s_h = s[odd * 2 * blk:(odd + 1) * 2 * blk, lanes]
            q_rms = lax.rsqrt(ssq[odd * 8:odd * 8 + 1, lanes] * (1.0 / HEAD_DIM) + EPS)
            logits = jnp.where(from_prev, s_h[:blk], s_h[blk:]) * q_rms + bias_ref[table, 4 * kv + i]
            sink = sinks_ref[4 * kv + i]
            m = jnp.maximum(jnp.max(logits, axis=0, keepdims=True), sink)
            p = jnp.exp(logits - m)
            inv_denoms.append(1.0 / (jnp.sum(p, axis=0, keepdims=True) + jnp.exp(sink - m)))
            probs.append(jnp.concatenate([jnp.where(from_prev, p, 0.0), jnp.where(from_prev, 0.0, p)], axis=0))
        o = _dot(v2t[kv * HEAD_DIM:(kv + 1) * HEAD_DIM], jnp.concatenate(probs, axis=1).astype(BF16))
        for tile in range(2):
            pair = [o[:, i * blk:(i + 1) * blk] * inv_denoms[i] for i in (2 * tile, 2 * tile + 1)]
            y_ref[:, tiles[tile]] = jnp.concatenate(pair, axis=0).T
    kp_ref[...] = k_cur
    vp_ref[...] = v_cur

    @pl.when(n == pl.num_programs(1) - 1)
    def _():
        ks_ref[0] = k_cur
        vs_ref[0] = v_cur


def _prompt_attn(z, sinks, q_gain, k_gain, bsz, seq, n_tok, col_q, attn_w, kv_w):
    blk = WINDOW
    nb = seq // blk
    n_heads = attn_w // HEAD_DIM
    row = lambda b, n: b * nb + n
    q_blk, k_blk, v_blk = col_q // attn_w, (col_q + attn_w) // kv_w, (col_q + attn_w + kv_w) // kv_w
    qi, kj = np.arange(blk)[:, None], np.arange(blk)[None, :]
    first, later = [np.take_along_axis(_bias_table(first_pos + np.arange(blk), first_pos - blk + np.arange(2 * blk), n_heads),
                                       np.broadcast_to(np.where(kj > qi, kj, kj + blk), (n_heads, blk, blk)), axis=2)
                    for first_pos in (0, blk)]
    bias = np.stack([first, later]).swapaxes(2, 3)
    return pl.pallas_call(
        functools.partial(_prompt_attn_kernel, n_heads=n_heads),
        grid=(bsz, nb),
        in_specs=[
            pl.BlockSpec(memory_space=pltpu.SMEM),
            pl.BlockSpec((blk, attn_w), lambda b, n: (row(b, n), q_blk)),
            pl.BlockSpec((blk, kv_w), lambda b, n: (row(b, n), k_blk)),
            pl.BlockSpec((blk, kv_w), lambda b, n: (row(b, n), v_blk)),
            pl.BlockSpec((1, attn_w), lambda b, n: (0, 0)),
            pl.BlockSpec((1, HEAD_DIM), lambda b, n: (0, 0)),
            pl.BlockSpec(bias.shape, lambda b, n: (0, 0, 0, 0)),
        ],
        out_specs=[
            pl.BlockSpec((blk, attn_w), lambda b, n: (row(b, n), 0)),
            pl.BlockSpec((1, blk, kv_w), lambda b, n: (b, 0, 0)),
            pl.BlockSpec((1, blk, kv_w), lambda b, n: (b, 0, 0)),
        ],
        out_shape=[jax.ShapeDtypeStruct((n_tok, attn_w), F32),
                   jax.ShapeDtypeStruct((bsz, blk, kv_w), F32),
                   jax.ShapeDtypeStruct((bsz, blk, kv_w), F32)],
        scratch_shapes=[pltpu.VMEM((blk, kv_w), F32)] * 2,
        compiler_params=_compiler_params("parallel", "arbitrary"),
        name="prompt_attn",
    )(sinks, z, z, z, jnp.tile(q_gain, n_heads).reshape(1, attn_w), k_gain.reshape(1, HEAD_DIM), jnp.asarray(bias))


def _sample_attn_kernel(*refs, dec_seq, aliased):
    q_ref, k_ref, v_ref, ck_ref, cv_ref, qg_ref, kg_ref, bias_ref, sink_ref = refs[:9]
    y_ref, ks_ref, vs_ref, kbuf_ref, vbuf_ref = refs[9 + 2 * aliased:]
    cache_w = ck_ref.shape[1]
    k_new = _head_rms(k_ref[...], kg_ref[...])
    v_new = v_ref[...]
    pad = jnp.zeros((kbuf_ref.shape[1] - cache_w, kbuf_ref.shape[2]), F32)
    for b in range(q_ref.shape[0]):
        rows = slice(b * dec_seq, (b + 1) * dec_seq)
        kbuf_ref[b, 0:cache_w, :] = ck_ref[b]
        vbuf_ref[b, 0:cache_w, :] = cv_ref[b]
        kbuf_ref[b, cache_w:, :] = pad
        vbuf_ref[b, cache_w:, :] = pad
        kbuf_ref[b, cache_w:cache_w + dec_seq, :] = k_new[rows]
        vbuf_ref[b, cache_w:cache_w + dec_seq, :] = v_new[rows]
        q = q_ref[b]
        q = q * lax.rsqrt(jnp.sum(q * q, axis=-1, keepdims=True) * (1.0 / HEAD_DIM) + EPS) * qg_ref[...]
        logits = _dot_nt((q * QK_SCALE).astype(BF16), kbuf_ref[b].astype(BF16)) + bias_ref[...]
        y_ref[b] = _softmax_sink_pv(logits, sink_ref[...], vbuf_ref[b].astype(BF16))
        ks_ref[b, 0:cache_w - dec_seq, :] = ck_ref[b, dec_seq:cache_w, :]
        vs_ref[b, 0:cache_w - dec_seq, :] = cv_ref[b, dec_seq:cache_w, :]
        ks_ref[b, cache_w - dec_seq:cache_w, :] = k_new[rows]
        vs_ref[b, cache_w - dec_seq:cache_w, :] = v_new[rows]


def _sample_attn(z, cache_k, cache_v, windows, layer, sinks, q_gain, k_gain, row0, dec_seq, col_q, attn_w, kv_w):
    _, dec_batch, cache_w, _ = cache_k.shape
    n_heads = attn_w // HEAD_DIM
    q_per_kv = n_heads // N_KV_HEADS
    tile = SAMPLE_BATCH_TILE
    r0 = row0 // (tile * dec_seq)
    k_blk, v_blk = (col_q + attn_w) // kv_w, (col_q + attn_w + kv_w) // kv_w
    keys = 2 * cache_w
    k_pos = np.full(keys, -1)
    k_pos[:cache_w] = PAST_LEN - cache_w + np.arange(cache_w)
    k_pos[cache_w:cache_w + dec_seq] = PAST_LEN + np.arange(dec_seq)
    bias = _bias_table(PAST_LEN + np.arange(dec_seq), k_pos, n_heads).reshape(n_heads * dec_seq, keys)
    sink_col = jnp.repeat(sinks, dec_seq).reshape(n_heads * dec_seq, 1)
    q = z[row0:, col_q:col_q + attn_w].reshape(dec_batch, dec_seq, N_KV_HEADS, q_per_kv, 1, HEAD_DIM)
    own_kv = np.eye(N_KV_HEADS, dtype=np.float32)[None, :, None, None, :, None]
    q = (q.transpose(0, 2, 3, 1, 4, 5) * own_kv).reshape(dec_batch, n_heads * dec_seq, kv_w)
    window = pl.BlockSpec((None, tile, cache_w, kv_w), lambda i: (layer, i, 0, 0))
    aliased = windows is not None
    in_specs = [
        pl.BlockSpec((tile, n_heads * dec_seq, kv_w), lambda i: (i, 0, 0)),
        pl.BlockSpec((tile * dec_seq, kv_w), lambda i: (r0 + i, k_blk)),
        pl.BlockSpec((tile * dec_seq, kv_w), lambda i: (r0 + i, v_blk)),
        window,
        window,
        pl.BlockSpec((1, kv_w), lambda i: (0, 0)),
        pl.BlockSpec((1, HEAD_DIM), lambda i: (0, 0)),
        pl.BlockSpec(bias.shape, lambda i: (0, 0)),
        pl.BlockSpec(sink_col.shape, lambda i: (0, 0)),
    ]
    args = [q, z, z, cache_k, cache_v, jnp.tile(q_gain, N_KV_HEADS).reshape(1, kv_w), k_gain.reshape(1, HEAD_DIM),
            jnp.asarray(bias), sink_col]
    if aliased:
        in_specs += [pl.BlockSpec(memory_space=pl.ANY)] * 2
        args += list(windows)
    y, ks, vs = pl.pallas_call(
        functools.partial(_sample_attn_kernel, dec_seq=dec_seq, aliased=aliased),
        grid=(dec_batch // tile,),
        in_specs=in_specs,
        out_specs=[pl.BlockSpec((tile, n_heads * dec_seq, kv_w), lambda i: (i, 0, 0)), window, window],
        out_shape=[jax.ShapeDtypeStruct(q.shape, F32),
                   jax.ShapeDtypeStruct(cache_k.shape, F32),
                   jax.ShapeDtypeStruct(cache_v.shape, F32)],
        scratch_shapes=[pltpu.VMEM((tile, keys, kv_w), F32)] * 2,
        input_output_aliases={len(args) - 2: 1, len(args) - 1: 2} if aliased else {},
        compiler_params=_compiler_params("parallel"),
        name="sample_attn",
    )(*args)
    y = y.reshape(dec_batch, N_KV_HEADS, q_per_kv, dec_seq, N_KV_HEADS, HEAD_DIM)
    y = jnp.stack([y[:, kv, :, :, kv] for kv in range(N_KV_HEADS)], axis=1)
    return y.transpose(0, 3, 1, 2, 4).reshape(dec_batch * dec_seq, attn_w), ks, vs


def kernel(x_prompt, x_sample, cache_k, cache_v, state_ssm_re, state_ssm_im, ffn1_norm, ffn1_w_gate, ffn1_w_up, ffn1_w_down, mix_norm, w_in, ssm_A_re, ssm_A_im, ssm_B_re, ssm_B_im, ssm_C_re, ssm_C_im, ssm_D, ssm_log_dt, ssm_w_glu, q_norm, k_norm, sinks, ssm_out_norm, attn_out_norm, w_out, ffn2_norm, ffn2_w_gate, ffn2_w_up, ffn2_w_down):
    bsz, seq, d = x_prompt.shape
    dec_batch, dec_seq, _ = x_sample.shape
    depth = w_in.shape[0]
    ssm_w = ssm_A_re.shape[1] * SSM_GROUP_CH
    attn_w = w_out.shape[1] - ssm_w
    kv_w = N_KV_HEADS * HEAD_DIM
    cache_w = cache_k.shape[2]
    n_prompt = bsz * seq
    n_tok = n_prompt + dec_batch * dec_seq

    x = jnp.concatenate([x_prompt.reshape(n_prompt, d), x_sample.reshape(dec_batch * dec_seq, d)], axis=0)
    n_groups = ssm_w // SSM_GROUP_CH
    n_merged = n_groups // SSM_MERGE
    zeros_state = jnp.zeros((n_merged, bsz, SSM_MERGE * SSM_STATE), F32)
    outs = [[] for _ in range(6)]
    cache_k_flat = cache_k.reshape(depth, dec_batch, cache_w, kv_w)
    cache_v_flat = cache_v.reshape(depth, dec_batch, cache_w, kv_w)
    windows = None
    ffn1_w, ffn2_w = (ffn1_w_gate, ffn1_w_up, ffn1_w_down), (ffn2_w_gate, ffn2_w_up, ffn2_w_down)
    w_in_bf, w_out_bf = w_in.astype(BF16), w_out.astype(BF16)
    for l in range(depth):
        x = _ffn(x, ffn1_norm[l], *ffn1_w, layer=l)
        z = _norm_matmul(x, mix_norm[l], w_in_bf, layer=l, tn=w_in.shape[2])

        prep = _ssm_prep(*_merge_groups(ssm_A_re[l], ssm_A_im[l], ssm_log_dt[l], ssm_B_re[l], ssm_B_im[l],
                                        ssm_C_re[l], ssm_C_im[l], ssm_D[l], ssm_w_glu[l], SSM_MERGE))
        tab_p = _ssm_tables(prep, PROMPT_CHUNK)
        tab_s = _ssm_tables(prep, dec_seq)
        grp_major = lambda a: a.reshape(a.shape[0], n_merged, SSM_MERGE * SSM_STATE).transpose(1, 0, 2)
        seq_major = lambda a: a.transpose(1, 0, 2).reshape(a.shape[1], n_groups, SSM_STATE)
        y_ssm, hp_re, hp_im = _ssm_mix(z, None, tab_p, zeros_state, zeros_state, 0, n_prompt, bsz, PROMPT_CHUNK)
        y_ssm, hs_re, hs_im = _ssm_mix(z, y_ssm, tab_s, grp_major(state_ssm_re[l]), grp_major(state_ssm_im[l]),
                                       n_prompt, dec_batch * dec_seq, dec_batch, dec_seq)
        hp_re, hp_im, hs_re, hs_im = map(seq_major, (hp_re, hp_im, hs_re, hs_im))

        y_attn, pk, pv = _prompt_attn(z, sinks[l], q_norm[l], k_norm[l], bsz, seq, n_tok, ssm_w, attn_w, kv_w)
        ya_s, *windows = _sample_attn(z, cache_k_flat, cache_v_flat, windows, l, sinks[l], q_norm[l], k_norm[l],
                                      n_prompt, dec_seq, ssm_w, attn_w, kv_w)
        y_attn = lax.dynamic_update_slice(y_attn, ya_s, (n_prompt, 0))

        x = _out_proj(x, y_ssm, y_attn, ssm_out_norm[l], attn_out_norm[l], w_out_bf, layer=l)
        x = _ffn(x, ffn2_norm[l], *ffn2_w, layer=l)

        kv_shape = (-1, N_KV_HEADS, HEAD_DIM)
        for dst, val in zip(outs, (pk.reshape(bsz, *kv_shape), pv.reshape(bsz, *kv_shape), hp_re, hp_im, hs_re, hs_im)):
            dst.append(val)
    y_prompt = x[:n_prompt].reshape(bsz, seq, d)
    y_sample = x[n_prompt:].reshape(dec_batch, dec_seq, d)
    pk, pv, hp_re, hp_im, hs_re, hs_im = (jnp.stack(o) for o in outs)
    sk, sv = (w.reshape(cache_k.shape) for w in windows)
    return (y_prompt, y_sample, pk, pv, hp_re, hp_im, sk, sv, hs_re, hs_im)
```

```python
import functools

import jax
import jax.numpy as jnp
import numpy as np
from jax import lax
from jax.experimental import pallas as pl
from jax.experimental.pallas import tpu as pltpu

F32 = jnp.float32
BF16 = jnp.bfloat16

EPS = 1e-6
SSM_GROUP_CH = 16
SSM_STATE = 64
HEAD_DIM = 64
N_KV_HEADS = 4
WINDOW = 128
PAST_LEN = 8192
QK_SCALE = HEAD_DIM ** -0.5
PROMPT_CHUNK = 16

TOKEN_TILE = 1024
SMALL_TOKEN_TILE = 512
IN_PROJ_TILE = 640
FFN_VMEM_LIMIT_BYTES = 60 * 1024 * 1024
FF_TILE = 256
SAMPLE_BATCH_TILE = 8
VMEM_LIMIT_BYTES = 48 * 1024 * 1024

HIGHEST = lax.Precision.HIGHEST
LANES = 128
SSM_MERGE = 2


def _rms(x, gain):
    return x * lax.rsqrt(jnp.mean(x * x, axis=-1, keepdims=True) + EPS) * gain


def _dot(a, b, precision=None):
    return jnp.dot(a, b, preferred_element_type=F32, precision=precision)


def _dot_nt(a, b, precision=None):
    return lax.dot_general(a, b, (((1,), (1,)), ((), ())), preferred_element_type=F32, precision=precision)


def _compiler_params(*semantics):
    return pltpu.CompilerParams(dimension_semantics=semantics, vmem_limit_bytes=VMEM_LIMIT_BYTES)


def _ffn_kernel(x_ref, g_ref, wg_ref, wu_ref, wd_ref, o_ref, xn_ref):
    j = pl.program_id(1)

    @pl.when(j == 0)
    def _():
        xn_ref[...] = _rms(x_ref[...], g_ref[...]).astype(BF16)
        o_ref[...] = jnp.zeros_like(o_ref)

    xn = xn_ref[...]
    gate = _dot(xn, wg_ref[...].astype(BF16))
    up = _dot(xn, wu_ref[...].astype(BF16))
    h = (gate * jax.nn.sigmoid(gate) * up).astype(BF16)
    o_ref[...] += _dot(h, wd_ref[...].astype(BF16))

    @pl.when(j == pl.num_programs(1) - 1)
    def _():
        o_ref[...] = x_ref[...] + 0.5 * o_ref[...]


def _ffn(x, gain, w_gate, w_up, w_down, layer):
    n_tok, d = x.shape
    d_ff = w_gate.shape[2]
    tm = _token_tile(n_tok)
    tf = FF_TILE
    return pl.pallas_call(
        _ffn_kernel,
        grid=(n_tok // tm, d_ff // tf),
        in_specs=[
            pl.BlockSpec((tm, d), lambda i, j: (i, 0)),
            pl.BlockSpec((1, d), lambda i, j: (0, 0)),
            pl.BlockSpec((None, d, tf), lambda i, j: (layer, 0, j)),
            pl.BlockSpec((None, d, tf), lambda i, j: (layer, 0, j)),
            pl.BlockSpec((None, tf, d), lambda i, j: (layer, j, 0)),
        ],
        out_specs=pl.BlockSpec((tm, d), lambda i, j: (i, 0)),
        out_shape=jax.ShapeDtypeStruct((n_tok, d), F32),
        scratch_shapes=[pltpu.VMEM((tm, d), BF16)],
        compiler_params=pltpu.CompilerParams(dimension_semantics=("parallel", "arbitrary"),
                                             vmem_limit_bytes=FFN_VMEM_LIMIT_BYTES),
        name="ffn",
    )(x, gain.reshape(1, d), w_gate, w_up, w_down)


def _norm_matmul_kernel(x_ref, g_ref, w_ref, o_ref, xn_ref):
    @pl.when(pl.program_id(1) == 0)
    def _():
        xn_ref[...] = _rms(x_ref[...], g_ref[...]).astype(BF16)

    o_ref[...] = _dot(xn_ref[...], w_ref[...].astype(BF16))


def _token_tile(n_tok):
    return TOKEN_TILE if n_tok % TOKEN_TILE == 0 else SMALL_TOKEN_TILE


def _norm_matmul(x, gain, w, layer, tn):
    n_tok, d = x.shape
    n_out = w.shape[2]
    tm = _token_tile(n_tok)
    return pl.pallas_call(
        _norm_matmul_kernel,
        grid=(n_tok // tm, n_out // tn),
        in_specs=[
            pl.BlockSpec((tm, d), lambda i, j: (i, 0)),
            pl.BlockSpec((1, d), lambda i, j: (0, 0)),
            pl.BlockSpec((None, d, tn), lambda i, j: (layer, 0, j)),
        ],
        out_specs=pl.BlockSpec((tm, tn), lambda i, j: (i, j)),
        out_shape=jax.ShapeDtypeStruct((n_tok, n_out), F32),
        scratch_shapes=[pltpu.VMEM((tm, d), BF16)],
        compiler_params=_compiler_params("parallel", "arbitrary"),
        name="in_proj",
    )(x, gain.reshape(1, d), w)


def _out_proj_kernel(x_ref, ys_ref, ya_ref, gs_ref, ga_ref, ws_ref, wa_ref, o_ref):
    ys = _rms(ys_ref[...], gs_ref[...]).astype(BF16)
    ya = _rms(ya_ref[...], ga_ref[...]).astype(BF16)
    o_ref[...] = x_ref[...] + _dot(ys, ws_ref[...]) + _dot(ya, wa_ref[...])


def _out_proj(x, y_ssm, y_attn, g_ssm, g_attn, w_out, layer):
    n_tok, d = x.shape
    ssm_w, attn_w = y_ssm.shape[1], y_attn.shape[1]
    tm = SMALL_TOKEN_TILE
    return pl.pallas_call(
        _out_proj_kernel,
        grid=(n_tok // tm,),
        in_specs=[
            pl.BlockSpec((tm, d), lambda i: (i, 0)),
            pl.BlockSpec((tm, ssm_w), lambda i: (i, 0)),
            pl.BlockSpec((tm, attn_w), lambda i: (i, 0)),
            pl.BlockSpec((1, ssm_w), lambda i: (0, 0)),
            pl.BlockSpec((1, attn_w), lambda i: (0, 0)),
            pl.BlockSpec((None, ssm_w, d), lambda i: (layer, 0, 0)),
            pl.BlockSpec((None, attn_w, d), lambda i: (layer, ssm_w // attn_w, 0)),
        ],
        out_specs=pl.BlockSpec((tm, d), lambda i: (i, 0)),
        out_shape=jax.ShapeDtypeStruct((n_tok, d), F32),
        compiler_params=_compiler_params("parallel"),
        name="out_proj",
    )(x, y_ssm, y_attn, g_ssm.reshape(1, ssm_w), g_attn.reshape(1, attn_w), w_out, w_out)


def _ssm_prep_kernel(*refs):
    for gi in range(refs[0].shape[0]):
        _ssm_prep_group(*[r.at[pl.ds(gi, 1)] for r in refs])


def _ssm_prep_group(are_ref, aim_ref, ldt_ref, btre_ref, btim_ref, cre_ref, cim_ref, d_ref, wglu_ref,
                    m_ref, wre_ref, wim_ref, ztre_ref, ztim_ref, glu_ref, dt_ref, powre_ref, powim_ref):
    a_re, a_im = are_ref[0], aim_ref[0]
    dt = jnp.exp(ldt_ref[0])
    mag = jnp.exp(a_re * dt)
    lb_re, lb_im = mag * jnp.cos(a_im * dt), mag * jnp.sin(a_im * dt)
    den = a_re * a_re + a_im * a_im
    q_re = ((lb_re - 1.0) * a_re + lb_im * a_im) / den
    q_im = (lb_im * a_re - (lb_re - 1.0) * a_im) / den
    bt_re, bt_im = btre_ref[0], btim_ref[0]
    bb_re = q_re * bt_re - q_im * bt_im
    bb_im = q_re * bt_im + q_im * bt_re
    c_re, c_im = cre_ref[0], cim_ref[0]

    pw_re, pw_im = [jnp.ones_like(lb_re)], [jnp.zeros_like(lb_im)]
    for _ in range(PROMPT_CHUNK):
        pr, pi = pw_re[-1], pw_im[-1]
        pw_re.append(pr * lb_re - pi * lb_im)
        pw_im.append(pr * lb_im + pi * lb_re)

    steps = PROMPT_CHUNK
    c = c_re.shape[0]
    w_re = jnp.concatenate([bb_re * pw_re[steps - 1 - s] - bb_im * pw_im[steps - 1 - s] for s in range(steps)], axis=0)
    w_im = jnp.concatenate([bb_re * pw_im[steps - 1 - s] + bb_im * pw_re[steps - 1 - s] for s in range(steps)], axis=0)
    wre_ref[0] = w_re.astype(BF16)
    wim_ref[0] = w_im.astype(BF16)
    ztre_ref[0] = jnp.concatenate([c_re * pw_re[t + 1] - c_im * pw_im[t + 1] for t in range(steps)], axis=0).astype(BF16)
    ztim_ref[0] = jnp.concatenate([-(c_re * pw_im[t + 1] + c_im * pw_re[t + 1]) for t in range(steps)], axis=0).astype(BF16)

    lag = _dot_nt(w_re, c_re, HIGHEST) - _dot_nt(w_im, c_im, HIGHEST)
    cols = []
    for t in range(steps):
        live = lag[(steps - 1 - t) * c:, :]
        cols.append(live if t == steps - 1 else
                    jnp.concatenate([live, jnp.zeros(((steps - 1 - t) * c, c), F32)], axis=0))
    m_ref[0] = jnp.concatenate(cols, axis=1).astype(BF16)

    tiled = jnp.concatenate([jnp.concatenate([wglu_ref[0]] * steps, axis=0)] * steps, axis=1)
    shift = c.bit_length() - 1
    row_step = lax.shift_right_logical(lax.broadcasted_iota(jnp.int32, tiled.shape, 0), shift)
    col_step = lax.shift_right_logical(lax.broadcasted_iota(jnp.int32, tiled.shape, 1), shift)
    glu_ref[0] = jnp.where(row_step == col_step, tiled, 0.0).astype(BF16)
    dt_ref[0] = jnp.concatenate([d_ref[0]] * steps, axis=1)

    pad = jnp.zeros((powre_ref.shape[1] - (steps + 1), a_re.shape[1]), F32)
    powre_ref[0] = jnp.concatenate(pw_re + [pad], axis=0)
    powim_ref[0] = jnp.concatenate(pw_im + [pad], axis=0)


def _merge_groups(a_re, a_im, log_dt, b_re, b_im, c_re, c_im, d_skip, w_glu, n):
    g, p = a_re.shape
    gm = g // n
    own = np.eye(n, dtype=np.float32)[None, :, :, None, None]

    def block_diag(x):
        r, q = x.shape[1:]
        return (x.reshape(gm, n, 1, r, q) * own).transpose(0, 1, 3, 2, 4).reshape(gm, n * r, n * q)

    lanes = lambda x: x.reshape(gm, 1, -1)
    return (lanes(a_re), lanes(a_im), lanes(jnp.repeat(log_dt, p)),
            block_diag(jnp.swapaxes(b_re, 1, 2)), block_diag(jnp.swapaxes(b_im, 1, 2)),
            block_diag(c_re), block_diag(c_im), lanes(d_skip), block_diag(w_glu))


def _ssm_prep(a_re, a_im, log_dt, bt_re, bt_im, c_re, c_im, d_skip, w_glu):
    g, c, p = c_re.shape
    assert c & (c - 1) == 0
    tc = PROMPT_CHUNK * c
    pow_rows = 24
    sds = lambda dtype, *shape: jax.ShapeDtypeStruct((g,) + shape, dtype)
    gps = LANES // c
    gs = lambda *shape: pl.BlockSpec((gps,) + shape, lambda i: (i,) + (0,) * len(shape))
    names = ("m", "w_re", "w_im", "zt_re", "zt_im", "glu", "d", "pow_re", "pow_im")
    outs = pl.pallas_call(
        _ssm_prep_kernel,
        grid=(g // gps,),
        in_specs=[gs(1, p), gs(1, p), gs(1, p), gs(c, p), gs(c, p), gs(c, p), gs(c, p), gs(1, c), gs(c, c)],
        out_specs=[gs(tc, tc), gs(tc, p), gs(tc, p), gs(tc, p), gs(tc, p), gs(tc, tc), gs(1, tc),
                   gs(pow_rows, p), gs(pow_rows, p)],
        out_shape=[sds(BF16, tc, tc), sds(BF16, tc, p), sds(BF16, tc, p), sds(BF16, tc, p), sds(BF16, tc, p),
                   sds(BF16, tc, tc), sds(F32, 1, tc), sds(F32, pow_rows, p), sds(F32, pow_rows, p)],
        compiler_params=_compiler_params("parallel"),
        name="ssm_prep",
    )(a_re, a_im, log_dt, bt_re, bt_im, c_re, c_im, d_skip, w_glu)
    return dict(zip(names, outs))


def _ssm_tables(prep, t):
    n = t * (prep["d"].shape[2] // PROMPT_CHUNK)
    full = prep["m"].shape[1]
    return dict(m=prep["m"][:, :n, :n], w_re=prep["w_re"][:, full - n:], w_im=prep["w_im"][:, full - n:],
                zt_re=prep["zt_re"][:, :n], zt_im=prep["zt_im"][:, :n], glu=prep["glu"][:, :n, :n],
                d=prep["d"][:, :, :n], a_re=prep["pow_re"][:, t:t + 1], a_im=prep["pow_im"][:, t:t + 1])


def _ssm_mix_kernel(u_ref, m_ref, wre_ref, wim_ref, ztre_ref, ztim_ref, are_ref, aim_ref, d_ref, glu_ref,
                    h0re_ref, h0im_ref, y_ref, hre_ref, him_ref, xs_ref, ug_ref, vre_ref, vim_ref, pre_ref, pim_ref,
                    *, bsz, t):
    rows = u_ref.shape[0] // t
    chunks = rows // bsz
    n_grp, tc = m_ref.shape[0], m_ref.shape[1]
    c = tc // t

    def token_rows(b, s):
        return pl.ds(b * chunks * t + s, chunks, stride=t)

    def chunk_rows(b):
        return pl.ds(b, chunks, stride=bsz)

    for s in range(t):
        if chunks == 1:
            xs_ref[s] = u_ref[pl.ds(s, rows, stride=t), :]
        else:
            for b in range(bsz):
                xs_ref[s, chunk_rows(b), :] = u_ref[token_rows(b, s), :]
    for gi in range(n_grp):
        lanes = slice(gi * c, (gi + 1) * c)
        u = jnp.concatenate([xs_ref[s, :, lanes] for s in range(t)], axis=1)
        ug_ref[gi] = u
        vre_ref[gi] = _dot(u.astype(BF16), wre_ref[gi])
        vim_ref[gi] = _dot(u.astype(BF16), wim_ref[gi])

    per_tile = max(1, 8 // bsz)
    tile_rows = per_tile * bsz

    def chunk_tile(i, h):
        sel = pl.ds(pl.multiple_of(i * tile_rows, tile_rows), tile_rows)
        nxt = []
        for gi in range(n_grp):
            h_re, h_im = h[2 * gi], h[2 * gi + 1]
            v_re, v_im = vre_ref[gi, sel, :], vim_ref[gi, sel, :]
            a_re, a_im = are_ref[gi], aim_ref[gi]
            ent_re, ent_im = [], []
            for j in range(per_tile):
                ent_re.append(h_re)
                ent_im.append(h_im)
                part = slice(j * bsz, (j + 1) * bsz)
                h_re, h_im = a_re * h_re - a_im * h_im + v_re[part], a_re * h_im + a_im * h_re + v_im[part]
            pre_ref[gi, sel, :] = jnp.concatenate(ent_re, axis=0)
            pim_ref[gi, sel, :] = jnp.concatenate(ent_im, axis=0)
            nxt += [h_re, h_im]
        return tuple(nxt)

    h = tuple(r[gi] for gi in range(n_grp) for r in (h0re_ref, h0im_ref))
    n_tiles = chunks // per_tile
    h = lax.fori_loop(0, n_tiles, chunk_tile, h, unroll=min(n_tiles, 4))
    for gi in range(n_grp):
        hre_ref[gi] = h[2 * gi]
        him_ref[gi] = h[2 * gi + 1]

    for gi in range(n_grp):
        u = ug_ref[gi]
        y = (_dot(u.astype(BF16), m_ref[gi]) + _dot_nt(pre_ref[gi].astype(BF16), ztre_ref[gi])
             + _dot_nt(pim_ref[gi].astype(BF16), ztim_ref[gi]) + d_ref[gi] * u)
        y = jax.nn.gelu(y)
        y = y * jax.nn.sigmoid(_dot(y.astype(BF16), glu_ref[gi]))
        for s in range(t):
            xs_ref[s, :, gi * c:(gi + 1) * c] = y[:, s * c:(s + 1) * c]
    for s in range(t):
        if chunks == 1:
            y_ref[pl.ds(s, rows, stride=t), :] = xs_ref[s]
        else:
            for b in range(bsz):
                y_ref[token_rows(b, s), :] = xs_ref[s, chunk_rows(b), :]


def _ssm_mix(z, tab, h0_re, h0_im, bsz, t):
    g, tc, _ = tab["m"].shape
    p = h0_re.shape[2]
    n_tok = z.shape[0]
    c = tc // t
    lanes = LANES
    gps = lanes // c
    rows = n_tok // t
    blk = lambda *shape: pl.BlockSpec((gps,) + shape, lambda i: (i,) + (0,) * len(shape))
    tok = pl.BlockSpec((n_tok, lanes), lambda i: (0, i))
    return pl.pallas_call(
        functools.partial(_ssm_mix_kernel, bsz=bsz, t=t),
        grid=(g // gps,),
        in_specs=[tok, blk(tc, tc), blk(tc, p), blk(tc, p), blk(tc, p), blk(tc, p), blk(1, p), blk(1, p), blk(1, tc),
                  blk(tc, tc), blk(bsz, p), blk(bsz, p)],
        out_specs=[tok, blk(bsz, p), blk(bsz, p)],
        out_shape=[jax.ShapeDtypeStruct((n_tok, g * c), F32), jax.ShapeDtypeStruct((g, bsz, p), F32),
                   jax.ShapeDtypeStruct((g, bsz, p), F32)],
        scratch_shapes=[pltpu.VMEM((t, rows, lanes), F32), pltpu.VMEM((gps, rows, tc), F32)]
        + [pltpu.VMEM((gps, rows, p), F32)] * 4,
        compiler_params=_compiler_params("parallel"),
        name="ssm_mix",
    )(z, tab["m"], tab["w_re"], tab["w_im"], tab["zt_re"], tab["zt_im"], tab["a_re"], tab["a_im"], tab["d"],
      tab["glu"], h0_re, h0_im)


def _bias_table(q_pos, k_pos, n_heads):
    slopes = np.exp2(-8.0 * np.arange(1, n_heads + 1, dtype=np.float64) / n_heads)
    dist = q_pos[:, None] - k_pos[None, :]
    valid = (dist >= 0) & (dist < WINDOW) & (k_pos[None, :] >= 0)
    bias = -slopes[:, None, None] * dist[None].astype(np.float64)
    return np.where(valid[None], bias, -np.inf).astype(np.float32)


def _head_rms(x, gain):
    heads = [_rms(x[:, i:i + HEAD_DIM], gain) for i in range(0, x.shape[1], HEAD_DIM)]
    return jnp.concatenate(heads, axis=1)


def _softmax_sink_pv(logits, sink, v):
    m = jnp.maximum(jnp.max(logits, axis=-1, keepdims=True), sink)
    p = jnp.exp(logits - m)
    denom = jnp.sum(p, axis=-1, keepdims=True) + jnp.exp(sink - m)
    return _dot(p.astype(BF16), v) / denom


def _prompt_attn_kernel(sinks_ref, q_ref, k_ref, v_ref, qg_ref, kg_ref, bias_ref, y_ref, ks_ref, vs_ref,
                        kp_ref, vp_ref, *, n_heads):
    n = pl.program_id(1)

    @pl.when(n == 0)
    def _():
        kp_ref[...] = jnp.zeros_like(kp_ref)
        vp_ref[...] = jnp.zeros_like(vp_ref)

    q_all = q_ref[...]
    blk = q_all.shape[0]
    half = 2 * HEAD_DIM
    assert n_heads // N_KV_HEADS == 4 and half == 128
    k_cur = _head_rms(k_ref[...], kg_ref[...])
    v_cur = v_ref[...]
    k2 = jnp.concatenate([kp_ref[...], k_cur], axis=0)
    v2t = jnp.concatenate([vp_ref[...], v_cur], axis=0).T.astype(BF16)
    qs = (q_all * qg_ref[...] * QK_SCALE).astype(BF16)
    qsq = q_all * q_all
    table = jnp.minimum(n, 1)
    from_prev = (lax.broadcasted_iota(jnp.int32, (blk, blk), 0) > lax.broadcasted_iota(jnp.int32, (blk, blk), 1))
    low_lanes = lax.broadcasted_iota(jnp.int32, (2 * blk, half), 1) < HEAD_DIM
    low8 = lax.broadcasted_iota(jnp.int32, (8, half), 1) < HEAD_DIM
    pick = jnp.concatenate([jnp.where(low8, 1.0, 0.0), jnp.where(low8, 0.0, 1.0)], axis=0)
    for kv in range(N_KV_HEADS):
        k_tile = k2[:, (kv // 2) * half:(kv // 2 + 1) * half]
        k_own = jnp.where(low_lanes if kv % 2 == 0 else ~low_lanes, k_tile, 0.0)
        k_swap = pltpu.roll(k_own, HEAD_DIM, axis=1)
        k_lhs = jnp.concatenate([k_own, k_swap] if kv % 2 == 0 else [k_swap, k_own], axis=0).astype(BF16)
        tiles = slice(2 * kv * half, (2 * kv + 1) * half), slice((2 * kv + 1) * half, (2 * kv + 2) * half)
        q_rows = jnp.concatenate([qs[:, t] for t in tiles], axis=0)
        qsq_rows = jnp.concatenate([qsq[:, t] for t in tiles], axis=0)
        s = _dot_nt(k_lhs, q_rows)
        ssq = _dot_nt(pick, qsq_rows, HIGHEST)
        probs, inv_denoms = [], []
        for i in range(4):
            odd, tile = i % 2, i // 2
            lanes = slice(tile * blk, (tile + 1) * blk)
            s_h = s[odd * 2 * blk:(odd + 1) * 2 * blk, lanes]
            q_rms = lax.rsqrt(ssq[odd * 8:odd * 8 + 1, lanes] * (1.0 / HEAD_DIM) + EPS)
            logits = jnp.where(from_prev, s_h[:blk], s_h[blk:]) * q_rms + bias_ref[table, 4 * kv + i]
            sink = sinks_ref[4 * kv + i]
            m = jnp.maximum(jnp.max(logits, axis=0, keepdims=True), sink)
            p = jnp.exp(logits - m)
            inv_denoms.append(1.0 / (jnp.sum(p, axis=0, keepdims=True) + jnp.exp(sink - m)))
            probs.append(jnp.concatenate([jnp.where(from_prev, p, 0.0), jnp.where(from_prev, 0.0, p)], axis=0))
        o = _dot(v2t[kv * HEAD_DIM:(kv + 1) * HEAD_DIM], jnp.concatenate(probs, axis=1).astype(BF16))
        for tile in range(2):
            pair = [o[:, i * blk:(i + 1) * blk] * inv_denoms[i] for i in (2 * tile, 2 * tile + 1)]
            y_ref[:, tiles[tile]] = jnp.concatenate(pair, axis=0).T
    kp_ref[...] = k_cur
    vp_ref[...] = v_cur

    @pl.when(n == pl.num_programs(1) - 1)
    def _():
        ks_ref[0] = k_cur
        vs_ref[0] = v_cur


def _prompt_attn(z, sinks, q_gain, k_gain, bsz, seq, col_q, attn_w, kv_w):
    n_tok = z.shape[0]
    blk = WINDOW
    nb = seq // blk
    n_heads = attn_w // HEAD_DIM
    row = lambda b, n: b * nb + n
    q_blk, k_blk, v_blk = col_q // attn_w, (col_q + attn_w) // kv_w, (col_q + attn_w + kv_w) // kv_w
    qi, kj = np.arange(blk)[:, None], np.arange(blk)[None, :]
    first, later = [np.take_along_axis(_bias_table(first_pos + np.arange(blk), first_pos - blk + np.arange(2 * blk), n_heads),
                                       np.broadcast_to(np.where(kj > qi, kj, kj + blk), (n_heads, blk, blk)), axis=2)
                    for first_pos in (0, blk)]
    bias = np.stack([first, later]).swapaxes(2, 3)
    return pl.pallas_call(
        functools.partial(_prompt_attn_kernel, n_heads=n_heads),
        grid=(bsz, nb),
        in_specs=[
            pl.BlockSpec(memory_space=pltpu.SMEM),
            pl.BlockSpec((blk, attn_w), lambda b, n: (row(b, n), q_blk)),
            pl.BlockSpec((blk, kv_w), lambda b, n: (row(b, n), k_blk)),
            pl.BlockSpec((blk, kv_w), lambda b, n: (row(b, n), v_blk)),
            pl.BlockSpec((1, attn_w), lambda b, n: (0, 0)),
            pl.BlockSpec((1, HEAD_DIM), lambda b, n: (0, 0)),
            pl.BlockSpec(bias.shape, lambda b, n: (0, 0, 0, 0)),
        ],
        out_specs=[
            pl.BlockSpec((blk, attn_w), lambda b, n: (row(b, n), 0)),
            pl.BlockSpec((1, blk, kv_w), lambda b, n: (b, 0, 0)),
            pl.BlockSpec((1, blk, kv_w), lambda b, n: (b, 0, 0)),
        ],
        out_shape=[jax.ShapeDtypeStruct((n_tok, attn_w), F32),
                   jax.ShapeDtypeStruct((bsz, blk, kv_w), F32),
                   jax.ShapeDtypeStruct((bsz, blk, kv_w), F32)],
        scratch_shapes=[pltpu.VMEM((blk, kv_w), F32)] * 2,
        compiler_params=_compiler_params("parallel", "arbitrary"),
        name="prompt_attn",
    )(sinks, z, z, z, jnp.tile(q_gain, n_heads).reshape(1, attn_w), k_gain.reshape(1, HEAD_DIM), jnp.asarray(bias))


def _sample_attn_kernel(*refs, dec_seq, aliased):
    q_ref, k_ref, v_ref, ck_ref, cv_ref, qg_ref, kg_ref, bias_ref, sink_ref = refs[:9]
    y_ref, ks_ref, vs_ref, kbuf_ref, vbuf_ref = refs[9 + 2 * aliased:]
    cache_w = ck_ref.shape[1]
    k_new = _head_rms(k_ref[...], kg_ref[...])
    v_new = v_ref[...]
    pad = jnp.zeros((kbuf_ref.shape[1] - cache_w, kbuf_ref.shape[2]), F32)
    for b in range(q_ref.shape[0]):
        rows = slice(b * dec_seq, (b + 1) * dec_seq)
        kbuf_ref[b, 0:cache_w, :] = ck_ref[b]
        vbuf_ref[b, 0:cache_w, :] = cv_ref[b]
        kbuf_ref[b, cache_w:, :] = pad
        vbuf_ref[b, cache_w:, :] = pad
        kbuf_ref[b, cache_w:cache_w + dec_seq, :] = k_new[rows]
        vbuf_ref[b, cache_w:cache_w + dec_seq, :] = v_new[rows]
        q = q_ref[b]
        q = q * lax.rsqrt(jnp.sum(q * q, axis=-1, keepdims=True) * (1.0 / HEAD_DIM) + EPS) * qg_ref[...]
        logits = _dot_nt((q * QK_SCALE).astype(BF16), kbuf_ref[b].astype(BF16)) + bias_ref[...]
        y_ref[b] = _softmax_sink_pv(logits, sink_ref[...], vbuf_ref[b].astype(BF16))
        ks_ref[b, 0:cache_w - dec_seq, :] = ck_ref[b, dec_seq:cache_w, :]
        vs_ref[b, 0:cache_w - dec_seq, :] = cv_ref[b, dec_seq:cache_w, :]
        ks_ref[b, cache_w - dec_seq:cache_w, :] = k_new[rows]
        vs_ref[b, cache_w - dec_seq:cache_w, :] = v_new[rows]


def _sample_attn(z, cache_k, cache_v, windows, layer, sinks, q_gain, k_gain, dec_seq, col_q, attn_w, kv_w):
    _, dec_batch, cache_w, _ = cache_k.shape
    n_heads = attn_w // HEAD_DIM
    q_per_kv = n_heads // N_KV_HEADS
    tile = SAMPLE_BATCH_TILE
    k_blk, v_blk = (col_q + attn_w) // kv_w, (col_q + attn_w + kv_w) // kv_w
    keys = 2 * cache_w
    k_pos = np.full(keys, -1)
    k_pos[:cache_w] = PAST_LEN - cache_w + np.arange(cache_w)
    k_pos[cache_w:cache_w + dec_seq] = PAST_LEN + np.arange(dec_seq)
    bias = _bias_table(PAST_LEN + np.arange(dec_seq), k_pos, n_heads).reshape(n_heads * dec_seq, keys)
    sink_col = jnp.repeat(sinks, dec_seq).reshape(n_heads * dec_seq, 1)
    q = z[:, col_q:col_q + attn_w].reshape(dec_batch, dec_seq, N_KV_HEADS, q_per_kv, 1, HEAD_DIM)
    own_kv = np.eye(N_KV_HEADS, dtype=np.float32)[None, :, None, None, :, None]
    q = (q.transpose(0, 2, 3, 1, 4, 5) * own_kv).reshape(dec_batch, n_heads * dec_seq, kv_w)
    window = pl.BlockSpec((None, tile, cache_w, kv_w), lambda i: (layer, i, 0, 0))
    aliased = windows is not None
    in_specs = [
        pl.BlockSpec((tile, n_heads * dec_seq, kv_w), lambda i: (i, 0, 0)),
        pl.BlockSpec((tile * dec_seq, kv_w), lambda i: (i, k_blk)),
        pl.BlockSpec((tile * dec_seq, kv_w), lambda i: (i, v_blk)),
        window,
        window,
        pl.BlockSpec((1, kv_w), lambda i: (0, 0)),
        pl.BlockSpec((1, HEAD_DIM), lambda i: (0, 0)),
        pl.BlockSpec(bias.shape, lambda i: (0, 0)),
        pl.BlockSpec(sink_col.shape, lambda i: (0, 0)),
    ]
    args = [q, z, z, cache_k, cache_v, jnp.tile(q_gain, N_KV_HEADS).reshape(1, kv_w), k_gain.reshape(1, HEAD_DIM),
            jnp.asarray(bias), sink_col]
    if aliased:
        in_specs += [pl.BlockSpec(memory_space=pl.ANY)] * 2
        args += list(windows)
    y, ks, vs = pl.pallas_call(
        functools.partial(_sample_attn_kernel, dec_seq=dec_seq, aliased=aliased),
        grid=(dec_batch // tile,),
        in_specs=in_specs,
        out_specs=[pl.BlockSpec((tile, n_heads * dec_seq, kv_w), lambda i: (i, 0, 0)), window, window],
        out_shape=[jax.ShapeDtypeStruct(q.shape, F32),
                   jax.ShapeDtypeStruct(cache_k.shape, F32),
                   jax.ShapeDtypeStruct(cache_v.shape, F32)],
        scratch_shapes=[pltpu.VMEM((tile, keys, kv_w), F32)] * 2,
        input_output_aliases={len(args) - 2: 1, len(args) - 1: 2} if aliased else {},
        compiler_params=_compiler_params("parallel"),
        name="sample_attn",
    )(*args)
    y = y.reshape(dec_batch, N_KV_HEADS, q_per_kv, dec_seq, N_KV_HEADS, HEAD_DIM)
    y = jnp.stack([y[:, kv, :, :, kv] for kv in range(N_KV_HEADS)], axis=1)
    return y.transpose(0, 3, 1, 2, 4).reshape(dec_batch * dec_seq, attn_w), ks, vs


def kernel(x_prompt, x_sample, cache_k, cache_v, state_ssm_re, state_ssm_im, ffn1_norm, ffn1_w_gate, ffn1_w_up, ffn1_w_down, mix_norm, w_in, ssm_A_re, ssm_A_im, ssm_B_re, ssm_B_im, ssm_C_re, ssm_C_im, ssm_D, ssm_log_dt, ssm_w_glu, q_norm, k_norm, sinks, ssm_out_norm, attn_out_norm, w_out, ffn2_norm, ffn2_w_gate, ffn2_w_up, ffn2_w_down):
    bsz, seq, d = x_prompt.shape
    dec_batch, dec_seq, _ = x_sample.shape
    depth = w_in.shape[0]
    ssm_w = ssm_A_re.shape[1] * SSM_GROUP_CH
    attn_w = w_out.shape[1] - ssm_w
    kv_w = N_KV_HEADS * HEAD_DIM
    cache_w = cache_k.shape[2]
    xs = [x_prompt.reshape(bsz * seq, d), x_sample.reshape(dec_batch * dec_seq, d)]
    n_groups = ssm_w // SSM_GROUP_CH
    n_merged = n_groups // SSM_MERGE
    zeros_state = jnp.zeros((n_merged, bsz, SSM_MERGE * SSM_STATE), F32)
    outs = [[] for _ in range(6)]
    cache_k_flat = cache_k.reshape(depth, dec_batch, cache_w, kv_w)
    cache_v_flat = cache_v.reshape(depth, dec_batch, cache_w, kv_w)
    windows = None
    ffn1_w, ffn2_w = (ffn1_w_gate, ffn1_w_up, ffn1_w_down), (ffn2_w_gate, ffn2_w_up, ffn2_w_down)
    w_out_bf = w_out.astype(BF16)
    for l in range(depth):
        xs = [_ffn(x, ffn1_norm[l], *ffn1_w, layer=l) for x in xs]
        zp, zs = [_norm_matmul(x, mix_norm[l], w_in, layer=l, tn=IN_PROJ_TILE) for x in xs]

        prep = _ssm_prep(*_merge_groups(ssm_A_re[l], ssm_A_im[l], ssm_log_dt[l], ssm_B_re[l], ssm_B_im[l],
                                        ssm_C_re[l], ssm_C_im[l], ssm_D[l], ssm_w_glu[l], SSM_MERGE))
        tab_p = _ssm_tables(prep, PROMPT_CHUNK)
        tab_s = _ssm_tables(prep, dec_seq)
        grp_major = lambda a: a.reshape(a.shape[0], n_merged, SSM_MERGE * SSM_STATE).transpose(1, 0, 2)
        seq_major = lambda a: a.transpose(1, 0, 2).reshape(a.shape[1], n_groups, SSM_STATE)
        yp_ssm, hp_re, hp_im = _ssm_mix(zp, tab_p, zeros_state, zeros_state, bsz, PROMPT_CHUNK)
        ys_ssm, hs_re, hs_im = _ssm_mix(zs, tab_s, grp_major(state_ssm_re[l]), grp_major(state_ssm_im[l]),
                                        dec_batch, dec_seq)
        hp_re, hp_im, hs_re, hs_im = map(seq_major, (hp_re, hp_im, hs_re, hs_im))

        yp_attn, pk, pv = _prompt_attn(zp, sinks[l], q_norm[l], k_norm[l], bsz, seq, ssm_w, attn_w, kv_w)
        ys_attn, *windows = _sample_attn(zs, cache_k_flat, cache_v_flat, windows, l, sinks[l], q_norm[l], k_norm[l],
                                         dec_seq, ssm_w, attn_w, kv_w)

        xs = [_out_proj(x, y_ssm, y_attn, ssm_out_norm[l], attn_out_norm[l], w_out_bf, layer=l)
              for x, y_ssm, y_attn in zip(xs, (yp_ssm, ys_ssm), (yp_attn, ys_attn))]
        xs = [_ffn(x, ffn2_norm[l], *ffn2_w, layer=l) for x in xs]

        kv_shape = (-1, N_KV_HEADS, HEAD_DIM)
        for dst, val in zip(outs, (pk.reshape(bsz, *kv_shape), pv.reshape(bsz, *kv_shape), hp_re, hp_im, hs_re, hs_im)):
            dst.append(val)
    y_prompt = xs[0].reshape(bsz, seq, d)
    y_sample = xs[1].reshape(dec_batch, dec_seq, d)
    pk, pv, hp_re, hp_im, hs_re, hs_im = (jnp.stack(o) for o in outs)
    sk, sv = (w.reshape(cache_k.shape) for w in windows)
    return (y_prompt, y_sample, pk, pv, hp_re, hp_im, sk, sv, hs_re, hs_im)
```

```python
import functools

import jax
import jax.numpy as jnp
import numpy as np
from jax import lax
from jax.experimental import pallas as pl
from jax.experimental.pallas import tpu as pltpu

F32 = jnp.float32
BF16 = jnp.bfloat16

EPS = 1e-6
SSM_GROUP_CH = 16
SSM_STATE = 64
HEAD_DIM = 64
N_KV_HEADS = 4
WINDOW = 128
PAST_LEN = 8192
QK_SCALE = HEAD_DIM ** -0.5
PROMPT_CHUNK = 16

TOKEN_TILE = 512
FFN_TOKEN_TILE = 1088
FFN_VMEM_LIMIT_BYTES = 60 * 1024 * 1024
FF_TILE = 256
SAMPLE_BATCH_TILE = 8
VMEM_LIMIT_BYTES = 48 * 1024 * 1024

HIGHEST = lax.Precision.HIGHEST
LANES = 128
SSM_MERGE = 2


def _rms(x, gain):
    return x * lax.rsqrt(jnp.mean(x * x, axis=-1, keepdims=True) + EPS) * gain


def _dot(a, b, precision=None):
    return jnp.dot(a, b, preferred_element_type=F32, precision=precision)


def _dot_nt(a, b, precision=None):
    return lax.dot_general(a, b, (((1,), (1,)), ((), ())), preferred_element_type=F32, precision=precision)


def _compiler_params(*semantics):
    return pltpu.CompilerParams(dimension_semantics=semantics, vmem_limit_bytes=VMEM_LIMIT_BYTES)


def _ffn_kernel(*refs, n_in, n_out, split_row):
    x_refs, (g_ref, wg_ref, wu_ref, wd_ref) = refs[:n_in], refs[n_in:n_in + 4]
    o_refs, xn_ref = refs[n_in + 4:n_in + 4 + n_out], refs[-1]
    acc_ref = o_refs[0]
    i, j = pl.program_id(0), pl.program_id(1)
    boundary = i == pl.num_programs(0) - 1

    def with_x_tile(fn):
        if n_in == 1:
            fn(x_refs[0][...])
        else:
            pl.when(jnp.logical_not(boundary))(lambda: fn(x_refs[0][...]))
            pl.when(boundary)(lambda: fn(jnp.concatenate([x_refs[0][0:split_row, :], x_refs[1][...]], axis=0)))

    @pl.when(j == 0)
    def _():
        acc_ref[...] = jnp.zeros_like(acc_ref)

        def normalise(x):
            xn_ref[...] = _rms(x, g_ref[...]).astype(BF16)
        with_x_tile(normalise)

    xn = xn_ref[...]
    gate = _dot(xn, wg_ref[...].astype(BF16))
    up = _dot(xn, wu_ref[...].astype(BF16))
    h = (gate * jax.nn.sigmoid(gate) * up).astype(BF16)
    acc_ref[...] += _dot(h, wd_ref[...].astype(BF16))

    @pl.when(j == pl.num_programs(1) - 1)
    def _():
        def residual(x):
            acc_ref[...] = x + 0.5 * acc_ref[...]
        with_x_tile(residual)
        if n_out == 2:
            @pl.when(boundary)
            def _():
                o_refs[1][...] = acc_ref[split_row:, :]


def _ffn(xs, gain, w_gate, w_up, w_down, layer, out_rows=None):
    d = xs[0].shape[1]
    n_tok = sum(x.shape[0] for x in xs)
    d_ff = w_gate.shape[2]
    if len(xs) == 2 or out_rows is not None:
        tm = FFN_TOKEN_TILE if n_tok % FFN_TOKEN_TILE == 0 else n_tok
        n0, n1 = (x.shape[0] for x in xs) if len(xs) == 2 else out_rows
        split_row = n0 - (n_tok // tm - 1) * tm
        assert split_row + n1 == tm and split_row % 8 == 0
    else:
        tm = FFN_TOKEN_TILE if n_tok % FFN_TOKEN_TILE == 0 else TOKEN_TILE
        split_row = None
    tf = FF_TILE
    tile_spec = pl.BlockSpec((tm, d), lambda i, j: (i, 0))
    tail_spec = lambda rows: pl.BlockSpec((rows, d), lambda i, j: (0, 0))
    x_specs = [tile_spec] + ([tail_spec(xs[1].shape[0])] if len(xs) == 2 else [])
    if out_rows is None:
        out_specs, out_shape = [tile_spec], [jax.ShapeDtypeStruct((n_tok, d), F32)]
    else:
        out_specs = [tile_spec, tail_spec(out_rows[1])]
        out_shape = [jax.ShapeDtypeStruct((rows, d), F32) for rows in out_rows]
    outs = pl.pallas_call(
        functools.partial(_ffn_kernel, n_in=len(xs), n_out=len(out_specs), split_row=split_row),
        grid=(n_tok // tm, d_ff // tf),
        in_specs=x_specs + [
            pl.BlockSpec((1, d), lambda i, j: (0, 0)),
            pl.BlockSpec((None, d, tf), lambda i, j: (layer, 0, j)),
            pl.BlockSpec((None, d, tf), lambda i, j: (layer, 0, j)),
            pl.BlockSpec((None, tf, d), lambda i, j: (layer, j, 0)),
        ],
        out_specs=out_specs,
        out_shape=out_shape,
        scratch_shapes=[pltpu.VMEM((tm, d), BF16)],
        compiler_params=pltpu.CompilerParams(dimension_semantics=("arbitrary", "arbitrary"),
                                             vmem_limit_bytes=FFN_VMEM_LIMIT_BYTES),
        name="ffn",
    )(*xs, gain.reshape(1, d), w_gate, w_up, w_down)
    return outs[0] if out_rows is None else outs


def _norm_matmul_kernel(x_ref, g_ref, w_ref, o_ref, xn_ref):
    @pl.when(pl.program_id(1) == 0)
    def _():
        xn_ref[...] = _rms(x_ref[...], g_ref[...]).astype(BF16)

    o_ref[...] = _dot(xn_ref[...], w_ref[...])


def _norm_matmul(x, gain, w, layer, tn):
    n_tok, d = x.shape
    n_out = w.shape[2]
    tm = TOKEN_TILE
    return pl.pallas_call(
        _norm_matmul_kernel,
        grid=(n_tok // tm, n_out // tn),
        in_specs=[
            pl.BlockSpec((tm, d), lambda i, j: (i, 0)),
            pl.BlockSpec((1, d), lambda i, j: (0, 0)),
            pl.BlockSpec((None, d, tn), lambda i, j: (layer, 0, j)),
        ],
        out_specs=pl.BlockSpec((tm, tn), lambda i, j: (i, j)),
        out_shape=jax.ShapeDtypeStruct((n_tok, n_out), F32),
        scratch_shapes=[pltpu.VMEM((tm, d), BF16)],
        compiler_params=_compiler_params("parallel", "arbitrary"),
        name="in_proj",
    )(x, gain.reshape(1, d), w)


def _out_proj_kernel(x_ref, ys_ref, ya_ref, gs_ref, ga_ref, ws_ref, wa_ref, o_ref):
    ys = _rms(ys_ref[...], gs_ref[...]).astype(BF16)
    ya = _rms(ya_ref[...], ga_ref[...]).astype(BF16)
    o_ref[...] = x_ref[...] + _dot(ys, ws_ref[...]) + _dot(ya, wa_ref[...])


def _out_proj(x, y_ssm, y_attn, g_ssm, g_attn, w_out, layer):
    n_tok, d = x.shape
    ssm_w, attn_w = y_ssm.shape[1], y_attn.shape[1]
    tm = TOKEN_TILE
    return pl.pallas_call(
        _out_proj_kernel,
        grid=(n_tok // tm,),
        in_specs=[
            pl.BlockSpec((tm, d), lambda i: (i, 0)),
            pl.BlockSpec((tm, ssm_w), lambda i: (i, 0)),
            pl.BlockSpec((tm, attn_w), lambda i: (i, 0)),
            pl.BlockSpec((1, ssm_w), lambda i: (0, 0)),
            pl.BlockSpec((1, attn_w), lambda i: (0, 0)),
            pl.BlockSpec((None, ssm_w, d), lambda i: (layer, 0, 0)),
            pl.BlockSpec((None, attn_w, d), lambda i: (layer, ssm_w // attn_w, 0)),
        ],
        out_specs=pl.BlockSpec((tm, d), lambda i: (i, 0)),
        out_shape=jax.ShapeDtypeStruct((n_tok, d), F32),
        compiler_params=_compiler_params("parallel"),
        name="out_proj",
    )(x, y_ssm, y_attn, g_ssm.reshape(1, ssm_w), g_attn.reshape(1, attn_w), w_out, w_out)


def _ssm_prep_kernel(*refs):
    for gi in range(refs[0].shape[0]):
        _ssm_prep_group(*[r.at[pl.ds(gi, 1)] for r in refs])


def _ssm_prep_group(are_ref, aim_ref, ldt_ref, btre_ref, btim_ref, cre_ref, cim_ref, d_ref, wglu_ref,
                    m_ref, wre_ref, wim_ref, ztre_ref, ztim_ref, glu_ref, dt_ref, powre_ref, powim_ref):
    a_re, a_im = are_ref[0], aim_ref[0]
    dt = jnp.exp(ldt_ref[0])
    mag = jnp.exp(a_re * dt)
    lb_re, lb_im = mag * jnp.cos(a_im * dt), mag * jnp.sin(a_im * dt)
    den = a_re * a_re + a_im * a_im
    q_re = ((lb_re - 1.0) * a_re + lb_im * a_im) / den
    q_im = (lb_im * a_re - (lb_re - 1.0) * a_im) / den
    bt_re, bt_im = btre_ref[0], btim_ref[0]
    bb_re = q_re * bt_re - q_im * bt_im
    bb_im = q_re * bt_im + q_im * bt_re
    c_re, c_im = cre_ref[0], cim_ref[0]

    pw_re, pw_im = [jnp.ones_like(lb_re)], [jnp.zeros_like(lb_im)]
    for _ in range(PROMPT_CHUNK):
        pr, pi = pw_re[-1], pw_im[-1]
        pw_re.append(pr * lb_re - pi * lb_im)
        pw_im.append(pr * lb_im + pi * lb_re)

    steps = PROMPT_CHUNK
    c = c_re.shape[0]
    w_re = jnp.concatenate([bb_re * pw_re[steps - 1 - s] - bb_im * pw_im[steps - 1 - s] for s in range(steps)], axis=0)
    w_im = jnp.concatenate([bb_re * pw_im[steps - 1 - s] + bb_im * pw_re[steps - 1 - s] for s in range(steps)], axis=0)
    wre_ref[0] = w_re.astype(BF16)
    wim_ref[0] = w_im.astype(BF16)
    ztre_ref[0] = jnp.concatenate([c_re * pw_re[t + 1] - c_im * pw_im[t + 1] for t in range(steps)], axis=0).astype(BF16)
    ztim_ref[0] = jnp.concatenate([-(c_re * pw_im[t + 1] + c_im * pw_re[t + 1]) for t in range(steps)], axis=0).astype(BF16)

    lag = _dot_nt(w_re, c_re, HIGHEST) - _dot_nt(w_im, c_im, HIGHEST)
    cols = []
    for t in range(steps):
        live = lag[(steps - 1 - t) * c:, :]
        cols.append(live if t == steps - 1 else
                    jnp.concatenate([live, jnp.zeros(((steps - 1 - t) * c, c), F32)], axis=0))
    m_ref[0] = jnp.concatenate(cols, axis=1).astype(BF16)

    tiled = jnp.concatenate([jnp.concatenate([wglu_ref[0]] * steps, axis=0)] * steps, axis=1)
    shift = c.bit_length() - 1
    row_step = lax.shift_right_logical(lax.broadcasted_iota(jnp.int32, tiled.shape, 0), shift)
    col_step = lax.shift_right_logical(lax.broadcasted_iota(jnp.int32, tiled.shape, 1), shift)
    glu_ref[0] = jnp.where(row_step == col_step, tiled, 0.0).astype(BF16)
    dt_ref[0] = jnp.concatenate([d_ref[0]] * steps, axis=1)

    pad = jnp.zeros((powre_ref.shape[1] - (steps + 1), a_re.shape[1]), F32)
    powre_ref[0] = jnp.concatenate(pw_re + [pad], axis=0)
    powim_ref[0] = jnp.concatenate(pw_im + [pad], axis=0)


def _merge_groups(a_re, a_im, log_dt, b_re, b_im, c_re, c_im, d_skip, w_glu, n):
    g, p = a_re.shape
    gm = g // n
    own = np.eye(n, dtype=np.float32)[None, :, :, None, None]

    def block_diag(x):
        r, q = x.shape[1:]
        return (x.reshape(gm, n, 1, r, q) * own).transpose(0, 1, 3, 2, 4).reshape(gm, n * r, n * q)

    lanes = lambda x: x.reshape(gm, 1, -1)
    return (lanes(a_re), lanes(a_im), lanes(jnp.repeat(log_dt, p)),
            block_diag(jnp.swapaxes(b_re, 1, 2)), block_diag(jnp.swapaxes(b_im, 1, 2)),
            block_diag(c_re), block_diag(c_im), lanes(d_skip), block_diag(w_glu))


def _ssm_prep(a_re, a_im, log_dt, bt_re, bt_im, c_re, c_im, d_skip, w_glu):
    g, c, p = c_re.shape
    assert c & (c - 1) == 0
    tc = PROMPT_CHUNK * c
    pow_rows = 24
    sds = lambda dtype, *shape: jax.ShapeDtypeStruct((g,) + shape, dtype)
    gps = LANES // c
    gs = lambda *shape: pl.BlockSpec((gps,) + shape, lambda i: (i,) + (0,) * len(shape))
    names = ("m", "w_re", "w_im", "zt_re", "zt_im", "glu", "d", "pow_re", "pow_im")
    outs = pl.pallas_call(
        _ssm_prep_kernel,
        grid=(g // gps,),
        in_specs=[gs(1, p), gs(1, p), gs(1, p), gs(c, p), gs(c, p), gs(c, p), gs(c, p), gs(1, c), gs(c, c)],
        out_specs=[gs(tc, tc), gs(tc, p), gs(tc, p), gs(tc, p), gs(tc, p), gs(tc, tc), gs(1, tc),
                   gs(pow_rows, p), gs(pow_rows, p)],
        out_shape=[sds(BF16, tc, tc), sds(BF16, tc, p), sds(BF16, tc, p), sds(BF16, tc, p), sds(BF16, tc, p),
                   sds(BF16, tc, tc), sds(F32, 1, tc), sds(F32, pow_rows, p), sds(F32, pow_rows, p)],
        compiler_params=_compiler_params("parallel"),
        name="ssm_prep",
    )(a_re, a_im, log_dt, bt_re, bt_im, c_re, c_im, d_skip, w_glu)
    return dict(zip(names, outs))


def _ssm_tables(prep, t):
    n = t * (prep["d"].shape[2] // PROMPT_CHUNK)
    full = prep["m"].shape[1]
    return dict(m=prep["m"][:, :n, :n], w_re=prep["w_re"][:, full - n:], w_im=prep["w_im"][:, full - n:],
                zt_re=prep["zt_re"][:, :n], zt_im=prep["zt_im"][:, :n], glu=prep["glu"][:, :n, :n],
                d=prep["d"][:, :, :n], a_re=prep["pow_re"][:, t:t + 1], a_im=prep["pow_im"][:, t:t + 1])


def _ssm_mix_kernel(*refs, bsz, t, aliased):
    (u_ref, m_ref, wre_ref, wim_ref, ztre_ref, ztim_ref, are_ref, aim_ref, d_ref, glu_ref, h0re_ref, h0im_ref) = refs[:12]
    y_ref, hre_ref, him_ref, xs_ref, ug_ref, vre_ref, vim_ref, pre_ref, pim_ref = refs[12 + aliased:]
    rows = u_ref.shape[0] // t
    chunks = rows // bsz
    n_grp, tc = m_ref.shape[0], m_ref.shape[1]
    c = tc // t

    def token_rows(b, s):
        return pl.ds(b * chunks * t + s, chunks, stride=t)

    def chunk_rows(b):
        return pl.ds(b, chunks, stride=bsz)

    for s in range(t):
        if chunks == 1:
            xs_ref[s] = u_ref[pl.ds(s, rows, stride=t), :]
        else:
            for b in range(bsz):
                xs_ref[s, chunk_rows(b), :] = u_ref[token_rows(b, s), :]
    for gi in range(n_grp):
        lanes = slice(gi * c, (gi + 1) * c)
        u = jnp.concatenate([xs_ref[s, :, lanes] for s in range(t)], axis=1)
        ug_ref[gi] = u
        vre_ref[gi] = _dot(u.astype(BF16), wre_ref[gi])
        vim_ref[gi] = _dot(u.astype(BF16), wim_ref[gi])

    per_tile = max(1, 8 // bsz)
    tile_rows = per_tile * bsz

    def chunk_tile(i, h):
        sel = pl.ds(pl.multiple_of(i * tile_rows, tile_rows), tile_rows)
        nxt = []
        for gi in range(n_grp):
            h_re, h_im = h[2 * gi], h[2 * gi + 1]
            v_re, v_im = vre_ref[gi, sel, :], vim_ref[gi, sel, :]
            a_re, a_im = are_ref[gi], aim_ref[gi]
            ent_re, ent_im = [], []
            for j in range(per_tile):
                ent_re.append(h_re)
                ent_im.append(h_im)
                part = slice(j * bsz, (j + 1) * bsz)
                h_re, h_im = a_re * h_re - a_im * h_im + v_re[part], a_re * h_im + a_im * h_re + v_im[part]
            pre_ref[gi, sel, :] = jnp.concatenate(ent_re, axis=0)
            pim_ref[gi, sel, :] = jnp.concatenate(ent_im, axis=0)
            nxt += [h_re, h_im]
        return tuple(nxt)

    h = tuple(r[gi] for gi in range(n_grp) for r in (h0re_ref, h0im_ref))
    n_tiles = chunks // per_tile
    h = lax.fori_loop(0, n_tiles, chunk_tile, h, unroll=min(n_tiles, 4))
    for gi in range(n_grp):
        hre_ref[gi] = h[2 * gi]
        him_ref[gi] = h[2 * gi + 1]

    for gi in range(n_grp):
        u = ug_ref[gi]
        y = (_dot(u.astype(BF16), m_ref[gi]) + _dot_nt(pre_ref[gi].astype(BF16), ztre_ref[gi])
             + _dot_nt(pim_ref[gi].astype(BF16), ztim_ref[gi]) + d_ref[gi] * u)
        y = jax.nn.gelu(y)
        y = y * jax.nn.sigmoid(_dot(y.astype(BF16), glu_ref[gi]))
        for s in range(t):
            xs_ref[s, :, gi * c:(gi + 1) * c] = y[:, s * c:(s + 1) * c]
    for s in range(t):
        if chunks == 1:
            y_ref[pl.ds(s, rows, stride=t), :] = xs_ref[s]
        else:
            for b in range(bsz):
                y_ref[token_rows(b, s), :] = xs_ref[s, chunk_rows(b), :]


def _ssm_mix(z, y_prev, tab, h0_re, h0_im, row0, n_rows, bsz, t):
    g, tc, _ = tab["m"].shape
    p = h0_re.shape[2]
    n_tok = z.shape[0]
    c = tc // t
    width = g * c
    lanes = LANES
    gps = lanes // c
    rows = n_rows // t
    blk = lambda *shape: pl.BlockSpec((gps,) + shape, lambda i: (i,) + (0,) * len(shape))
    tok = pl.BlockSpec((n_rows, lanes), lambda i: (row0 // n_rows, i))
    aliased = y_prev is not None
    in_specs = [tok, blk(tc, tc), blk(tc, p), blk(tc, p), blk(tc, p), blk(tc, p), blk(1, p), blk(1, p), blk(1, tc),
                blk(tc, tc), blk(bsz, p), blk(bsz, p)]
    args = [z, tab["m"], tab["w_re"], tab["w_im"], tab["zt_re"], tab["zt_im"], tab["a_re"], tab["a_im"], tab["d"],
            tab["glu"], h0_re, h0_im]
    if aliased:
        in_specs.append(pl.BlockSpec(memory_space=pl.ANY))
        args.append(y_prev)
    return pl.pallas_call(
        functools.partial(_ssm_mix_kernel, bsz=bsz, t=t, aliased=aliased),
        grid=(g // gps,),
        in_specs=in_specs,
        out_specs=[tok, blk(bsz, p), blk(bsz, p)],
        out_shape=[jax.ShapeDtypeStruct((n_tok, width), F32), jax.ShapeDtypeStruct((g, bsz, p), F32),
                   jax.ShapeDtypeStruct((g, bsz, p), F32)],
        scratch_shapes=[pltpu.VMEM((t, rows, lanes), F32), pltpu.VMEM((gps, rows, tc), F32)]
        + [pltpu.VMEM((gps, rows, p), F32)] * 4,
        input_output_aliases={len(args) - 1: 0} if aliased else {},
        compiler_params=_compiler_params("parallel"),
        name="ssm_mix",
    )(*args)


def _bias_table(q_pos, k_pos, n_heads):
    slopes = np.exp2(-8.0 * np.arange(1, n_heads + 1, dtype=np.float64) / n_heads)
    dist = q_pos[:, None] - k_pos[None, :]
    valid = (dist >= 0) & (dist < WINDOW) & (k_pos[None, :] >= 0)
    bias = -slopes[:, None, None] * dist[None].astype(np.float64)
    return np.where(valid[None], bias, -np.inf).astype(np.float32)


def _head_rms(x, gain):
    heads = [_rms(x[:, i:i + HEAD_DIM], gain) for i in range(0, x.shape[1], HEAD_DIM)]
    return jnp.concatenate(heads, axis=1)


def _softmax_sink_pv(logits, sink, v):
    m = jnp.maximum(jnp.max(logits, axis=-1, keepdims=True), sink)
    p = jnp.exp(logits - m)
    denom = jnp.sum(p, axis=-1, keepdims=True) + jnp.exp(sink - m)
    return _dot(p.astype(BF16), v) / denom


def _prompt_attn_kernel(sinks_ref, q_ref, k_ref, v_ref, qg_ref, kg_ref, bias_ref, y_ref, ks_ref, vs_ref,
                        kp_ref, vp_ref, *, n_heads):
    n = pl.program_id(1)

    @pl.when(n == 0)
    def _():
        kp_ref[...] = jnp.zeros_like(kp_ref)
        vp_ref[...] = jnp.zeros_like(vp_ref)

    q_all = q_ref[...]
    blk = q_all.shape[0]
    half = 2 * HEAD_DIM
    assert n_heads // N_KV_HEADS == 4 and half == 128
    k_cur = _head_rms(k_ref[...], kg_ref[...])
    v_cur = v_ref[...]
    k2 = jnp.concatenate([kp_ref[...], k_cur], axis=0)
    v2t = jnp.concatenate([vp_ref[...], v_cur], axis=0).T.astype(BF16)
    qs = (q_all * qg_ref[...] * QK_SCALE).astype(BF16)
    qsq = q_all * q_all
    table = jnp.minimum(n, 1)
    from_prev = (lax.broadcasted_iota(jnp.int32, (blk, blk), 0) > lax.broadcasted_iota(jnp.int32, (blk, blk), 1))
    low_lanes = lax.broadcasted_iota(jnp.int32, (2 * blk, half), 1) < HEAD_DIM
    low8 = lax.broadcasted_iota(jnp.int32, (8, half), 1) < HEAD_DIM
    pick = jnp.concatenate([jnp.where(low8, 1.0, 0.0), jnp.where(low8, 0.0, 1.0)], axis=0)
    for kv in range(N_KV_HEADS):
        k_tile = k2[:, (kv // 2) * half:(kv // 2 + 1) * half]
        k_own = jnp.where(low_lanes if kv % 2 == 0 else ~low_lanes, k_tile, 0.0)
        k_swap = pltpu.roll(k_own, HEAD_DIM, axis=1)
        k_lhs = jnp.concatenate([k_own, k_swap] if kv % 2 == 0 else [k_swap, k_own], axis=0).astype(BF16)
        tiles = slice(2 * kv * half, (2 * kv + 1) * half), slice((2 * kv + 1) * half, (2 * kv + 2) * half)
        q_rows = jnp.concatenate([qs[:, t] for t in tiles], axis=0)
        qsq_rows = jnp.concatenate([qsq[:, t] for t in tiles], axis=0)
        s = _dot_nt(k_lhs, q_rows)
        ssq = _dot_nt(pick, qsq_rows, HIGHEST)
        probs, inv_denoms = [], []
        for i in range(4):
            odd, tile = i % 2, i // 2
            lanes = slice(tile * blk, (tile + 1) * blk)
            s_h = s[odd * 2 * blk:(odd + 1) * 2 * blk, lanes]
            q_rms = lax.rsqrt(ssq[odd * 8:odd * 8 + 1, lanes] * (1.0 / HEAD_DIM) + EPS)
            logits = jnp.where(from_prev, s_h[:blk], s_h[blk:]) * q_rms + bias_ref[table, 4 * kv + i]
            sink = sinks_ref[4 * kv + i]
            m = jnp.maximum(jnp.max(logits, axis=0, keepdims=True), sink)
            p = jnp.exp(logits - m)
            inv_denoms.append(1.0 / (jnp.sum(p, axis=0, keepdims=True) + jnp.exp(sink - m)))
            probs.append(jnp.concatenate([jnp.where(from_prev, p, 0.0), jnp.where(from_prev, 0.0, p)], axis=0))
        o = _dot(v2t[kv * HEAD_DIM:(kv + 1) * HEAD_DIM], jnp.concatenate(probs, axis=1).astype(BF16))
        for tile in range(2):
            pair = [o[:, i * blk:(i + 1) * blk] * inv_denoms[i] for i in (2 * tile, 2 * tile + 1)]
            y_ref[:, tiles[tile]] = jnp.concatenate(pair, axis=0).T
    kp_ref[...] = k_cur
    vp_ref[...] = v_cur

    @pl.when(n == pl.num_programs(1) - 1)
    def _():
        ks_ref[0] = k_cur
        vs_ref[0] = v_cur


def _prompt_attn(z, sinks, q_gain, k_gain, bsz, seq, n_tok, col_q, attn_w, kv_w):
    blk = WINDOW
    nb = seq // blk
    n_heads = attn_w // HEAD_DIM
    row = lambda b, n: b * nb + n
    q_blk, k_blk, v_blk = col_q // attn_w, (col_q + attn_w) // kv_w, (col_q + attn_w + kv_w) // kv_w
    qi, kj = np.arange(blk)[:, None], np.arange(blk)[None, :]
    first, later = [np.take_along_axis(_bias_table(first_pos + np.arange(blk), first_pos - blk + np.arange(2 * blk), n_heads),
                                       np.broadcast_to(np.where(kj > qi, kj, kj + blk), (n_heads, blk, blk)), axis=2)
                    for first_pos in (0, blk)]
    bias = np.stack([first, later]).swapaxes(2, 3)
    return pl.pallas_call(
        functools.partial(_prompt_attn_kernel, n_heads=n_heads),
        grid=(bsz, nb),
        in_specs=[
            pl.BlockSpec(memory_space=pltpu.SMEM),
            pl.BlockSpec((blk, attn_w), lambda b, n: (row(b, n), q_blk)),
            pl.BlockSpec((blk, kv_w), lambda b, n: (row(b, n), k_blk)),
            pl.BlockSpec((blk, kv_w), lambda b, n: (row(b, n), v_blk)),
            pl.BlockSpec((1, attn_w), lambda b, n: (0, 0)),
            pl.BlockSpec((1, HEAD_DIM), lambda b, n: (0, 0)),
            pl.BlockSpec(bias.shape, lambda b, n: (0, 0, 0, 0)),
        ],
        out_specs=[
            pl.BlockSpec((blk, attn_w), lambda b, n: (row(b, n), 0)),
            pl.BlockSpec((1, blk, kv_w), lambda b, n: (b, 0, 0)),
            pl.BlockSpec((1, blk, kv_w), lambda b, n: (b, 0, 0)),
        ],
        out_shape=[jax.ShapeDtypeStruct((n_tok, attn_w), F32),
                   jax.ShapeDtypeStruct((bsz, blk, kv_w), F32),
                   jax.ShapeDtypeStruct((bsz, blk, kv_w), F32)],
        scratch_shapes=[pltpu.VMEM((blk, kv_w), F32)] * 2,
        compiler_params=_compiler_params("parallel", "arbitrary"),
        name="prompt_attn",
    )(sinks, z, z, z, jnp.tile(q_gain, n_heads).reshape(1, attn_w), k_gain.reshape(1, HEAD_DIM), jnp.asarray(bias))


def _sample_attn_kernel(*refs, dec_seq, aliased):
    q_ref, k_ref, v_ref, ck_ref, cv_ref, qg_ref, kg_ref, bias_ref, sink_ref = refs[:9]
    y_ref, ks_ref, vs_ref, kbuf_ref, vbuf_ref = refs[9 + 2 * aliased:]
    cache_w = ck_ref.shape[1]
    k_new = _head_rms(k_ref[...], kg_ref[...])
    v_new = v_ref[...]
    pad = jnp.zeros((kbuf_ref.shape[1] - cache_w, kbuf_ref.shape[2]), F32)
    for b in range(q_ref.shape[0]):
        rows = slice(b * dec_seq, (b + 1) * dec_seq)
        kbuf_ref[b, 0:cache_w, :] = ck_ref[b]
        vbuf_ref[b, 0:cache_w, :] = cv_ref[b]
        kbuf_ref[b, cache_w:, :] = pad
        vbuf_ref[b, cache_w:, :] = pad
        kbuf_ref[b, cache_w:cache_w + dec_seq, :] = k_new[rows]
        vbuf_ref[b, cache_w:cache_w + dec_seq, :] = v_new[rows]
        q = q_ref[b]
        q = q * lax.rsqrt(jnp.sum(q * q, axis=-1, keepdims=True) * (1.0 / HEAD_DIM) + EPS) * qg_ref[...]
        logits = _dot_nt((q * QK_SCALE).astype(BF16), kbuf_ref[b].astype(BF16)) + bias_ref[...]
        y_ref[b] = _softmax_sink_pv(logits, sink_ref[...], vbuf_ref[b].astype(BF16))
        ks_ref[b, 0:cache_w - dec_seq, :] = ck_ref[b, dec_seq:cache_w, :]
        vs_ref[b, 0:cache_w - dec_seq, :] = cv_ref[b, dec_seq:cache_w, :]
        ks_ref[b, cache_w - dec_seq:cache_w, :] = k_new[rows]
        vs_ref[b, cache_w - dec_seq:cache_w, :] = v_new[rows]


def _sample_attn(z, cache_k, cache_v, windows, layer, sinks, q_gain, k_gain, row0, dec_seq, col_q, attn_w, kv_w):
    _, dec_batch, cache_w, _ = cache_k.shape
    n_heads = attn_w // HEAD_DIM
    q_per_kv = n_heads // N_KV_HEADS
    tile = SAMPLE_BATCH_TILE
    r0 = row0 // (tile * dec_seq)
    k_blk, v_blk = (col_q + attn_w) // kv_w, (col_q + attn_w + kv_w) // kv_w
    keys = 2 * cache_w
    k_pos = np.full(keys, -1)
    k_pos[:cache_w] = PAST_LEN - cache_w + np.arange(cache_w)
    k_pos[cache_w:cache_w + dec_seq] = PAST_LEN + np.arange(dec_seq)
    bias = _bias_table(PAST_LEN + np.arange(dec_seq), k_pos, n_heads).reshape(n_heads * dec_seq, keys)
    sink_col = jnp.repeat(sinks, dec_seq).reshape(n_heads * dec_seq, 1)
    q = z[row0:, col_q:col_q + attn_w].reshape(dec_batch, dec_seq, N_KV_HEADS, q_per_kv, 1, HEAD_DIM)
    own_kv = np.eye(N_KV_HEADS, dtype=np.float32)[None, :, None, None, :, None]
    q = (q.transpose(0, 2, 3, 1, 4, 5) * own_kv).reshape(dec_batch, n_heads * dec_seq, kv_w)
    window = pl.BlockSpec((None, tile, cache_w, kv_w), lambda i: (layer, i, 0, 0))
    aliased = windows is not None
    in_specs = [
        pl.BlockSpec((tile, n_heads * dec_seq, kv_w), lambda i: (i, 0, 0)),
        pl.BlockSpec((tile * dec_seq, kv_w), lambda i: (r0 + i, k_blk)),
        pl.BlockSpec((tile * dec_seq, kv_w), lambda i: (r0 + i, v_blk)),
        window,
        window,
        pl.BlockSpec((1, kv_w), lambda i: (0, 0)),
        pl.BlockSpec((1, HEAD_DIM), lambda i: (0, 0)),
        pl.BlockSpec(bias.shape, lambda i: (0, 0)),
        pl.BlockSpec(sink_col.shape, lambda i: (0, 0)),
    ]
    args = [q, z, z, cache_k, cache_v, jnp.tile(q_gain, N_KV_HEADS).reshape(1, kv_w), k_gain.reshape(1, HEAD_DIM),
            jnp.asarray(bias), sink_col]
    if aliased:
        in_specs += [pl.BlockSpec(memory_space=pl.ANY)] * 2
        args += list(windows)
    y, ks, vs = pl.pallas_call(
        functools.partial(_sample_attn_kernel, dec_seq=dec_seq, aliased=aliased),
        grid=(dec_batch // tile,),
        in_specs=in_specs,
        out_specs=[pl.BlockSpec((tile, n_heads * dec_seq, kv_w), lambda i: (i, 0, 0)), window, window],
        out_shape=[jax.ShapeDtypeStruct(q.shape, F32),
                   jax.ShapeDtypeStruct(cache_k.shape, F32),
                   jax.ShapeDtypeStruct(cache_v.shape, F32)],
        scratch_shapes=[pltpu.VMEM((tile, keys, kv_w), F32)] * 2,
        input_output_aliases={len(args) - 2: 1, len(args) - 1: 2} if aliased else {},
        compiler_params=_compiler_params("parallel"),
        name="sample_attn",
    )(*args)
    y = y.reshape(dec_batch, N_KV_HEADS, q_per_kv, dec_seq, N_KV_HEADS, HEAD_DIM)
    y = jnp.stack([y[:, kv, :, :, kv] for kv in range(N_KV_HEADS)], axis=1)
    return y.transpose(0, 3, 1, 2, 4).reshape(dec_batch * dec_seq, attn_w), ks, vs


def kernel(x_prompt, x_sample, cache_k, cache_v, state_ssm_re, state_ssm_im, ffn1_norm, ffn1_w_gate, ffn1_w_up, ffn1_w_down, mix_norm, w_in, ssm_A_re, ssm_A_im, ssm_B_re, ssm_B_im, ssm_C_re, ssm_C_im, ssm_D, ssm_log_dt, ssm_w_glu, q_norm, k_norm, sinks, ssm_out_norm, attn_out_norm, w_out, ffn2_norm, ffn2_w_gate, ffn2_w_up, ffn2_w_down):
    bsz, seq, d = x_prompt.shape
    dec_batch, dec_seq, _ = x_sample.shape
    depth = w_in.shape[0]
    ssm_w = ssm_A_re.shape[1] * SSM_GROUP_CH
    attn_w = w_out.shape[1] - ssm_w
    kv_w = N_KV_HEADS * HEAD_DIM
    cache_w = cache_k.shape[2]
    n_prompt = bsz * seq
    n_tok = n_prompt + dec_batch * dec_seq

    xs = [x_prompt.reshape(n_prompt, d), x_sample.reshape(dec_batch * dec_seq, d)]
    n_groups = ssm_w // SSM_GROUP_CH
    n_merged = n_groups // SSM_MERGE
    zeros_state = jnp.zeros((n_merged, bsz, SSM_MERGE * SSM_STATE), F32)
    outs = [[] for _ in range(6)]
    cache_k_flat = cache_k.reshape(depth, dec_batch, cache_w, kv_w)
    cache_v_flat = cache_v.reshape(depth, dec_batch, cache_w, kv_w)
    windows = None
    ffn1_w, ffn2_w = (ffn1_w_gate, ffn1_w_up, ffn1_w_down), (ffn2_w_gate, ffn2_w_up, ffn2_w_down)
    w_in_bf, w_out_bf = w_in.astype(BF16), w_out.astype(BF16)
    for l in range(depth):
        x = _ffn(xs if l == 0 else [x], ffn1_norm[l], *ffn1_w, layer=l)
        z = _norm_matmul(x, mix_norm[l], w_in_bf, layer=l, tn=w_in.shape[2])

        prep = _ssm_prep(*_merge_groups(ssm_A_re[l], ssm_A_im[l], ssm_log_dt[l], ssm_B_re[l], ssm_B_im[l],
                                        ssm_C_re[l], ssm_C_im[l], ssm_D[l], ssm_w_glu[l], SSM_MERGE))
        tab_p = _ssm_tables(prep, PROMPT_CHUNK)
        tab_s = _ssm_tables(prep, dec_seq)
        grp_major = lambda a: a.reshape(a.shape[0], n_merged, SSM_MERGE * SSM_STATE).transpose(1, 0, 2)
        seq_major = lambda a: a.transpose(1, 0, 2).reshape(a.shape[1], n_groups, SSM_STATE)
        y_ssm, hp_re, hp_im = _ssm_mix(z, None, tab_p, zeros_state, zeros_state, 0, n_prompt, bsz, PROMPT_CHUNK)
        y_ssm, hs_re, hs_im = _ssm_mix(z, y_ssm, tab_s, grp_major(state_ssm_re[l]), grp_major(state_ssm_im[l]),
                                       n_prompt, dec_batch * dec_seq, dec_batch, dec_seq)
        hp_re, hp_im, hs_re, hs_im = map(seq_major, (hp_re, hp_im, hs_re, hs_im))

        y_attn, pk, pv = _prompt_attn(z, sinks[l], q_norm[l], k_norm[l], bsz, seq, n_tok, ssm_w, attn_w, kv_w)
        ya_s, *windows = _sample_attn(z, cache_k_flat, cache_v_flat, windows, l, sinks[l], q_norm[l], k_norm[l],
                                      n_prompt, dec_seq, ssm_w, attn_w, kv_w)
        y_attn = lax.dynamic_update_slice(y_attn, ya_s, (n_prompt, 0))

        x = _out_proj(x, y_ssm, y_attn, ssm_out_norm[l], attn_out_norm[l], w_out_bf, layer=l)
        if l < depth - 1:
            x = _ffn([x], ffn2_norm[l], *ffn2_w, layer=l)

        kv_shape = (-1, N_KV_HEADS, HEAD_DIM)
        for dst, val in zip(outs, (pk.reshape(bsz, *kv_shape), pv.reshape(bsz, *kv_shape), hp_re, hp_im, hs_re, hs_im)):
            dst.append(val)
    y_prompt, y_sample = _ffn([x], ffn2_norm[depth - 1], *ffn2_w, layer=depth - 1, out_rows=(n_prompt, n_tok - n_prompt))
    y_prompt, y_sample = y_prompt.reshape(bsz, seq, d), y_sample.reshape(dec_batch, dec_seq, d)
    pk, pv, hp_re, hp_im, hs_re, hs_im = (jnp.stack(o) for o in outs)
    sk, sv = (w.reshape(cache_k.shape) for w in windows)
    return (y_prompt, y_sample, pk, pv, hp_re, hp_im, sk, sv, hs_re, hs_im)
```

```python
import functools

import jax
import jax.numpy as jnp
import numpy as np
from jax import lax
from jax.experimental import pallas as pl
from jax.experimental.pallas import tpu as pltpu

F32 = jnp.float32
BF16 = jnp.bfloat16

EPS = 1e-6
SSM_GROUP_CH = 16
SSM_STATE = 64
HEAD_DIM = 64
N_KV_HEADS = 4
WINDOW = 128
PAST_LEN = 8192
QK_SCALE = HEAD_DIM ** -0.5
PROMPT_CHUNK = 16

TOKEN_TILE = 512
FFN_TOKEN_TILE = 1088
FFN_VMEM_LIMIT_BYTES = 60 * 1024 * 1024
FF_TILE = 256
SAMPLE_BATCH_TILE = 8
VMEM_LIMIT_BYTES = 48 * 1024 * 1024

HIGHEST = lax.Precision.HIGHEST
LANES = 128
SSM_MERGE = 2


def _rms(x, gain):
    return x * lax.rsqrt(jnp.mean(x * x, axis=-1, keepdims=True) + EPS) * gain


def _dot(a, b, precision=None):
    return jnp.dot(a, b, preferred_element_type=F32, precision=precision)


def _dot_nt(a, b, precision=None):
    return lax.dot_general(a, b, (((1,), (1,)), ((), ())), preferred_element_type=F32, precision=precision)


def _compiler_params(*semantics):
    return pltpu.CompilerParams(dimension_semantics=semantics, vmem_limit_bytes=VMEM_LIMIT_BYTES)


def _ffn_kernel(*refs, n_in, n_out, split_row):
    x_refs, (g_ref, wg_ref, wu_ref, wd_ref) = refs[:n_in], refs[n_in:n_in + 4]
    o_refs, xn_ref = refs[n_in + 4:n_in + 4 + n_out], refs[-1]
    acc_ref = o_refs[0]
    i, j = pl.program_id(0), pl.program_id(1)
    boundary = i == pl.num_programs(0) - 1

    def with_x_tile(fn):
        if n_in == 1:
            fn(x_refs[0][...])
        else:
            pl.when(jnp.logical_not(boundary))(lambda: fn(x_refs[0][...]))
            pl.when(boundary)(lambda: fn(jnp.concatenate([x_refs[0][0:split_row, :], x_refs[1][...]], axis=0)))

    @pl.when(j == 0)
    def _():
        acc_ref[...] = jnp.zeros_like(acc_ref)

        def normalise(x):
            xn_ref[...] = _rms(x, g_ref[...]).astype(BF16)
        with_x_tile(normalise)

    xn = xn_ref[...]
    gate = _dot(xn, wg_ref[...].astype(BF16))
    up = _dot(xn, wu_ref[...].astype(BF16))
    h = (gate * jax.nn.sigmoid(gate) * up).astype(BF16)
    acc_ref[...] += _dot(h, wd_ref[...].astype(BF16))

    @pl.when(j == pl.num_programs(1) - 1)
    def _():
        def residual(x):
            acc_ref[...] = x + 0.5 * acc_ref[...]
        with_x_tile(residual)
        if n_out == 2:
            @pl.when(boundary)
            def _():
                o_refs[1][...] = acc_ref[split_row:, :]


def _ffn(xs, gain, w_gate, w_up, w_down, layer, out_rows=None):
    d = xs[0].shape[1]
    n_tok = sum(x.shape[0] for x in xs)
    d_ff = w_gate.shape[2]
    if len(xs) == 2 or out_rows is not None:
        tm = FFN_TOKEN_TILE if n_tok % FFN_TOKEN_TILE == 0 else n_tok
        n0, n1 = (x.shape[0] for x in xs) if len(xs) == 2 else out_rows
        split_row = n0 - (n_tok // tm - 1) * tm
        assert split_row + n1 == tm and split_row % 8 == 0
    else:
        tm = FFN_TOKEN_TILE if n_tok % FFN_TOKEN_TILE == 0 else TOKEN_TILE
        split_row = None
    tf = FF_TILE
    tile_spec = pl.BlockSpec((tm, d), lambda i, j: (i, 0))
    tail_spec = lambda rows: pl.BlockSpec((rows, d), lambda i, j: (0, 0))
    x_specs = [tile_spec] + ([tail_spec(xs[1].shape[0])] if len(xs) == 2 else [])
    if out_rows is None:
        out_specs, out_shape = [tile_spec], [jax.ShapeDtypeStruct((n_tok, d), F32)]
    else:
        out_specs = [tile_spec, tail_spec(out_rows[1])]
        out_shape = [jax.ShapeDtypeStruct((rows, d), F32) for rows in out_rows]
    outs = pl.pallas_call(
        functools.partial(_ffn_kernel, n_in=len(xs), n_out=len(out_specs), split_row=split_row),
        grid=(n_tok // tm, d_ff // tf),
        in_specs=x_specs + [
            pl.BlockSpec((1, d), lambda i, j: (0, 0)),
            pl.BlockSpec((None, d, tf), lambda i, j: (layer, 0, j)),
            pl.BlockSpec((None, d, tf), lambda i, j: (layer, 0, j)),
            pl.BlockSpec((None, tf, d), lambda i, j: (layer, j, 0)),
        ],
        out_specs=out_specs,
        out_shape=out_shape,
        scratch_shapes=[pltpu.VMEM((tm, d), BF16)],
        compiler_params=pltpu.CompilerParams(dimension_semantics=("arbitrary", "arbitrary"),
                                             vmem_limit_bytes=FFN_VMEM_LIMIT_BYTES),
        name="ffn",
    )(*xs, gain.reshape(1, d), w_gate, w_up, w_down)
    return outs[0] if out_rows is None else outs


def _norm_matmul_kernel(x_ref, g_ref, w_ref, o_ref, xn_ref):
    @pl.when(pl.program_id(1) == 0)
    def _():
        xn_ref[...] = _rms(x_ref[...], g_ref[...]).astype(BF16)

    o_ref[...] = _dot(xn_ref[...], w_ref[...])


def _norm_matmul(x, gain, w, layer, tn):
    n_tok, d = x.shape
    n_out = w.shape[2]
    tm = TOKEN_TILE
    return pl.pallas_call(
        _norm_matmul_kernel,
        grid=(n_tok // tm, n_out // tn),
        in_specs=[
            pl.BlockSpec((tm, d), lambda i, j: (i, 0)),
            pl.BlockSpec((1, d), lambda i, j: (0, 0)),
            pl.BlockSpec((None, d, tn), lambda i, j: (layer, 0, j)),
        ],
        out_specs=pl.BlockSpec((tm, tn), lambda i, j: (i, j)),
        out_shape=jax.ShapeDtypeStruct((n_tok, n_out), F32),
        scratch_shapes=[pltpu.VMEM((tm, d), BF16)],
        compiler_params=_compiler_params("parallel", "arbitrary"),
        name="in_proj",
    )(x, gain.reshape(1, d), w)


def _out_proj_kernel(x_ref, ys_ref, ya_ref, gs_ref, ga_ref, ws_ref, wa_ref, o_ref):
    ys = _rms(ys_ref[...], gs_ref[...]).astype(BF16)
    ya = _rms(ya_ref[...], ga_ref[...]).astype(BF16)
    o_ref[...] = x_ref[...] + _dot(ys, ws_ref[...]) + _dot(ya, wa_ref[...])


def _out_proj(x, y_ssm, y_attn, g_ssm, g_attn, w_out, layer):
    n_tok, d = x.shape
    ssm_w, attn_w = y_ssm.shape[1], y_attn.shape[1]
    tm = TOKEN_TILE
    return pl.pallas_call(
        _out_proj_kernel,
        grid=(n_tok // tm,),
        in_specs=[
            pl.BlockSpec((tm, d), lambda i: (i, 0)),
            pl.BlockSpec((tm, ssm_w), lambda i: (i, 0)),
            pl.BlockSpec((tm, attn_w), lambda i: (i, 0)),
            pl.BlockSpec((1, ssm_w), lambda i: (0, 0)),
            pl.BlockSpec((1, attn_w), lambda i: (0, 0)),
            pl.BlockSpec((None, ssm_w, d), lambda i: (layer, 0, 0)),
            pl.BlockSpec((None, attn_w, d), lambda i: (layer, ssm_w // attn_w, 0)),
        ],
        out_specs=pl.BlockSpec((tm, d), lambda i: (i, 0)),
        out_shape=jax.ShapeDtypeStruct((n_tok, d), F32),
        compiler_params=_compiler_params("parallel"),
        name="out_proj",
    )(x, y_ssm, y_attn, g_ssm.reshape(1, ssm_w), g_attn.reshape(1, attn_w), w_out, w_out)


def _ssm_prep_kernel(*refs, t_small):
    for gi in range(refs[0].shape[0]):
        _ssm_prep_group(*[r.at[pl.ds(gi, 1)] for r in refs], t_small=t_small)


def _ssm_prep_group(are_ref, aim_ref, ldt_ref, btre_ref, btim_ref, cre_ref, cim_ref, d_ref, wglu_ref,
                    *table_refs, t_small):
    per_set = len(table_refs) // 2
    full_refs, small_refs = table_refs[:per_set], table_refs[per_set:]
    a_re, a_im = are_ref[0], aim_ref[0]
    dt = jnp.exp(ldt_ref[0])
    mag = jnp.exp(a_re * dt)
    lb_re, lb_im = mag * jnp.cos(a_im * dt), mag * jnp.sin(a_im * dt)
    den = a_re * a_re + a_im * a_im
    q_re = ((lb_re - 1.0) * a_re + lb_im * a_im) / den
    q_im = (lb_im * a_re - (lb_re - 1.0) * a_im) / den
    bt_re, bt_im = btre_ref[0], btim_ref[0]
    bb_re = q_re * bt_re - q_im * bt_im
    bb_im = q_re * bt_im + q_im * bt_re
    c_re, c_im = cre_ref[0], cim_ref[0]

    pw_re, pw_im = [jnp.ones_like(lb_re)], [jnp.zeros_like(lb_im)]
    for _ in range(PROMPT_CHUNK):
        pr, pi = pw_re[-1], pw_im[-1]
        pw_re.append(pr * lb_re - pi * lb_im)
        pw_im.append(pr * lb_im + pi * lb_re)

    steps = PROMPT_CHUNK
    c = c_re.shape[0]
    w_re = jnp.concatenate([bb_re * pw_re[steps - 1 - s] - bb_im * pw_im[steps - 1 - s] for s in range(steps)], axis=0)
    w_im = jnp.concatenate([bb_re * pw_im[steps - 1 - s] + bb_im * pw_re[steps - 1 - s] for s in range(steps)], axis=0)
    zt_re = jnp.concatenate([c_re * pw_re[t + 1] - c_im * pw_im[t + 1] for t in range(steps)], axis=0)
    zt_im = jnp.concatenate([-(c_re * pw_im[t + 1] + c_im * pw_re[t + 1]) for t in range(steps)], axis=0)

    lag = _dot_nt(w_re, c_re, HIGHEST) - _dot_nt(w_im, c_im, HIGHEST)
    cols = []
    for t in range(steps):
        live = lag[(steps - 1 - t) * c:, :]
        cols.append(live if t == steps - 1 else
                    jnp.concatenate([live, jnp.zeros(((steps - 1 - t) * c, c), F32)], axis=0))
    m = jnp.concatenate(cols, axis=1)

    tiled = jnp.concatenate([jnp.concatenate([wglu_ref[0]] * steps, axis=0)] * steps, axis=1)
    shift = c.bit_length() - 1
    row_step = lax.shift_right_logical(lax.broadcasted_iota(jnp.int32, tiled.shape, 0), shift)
    col_step = lax.shift_right_logical(lax.broadcasted_iota(jnp.int32, tiled.shape, 1), shift)
    glu = jnp.where(row_step == col_step, tiled, 0.0)
    d_t = jnp.concatenate([d_ref[0]] * steps, axis=1)

    for t, refs in ((steps, full_refs), (t_small, small_refs)):
        m_ref, wre_ref, wim_ref, ztre_ref, ztim_ref, glu_ref, dt_ref, ore_ref, oim_ref = refs
        n, full = t * c, steps * c
        m_ref[0] = m[:n, :n].astype(BF16)
        wre_ref[0] = w_re[full - n:].astype(BF16)
        wim_ref[0] = w_im[full - n:].astype(BF16)
        ztre_ref[0] = zt_re[:n].astype(BF16)
        ztim_ref[0] = zt_im[:n].astype(BF16)
        glu_ref[0] = glu[:n, :n].astype(BF16)
        dt_ref[0] = d_t[:, :n]
        ore_ref[0] = pw_re[t]
        oim_ref[0] = pw_im[t]


def _merge_groups(a_re, a_im, log_dt, b_re, b_im, c_re, c_im, d_skip, w_glu, n):
    g, p = a_re.shape
    gm = g // n
    own = np.eye(n, dtype=np.float32)[None, :, :, None, None]

    def block_diag(x):
        r, q = x.shape[1:]
        return (x.reshape(gm, n, 1, r, q) * own).transpose(0, 1, 3, 2, 4).reshape(gm, n * r, n * q)

    lanes = lambda x: x.reshape(gm, 1, -1)
    return (lanes(a_re), lanes(a_im), lanes(jnp.repeat(log_dt, p)),
            block_diag(jnp.swapaxes(b_re, 1, 2)), block_diag(jnp.swapaxes(b_im, 1, 2)),
            block_diag(c_re), block_diag(c_im), lanes(d_skip), block_diag(w_glu))


def _ssm_prep(a_re, a_im, log_dt, bt_re, bt_im, c_re, c_im, d_skip, w_glu, t_small):
    g, c, p = c_re.shape
    assert c & (c - 1) == 0
    gps = LANES // c
    gs = lambda *shape: pl.BlockSpec((gps,) + shape, lambda i: (i,) + (0,) * len(shape))
    names = ("m", "w_re", "w_im", "zt_re", "zt_im", "glu", "d", "a_re", "a_im")
    out_specs, out_shape = [], []
    for t in (PROMPT_CHUNK, t_small):
        tc = t * c
        shapes = [(tc, tc), (tc, p), (tc, p), (tc, p), (tc, p), (tc, tc), (1, tc), (1, p), (1, p)]
        dtypes = [BF16] * 6 + [F32] * 3
        out_specs += [gs(*shape) for shape in shapes]
        out_shape += [jax.ShapeDtypeStruct((g,) + shape, dtype) for shape, dtype in zip(shapes, dtypes)]
    outs = pl.pallas_call(
        functools.partial(_ssm_prep_kernel, t_small=t_small),
        grid=(g // gps,),
        in_specs=[gs(1, p), gs(1, p), gs(1, p), gs(c, p), gs(c, p), gs(c, p), gs(c, p), gs(1, c), gs(c, c)],
        out_specs=out_specs,
        out_shape=out_shape,
        compiler_params=_compiler_params("parallel"),
        name="ssm_prep",
    )(a_re, a_im, log_dt, bt_re, bt_im, c_re, c_im, d_skip, w_glu)
    return dict(zip(names, outs[:len(names)])), dict(zip(names, outs[len(names):]))


def _ssm_mix_kernel(*refs, bsz, t, aliased):
    (u_ref, m_ref, wre_ref, wim_ref, ztre_ref, ztim_ref, are_ref, aim_ref, d_ref, glu_ref, h0re_ref, h0im_ref) = refs[:12]
    y_ref, hre_ref, him_ref, xs_ref, ug_ref, vre_ref, vim_ref, pre_ref, pim_ref = refs[12 + aliased:]
    rows = u_ref.shape[0] // t
    chunks = rows // bsz
    n_grp, tc = m_ref.shape[0], m_ref.shape[1]
    c = tc // t

    def token_rows(b, s):
        return pl.ds(b * chunks * t + s, chunks, stride=t)

    def chunk_rows(b):
        return pl.ds(b, chunks, stride=bsz)

    for s in range(t):
        if chunks == 1:
            xs_ref[s] = u_ref[pl.ds(s, rows, stride=t), :]
        else:
            for b in range(bsz):
                xs_ref[s, chunk_rows(b), :] = u_ref[token_rows(b, s), :]
    for gi in range(n_grp):
        lanes = slice(gi * c, (gi + 1) * c)
        u = jnp.concatenate([xs_ref[s, :, lanes] for s in range(t)], axis=1)
        ug_ref[gi] = u
        vre_ref[gi] = _dot(u.astype(BF16), wre_ref[gi])
        vim_ref[gi] = _dot(u.astype(BF16), wim_ref[gi])

    per_tile = max(1, 8 // bsz)
    tile_rows = per_tile * bsz

    def chunk_tile(i, h):
        sel = pl.ds(pl.multiple_of(i * tile_rows, tile_rows), tile_rows)
        nxt = []
        for gi in range(n_grp):
            h_re, h_im = h[2 * gi], h[2 * gi + 1]
            v_re, v_im = vre_ref[gi, sel, :], vim_ref[gi, sel, :]
            a_re, a_im = are_ref[gi], aim_ref[gi]
            ent_re, ent_im = [], []
            for j in range(per_tile):
                ent_re.append(h_re)
                ent_im.append(h_im)
                part = slice(j * bsz, (j + 1) * bsz)
                h_re, h_im = a_re * h_re - a_im * h_im + v_re[part], a_re * h_im + a_im * h_re + v_im[part]
            pre_ref[gi, sel, :] = jnp.concatenate(ent_re, axis=0)
            pim_ref[gi, sel, :] = jnp.concatenate(ent_im, axis=0)
            nxt += [h_re, h_im]
        return tuple(nxt)

    h = tuple(r[gi] for gi in range(n_grp) for r in (h0re_ref, h0im_ref))
    n_tiles = chunks // per_tile
    h = lax.fori_loop(0, n_tiles, chunk_tile, h, unroll=min(n_tiles, 4))
    for gi in range(n_grp):
        hre_ref[gi] = h[2 * gi]
        him_ref[gi] = h[2 * gi + 1]

    for gi in range(n_grp):
        u = ug_ref[gi]
        y = (_dot(u.astype(BF16), m_ref[gi]) + _dot_nt(pre_ref[gi].astype(BF16), ztre_ref[gi])
             + _dot_nt(pim_ref[gi].astype(BF16), ztim_ref[gi]) + d_ref[gi] * u)
        y = jax.nn.gelu(y)
        y = y * jax.nn.sigmoid(_dot(y.astype(BF16), glu_ref[gi]))
        for s in range(t):
            xs_ref[s, :, gi * c:(gi + 1) * c] = y[:, s * c:(s + 1) * c]
    for s in range(t):
        if chunks == 1:
            y_ref[pl.ds(s, rows, stride=t), :] = xs_ref[s]
        else:
            for b in range(bsz):
                y_ref[token_rows(b, s), :] = xs_ref[s, chunk_rows(b), :]


def _ssm_mix(z, y_prev, tab, layer, h0_re, h0_im, row0, n_rows, bsz, t):
    g, _, p = h0_re.shape
    tc = tab["m"].shape[1]
    n_tok = z.shape[0]
    c = tc // t
    width = g * c
    lanes = LANES
    gps = lanes // c
    rows = n_rows // t
    first = layer * (g // gps)
    blk = lambda *shape: pl.BlockSpec((gps,) + shape, lambda i: (i,) + (0,) * len(shape))
    tbl = lambda *shape: pl.BlockSpec((gps,) + shape, lambda i: (first + i,) + (0,) * len(shape))
    tok = pl.BlockSpec((n_rows, lanes), lambda i: (row0 // n_rows, i))
    aliased = y_prev is not None
    in_specs = [tok, tbl(tc, tc), tbl(tc, p), tbl(tc, p), tbl(tc, p), tbl(tc, p), tbl(1, p), tbl(1, p), tbl(1, tc),
                tbl(tc, tc), blk(bsz, p), blk(bsz, p)]
    args = [z, tab["m"], tab["w_re"], tab["w_im"], tab["zt_re"], tab["zt_im"], tab["a_re"], tab["a_im"], tab["d"],
            tab["glu"], h0_re, h0_im]
    if aliased:
        in_specs.append(pl.BlockSpec(memory_space=pl.ANY))
        args.append(y_prev)
    return pl.pallas_call(
        functools.partial(_ssm_mix_kernel, bsz=bsz, t=t, aliased=aliased),
        grid=(g // gps,),
        in_specs=in_specs,
        out_specs=[tok, blk(bsz, p), blk(bsz, p)],
        out_shape=[jax.ShapeDtypeStruct((n_tok, width), F32), jax.ShapeDtypeStruct((g, bsz, p), F32),
                   jax.ShapeDtypeStruct((g, bsz, p), F32)],
        scratch_shapes=[pltpu.VMEM((t, rows, lanes), F32), pltpu.VMEM((gps, rows, tc), F32)]
        + [pltpu.VMEM((gps, rows, p), F32)] * 4,
        input_output_aliases={len(args) - 1: 0} if aliased else {},
        compiler_params=_compiler_params("parallel"),
        name="ssm_mix",
    )(*args)


def _bias_table(q_pos, k_pos, n_heads):
    slopes = np.exp2(-8.0 * np.arange(1, n_heads + 1, dtype=np.float64) / n_heads)
    dist = q_pos[:, None] - k_pos[None, :]
    valid = (dist >= 0) & (dist < WINDOW) & (k_pos[None, :] >= 0)
    bias = -slopes[:, None, None] * dist[None].astype(np.float64)
    return np.where(valid[None], bias, -np.inf).astype(np.float32)


def _head_rms(x, gain):
    heads = [_rms(x[:, i:i + HEAD_DIM], gain) for i in range(0, x.shape[1], HEAD_DIM)]
    return jnp.concatenate(heads, axis=1)


def _softmax_sink_pv(logits, sink, v):
    m = jnp.maximum(jnp.max(logits, axis=-1, keepdims=True), sink)
    p = jnp.exp(logits - m)
    denom = jnp.sum(p, axis=-1, keepdims=True) + jnp.exp(sink - m)
    return _dot(p.astype(BF16), v) / denom


def _prompt_attn_kernel(sinks_ref, q_ref, k_ref, v_ref, qg_ref, kg_ref, bias_ref, y_ref, ks_ref, vs_ref,
                        kp_ref, vp_ref, *, n_heads):
    n = pl.program_id(1)

    @pl.when(n == 0)
    def _():
        kp_ref[...] = jnp.zeros_like(kp_ref)
        vp_ref[...] = jnp.zeros_like(vp_ref)

    q_all = q_ref[...]
    blk = q_all.shape[0]
    half = 2 * HEAD_DIM
    assert n_heads // N_KV_HEADS == 4 and half == 128
    k_cur = _head_rms(k_ref[...], kg_ref[...])
    v_cur = v_ref[...]
    k2 = jnp.concatenate([kp_ref[...], k_cur], axis=0)
    v2t = jnp.concatenate([vp_ref[...], v_cur], axis=0).T.astype(BF16)
    qs = (q_all * qg_ref[...] * QK_SCALE).astype(BF16)
    qsq = q_all * q_all
    table = jnp.minimum(n, 1)
    from_prev = (lax.broadcasted_iota(jnp.int32, (blk, blk), 0) > lax.broadcasted_iota(jnp.int32, (blk, blk), 1))
    low_lanes = lax.broadcasted_iota(jnp.int32, (2 * blk, half), 1) < HEAD_DIM
    low8 = lax.broadcasted_iota(jnp.int32, (8, half), 1) < HEAD_DIM
    pick = jnp.concatenate([jnp.where(low8, 1.0, 0.0), jnp.where(low8, 0.0, 1.0)], axis=0)
    for kv in range(N_KV_HEADS):
        k_tile = k2[:, (kv // 2) * half:(kv // 2 + 1) * half]
        k_own = jnp.where(low_lanes if kv % 2 == 0 else ~low_lanes, k_tile, 0.0)
        k_swap = pltpu.roll(k_own, HEAD_DIM, axis=1)
        k_lhs = jnp.concatenate([k_own, k_swap] if kv % 2 == 0 else [k_swap, k_own], axis=0).astype(BF16)
        tiles = slice(2 * kv * half, (2 * kv + 1) * half), slice((2 * kv + 1) * half, (2 * kv + 2) * half)
        q_rows = jnp.concatenate([qs[:, t] for t in tiles], axis=0)
        qsq_rows = jnp.concatenate([qsq[:, t] for t in tiles], axis=0)
        s = _dot_nt(k_lhs, q_rows)
        ssq = _dot_nt(pick, qsq_rows, HIGHEST)
        probs, inv_denoms = [], []
        for i in range(4):
            odd, tile = i % 2, i // 2
            lanes = slice(tile * blk, (tile + 1) * blk)
            s_h = s[odd * 2 * blk:(odd + 1) * 2 * blk, lanes]
            q_rms = lax.rsqrt(ssq[odd * 8:odd * 8 + 1, lanes] * (1.0 / HEAD_DIM) + EPS)
            logits = jnp.where(from_prev, s_h[:blk], s_h[blk:]) * q_rms + bias_ref[table, 4 * kv + i]
            sink = sinks_ref[4 * kv + i]
            m = jnp.maximum(jnp.max(logits, axis=0, keepdims=True), sink)
            p = jnp.exp(logits - m)
            inv_denoms.append(1.0 / (jnp.sum(p, axis=0, keepdims=True) + jnp.exp(sink - m)))
            probs.append(jnp.concatenate([jnp.where(from_prev, p, 0.0), jnp.where(from_prev, 0.0, p)], axis=0))
        o = _dot(v2t[kv * HEAD_DIM:(kv + 1) * HEAD_DIM], jnp.concatenate(probs, axis=1).astype(BF16))
        for tile in range(2):
            pair = [o[:, i * blk:(i + 1) * blk] * inv_denoms[i] for i in (2 * tile, 2 * tile + 1)]
            y_ref[:, tiles[tile]] = jnp.concatenate(pair, axis=0).T
    kp_ref[...] = k_cur
    vp_ref[...] = v_cur

    @pl.when(n == pl.num_programs(1) - 1)
    def _():
        ks_ref[0] = k_cur
        vs_ref[0] = v_cur


def _prompt_attn(z, sinks, q_gain, k_gain, bsz, seq, n_tok, col_q, attn_w, kv_w):
    blk = WINDOW
    nb = seq // blk
    n_heads = attn_w // HEAD_DIM
    row = lambda b, n: b * nb + n
    q_blk, k_blk, v_blk = col_q // attn_w, (col_q + attn_w) // kv_w, (col_q + attn_w + kv_w) // kv_w
    qi, kj = np.arange(blk)[:, None], np.arange(blk)[None, :]
    first, later = [np.take_along_axis(_bias_table(first_pos + np.arange(blk), first_pos - blk + np.arange(2 * blk), n_heads),
                                       np.broadcast_to(np.where(kj > qi, kj, kj + blk), (n_heads, blk, blk)), axis=2)
                    for first_pos in (0, blk)]
    bias = np.stack([first, later]).swapaxes(2, 3)
    return pl.pallas_call(
        functools.partial(_prompt_attn_kernel, n_heads=n_heads),
        grid=(bsz, nb),
        in_specs=[
            pl.BlockSpec(memory_space=pltpu.SMEM),
            pl.BlockSpec((blk, attn_w), lambda b, n: (row(b, n), q_blk)),
            pl.BlockSpec((blk, kv_w), lambda b, n: (row(b, n), k_blk)),
            pl.BlockSpec((blk, kv_w), lambda b, n: (row(b, n), v_blk)),
            pl.BlockSpec((1, attn_w), lambda b, n: (0, 0)),
            pl.BlockSpec((1, HEAD_DIM), lambda b, n: (0, 0)),
            pl.BlockSpec(bias.shape, lambda b, n: (0, 0, 0, 0)),
        ],
        out_specs=[
            pl.BlockSpec((blk, attn_w), lambda b, n: (row(b, n), 0)),
            pl.BlockSpec((1, blk, kv_w), lambda b, n: (b, 0, 0)),
            pl.BlockSpec((1, blk, kv_w), lambda b, n: (b, 0, 0)),
        ],
        out_shape=[jax.ShapeDtypeStruct((n_tok, attn_w), F32),
                   jax.ShapeDtypeStruct((bsz, blk, kv_w), F32),
                   jax.ShapeDtypeStruct((bsz, blk, kv_w), F32)],
        scratch_shapes=[pltpu.VMEM((blk, kv_w), F32)] * 2,
        compiler_params=_compiler_params("parallel", "arbitrary"),
        name="prompt_attn",
    )(sinks, z, z, z, jnp.tile(q_gain, n_heads).reshape(1, attn_w), k_gain.reshape(1, HEAD_DIM), jnp.asarray(bias))


def _sample_attn_kernel(*refs, dec_seq, aliased):
    q_ref, k_ref, v_ref, ck_ref, cv_ref, qg_ref, kg_ref, bias_ref, sink_ref = refs[:9]
    y_ref, ks_ref, vs_ref, kbuf_ref, vbuf_ref, p_ref = refs[9 + 2 * aliased:]
    tile, q_rows, _ = q_ref.shape
    cache_w = ck_ref.shape[1]
    keys = kbuf_ref.shape[0] // tile
    k_new = _head_rms(k_ref[...], kg_ref[...])
    v_new = v_ref[...]
    pad = jnp.zeros((keys - cache_w, kbuf_ref.shape[1]), F32)
    for b in range(tile):
        rows = slice(b * dec_seq, (b + 1) * dec_seq)
        kbuf_ref[b * keys:b * keys + cache_w, :] = ck_ref[b]
        vbuf_ref[b * keys:b * keys + cache_w, :] = cv_ref[b]
        kbuf_ref[b * keys + cache_w:(b + 1) * keys, :] = pad
        vbuf_ref[b * keys + cache_w:(b + 1) * keys, :] = pad
        kbuf_ref[b * keys + cache_w:b * keys + cache_w + dec_seq, :] = k_new[rows]
        vbuf_ref[b * keys + cache_w:b * keys + cache_w + dec_seq, :] = v_new[rows]
    q = q_ref[...].reshape(tile * q_rows, q_ref.shape[2])
    q = q * lax.rsqrt(jnp.sum(q * q, axis=-1, keepdims=True) * (1.0 / HEAD_DIM) + EPS) * qg_ref[...]
    s = _dot_nt((q * QK_SCALE).astype(BF16), kbuf_ref[...].astype(BF16))
    p_ref[...] = jnp.zeros_like(p_ref)
    inv_denoms = []
    for b in range(tile):
        own_rows, own_keys = slice(b * q_rows, (b + 1) * q_rows), slice(b * keys, (b + 1) * keys)
        logits = s[own_rows, own_keys] + bias_ref[...]
        sink = sink_ref[...]
        m = jnp.maximum(jnp.max(logits, axis=-1, keepdims=True), sink)
        p = jnp.exp(logits - m)
        inv_denoms.append(1.0 / (jnp.sum(p, axis=-1, keepdims=True) + jnp.exp(sink - m)))
        p_ref[own_rows, own_keys] = p.astype(BF16)
    o = _dot(p_ref[...], vbuf_ref[...].astype(BF16))
    for b in range(tile):
        rows = slice(b * dec_seq, (b + 1) * dec_seq)
        y_ref[b] = o[b * q_rows:(b + 1) * q_rows] * inv_denoms[b]
        ks_ref[b, 0:cache_w - dec_seq, :] = ck_ref[b, dec_seq:cache_w, :]
        vs_ref[b, 0:cache_w - dec_seq, :] = cv_ref[b, dec_seq:cache_w, :]
        ks_ref[b, cache_w - dec_seq:cache_w, :] = k_new[rows]
        vs_ref[b, cache_w - dec_seq:cache_w, :] = v_new[rows]


def _sample_attn(z, cache_k, cache_v, windows, layer, sinks, q_gain, k_gain, row0, dec_seq, col_q, attn_w, kv_w):
    _, dec_batch, cache_w, _ = cache_k.shape
    n_heads = attn_w // HEAD_DIM
    q_per_kv = n_heads // N_KV_HEADS
    tile = SAMPLE_BATCH_TILE
    r0 = row0 // (tile * dec_seq)
    k_blk, v_blk = (col_q + attn_w) // kv_w, (col_q + attn_w + kv_w) // kv_w
    keys = 2 * cache_w
    k_pos = np.full(keys, -1)
    k_pos[:cache_w] = PAST_LEN - cache_w + np.arange(cache_w)
    k_pos[cache_w:cache_w + dec_seq] = PAST_LEN + np.arange(dec_seq)
    bias = _bias_table(PAST_LEN + np.arange(dec_seq), k_pos, n_heads).reshape(n_heads * dec_seq, keys)
    sink_col = jnp.repeat(sinks, dec_seq).reshape(n_heads * dec_seq, 1)
    q = z[row0:, col_q:col_q + attn_w].reshape(dec_batch, dec_seq, N_KV_HEADS, q_per_kv, 1, HEAD_DIM)
    own_kv = np.eye(N_KV_HEADS, dtype=np.float32)[None, :, None, None, :, None]
    q = (q.transpose(0, 2, 3, 1, 4, 5) * own_kv).reshape(dec_batch, n_heads * dec_seq, kv_w)
    window = pl.BlockSpec((None, tile, cache_w, kv_w), lambda i: (layer, i, 0, 0))
    aliased = windows is not None
    in_specs = [
        pl.BlockSpec((tile, n_heads * dec_seq, kv_w), lambda i: (i, 0, 0)),
        pl.BlockSpec((tile * dec_seq, kv_w), lambda i: (r0 + i, k_blk)),
        pl.BlockSpec((tile * dec_seq, kv_w), lambda i: (r0 + i, v_blk)),
        window,
        window,
        pl.BlockSpec((1, kv_w), lambda i: (0, 0)),
        pl.BlockSpec((1, HEAD_DIM), lambda i: (0, 0)),
        pl.BlockSpec(bias.shape, lambda i: (0, 0)),
        pl.BlockSpec(sink_col.shape, lambda i: (0, 0)),
    ]
    args = [q, z, z, cache_k, cache_v, jnp.tile(q_gain, N_KV_HEADS).reshape(1, kv_w), k_gain.reshape(1, HEAD_DIM),
            jnp.asarray(bias), sink_col]
    if aliased:
        in_specs += [pl.BlockSpec(memory_space=pl.ANY)] * 2
        args += list(windows)
    y, ks, vs = pl.pallas_call(
        functools.partial(_sample_attn_kernel, dec_seq=dec_seq, aliased=aliased),
        grid=(dec_batch // tile,),
        in_specs=in_specs,
        out_specs=[pl.BlockSpec((tile, n_heads * dec_seq, kv_w), lambda i: (i, 0, 0)), window, window],
        out_shape=[jax.ShapeDtypeStruct(q.shape, F32),
                   jax.ShapeDtypeStruct(cache_k.shape, F32),
                   jax.ShapeDtypeStruct(cache_v.shape, F32)],
        scratch_shapes=[pltpu.VMEM((tile * keys, kv_w), F32)] * 2
        + [pltpu.VMEM((tile * n_heads * dec_seq, tile * keys), BF16)],
        input_output_aliases={len(args) - 2: 1, len(args) - 1: 2} if aliased else {},
        compiler_params=_compiler_params("parallel"),
        name="sample_attn",
    )(*args)
    y = y.reshape(dec_batch, N_KV_HEADS, q_per_kv, dec_seq, N_KV_HEADS, HEAD_DIM)
    y = jnp.stack([y[:, kv, :, :, kv] for kv in range(N_KV_HEADS)], axis=1)
    return y.transpose(0, 3, 1, 2, 4).reshape(dec_batch * dec_seq, attn_w), ks, vs


def kernel(x_prompt, x_sample, cache_k, cache_v, state_ssm_re, state_ssm_im, ffn1_norm, ffn1_w_gate, ffn1_w_up, ffn1_w_down, mix_norm, w_in, ssm_A_re, ssm_A_im, ssm_B_re, ssm_B_im, ssm_C_re, ssm_C_im, ssm_D, ssm_log_dt, ssm_w_glu, q_norm, k_norm, sinks, ssm_out_norm, attn_out_norm, w_out, ffn2_norm, ffn2_w_gate, ffn2_w_up, ffn2_w_down):
    bsz, seq, d = x_prompt.shape
    dec_batch, dec_seq, _ = x_sample.shape
    depth = w_in.shape[0]
    ssm_w = ssm_A_re.shape[1] * SSM_GROUP_CH
    attn_w = w_out.shape[1] - ssm_w
    kv_w = N_KV_HEADS * HEAD_DIM
    cache_w = cache_k.shape[2]
    n_prompt = bsz * seq
    n_tok = n_prompt + dec_batch * dec_seq

    xs = [x_prompt.reshape(n_prompt, d), x_sample.reshape(dec_batch * dec_seq, d)]
    n_groups = ssm_w // SSM_GROUP_CH
    n_merged = n_groups // SSM_MERGE
    zeros_state = jnp.zeros((n_merged, bsz, SSM_MERGE * SSM_STATE), F32)
    outs = [[] for _ in range(6)]
    cache_k_flat = cache_k.reshape(depth, dec_batch, cache_w, kv_w)
    cache_v_flat = cache_v.reshape(depth, dec_batch, cache_w, kv_w)
    windows = None
    ffn1_w, ffn2_w = (ffn1_w_gate, ffn1_w_up, ffn1_w_down), (ffn2_w_gate, ffn2_w_up, ffn2_w_down)
    w_in_bf, w_out_bf = w_in.astype(BF16), w_out.astype(BF16)
    all_layers = lambda a: a.reshape((-1,) + a.shape[2:])
    tab_p, tab_s = _ssm_prep(*_merge_groups(*map(all_layers, (ssm_A_re, ssm_A_im, ssm_log_dt, ssm_B_re, ssm_B_im,
                                                               ssm_C_re, ssm_C_im, ssm_D, ssm_w_glu)), SSM_MERGE),
                             t_small=dec_seq)
    for l in range(depth):
        x = _ffn(xs if l == 0 else [x], ffn1_norm[l], *ffn1_w, layer=l)
        z = _norm_matmul(x, mix_norm[l], w_in_bf, layer=l, tn=w_in.shape[2])

        grp_major = lambda a: a.reshape(a.shape[0], n_merged, SSM_MERGE * SSM_STATE).transpose(1, 0, 2)
        seq_major = lambda a: a.transpose(1, 0, 2).reshape(a.shape[1], n_groups, SSM_STATE)
        y_ssm, hp_re, hp_im = _ssm_mix(z, None, tab_p, l, zeros_state, zeros_state, 0, n_prompt, bsz, PROMPT_CHUNK)
        y_ssm, hs_re, hs_im = _ssm_mix(z, y_ssm, tab_s, l, grp_major(state_ssm_re[l]), grp_major(state_ssm_im[l]),
                                       n_prompt, dec_batch * dec_seq, dec_batch, dec_seq)
        hp_re, hp_im, hs_re, hs_im = map(seq_major, (hp_re, hp_im, hs_re, hs_im))

        y_attn, pk, pv = _prompt_attn(z, sinks[l], q_norm[l], k_norm[l], bsz, seq, n_tok, ssm_w, attn_w, kv_w)
        ya_s, *windows = _sample_attn(z, cache_k_flat, cache_v_flat, windows, l, sinks[l], q_norm[l], k_norm[l],
                                      n_prompt, dec_seq, ssm_w, attn_w, kv_w)
        y_attn = lax.dynamic_update_slice(y_attn, ya_s, (n_prompt, 0))

        x = _out_proj(x, y_ssm, y_attn, ssm_out_norm[l], attn_out_norm[l], w_out_bf, layer=l)
        if l < depth - 1:
            x = _ffn([x], ffn2_norm[l], *ffn2_w, layer=l)

        kv_shape = (-1, N_KV_HEADS, HEAD_DIM)
        for dst, val in zip(outs, (pk.reshape(bsz, *kv_shape), pv.reshape(bsz, *kv_shape), hp_re, hp_im, hs_re, hs_im)):
            dst.append(val)
    y_prompt, y_sample = _ffn([x], ffn2_norm[depth - 1], *ffn2_w, layer=depth - 1, out_rows=(n_prompt, n_tok - n_prompt))
    y_prompt, y_sample = y_prompt.reshape(bsz, seq, d), y_sample.reshape(dec_batch, dec_seq, d)
    pk, pv, hp_re, hp_im, hs_re, hs_im = (jnp.stack(o) for o in outs)
    sk, sv = (w.reshape(cache_k.shape) for w in windows)
    return (y_prompt, y_sample, pk, pv, hp_re, hp_im, sk, sv, hs_re, hs_im)
```

```python
import functools

import jax
import jax.numpy as jnp
import numpy as np
from jax import lax
from jax.experimental import pallas as pl
from jax.experimental.pallas import tpu as pltpu

F32 = jnp.float32
BF16 = jnp.bfloat16

EPS = 1e-6
SSM_GROUP_CH = 16
SSM_STATE = 64
HEAD_DIM = 64
N_KV_HEADS = 4
WINDOW = 128
PAST_LEN = 8192
QK_SCALE = HEAD_DIM ** -0.5
PROMPT_CHUNK = 16

TOKEN_TILE = 512
FFN_TOKEN_TILE = 1088
FFN_VMEM_LIMIT_BYTES = 63 * 1024 * 1024
FF_TILE = 256
FFN_WEIGHT_SLOTS = 3
SAMPLE_BATCH_TILE = 8
VMEM_LIMIT_BYTES = 48 * 1024 * 1024

HIGHEST = lax.Precision.HIGHEST
LANES = 128
SSM_MERGE = 2


def _rms(x, gain):
    return x * lax.rsqrt(jnp.mean(x * x, axis=-1, keepdims=True) + EPS) * gain


def _dot(a, b, precision=None):
    return jnp.dot(a, b, preferred_element_type=F32, precision=precision)


def _dot_nt(a, b, precision=None):
    return lax.dot_general(a, b, (((1,), (1,)), ((), ())), preferred_element_type=F32, precision=precision)


def _compiler_params(*semantics):
    return pltpu.CompilerParams(dimension_semantics=semantics, vmem_limit_bytes=VMEM_LIMIT_BYTES)


def _ffn_kernel(*refs, n_in, n_out, split_row, layer):
    x_refs, (g_ref, wg_hbm, wu_hbm, wd_hbm) = refs[:n_in], refs[n_in:n_in + 4]
    o_refs = refs[n_in + 4:n_in + 4 + n_out]
    xn_ref, wg_buf, wu_buf, wd_buf, sems = refs[n_in + 4 + n_out:]
    acc_ref = o_refs[0]
    i, j = pl.program_id(0), pl.program_id(1)
    n_ff = pl.num_programs(1)
    boundary = i == pl.num_programs(0) - 1
    tf = wg_buf.shape[2]
    step, n_steps = i * n_ff + j, pl.num_programs(0) * n_ff

    def weight_copies(s):
        slot = lax.rem(s, FFN_WEIGHT_SLOTS)
        cols = pl.ds(pl.multiple_of(lax.rem(s, n_ff) * tf, tf), tf)
        return (pltpu.make_async_copy(wg_hbm.at[layer, :, cols], wg_buf.at[slot], sems.at[slot, 0]),
                pltpu.make_async_copy(wu_hbm.at[layer, :, cols], wu_buf.at[slot], sems.at[slot, 1]),
                pltpu.make_async_copy(wd_hbm.at[layer, cols, :], wd_buf.at[slot], sems.at[slot, 2]))

    @pl.when(step == 0)
    def _():
        for s in range(FFN_WEIGHT_SLOTS - 1):
            for copy in weight_copies(s):
                copy.start()

    @pl.when(step + FFN_WEIGHT_SLOTS - 1 < n_steps)
    def _():
        for copy in weight_copies(step + FFN_WEIGHT_SLOTS - 1):
            copy.start()

    for copy in weight_copies(step):
        copy.wait()
    slot = lax.rem(step, FFN_WEIGHT_SLOTS)

    def with_x_tile(fn):
        if n_in == 1:
            fn(x_refs[0][...])
        else:
            pl.when(jnp.logical_not(boundary))(lambda: fn(x_refs[0][...]))
            pl.when(boundary)(lambda: fn(jnp.concatenate([x_refs[0][0:split_row, :], x_refs[1][...]], axis=0)))

    @pl.when(j == 0)
    def _():
        acc_ref[...] = jnp.zeros_like(acc_ref)

        def normalise(x):
            xn_ref[...] = _rms(x, g_ref[...]).astype(BF16)
        with_x_tile(normalise)

    xn = xn_ref[...]
    gate = _dot(xn, wg_buf[slot].astype(BF16))
    up = _dot(xn, wu_buf[slot].astype(BF16))
    h = (gate * jax.nn.sigmoid(gate) * up).astype(BF16)
    acc_ref[...] += _dot(h, wd_buf[slot].astype(BF16))

    @pl.when(j == pl.num_programs(1) - 1)
    def _():
        def residual(x):
            acc_ref[...] = x + 0.5 * acc_ref[...]
        with_x_tile(residual)
        if n_out == 2:
            @pl.when(boundary)
            def _():
                o_refs[1][...] = acc_ref[split_row:, :]


def _ffn(xs, gain, w_gate, w_up, w_down, layer, out_rows=None):
    d = xs[0].shape[1]
    n_tok = sum(x.shape[0] for x in xs)
    d_ff = w_gate.shape[2]
    if len(xs) == 2 or out_rows is not None:
        tm = FFN_TOKEN_TILE if n_tok % FFN_TOKEN_TILE == 0 else n_tok
        n0, n1 = (x.shape[0] for x in xs) if len(xs) == 2 else out_rows
        split_row = n0 - (n_tok // tm - 1) * tm
        assert split_row + n1 == tm and split_row % 8 == 0
    else:
        tm = FFN_TOKEN_TILE if n_tok % FFN_TOKEN_TILE == 0 else TOKEN_TILE
        split_row = None
    tf = FF_TILE
    tile_spec = pl.BlockSpec((tm, d), lambda i, j: (i, 0))
    tail_spec = lambda rows: pl.BlockSpec((rows, d), lambda i, j: (0, 0))
    x_specs = [tile_spec] + ([tail_spec(xs[1].shape[0])] if len(xs) == 2 else [])
    if out_rows is None:
        out_specs, out_shape = [tile_spec], [jax.ShapeDtypeStruct((n_tok, d), F32)]
    else:
        out_specs = [tile_spec, tail_spec(out_rows[1])]
        out_shape = [jax.ShapeDtypeStruct((rows, d), F32) for rows in out_rows]
    grid = (n_tok // tm, d_ff // tf)
    assert grid[0] * grid[1] >= FFN_WEIGHT_SLOTS - 1
    outs = pl.pallas_call(
        functools.partial(_ffn_kernel, n_in=len(xs), n_out=len(out_specs), split_row=split_row, layer=layer),
        grid=grid,
        in_specs=x_specs + [pl.BlockSpec((1, d), lambda i, j: (0, 0))] + [pl.BlockSpec(memory_space=pl.ANY)] * 3,
        out_specs=out_specs,
        out_shape=out_shape,
        scratch_shapes=[pltpu.VMEM((tm, d), BF16),
                        pltpu.VMEM((FFN_WEIGHT_SLOTS, d, tf), w_gate.dtype),
                        pltpu.VMEM((FFN_WEIGHT_SLOTS, d, tf), w_up.dtype),
                        pltpu.VMEM((FFN_WEIGHT_SLOTS, tf, d), w_down.dtype),
                        pltpu.SemaphoreType.DMA((FFN_WEIGHT_SLOTS, 3))],
        compiler_params=pltpu.CompilerParams(dimension_semantics=("arbitrary", "arbitrary"),
                                             vmem_limit_bytes=FFN_VMEM_LIMIT_BYTES),
        name="ffn",
    )(*xs, gain.reshape(1, d), w_gate, w_up, w_down)
    return outs[0] if out_rows is None else outs


def _norm_matmul_kernel(x_ref, g_ref, w_ref, o_ref, xn_ref):
    @pl.when(pl.program_id(1) == 0)
    def _():
        xn_ref[...] = _rms(x_ref[...], g_ref[...]).astype(BF16)

    o_ref[...] = _dot(xn_ref[...], w_ref[...])


def _norm_matmul(x, gain, w, layer, tn):
    n_tok, d = x.shape
    n_out = w.shape[2]
    tm = TOKEN_TILE
    return pl.pallas_call(
        _norm_matmul_kernel,
        grid=(n_tok // tm, n_out // tn),
        in_specs=[
            pl.BlockSpec((tm, d), lambda i, j: (i, 0)),
            pl.BlockSpec((1, d), lambda i, j: (0, 0)),
            pl.BlockSpec((None, d, tn), lambda i, j: (layer, 0, j)),
        ],
        out_specs=pl.BlockSpec((tm, tn), lambda i, j: (i, j)),
        out_shape=jax.ShapeDtypeStruct((n_tok, n_out), F32),
        scratch_shapes=[pltpu.VMEM((tm, d), BF16)],
        compiler_params=_compiler_params("parallel", "arbitrary"),
        name="in_proj",
    )(x, gain.reshape(1, d), w)


def _out_proj_kernel(x_ref, ys_ref, ya_ref, gs_ref, ga_ref, ws_ref, wa_ref, o_ref):
    ys = _rms(ys_ref[...], gs_ref[...]).astype(BF16)
    ya = _rms(ya_ref[...], ga_ref[...]).astype(BF16)
    o_ref[...] = x_ref[...] + _dot(ys, ws_ref[...]) + _dot(ya, wa_ref[...])


def _out_proj(x, y_ssm, y_attn, g_ssm, g_attn, w_out, layer):
    n_tok, d = x.shape
    ssm_w, attn_w = y_ssm.shape[1], y_attn.shape[1]
    tm = TOKEN_TILE
    return pl.pallas_call(
        _out_proj_kernel,
        grid=(n_tok // tm,),
        in_specs=[
            pl.BlockSpec((tm, d), lambda i: (i, 0)),
            pl.BlockSpec((tm, ssm_w), lambda i: (i, 0)),
            pl.BlockSpec((tm, attn_w), lambda i: (i, 0)),
            pl.BlockSpec((1, ssm_w), lambda i: (0, 0)),
            pl.BlockSpec((1, attn_w), lambda i: (0, 0)),
            pl.BlockSpec((None, ssm_w, d), lambda i: (layer, 0, 0)),
            pl.BlockSpec((None, attn_w, d), lambda i: (layer, ssm_w // attn_w, 0)),
        ],
        out_specs=pl.BlockSpec((tm, d), lambda i: (i, 0)),
        out_shape=jax.ShapeDtypeStruct((n_tok, d), F32),
        compiler_params=_compiler_params("parallel"),
        name="out_proj",
    )(x, y_ssm, y_attn, g_ssm.reshape(1, ssm_w), g_attn.reshape(1, attn_w), w_out, w_out)


def _ssm_prep_kernel(*refs, t_small):
    for gi in range(refs[0].shape[0]):
        _ssm_prep_group(*[r.at[pl.ds(gi, 1)] for r in refs], t_small=t_small)


def _ssm_prep_group(are_ref, aim_ref, ldt_ref, btre_ref, btim_ref, cre_ref, cim_ref, d_ref, wglu_ref,
                    *table_refs, t_small):
    per_set = len(table_refs) // 2
    full_refs, small_refs = table_refs[:per_set], table_refs[per_set:]
    a_re, a_im = are_ref[0], aim_ref[0]
    dt = jnp.exp(ldt_ref[0])
    mag = jnp.exp(a_re * dt)
    lb_re, lb_im = mag * jnp.cos(a_im * dt), mag * jnp.sin(a_im * dt)
    den = a_re * a_re + a_im * a_im
    q_re = ((lb_re - 1.0) * a_re + lb_im * a_im) / den
    q_im = (lb_im * a_re - (lb_re - 1.0) * a_im) / den
    bt_re, bt_im = btre_ref[0], btim_ref[0]
    bb_re = q_re * bt_re - q_im * bt_im
    bb_im = q_re * bt_im + q_im * bt_re
    c_re, c_im = cre_ref[0], cim_ref[0]

    pw_re, pw_im = [jnp.ones_like(lb_re)], [jnp.zeros_like(lb_im)]
    for _ in range(PROMPT_CHUNK):
        pr, pi = pw_re[-1], pw_im[-1]
        pw_re.append(pr * lb_re - pi * lb_im)
        pw_im.append(pr * lb_im + pi * lb_re)

    steps = PROMPT_CHUNK
    c = c_re.shape[0]
    w_re = jnp.concatenate([bb_re * pw_re[steps - 1 - s] - bb_im * pw_im[steps - 1 - s] for s in range(steps)], axis=0)
    w_im = jnp.concatenate([bb_re * pw_im[steps - 1 - s] + bb_im * pw_re[steps - 1 - s] for s in range(steps)], axis=0)
    zt_re = jnp.concatenate([c_re * pw_re[t + 1] - c_im * pw_im[t + 1] for t in range(steps)], axis=0)
    zt_im = jnp.concatenate([-(c_re * pw_im[t + 1] + c_im * pw_re[t + 1]) for t in range(steps)], axis=0)

    lag = _dot_nt(w_re, c_re, HIGHEST) - _dot_nt(w_im, c_im, HIGHEST)
    cols = []
    for t in range(steps):
        live = lag[(steps - 1 - t) * c:, :]
        cols.append(live if t == steps - 1 else
                    jnp.concatenate([live, jnp.zeros(((steps - 1 - t) * c, c), F32)], axis=0))
    m = jnp.concatenate(cols, axis=1)

    tiled = jnp.concatenate([jnp.concatenate([wglu_ref[0]] * steps, axis=0)] * steps, axis=1)
    shift = c.bit_length() - 1
    row_step = lax.shift_right_logical(lax.broadcasted_iota(jnp.int32, tiled.shape, 0), shift)
    col_step = lax.shift_right_logical(lax.broadcasted_iota(jnp.int32, tiled.shape, 1), shift)
    glu = jnp.where(row_step == col_step, tiled, 0.0)
    d_t = jnp.concatenate([d_ref[0]] * steps, axis=1)

    for t, refs in ((steps, full_refs), (t_small, small_refs)):
        m_ref, wre_ref, wim_ref, ztre_ref, ztim_ref, glu_ref, dt_ref, ore_ref, oim_ref = refs
        n, full = t * c, steps * c
        m_ref[0] = m[:n, :n].astype(BF16)
        wre_ref[0] = w_re[full - n:].astype(BF16)
        wim_ref[0] = w_im[full - n:].astype(BF16)
        ztre_ref[0] = zt_re[:n].astype(BF16)
        ztim_ref[0] = zt_im[:n].astype(BF16)
        glu_ref[0] = glu[:n, :n].astype(BF16)
        dt_ref[0] = d_t[:, :n]
        ore_ref[0] = pw_re[t]
        oim_ref[0] = pw_im[t]


def _merge_groups(a_re, a_im, log_dt, b_re, b_im, c_re, c_im, d_skip, w_glu, n):
    g, p = a_re.shape
    gm = g // n
    own = np.eye(n, dtype=np.float32)[None, :, :, None, None]

    def block_diag(x):
        r, q = x.shape[1:]
        return (x.reshape(gm, n, 1, r, q) * own).transpose(0, 1, 3, 2, 4).reshape(gm, n * r, n * q)

    lanes = lambda x: x.reshape(gm, 1, -1)
    return (lanes(a_re), lanes(a_im), lanes(jnp.repeat(log_dt, p)),
            block_diag(jnp.swapaxes(b_re, 1, 2)), block_diag(jnp.swapaxes(b_im, 1, 2)),
            block_diag(c_re), block_diag(c_im), lanes(d_skip), block_diag(w_glu))


def _ssm_prep(a_re, a_im, log_dt, bt_re, bt_im, c_re, c_im, d_skip, w_glu, t_small):
    g, c, p = c_re.shape
    assert c & (c - 1) == 0
    gps = LANES // c
    gs = lambda *shape: pl.BlockSpec((gps,) + shape, lambda i: (i,) + (0,) * len(shape))
    names = ("m", "w_re", "w_im", "zt_re", "zt_im", "glu", "d", "a_re", "a_im")
    out_specs, out_shape = [], []
    for t in (PROMPT_CHUNK, t_small):
        tc = t * c
        shapes = [(tc, tc), (tc, p), (tc, p), (tc, p), (tc, p), (tc, tc), (1, tc), (1, p), (1, p)]
        dtypes = [BF16] * 6 + [F32] * 3
        out_specs += [gs(*shape) for shape in shapes]
        out_shape += [jax.ShapeDtypeStruct((g,) + shape, dtype) for shape, dtype in zip(shapes, dtypes)]
    outs = pl.pallas_call(
        functools.partial(_ssm_prep_kernel, t_small=t_small),
        grid=(g // gps,),
        in_specs=[gs(1, p), gs(1, p), gs(1, p), gs(c, p), gs(c, p), gs(c, p), gs(c, p), gs(1, c), gs(c, c)],
        out_specs=out_specs,
        out_shape=out_shape,
        compiler_params=_compiler_params("parallel"),
        name="ssm_prep",
    )(a_re, a_im, log_dt, bt_re, bt_im, c_re, c_im, d_skip, w_glu)
    return dict(zip(names, outs[:len(names)])), dict(zip(names, outs[len(names):]))


def _ssm_mix_kernel(*refs, bsz, t, aliased):
    (u_ref, m_ref, wre_ref, wim_ref, ztre_ref, ztim_ref, are_ref, aim_ref, d_ref, glu_ref, h0re_ref, h0im_ref) = refs[:12]
    y_ref, hre_ref, him_ref, xs_ref, ug_ref, vre_ref, vim_ref, pre_ref, pim_ref = refs[12 + aliased:]
    rows = u_ref.shape[0] // t
    chunks = rows // bsz
    n_grp, tc = m_ref.shape[0], m_ref.shape[1]
    c = tc // t

    def token_rows(b, s):
        return pl.ds(b * chunks * t + s, chunks, stride=t)

    def chunk_rows(b):
        return pl.ds(b, chunks, stride=bsz)

    for s in range(t):
        if chunks == 1:
            xs_ref[s] = u_ref[pl.ds(s, rows, stride=t), :]
        else:
            for b in range(bsz):
                xs_ref[s, chunk_rows(b), :] = u_ref[token_rows(b, s), :]
    for gi in range(n_grp):
        lanes = slice(gi * c, (gi + 1) * c)
        u = jnp.concatenate([xs_ref[s, :, lanes] for s in range(t)], axis=1)
        ug_ref[gi] = u
        vre_ref[gi] = _dot(u.astype(BF16), wre_ref[gi])
        vim_ref[gi] = _dot(u.astype(BF16), wim_ref[gi])

    per_tile = max(1, 8 // bsz)
    tile_rows = per_tile * bsz

    def chunk_tile(i, h):
        sel = pl.ds(pl.multiple_of(i * tile_rows, tile_rows), tile_rows)
        nxt = []
        for gi in range(n_grp):
            h_re, h_im = h[2 * gi], h[2 * gi + 1]
            v_re, v_im = vre_ref[gi, sel, :], vim_ref[gi, sel, :]
            a_re, a_im = are_ref[gi], aim_ref[gi]
            ent_re, ent_im = [], []
            for j in range(per_tile):
                ent_re.append(h_re)
                ent_im.append(h_im)
                part = slice(j * bsz, (j + 1) * bsz)
                h_re, h_im = a_re * h_re - a_im * h_im + v_re[part], a_re * h_im + a_im * h_re + v_im[part]
            pre_ref[gi, sel, :] = jnp.concatenate(ent_re, axis=0)
            pim_ref[gi, sel, :] = jnp.concatenate(ent_im, axis=0)
            nxt += [h_re, h_im]
        return tuple(nxt)

    h = tuple(r[gi] for gi in range(n_grp) for r in (h0re_ref, h0im_ref))
    n_tiles = chunks // per_tile
    h = lax.fori_loop(0, n_tiles, chunk_tile, h, unroll=min(n_tiles, 4))
    for gi in range(n_grp):
        hre_ref[gi] = h[2 * gi]
        him_ref[gi] = h[2 * gi + 1]

    for gi in range(n_grp):
        u = ug_ref[gi]
        y = (_dot(u.astype(BF16), m_ref[gi]) + _dot_nt(pre_ref[gi].astype(BF16), ztre_ref[gi])
             + _dot_nt(pim_ref[gi].astype(BF16), ztim_ref[gi]) + d_ref[gi] * u)
        y = jax.nn.gelu(y)
        y = y * jax.nn.sigmoid(_dot(y.astype(BF16), glu_ref[gi]))
        for s in range(t):
            xs_ref[s, :, gi * c:(gi + 1) * c] = y[:, s * c:(s + 1) * c]
    for s in range(t):
        if chunks == 1:
            y_ref[pl.ds(s, rows, stride=t), :] = xs_ref[s]
        else:
            for b in range(bsz):
                y_ref[token_rows(b, s), :] = xs_ref[s, chunk_rows(b), :]


def _ssm_mix(z, y_prev, tab, layer, h0_re, h0_im, row0, n_rows, bsz, t):
    g, _, p = h0_re.shape
    tc = tab["m"].shape[1]
    n_tok = z.shape[0]
    c = tc // t
    width = g * c
    lanes = LANES
    gps = lanes // c
    rows = n_rows // t
    first = layer * (g // gps)
    blk = lambda *shape: pl.BlockSpec((gps,) + shape, lambda i: (i,) + (0,) * len(shape))
    tbl = lambda *shape: pl.BlockSpec((gps,) + shape, lambda i: (first + i,) + (0,) * len(shape))
    tok = pl.BlockSpec((n_rows, lanes), lambda i: (row0 // n_rows, i))
    aliased = y_prev is not None
    in_specs = [tok, tbl(tc, tc), tbl(tc, p), tbl(tc, p), tbl(tc, p), tbl(tc, p), tbl(1, p), tbl(1, p), tbl(1, tc),
                tbl(tc, tc), blk(bsz, p), blk(bsz, p)]
    args = [z, tab["m"], tab["w_re"], tab["w_im"], tab["zt_re"], tab["zt_im"], tab["a_re"], tab["a_im"], tab["d"],
            tab["glu"], h0_re, h0_im]
    if aliased:
        in_specs.append(pl.BlockSpec(memory_space=pl.ANY))
        args.append(y_prev)
    return pl.pallas_call(
        functools.partial(_ssm_mix_kernel, bsz=bsz, t=t, aliased=aliased),
        grid=(g // gps,),
        in_specs=in_specs,
        out_specs=[tok, blk(bsz, p), blk(bsz, p)],
        out_shape=[jax.ShapeDtypeStruct((n_tok, width), F32), jax.ShapeDtypeStruct((g, bsz, p), F32),
                   jax.ShapeDtypeStruct((g, bsz, p), F32)],
        scratch_shapes=[pltpu.VMEM((t, rows, lanes), F32), pltpu.VMEM((gps, rows, tc), F32)]
        + [pltpu.VMEM((gps, rows, p), F32)] * 4,
        input_output_aliases={len(args) - 1: 0} if aliased else {},
        compiler_params=_compiler_params("parallel"),
        name="ssm_mix",
    )(*args)


def _bias_table(q_pos, k_pos, n_heads):
    slopes = np.exp2(-8.0 * np.arange(1, n_heads + 1, dtype=np.float64) / n_heads)
    dist = q_pos[:, None] - k_pos[None, :]
    valid = (dist >= 0) & (dist < WINDOW) & (k_pos[None, :] >= 0)
    bias = -slopes[:, None, None] * dist[None].astype(np.float64)
    return np.where(valid[None], bias, -np.inf).astype(np.float32)


def _head_rms(x, gain):
    heads = [_rms(x[:, i:i + HEAD_DIM], gain) for i in range(0, x.shape[1], HEAD_DIM)]
    return jnp.concatenate(heads, axis=1)


def _softmax_sink_pv(logits, sink, v):
    m = jnp.maximum(jnp.max(logits, axis=-1, keepdims=True), sink)
    p = jnp.exp(logits - m)
    denom = jnp.sum(p, axis=-1, keepdims=True) + jnp.exp(sink - m)
    return _dot(p.astype(BF16), v) / denom


def _prompt_attn_kernel(sinks_ref, q_ref, k_ref, v_ref, qg_ref, kg_ref, bias_ref, y_ref, ks_ref, vs_ref,
                        kp_ref, vp_ref, *, n_heads):
    n = pl.program_id(1)

    @pl.when(n == 0)
    def _():
        kp_ref[...] = jnp.zeros_like(kp_ref)
        vp_ref[...] = jnp.zeros_like(vp_ref)

    q_all = q_ref[...]
    blk = q_all.shape[0]
    half = 2 * HEAD_DIM
    assert n_heads // N_KV_HEADS == 4 and half == 128
    k_cur = _head_rms(k_ref[...], kg_ref[...])
    v_cur = v_ref[...]
    k2 = jnp.concatenate([kp_ref[...], k_cur], axis=0)
    v2t = jnp.concatenate([vp_ref[...], v_cur], axis=0).T.astype(BF16)
    qs = (q_all * qg_ref[...] * QK_SCALE).astype(BF16)
    qsq = q_all * q_all
    table = jnp.minimum(n, 1)
    from_prev = (lax.broadcasted_iota(jnp.int32, (blk, blk), 0) > lax.broadcasted_iota(jnp.int32, (blk, blk), 1))
    low_lanes = lax.broadcasted_iota(jnp.int32, (2 * blk, half), 1) < HEAD_DIM
    low8 = lax.broadcasted_iota(jnp.int32, (8, half), 1) < HEAD_DIM
    pick = jnp.concatenate([jnp.where(low8, 1.0, 0.0), jnp.where(low8, 0.0, 1.0)], axis=0)
    for kv in range(N_KV_HEADS):
        k_tile = k2[:, (kv // 2) * half:(kv // 2 + 1) * half]
        k_own = jnp.where(low_lanes if kv % 2 == 0 else ~low_lanes, k_tile, 0.0)
        k_swap = pltpu.roll(k_own, HEAD_DIM, axis=1)
        k_lhs = jnp.concatenate([k_own, k_swap] if kv % 2 == 0 else [k_swap, k_own], axis=0).astype(BF16)
        tiles = slice(2 * kv * half, (2 * kv + 1) * half), slice((2 * kv + 1) * half, (2 * kv + 2) * half)
        q_rows = jnp.concatenate([qs[:, t] for t in tiles], axis=0)
        qsq_rows = jnp.concatenate([qsq[:, t] for t in tiles], axis=0)
        s = _dot_nt(k_lhs, q_rows)
        ssq = _dot_nt(pick, qsq_rows, HIGHEST)
        probs, inv_denoms = [], []
        for i in range(4):
            odd, tile = i % 2, i // 2
            lanes = slice(tile * blk, (tile + 1) * blk)
            s_h = s[odd * 2 * blk:(odd + 1) * 2 * blk, lanes]
            q_rms = lax.rsqrt(ssq[odd * 8:odd * 8 + 1, lanes] * (1.0 / HEAD_DIM) + EPS)
            logits = jnp.where(from_prev, s_h[:blk], s_h[blk:]) * q_rms + bias_ref[table, 4 * kv + i]
            sink = sinks_ref[4 * kv + i]
            m = jnp.maximum(jnp.max(logits, axis=0, keepdims=True), sink)
            p = jnp.exp(logits - m)
            inv_denoms.append(1.0 / (jnp.sum(p, axis=0, keepdims=True) + jnp.exp(sink - m)))
            probs.append(jnp.concatenate([jnp.where(from_prev, p, 0.0), jnp.where(from_prev, 0.0, p)], axis=0))
        o = _dot(v2t[kv * HEAD_DIM:(kv + 1) * HEAD_DIM], jnp.concatenate(probs, axis=1).astype(BF16))
        for tile in range(2):
            pair = [o[:, i * blk:(i + 1) * blk] * inv_denoms[i] for i in (2 * tile, 2 * tile + 1)]
            y_ref[:, tiles[tile]] = jnp.concatenate(pair, axis=0).T
    kp_ref[...] = k_cur
    vp_ref[...] = v_cur

    @pl.when(n == pl.num_programs(1) - 1)
    def _():
        ks_ref[0] = k_cur
        vs_ref[0] = v_cur


def _prompt_attn(z, sinks, q_gain, k_gain, bsz, seq, n_tok, col_q, attn_w, kv_w):
    blk = WINDOW
    nb = seq // blk
    n_heads = attn_w // HEAD_DIM
    row = lambda b, n: b * nb + n
    q_blk, k_blk, v_blk = col_q // attn_w, (col_q + attn_w) // kv_w, (col_q + attn_w + kv_w) // kv_w
    qi, kj = np.arange(blk)[:, None], np.arange(blk)[None, :]
    first, later = [np.take_along_axis(_bias_table(first_pos + np.arange(blk), first_pos - blk + np.arange(2 * blk), n_heads),
                                       np.broadcast_to(np.where(kj > qi, kj, kj + blk), (n_heads, blk, blk)), axis=2)
                    for first_pos in (0, blk)]
    bias = np.stack([first, later]).swapaxes(2, 3)
    return pl.pallas_call(
        functools.partial(_prompt_attn_kernel, n_heads=n_heads),
        grid=(bsz, nb),
        in_specs=[
            pl.BlockSpec(memory_space=pltpu.SMEM),
            pl.BlockSpec((blk, attn_w), lambda b, n: (row(b, n), q_blk)),
            pl.BlockSpec((blk, kv_w), lambda b, n: (row(b, n), k_blk)),
            pl.BlockSpec((blk, kv_w), lambda b, n: (row(b, n), v_blk)),
            pl.BlockSpec((1, attn_w), lambda b, n: (0, 0)),
            pl.BlockSpec((1, HEAD_DIM), lambda b, n: (0, 0)),
            pl.BlockSpec(bias.shape, lambda b, n: (0, 0, 0, 0)),
        ],
        out_specs=[
            pl.BlockSpec((blk, attn_w), lambda b, n: (row(b, n), 0)),
            pl.BlockSpec((1, blk, kv_w), lambda b, n: (b, 0, 0)),
            pl.BlockSpec((1, blk, kv_w), lambda b, n: (b, 0, 0)),
        ],
        out_shape=[jax.ShapeDtypeStruct((n_tok, attn_w), F32),
                   jax.ShapeDtypeStruct((bsz, blk, kv_w), F32),
                   jax.ShapeDtypeStruct((bsz, blk, kv_w), F32)],
        scratch_shapes=[pltpu.VMEM((blk, kv_w), F32)] * 2,
        compiler_params=_compiler_params("parallel", "arbitrary"),
        name="prompt_attn",
    )(sinks, z, z, z, jnp.tile(q_gain, n_heads).reshape(1, attn_w), k_gain.reshape(1, HEAD_DIM), jnp.asarray(bias))


def _sample_attn_kernel(*refs, dec_seq, aliased):
    q_ref, k_ref, v_ref, ck_ref, cv_ref, qg_ref, kg_ref, bias_ref, sink_ref = refs[:9]
    y_ref, ks_ref, vs_ref, kbuf_ref, vbuf_ref, p_ref = refs[9 + 2 * aliased:]
    tile, q_rows, _ = q_ref.shape
    cache_w = ck_ref.shape[1]
    keys = kbuf_ref.shape[0] // tile
    k_new = _head_rms(k_ref[...], kg_ref[...])
    v_new = v_ref[...]
    pad = jnp.zeros((keys - cache_w, kbuf_ref.shape[1]), F32)
    for b in range(tile):
        rows = slice(b * dec_seq, (b + 1) * dec_seq)
        kbuf_ref[b * keys:b * keys + cache_w, :] = ck_ref[b]
        vbuf_ref[b * keys:b * keys + cache_w, :] = cv_ref[b]
        kbuf_ref[b * keys + cache_w:(b + 1) * keys, :] = pad
        vbuf_ref[b * keys + cache_w:(b + 1) * keys, :] = pad
        kbuf_ref[b * keys + cache_w:b * keys + cache_w + dec_seq, :] = k_new[rows]
        vbuf_ref[b * keys + cache_w:b * keys + cache_w + dec_seq, :] = v_new[rows]
    q = q_ref[...].reshape(tile * q_rows, q_ref.shape[2])
    q = q * lax.rsqrt(jnp.sum(q * q, axis=-1, keepdims=True) * (1.0 / HEAD_DIM) + EPS) * qg_ref[...]
    s = _dot_nt((q * QK_SCALE).astype(BF16), kbuf_ref[...].astype(BF16))
    p_ref[...] = jnp.zeros_like(p_ref)
    inv_denoms = []
    for b in range(tile):
        own_rows, own_keys = slice(b * q_rows, (b + 1) * q_rows), slice(b * keys, (b + 1) * keys)
        logits = s[own_rows, own_keys] + bias_ref[...]
        sink = sink_ref[...]
        m = jnp.maximum(jnp.max(logits, axis=-1, keepdims=True), sink)
        p = jnp.exp(logits - m)
        inv_denoms.append(1.0 / (jnp.sum(p, axis=-1, keepdims=True) + jnp.exp(sink - m)))
        p_ref[own_rows, own_keys] = p.astype(BF16)
    o = _dot(p_ref[...], vbuf_ref[...].astype(BF16))
    for b in range(tile):
        rows = slice(b * dec_seq, (b + 1) * dec_seq)
        y_ref[b] = o[b * q_rows:(b + 1) * q_rows] * inv_denoms[b]
        ks_ref[b, 0:cache_w - dec_seq, :] = ck_ref[b, dec_seq:cache_w, :]
        vs_ref[b, 0:cache_w - dec_seq, :] = cv_ref[b, dec_seq:cache_w, :]
        ks_ref[b, cache_w - dec_seq:cache_w, :] = k_new[rows]
        vs_ref[b, cache_w - dec_seq:cache_w, :] = v_new[rows]


def _sample_attn(z, cache_k, cache_v, windows, layer, sinks, q_gain, k_gain, row0, dec_seq, col_q, attn_w, kv_w):
    _, dec_batch, cache_w, _ = cache_k.shape
    n_heads = attn_w // HEAD_DIM
    q_per_kv = n_heads // N_KV_HEADS
    tile = SAMPLE_BATCH_TILE
    r0 = row0 // (tile * dec_seq)
    k_blk, v_blk = (col_q + attn_w) // kv_w, (col_q + attn_w + kv_w) // kv_w
    keys = 2 * cache_w
    k_pos = np.full(keys, -1)
    k_pos[:cache_w] = PAST_LEN - cache_w + np.arange(cache_w)
    k_pos[cache_w:cache_w + dec_seq] = PAST_LEN + np.arange(dec_seq)
    bias = _bias_table(PAST_LEN + np.arange(dec_seq), k_pos, n_heads).reshape(n_heads * dec_seq, keys)
    sink_col = jnp.repeat(sinks, dec_seq).reshape(n_heads * dec_seq, 1)
    q = z[row0:, col_q:col_q + attn_w].reshape(dec_batch, dec_seq, N_KV_HEADS, q_per_kv, 1, HEAD_DIM)
    own_kv = np.eye(N_KV_HEADS, dtype=np.float32)[None, :, None, None, :, None]
    q = (q.transpose(0, 2, 3, 1, 4, 5) * own_kv).reshape(dec_batch, n_heads * dec_seq, kv_w)
    window = pl.BlockSpec((None, tile, cache_w, kv_w), lambda i: (layer, i, 0, 0))
    aliased = windows is not None
    in_specs = [
        pl.BlockSpec((tile, n_heads * dec_seq, kv_w), lambda i: (i, 0, 0)),
        pl.BlockSpec((tile * dec_seq, kv_w), lambda i: (r0 + i, k_blk)),
        pl.BlockSpec((tile * dec_seq, kv_w), lambda i: (r0 + i, v_blk)),
        window,
        window,
        pl.BlockSpec((1, kv_w), lambda i: (0, 0)),
        pl.BlockSpec((1, HEAD_DIM), lambda i: (0, 0)),
        pl.BlockSpec(bias.shape, lambda i: (0, 0)),
        pl.BlockSpec(sink_col.shape, lambda i: (0, 0)),
    ]
    args = [q, z, z, cache_k, cache_v, jnp.tile(q_gain, N_KV_HEADS).reshape(1, kv_w), k_gain.reshape(1, HEAD_DIM),
            jnp.asarray(bias), sink_col]
    if aliased:
        in_specs += [pl.BlockSpec(memory_space=pl.ANY)] * 2
        args += list(windows)
    y, ks, vs = pl.pallas_call(
        functools.partial(_sample_attn_kernel, dec_seq=dec_seq, aliased=aliased),
        grid=(dec_batch // tile,),
        in_specs=in_specs,
        out_specs=[pl.BlockSpec((tile, n_heads * dec_seq, kv_w), lambda i: (i, 0, 0)), window, window],
        out_shape=[jax.ShapeDtypeStruct(q.shape, F32),
                   jax.ShapeDtypeStruct(cache_k.shape, F32),
                   jax.ShapeDtypeStruct(cache_v.shape, F32)],
        scratch_shapes=[pltpu.VMEM((tile * keys, kv_w), F32)] * 2
        + [pltpu.VMEM((tile * n_heads * dec_seq, tile * keys), BF16)],
        input_output_aliases={len(args) - 2: 1, len(args) - 1: 2} if aliased else {},
        compiler_params=_compiler_params("parallel"),
        name="sample_attn",
    )(*args)
    y = y.reshape(dec_batch, N_KV_HEADS, q_per_kv, dec_seq, N_KV_HEADS, HEAD_DIM)
    y = jnp.stack([y[:, kv, :, :, kv] for kv in range(N_KV_HEADS)], axis=1)
    return y.transpose(0, 3, 1, 2, 4).reshape(dec_batch * dec_seq, attn_w), ks, vs


def kernel(x_prompt, x_sample, cache_k, cache_v, state_ssm_re, state_ssm_im, ffn1_norm, ffn1_w_gate, ffn1_w_up, ffn1_w_down, mix_norm, w_in, ssm_A_re, ssm_A_im, ssm_B_re, ssm_B_im, ssm_C_re, ssm_C_im, ssm_D, ssm_log_dt, ssm_w_glu, q_norm, k_norm, sinks, ssm_out_norm, attn_out_norm, w_out, ffn2_norm, ffn2_w_gate, ffn2_w_up, ffn2_w_down):
    bsz, seq, d = x_prompt.shape
    dec_batch, dec_seq, _ = x_sample.shape
    depth = w_in.shape[0]
    ssm_w = ssm_A_re.shape[1] * SSM_GROUP_CH
    attn_w = w_out.shape[1] - ssm_w
    kv_w = N_KV_HEADS * HEAD_DIM
    cache_w = cache_k.shape[2]
    n_prompt = bsz * seq
    n_tok = n_prompt + dec_batch * dec_seq

    xs = [x_prompt.reshape(n_prompt, d), x_sample.reshape(dec_batch * dec_seq, d)]
    n_groups = ssm_w // SSM_GROUP_CH
    n_merged = n_groups // SSM_MERGE
    zeros_state = jnp.zeros((n_merged, bsz, SSM_MERGE * SSM_STATE), F32)
    outs = [[] for _ in range(6)]
    cache_k_flat = cache_k.reshape(depth, dec_batch, cache_w, kv_w)
    cache_v_flat = cache_v.reshape(depth, dec_batch, cache_w, kv_w)
    windows = None
    ffn1_w, ffn2_w = (ffn1_w_gate, ffn1_w_up, ffn1_w_down), (ffn2_w_gate, ffn2_w_up, ffn2_w_down)
    w_in_bf, w_out_bf = w_in.astype(BF16), w_out.astype(BF16)
    all_layers = lambda a: a.reshape((-1,) + a.shape[2:])
    tab_p, tab_s = _ssm_prep(*_merge_groups(*map(all_layers, (ssm_A_re, ssm_A_im, ssm_log_dt, ssm_B_re, ssm_B_im,
                                                               ssm_C_re, ssm_C_im, ssm_D, ssm_w_glu)), SSM_MERGE),
                             t_small=dec_seq)
    for l in range(depth):
        x = _ffn(xs if l == 0 else [x], ffn1_norm[l], *ffn1_w, layer=l)
        z = _norm_matmul(x, mix_norm[l], w_in_bf, layer=l, tn=w_in.shape[2])

        grp_major = lambda a: a.reshape(a.shape[0], n_merged, SSM_MERGE * SSM_STATE).transpose(1, 0, 2)
        seq_major = lambda a: a.transpose(1, 0, 2).reshape(a.shape[1], n_groups, SSM_STATE)
        y_ssm, hp_re, hp_im = _ssm_mix(z, None, tab_p, l, zeros_state, zeros_state, 0, n_prompt, bsz, PROMPT_CHUNK)
        y_ssm, hs_re, hs_im = _ssm_mix(z, y_ssm, tab_s, l, grp_major(state_ssm_re[l]), grp_major(state_ssm_im[l]),
                                       n_prompt, dec_batch * dec_seq, dec_batch, dec_seq)
        hp_re, hp_im, hs_re, hs_im = map(seq_major, (hp_re, hp_im, hs_re, hs_im))

        y_attn, pk, pv = _prompt_attn(z, sinks[l], q_norm[l], k_norm[l], bsz, seq, n_tok, ssm_w, attn_w, kv_w)
        ya_s, *windows = _sample_attn(z, cache_k_flat, cache_v_flat, windows, l, sinks[l], q_norm[l], k_norm[l],
                                      n_prompt, dec_seq, ssm_w, attn_w, kv_w)
        y_attn = lax.dynamic_update_slice(y_attn, ya_s, (n_prompt, 0))

        x = _out_proj(x, y_ssm, y_attn, ssm_out_norm[l], attn_out_norm[l], w_out_bf, layer=l)
        if l < depth - 1:
            x = _ffn([x], ffn2_norm[l], *ffn2_w, layer=l)

        kv_shape = (-1, N_KV_HEADS, HEAD_DIM)
        for dst, val in zip(outs, (pk.reshape(bsz, *kv_shape), pv.reshape(bsz, *kv_shape), hp_re, hp_im, hs_re, hs_im)):
            dst.append(val)
    y_prompt, y_sample = _ffn([x], ffn2_norm[depth - 1], *ffn2_w, layer=depth - 1, out_rows=(n_prompt, n_tok - n_prompt))
    y_prompt, y_sample = y_prompt.reshape(bsz, seq, d), y_sample.reshape(dec_batch, dec_seq, d)
    pk, pv, hp_re, hp_im, hs_re, hs_im = (jnp.stack(o) for o in outs)
    sk, sv = (w.reshape(cache_k.shape) for w in windows)
    return (y_prompt, y_sample, pk, pv, hp_re, hp_im, sk, sv, hs_re, hs_im)
```

```python
import functools

import jax
import jax.numpy as jnp
import numpy as np
from jax import lax
from jax.experimental import pallas as pl
from jax.experimental.pallas import tpu as pltpu

F32 = jnp.float32
BF16 = jnp.bfloat16

EPS = 1e-6
SSM_GROUP_CH = 16
SSM_STATE = 64
HEAD_DIM = 64
N_KV_HEADS = 4
WINDOW = 128
PAST_LEN = 8192
QK_SCALE = HEAD_DIM ** -0.5
PROMPT_CHUNK = 16

TOKEN_TILE = 512
FFN_TOKEN_TILE = 1088
FFN_VMEM_LIMIT_BYTES = 60 * 1024 * 1024
FF_TILE = 256
FFN_ROW_CHUNKS = 4
SAMPLE_BATCH_TILE = 8
VMEM_LIMIT_BYTES = 48 * 1024 * 1024

HIGHEST = lax.Precision.HIGHEST
LANES = 128
SSM_MERGE = 2


def _rms(x, gain):
    return x * lax.rsqrt(jnp.mean(x * x, axis=-1, keepdims=True) + EPS) * gain


def _dot(a, b, precision=None):
    return jnp.dot(a, b, preferred_element_type=F32, precision=precision)


def _dot_nt(a, b, precision=None):
    return lax.dot_general(a, b, (((1,), (1,)), ((), ())), preferred_element_type=F32, precision=precision)


def _compiler_params(*semantics):
    return pltpu.CompilerParams(dimension_semantics=semantics, vmem_limit_bytes=VMEM_LIMIT_BYTES)


def _ffn_kernel(*refs, n_in, n_out, split_row):
    x_refs, (g_ref, wg_ref, wu_ref, wd_ref) = refs[:n_in], refs[n_in:n_in + 4]
    o_refs, xn_ref = refs[n_in + 4:n_in + 4 + n_out], refs[-1]
    acc_ref = o_refs[0]
    i, j = pl.program_id(0), pl.program_id(1)
    boundary = i == pl.num_programs(0) - 1

    def with_x_tile(fn):
        if n_in == 1:
            fn(x_refs[0][...])
        else:
            pl.when(jnp.logical_not(boundary))(lambda: fn(x_refs[0][...]))
            pl.when(boundary)(lambda: fn(jnp.concatenate([x_refs[0][0:split_row, :], x_refs[1][...]], axis=0)))

    def bf16_weights():
        return wg_ref[...].astype(BF16), wu_ref[...].astype(BF16), wd_ref[...].astype(BF16)

    def swiglu_part(xn, w_gate, w_up, w_down):
        gate, up = _dot(xn, w_gate), _dot(xn, w_up)
        return _dot((gate * jax.nn.sigmoid(gate) * up).astype(BF16), w_down)

    @pl.when(j == 0)
    def _():
        def first_step(x):
            weights = bf16_weights()
            rows = x.shape[0] // FFN_ROW_CHUNKS
            for r in range(FFN_ROW_CHUNKS):
                chunk = slice(r * rows, (r + 1) * rows)
                xn = _rms(x[chunk], g_ref[...]).astype(BF16)
                xn_ref[chunk, :] = xn
                acc_ref[chunk, :] = swiglu_part(xn, *weights)
        with_x_tile(first_step)

    last = pl.num_programs(1) - 1

    @pl.when(jnp.logical_and(j > 0, j < last))
    def _():
        acc_ref[...] += swiglu_part(xn_ref[...], *bf16_weights())

    @pl.when(j == last)
    def _():
        def last_step(x):
            acc_ref[...] = x + 0.5 * (acc_ref[...] + swiglu_part(xn_ref[...], *bf16_weights()))
        with_x_tile(last_step)
        if n_out == 2:
            @pl.when(boundary)
            def _():
                o_refs[1][...] = acc_ref[split_row:, :]


def _ffn(xs, gain, w_gate, w_up, w_down, layer, out_rows=None):
    d = xs[0].shape[1]
    n_tok = sum(x.shape[0] for x in xs)
    d_ff = w_gate.shape[2]
    if len(xs) == 2 or out_rows is not None:
        tm = FFN_TOKEN_TILE if n_tok % FFN_TOKEN_TILE == 0 else n_tok
        n0, n1 = (x.shape[0] for x in xs) if len(xs) == 2 else out_rows
        split_row = n0 - (n_tok // tm - 1) * tm
        assert split_row + n1 == tm and split_row % 8 == 0
    else:
        tm = FFN_TOKEN_TILE if n_tok % FFN_TOKEN_TILE == 0 else TOKEN_TILE
        split_row = None
    tf = FF_TILE
    tile_spec = pl.BlockSpec((tm, d), lambda i, j: (i, 0))
    tail_spec = lambda rows: pl.BlockSpec((rows, d), lambda i, j: (0, 0))
    x_specs = [tile_spec] + ([tail_spec(xs[1].shape[0])] if len(xs) == 2 else [])
    if out_rows is None:
        out_specs, out_shape = [tile_spec], [jax.ShapeDtypeStruct((n_tok, d), F32)]
    else:
        out_specs = [tile_spec, tail_spec(out_rows[1])]
        out_shape = [jax.ShapeDtypeStruct((rows, d), F32) for rows in out_rows]
    assert d_ff // tf >= 2 and tm % (8 * FFN_ROW_CHUNKS) == 0
    outs = pl.pallas_call(
        functools.partial(_ffn_kernel, n_in=len(xs), n_out=len(out_specs), split_row=split_row),
        grid=(n_tok // tm, d_ff // tf),
        in_specs=x_specs + [
            pl.BlockSpec((1, d), lambda i, j: (0, 0)),
            pl.BlockSpec((None, d, tf), lambda i, j: (layer, 0, j)),
            pl.BlockSpec((None, d, tf), lambda i, j: (layer, 0, j)),
            pl.BlockSpec((None, tf, d), lambda i, j: (layer, j, 0)),
        ],
        out_specs=out_specs,
        out_shape=out_shape,
        scratch_shapes=[pltpu.VMEM((tm, d), BF16)],
        compiler_params=pltpu.CompilerParams(dimension_semantics=("arbitrary", "arbitrary"),
                                             vmem_limit_bytes=FFN_VMEM_LIMIT_BYTES),
        name="ffn",
    )(*xs, gain.reshape(1, d), w_gate, w_up, w_down)
    return outs[0] if out_rows is None else outs


def _norm_matmul_kernel(x_ref, g_ref, w_ref, o_ref, xn_ref):
    @pl.when(pl.program_id(1) == 0)
    def _():
        xn_ref[...] = _rms(x_ref[...], g_ref[...]).astype(BF16)

    o_ref[...] = _dot(xn_ref[...], w_ref[...])


def _norm_matmul(x, gain, w, layer, tn):
    n_tok, d = x.shape
    n_out = w.shape[2]
    tm = TOKEN_TILE
    return pl.pallas_call(
        _norm_matmul_kernel,
        grid=(n_tok // tm, n_out // tn),
        in_specs=[
            pl.BlockSpec((tm, d), lambda i, j: (i, 0)),
            pl.BlockSpec((1, d), lambda i, j: (0, 0)),
            pl.BlockSpec((None, d, tn), lambda i, j: (layer, 0, j)),
        ],
        out_specs=pl.BlockSpec((tm, tn), lambda i, j: (i, j)),
        out_shape=jax.ShapeDtypeStruct((n_tok, n_out), F32),
        scratch_shapes=[pltpu.VMEM((tm, d), BF16)],
        compiler_params=_compiler_params("parallel", "arbitrary"),
        name="in_proj",
    )(x, gain.reshape(1, d), w)


def _out_proj_kernel(x_ref, ys_ref, ya_ref, gs_ref, ga_ref, ws_ref, wa_ref, o_ref):
    ys = _rms(ys_ref[...], gs_ref[...]).astype(BF16)
    ya = _rms(ya_ref[...], ga_ref[...]).astype(BF16)
    o_ref[...] = x_ref[...] + _dot(ys, ws_ref[...]) + _dot(ya, wa_ref[...])


def _out_proj(x, y_ssm, y_attn, g_ssm, g_attn, w_out, layer):
    n_tok, d = x.shape
    ssm_w, attn_w = y_ssm.shape[1], y_attn.shape[1]
    tm = TOKEN_TILE
    return pl.pallas_call(
        _out_proj_kernel,
        grid=(n_tok // tm,),
        in_specs=[
            pl.BlockSpec((tm, d), lambda i: (i, 0)),
            pl.BlockSpec((tm, ssm_w), lambda i: (i, 0)),
            pl.BlockSpec((tm, attn_w), lambda i: (i, 0)),
            pl.BlockSpec((1, ssm_w), lambda i: (0, 0)),
            pl.BlockSpec((1, attn_w), lambda i: (0, 0)),
            pl.BlockSpec((None, ssm_w, d), lambda i: (layer, 0, 0)),
            pl.BlockSpec((None, attn_w, d), lambda i: (layer, ssm_w // attn_w, 0)),
        ],
        out_specs=pl.BlockSpec((tm, d), lambda i: (i, 0)),
        out_shape=jax.ShapeDtypeStruct((n_tok, d), F32),
        compiler_params=_compiler_params("parallel"),
        name="out_proj",
    )(x, y_ssm, y_attn, g_ssm.reshape(1, ssm_w), g_attn.reshape(1, attn_w), w_out, w_out)


def _ssm_prep_kernel(*refs, t_small):
    for gi in range(refs[0].shape[0]):
        _ssm_prep_group(*[r.at[pl.ds(gi, 1)] for r in refs], t_small=t_small)


def _ssm_prep_group(are_ref, aim_ref, ldt_ref, btre_ref, btim_ref, cre_ref, cim_ref, d_ref, wglu_ref,
                    *table_refs, t_small):
    per_set = len(table_refs) // 2
    full_refs, small_refs = table_refs[:per_set], table_refs[per_set:]
    a_re, a_im = are_ref[0], aim_ref[0]
    dt = jnp.exp(ldt_ref[0])
    mag = jnp.exp(a_re * dt)
    lb_re, lb_im = mag * jnp.cos(a_im * dt), mag * jnp.sin(a_im * dt)
    den = a_re * a_re + a_im * a_im
    q_re = ((lb_re - 1.0) * a_re + lb_im * a_im) / den
    q_im = (lb_im * a_re - (lb_re - 1.0) * a_im) / den
    bt_re, bt_im = btre_ref[0], btim_ref[0]
    bb_re = q_re * bt_re - q_im * bt_im
    bb_im = q_re * bt_im + q_im * bt_re
    c_re, c_im = cre_ref[0], cim_ref[0]

    pw_re, pw_im = [jnp.ones_like(lb_re)], [jnp.zeros_like(lb_im)]
    for _ in range(PROMPT_CHUNK):
        pr, pi = pw_re[-1], pw_im[-1]
        pw_re.append(pr * lb_re - pi * lb_im)
        pw_im.append(pr * lb_im + pi * lb_re)

    steps = PROMPT_CHUNK
    c = c_re.shape[0]
    w_re = jnp.concatenate([bb_re * pw_re[steps - 1 - s] - bb_im * pw_im[steps - 1 - s] for s in range(steps)], axis=0)
    w_im = jnp.concatenate([bb_re * pw_im[steps - 1 - s] + bb_im * pw_re[steps - 1 - s] for s in range(steps)], axis=0)
    zt_re = jnp.concatenate([c_re * pw_re[t + 1] - c_im * pw_im[t + 1] for t in range(steps)], axis=0)
    zt_im = jnp.concatenate([-(c_re * pw_im[t + 1] + c_im * pw_re[t + 1]) for t in range(steps)], axis=0)

    lag = _dot_nt(w_re, c_re, HIGHEST) - _dot_nt(w_im, c_im, HIGHEST)
    cols = []
    for t in range(steps):
        live = lag[(steps - 1 - t) * c:, :]
        cols.append(live if t == steps - 1 else
                    jnp.concatenate([live, jnp.zeros(((steps - 1 - t) * c, c), F32)], axis=0))
    m = jnp.concatenate(cols, axis=1)

    tiled = jnp.concatenate([jnp.concatenate([wglu_ref[0]] * steps, axis=0)] * steps, axis=1)
    shift = c.bit_length() - 1
    row_step = lax.shift_right_logical(lax.broadcasted_iota(jnp.int32, tiled.shape, 0), shift)
    col_step = lax.shift_right_logical(lax.broadcasted_iota(jnp.int32, tiled.shape, 1), shift)
    glu = jnp.where(row_step == col_step, tiled, 0.0)
    d_t = jnp.concatenate([d_ref[0]] * steps, axis=1)

    for t, refs in ((steps, full_refs), (t_small, small_refs)):
        m_ref, wre_ref, wim_ref, ztre_ref, ztim_ref, glu_ref, dt_ref, ore_ref, oim_ref = refs
        n, full = t * c, steps * c
        m_ref[0] = m[:n, :n].astype(BF16)
        wre_ref[0] = w_re[full - n:].astype(BF16)
        wim_ref[0] = w_im[full - n:].astype(BF16)
        ztre_ref[0] = zt_re[:n].astype(BF16)
        ztim_ref[0] = zt_im[:n].astype(BF16)
        glu_ref[0] = glu[:n, :n].astype(BF16)
        dt_ref[0] = d_t[:, :n]
        ore_ref[0] = pw_re[t]
        oim_ref[0] = pw_im[t]


def _merge_groups(a_re, a_im, log_dt, b_re, b_im, c_re, c_im, d_skip, w_glu, n):
    g, p = a_re.shape
    gm = g // n
    own = np.eye(n, dtype=np.float32)[None, :, :, None, None]

    def block_diag(x):
        r, q = x.shape[1:]
        return (x.reshape(gm, n, 1, r, q) * own).transpose(0, 1, 3, 2, 4).reshape(gm, n * r, n * q)

    lanes = lambda x: x.reshape(gm, 1, -1)
    return (lanes(a_re), lanes(a_im), lanes(jnp.repeat(log_dt, p)),
            block_diag(jnp.swapaxes(b_re, 1, 2)), block_diag(jnp.swapaxes(b_im, 1, 2)),
            block_diag(c_re), block_diag(c_im), lanes(d_skip), block_diag(w_glu))


def _ssm_prep(a_re, a_im, log_dt, bt_re, bt_im, c_re, c_im, d_skip, w_glu, t_small):
    g, c, p = c_re.shape
    assert c & (c - 1) == 0
    gps = LANES // c
    gs = lambda *shape: pl.BlockSpec((gps,) + shape, lambda i: (i,) + (0,) * len(shape))
    names = ("m", "w_re", "w_im", "zt_re", "zt_im", "glu", "d", "a_re", "a_im")
    out_specs, out_shape = [], []
    for t in (PROMPT_CHUNK, t_small):
        tc = t * c
        shapes = [(tc, tc), (tc, p), (tc, p), (tc, p), (tc, p), (tc, tc), (1, tc), (1, p), (1, p)]
        dtypes = [BF16] * 6 + [F32] * 3
        out_specs += [gs(*shape) for shape in shapes]
        out_shape += [jax.ShapeDtypeStruct((g,) + shape, dtype) for shape, dtype in zip(shapes, dtypes)]
    outs = pl.pallas_call(
        functools.partial(_ssm_prep_kernel, t_small=t_small),
        grid=(g // gps,),
        in_specs=[gs(1, p), gs(1, p), gs(1, p), gs(c, p), gs(c, p), gs(c, p), gs(c, p), gs(1, c), gs(c, c)],
        out_specs=out_specs,
        out_shape=out_shape,
        compiler_params=_compiler_params("parallel"),
        name="ssm_prep",
    )(a_re, a_im, log_dt, bt_re, bt_im, c_re, c_im, d_skip, w_glu)
    return dict(zip(names, outs[:len(names)])), dict(zip(names, outs[len(names):]))


def _ssm_mix_kernel(*refs, bsz, t, aliased):
    (u_ref, m_ref, wre_ref, wim_ref, ztre_ref, ztim_ref, are_ref, aim_ref, d_ref, glu_ref, h0re_ref, h0im_ref) = refs[:12]
    y_ref, hre_ref, him_ref, xs_ref, ug_ref, vre_ref, vim_ref, pre_ref, pim_ref = refs[12 + aliased:]
    rows = u_ref.shape[0] // t
    chunks = rows // bsz
    n_grp, tc = m_ref.shape[0], m_ref.shape[1]
    c = tc // t

    def token_rows(b, s):
        return pl.ds(b * chunks * t + s, chunks, stride=t)

    def chunk_rows(b):
        return pl.ds(b, chunks, stride=bsz)

    for s in range(t):
        if chunks == 1:
            xs_ref[s] = u_ref[pl.ds(s, rows, stride=t), :]
        else:
            for b in range(bsz):
                xs_ref[s, chunk_rows(b), :] = u_ref[token_rows(b, s), :]
    for gi in range(n_grp):
        lanes = slice(gi * c, (gi + 1) * c)
        u = jnp.concatenate([xs_ref[s, :, lanes] for s in range(t)], axis=1)
        ug_ref[gi] = u
        vre_ref[gi] = _dot(u.astype(BF16), wre_ref[gi])
        vim_ref[gi] = _dot(u.astype(BF16), wim_ref[gi])

    per_tile = max(1, 8 // bsz)
    tile_rows = per_tile * bsz

    def chunk_tile(i, h):
        sel = pl.ds(pl.multiple_of(i * tile_rows, tile_rows), tile_rows)
        nxt = []
        for gi in range(n_grp):
            h_re, h_im = h[2 * gi], h[2 * gi + 1]
            v_re, v_im = vre_ref[gi, sel, :], vim_ref[gi, sel, :]
            a_re, a_im = are_ref[gi], aim_ref[gi]
            ent_re, ent_im = [], []
            for j in range(per_tile):
                ent_re.append(h_re)
                ent_im.append(h_im)
                part = slice(j * bsz, (j + 1) * bsz)
                h_re, h_im = a_re * h_re - a_im * h_im + v_re[part], a_re * h_im + a_im * h_re + v_im[part]
            pre_ref[gi, sel, :] = jnp.concatenate(ent_re, axis=0)
            pim_ref[gi, sel, :] = jnp.concatenate(ent_im, axis=0)
            nxt += [h_re, h_im]
        return tuple(nxt)

    h = tuple(r[gi] for gi in range(n_grp) for r in (h0re_ref, h0im_ref))
    n_tiles = chunks // per_tile
    h = lax.fori_loop(0, n_tiles, chunk_tile, h, unroll=min(n_tiles, 4))
    for gi in range(n_grp):
        hre_ref[gi] = h[2 * gi]
        him_ref[gi] = h[2 * gi + 1]

    for gi in range(n_grp):
        u = ug_ref[gi]
        y = (_dot(u.astype(BF16), m_ref[gi]) + _dot_nt(pre_ref[gi].astype(BF16), ztre_ref[gi])
             + _dot_nt(pim_ref[gi].astype(BF16), ztim_ref[gi]) + d_ref[gi] * u)
        y = jax.nn.gelu(y)
        y = y * jax.nn.sigmoid(_dot(y.astype(BF16), glu_ref[gi]))
        for s in range(t):
            xs_ref[s, :, gi * c:(gi + 1) * c] = y[:, s * c:(s + 1) * c]
    for s in range(t):
        if chunks == 1:
            y_ref[pl.ds(s, rows, stride=t), :] = xs_ref[s]
        else:
            for b in range(bsz):
                y_ref[token_rows(b, s), :] = xs_ref[s, chunk_rows(b), :]


def _ssm_mix(z, y_prev, tab, layer, h0_re, h0_im, row0, n_rows, bsz, t):
    g, _, p = h0_re.shape
    tc = tab["m"].shape[1]
    n_tok = z.shape[0]
    c = tc // t
    width = g * c
    lanes = LANES
    gps = lanes // c
    rows = n_rows // t
    first = layer * (g // gps)
    blk = lambda *shape: pl.BlockSpec((gps,) + shape, lambda i: (i,) + (0,) * len(shape))
    tbl = lambda *shape: pl.BlockSpec((gps,) + shape, lambda i: (first + i,) + (0,) * len(shape))
    tok = pl.BlockSpec((n_rows, lanes), lambda i: (row0 // n_rows, i))
    aliased = y_prev is not None
    in_specs = [tok, tbl(tc, tc), tbl(tc, p), tbl(tc, p), tbl(tc, p), tbl(tc, p), tbl(1, p), tbl(1, p), tbl(1, tc),
                tbl(tc, tc), blk(bsz, p), blk(bsz, p)]
    args = [z, tab["m"], tab["w_re"], tab["w_im"], tab["zt_re"], tab["zt_im"], tab["a_re"], tab["a_im"], tab["d"],
            tab["glu"], h0_re, h0_im]
    if aliased:
        in_specs.append(pl.BlockSpec(memory_space=pl.ANY))
        args.append(y_prev)
    return pl.pallas_call(
        functools.partial(_ssm_mix_kernel, bsz=bsz, t=t, aliased=aliased),
        grid=(g // gps,),
        in_specs=in_specs,
        out_specs=[tok, blk(bsz, p), blk(bsz, p)],
        out_shape=[jax.ShapeDtypeStruct((n_tok, width), F32), jax.ShapeDtypeStruct((g, bsz, p), F32),
                   jax.ShapeDtypeStruct((g, bsz, p), F32)],
        scratch_shapes=[pltpu.VMEM((t, rows, lanes), F32), pltpu.VMEM((gps, rows, tc), F32)]
        + [pltpu.VMEM((gps, rows, p), F32)] * 4,
        input_output_aliases={len(args) - 1: 0} if aliased else {},
        compiler_params=_compiler_params("parallel"),
        name="ssm_mix",
    )(*args)


def _bias_table(q_pos, k_pos, n_heads):
    slopes = np.exp2(-8.0 * np.arange(1, n_heads + 1, dtype=np.float64) / n_heads)
    dist = q_pos[:, None] - k_pos[None, :]
    valid = (dist >= 0) & (dist < WINDOW) & (k_pos[None, :] >= 0)
    bias = -slopes[:, None, None] * dist[None].astype(np.float64)
    return np.where(valid[None], bias, -np.inf).astype(np.float32)


def _head_rms(x, gain):
    heads = [_rms(x[:, i:i + HEAD_DIM], gain) for i in range(0, x.shape[1], HEAD_DIM)]
    return jnp.concatenate(heads, axis=1)


def _softmax_sink_pv(logits, sink, v):
    m = jnp.maximum(jnp.max(logits, axis=-1, keepdims=True), sink)
    p = jnp.exp(logits - m)
    denom = jnp.sum(p, axis=-1, keepdims=True) + jnp.exp(sink - m)
    return _dot(p.astype(BF16), v) / denom


def _prompt_attn_kernel(sinks_ref, q_ref, k_ref, v_ref, qg_ref, kg_ref, bias_ref, y_ref, ks_ref, vs_ref,
                        kp_ref, vp_ref, *, n_heads):
    n = pl.program_id(1)

    @pl.when(n == 0)
    def _():
        kp_ref[...] = jnp.zeros_like(kp_ref)
        vp_ref[...] = jnp.zeros_like(vp_ref)

    q_all = q_ref[...]
    blk = q_all.shape[0]
    half = 2 * HEAD_DIM
    assert n_heads // N_KV_HEADS == 4 and half == 128
    k_cur = _head_rms(k_ref[...], kg_ref[...])
    v_cur = v_ref[...]
    k2 = jnp.concatenate([kp_ref[...], k_cur], axis=0)
    v2t = jnp.concatenate([vp_ref[...], v_cur], axis=0).T.astype(BF16)
    qs = (q_all * qg_ref[...] * QK_SCALE).astype(BF16)
    qsq = q_all * q_all
    table = jnp.minimum(n, 1)
    from_prev = (lax.broadcasted_iota(jnp.int32, (blk, blk), 0) > lax.broadcasted_iota(jnp.int32, (blk, blk), 1))
    low_lanes = lax.broadcasted_iota(jnp.int32, (2 * blk, half), 1) < HEAD_DIM
    low8 = lax.broadcasted_iota(jnp.int32, (8, half), 1) < HEAD_DIM
    pick = jnp.concatenate([jnp.where(low8, 1.0, 0.0), jnp.where(low8, 0.0, 1.0)], axis=0)
    for kv in range(N_KV_HEADS):
        k_tile = k2[:, (kv // 2) * half:(kv // 2 + 1) * half]
        k_own = jnp.where(low_lanes if kv % 2 == 0 else ~low_lanes, k_tile, 0.0)
        k_swap = pltpu.roll(k_own, HEAD_DIM, axis=1)
        k_lhs = jnp.concatenate([k_own, k_swap] if kv % 2 == 0 else [k_swap, k_own], axis=0).astype(BF16)
        tiles = slice(2 * kv * half, (2 * kv + 1) * half), slice((2 * kv + 1) * half, (2 * kv + 2) * half)
        q_rows = jnp.concatenate([qs[:, t] for t in tiles], axis=0)
        qsq_rows = jnp.concatenate([qsq[:, t] for t in tiles], axis=0)
        s = _dot_nt(k_lhs, q_rows)
        ssq = _dot_nt(pick, qsq_rows, HIGHEST)
        probs, inv_denoms = [], []
        for i in range(4):
            odd, tile = i % 2, i // 2
            lanes = slice(tile * blk, (tile + 1) * blk)
            s_h = s[odd * 2 * blk:(odd + 1) * 2 * blk, lanes]
            q_rms = lax.rsqrt(ssq[odd * 8:odd * 8 + 1, lanes] * (1.0 / HEAD_DIM) + EPS)
            logits = jnp.where(from_prev, s_h[:blk], s_h[blk:]) * q_rms + bias_ref[table, 4 * kv + i]
            sink = sinks_ref[4 * kv + i]
            m = jnp.maximum(jnp.max(logits, axis=0, keepdims=True), sink)
            p = jnp.exp(logits - m)
            inv_denoms.append(1.0 / (jnp.sum(p, axis=0, keepdims=True) + jnp.exp(sink - m)))
            probs.append(jnp.concatenate([jnp.where(from_prev, p, 0.0), jnp.where(from_prev, 0.0, p)], axis=0))
        o = _dot(v2t[kv * HEAD_DIM:(kv + 1) * HEAD_DIM], jnp.concatenate(probs, axis=1).astype(BF16))
        for tile in range(2):
            pair = [o[:, i * blk:(i + 1) * blk] * inv_denoms[i] for i in (2 * tile, 2 * tile + 1)]
            y_ref[:, tiles[tile]] = jnp.concatenate(pair, axis=0).T
    kp_ref[...] = k_cur
    vp_ref[...] = v_cur

    @pl.when(n == pl.num_programs(1) - 1)
    def _():
        ks_ref[0] = k_cur
        vs_ref[0] = v_cur


def _prompt_attn(z, sinks, q_gain, k_gain, bsz, seq, n_tok, col_q, attn_w, kv_w):
    blk = WINDOW
    nb = seq // blk
    n_heads = attn_w // HEAD_DIM
    row = lambda b, n: b * nb + n
    q_blk, k_blk, v_blk = col_q // attn_w, (col_q + attn_w) // kv_w, (col_q + attn_w + kv_w) // kv_w
    qi, kj = np.arange(blk)[:, None], np.arange(blk)[None, :]
    first, later = [np.take_along_axis(_bias_table(first_pos + np.arange(blk), first_pos - blk + np.arange(2 * blk), n_heads),
                                       np.broadcast_to(np.where(kj > qi, kj, kj + blk), (n_heads, blk, blk)), axis=2)
                    for first_pos in (0, blk)]
    bias = np.stack([first, later]).swapaxes(2, 3)
    return pl.pallas_call(
        functools.partial(_prompt_attn_kernel, n_heads=n_heads),
        grid=(bsz, nb),
        in_specs=[
            pl.BlockSpec(memory_space=pltpu.SMEM),
            pl.BlockSpec((blk, attn_w), lambda b, n: (row(b, n), q_blk)),
            pl.BlockSpec((blk, kv_w), lambda b, n: (row(b, n), k_blk)),
            pl.BlockSpec((blk, kv_w), lambda b, n: (row(b, n), v_blk)),
            pl.BlockSpec((1, attn_w), lambda b, n: (0, 0)),
            pl.BlockSpec((1, HEAD_DIM), lambda b, n: (0, 0)),
            pl.BlockSpec(bias.shape, lambda b, n: (0, 0, 0, 0)),
        ],
        out_specs=[
            pl.BlockSpec((blk, attn_w), lambda b, n: (row(b, n), 0)),
            pl.BlockSpec((1, blk, kv_w), lambda b, n: (b, 0, 0)),
            pl.BlockSpec((1, blk, kv_w), lambda b, n: (b, 0, 0)),
        ],
        out_shape=[jax.ShapeDtypeStruct((n_tok, attn_w), F32),
                   jax.ShapeDtypeStruct((bsz, blk, kv_w), F32),
                   jax.ShapeDtypeStruct((bsz, blk, kv_w), F32)],
        scratch_shapes=[pltpu.VMEM((blk, kv_w), F32)] * 2,
        compiler_params=_compiler_params("parallel", "arbitrary"),
        name="prompt_attn",
    )(sinks, z, z, z, jnp.tile(q_gain, n_heads).reshape(1, attn_w), k_gain.reshape(1, HEAD_DIM), jnp.asarray(bias))


def _sample_attn_kernel(*refs, dec_seq, aliased):
    q_ref, k_ref, v_ref, ck_ref, cv_ref, qg_ref, kg_ref, bias_ref, sink_ref = refs[:9]
    y_ref, ks_ref, vs_ref, kbuf_ref, vbuf_ref, p_ref = refs[9 + 2 * aliased:]
    tile, q_rows, _ = q_ref.shape
    cache_w = ck_ref.shape[1]
    keys = kbuf_ref.shape[0] // tile
    k_new = _head_rms(k_ref[...], kg_ref[...])
    v_new = v_ref[...]
    pad = jnp.zeros((keys - cache_w, kbuf_ref.shape[1]), F32)
    for b in range(tile):
        rows = slice(b * dec_seq, (b + 1) * dec_seq)
        kbuf_ref[b * keys:b * keys + cache_w, :] = ck_ref[b]
        vbuf_ref[b * keys:b * keys + cache_w, :] = cv_ref[b]
        kbuf_ref[b * keys + cache_w:(b + 1) * keys, :] = pad
        vbuf_ref[b * keys + cache_w:(b + 1) * keys, :] = pad
        kbuf_ref[b * keys + cache_w:b * keys + cache_w + dec_seq, :] = k_new[rows]
        vbuf_ref[b * keys + cache_w:b * keys + cache_w + dec_seq, :] = v_new[rows]
    q = q_ref[...].reshape(tile * q_rows, q_ref.shape[2])
    q = q * lax.rsqrt(jnp.sum(q * q, axis=-1, keepdims=True) * (1.0 / HEAD_DIM) + EPS) * qg_ref[...]
    s = _dot_nt((q * QK_SCALE).astype(BF16), kbuf_ref[...].astype(BF16))
    p_ref[...] = jnp.zeros_like(p_ref)
    inv_denoms = []
    for b in range(tile):
        own_rows, own_keys = slice(b * q_rows, (b + 1) * q_rows), slice(b * keys, (b + 1) * keys)
        logits = s[own_rows, own_keys] + bias_ref[...]
        sink = sink_ref[...]
        m = jnp.maximum(jnp.max(logits, axis=-1, keepdims=True), sink)
        p = jnp.exp(logits - m)
        inv_denoms.append(1.0 / (jnp.sum(p, axis=-1, keepdims=True) + jnp.exp(sink - m)))
        p_ref[own_rows, own_keys] = p.astype(BF16)
    o = _dot(p_ref[...], vbuf_ref[...].astype(BF16))
    for b in range(tile):
        rows = slice(b * dec_seq, (b + 1) * dec_seq)
        y_ref[b] = o[b * q_rows:(b + 1) * q_rows] * inv_denoms[b]
        ks_ref[b, 0:cache_w - dec_seq, :] = ck_ref[b, dec_seq:cache_w, :]
        vs_ref[b, 0:cache_w - dec_seq, :] = cv_ref[b, dec_seq:cache_w, :]
        ks_ref[b, cache_w - dec_seq:cache_w, :] = k_new[rows]
        vs_ref[b, cache_w - dec_seq:cache_w, :] = v_new[rows]


def _sample_attn(z, cache_k, cache_v, windows, layer, sinks, q_gain, k_gain, row0, dec_seq, col_q, attn_w, kv_w):
    _, dec_batch, cache_w, _ = cache_k.shape
    n_heads = attn_w // HEAD_DIM
    q_per_kv = n_heads // N_KV_HEADS
    tile = SAMPLE_BATCH_TILE
    r0 = row0 // (tile * dec_seq)
    k_blk, v_blk = (col_q + attn_w) // kv_w, (col_q + attn_w + kv_w) // kv_w
    keys = 2 * cache_w
    k_pos = np.full(keys, -1)
    k_pos[:cache_w] = PAST_LEN - cache_w + np.arange(cache_w)
    k_pos[cache_w:cache_w + dec_seq] = PAST_LEN + np.arange(dec_seq)
    bias = _bias_table(PAST_LEN + np.arange(dec_seq), k_pos, n_heads).reshape(n_heads * dec_seq, keys)
    sink_col = jnp.repeat(sinks, dec_seq).reshape(n_heads * dec_seq, 1)
    q = z[row0:, col_q:col_q + attn_w].reshape(dec_batch, dec_seq, N_KV_HEADS, q_per_kv, 1, HEAD_DIM)
    own_kv = np.eye(N_KV_HEADS, dtype=np.float32)[None, :, None, None, :, None]
    q = (q.transpose(0, 2, 3, 1, 4, 5) * own_kv).reshape(dec_batch, n_heads * dec_seq, kv_w)
    window = pl.BlockSpec((None, tile, cache_w, kv_w), lambda i: (layer, i, 0, 0))
    aliased = windows is not None
    in_specs = [
        pl.BlockSpec((tile, n_heads * dec_seq, kv_w), lambda i: (i, 0, 0)),
        pl.BlockSpec((tile * dec_seq, kv_w), lambda i: (r0 + i, k_blk)),
        pl.BlockSpec((tile * dec_seq, kv_w), lambda i: (r0 + i, v_blk)),
        window,
        window,
        pl.BlockSpec((1, kv_w), lambda i: (0, 0)),
        pl.BlockSpec((1, HEAD_DIM), lambda i: (0, 0)),
        pl.BlockSpec(bias.shape, lambda i: (0, 0)),
        pl.BlockSpec(sink_col.shape, lambda i: (0, 0)),
    ]
    args = [q, z, z, cache_k, cache_v, jnp.tile(q_gain, N_KV_HEADS).reshape(1, kv_w), k_gain.reshape(1, HEAD_DIM),
            jnp.asarray(bias), sink_col]
    if aliased:
        in_specs += [pl.BlockSpec(memory_space=pl.ANY)] * 2
        args += list(windows)
    y, ks, vs = pl.pallas_call(
        functools.partial(_sample_attn_kernel, dec_seq=dec_seq, aliased=aliased),
        grid=(dec_batch // tile,),
        in_specs=in_specs,
        out_specs=[pl.BlockSpec((tile, n_heads * dec_seq, kv_w), lambda i: (i, 0, 0)), window, window],
        out_shape=[jax.ShapeDtypeStruct(q.shape, F32),
                   jax.ShapeDtypeStruct(cache_k.shape, F32),
                   jax.ShapeDtypeStruct(cache_v.shape, F32)],
        scratch_shapes=[pltpu.VMEM((tile * keys, kv_w), F32)] * 2
        + [pltpu.VMEM((tile * n_heads * dec_seq, tile * keys), BF16)],
        input_output_aliases={len(args) - 2: 1, len(args) - 1: 2} if aliased else {},
        compiler_params=_compiler_params("parallel"),
        name="sample_attn",
    )(*args)
    y = y.reshape(dec_batch, N_KV_HEADS, q_per_kv, dec_seq, N_KV_HEADS, HEAD_DIM)
    y = jnp.stack([y[:, kv, :, :, kv] for kv in range(N_KV_HEADS)], axis=1)
    return y.transpose(0, 3, 1, 2, 4).reshape(dec_batch * dec_seq, attn_w), ks, vs


def kernel(x_prompt, x_sample, cache_k, cache_v, state_ssm_re, state_ssm_im, ffn1_norm, ffn1_w_gate, ffn1_w_up, ffn1_w_down, mix_norm, w_in, ssm_A_re, ssm_A_im, ssm_B_re, ssm_B_im, ssm_C_re, ssm_C_im, ssm_D, ssm_log_dt, ssm_w_glu, q_norm, k_norm, sinks, ssm_out_norm, attn_out_norm, w_out, ffn2_norm, ffn2_w_gate, ffn2_w_up, ffn2_w_down):
    bsz, seq, d = x_prompt.shape
    dec_batch, dec_seq, _ = x_sample.shape
    depth = w_in.shape[0]
    ssm_w = ssm_A_re.shape[1] * SSM_GROUP_CH
    attn_w = w_out.shape[1] - ssm_w
    kv_w = N_KV_HEADS * HEAD_DIM
    cache_w = cache_k.shape[2]
    n_prompt = bsz * seq
    n_tok = n_prompt + dec_batch * dec_seq

    xs = [x_prompt.reshape(n_prompt, d), x_sample.reshape(dec_batch * dec_seq, d)]
    n_groups = ssm_w // SSM_GROUP_CH
    n_merged = n_groups // SSM_MERGE
    zeros_state = jnp.zeros((n_merged, bsz, SSM_MERGE * SSM_STATE), F32)
    outs = [[] for _ in range(6)]
    cache_k_flat = cache_k.reshape(depth, dec_batch, cache_w, kv_w)
    cache_v_flat = cache_v.reshape(depth, dec_batch, cache_w, kv_w)
    windows = None
    ffn1_w, ffn2_w = (ffn1_w_gate, ffn1_w_up, ffn1_w_down), (ffn2_w_gate, ffn2_w_up, ffn2_w_down)
    w_in_bf, w_out_bf = w_in.astype(BF16), w_out.astype(BF16)
    all_layers = lambda a: a.reshape((-1,) + a.shape[2:])
    tab_p, tab_s = _ssm_prep(*_merge_groups(*map(all_layers, (ssm_A_re, ssm_A_im, ssm_log_dt, ssm_B_re, ssm_B_im,
                                                               ssm_C_re, ssm_C_im, ssm_D, ssm_w_glu)), SSM_MERGE),
                             t_small=dec_seq)
    for l in range(depth):
        x = _ffn(xs if l == 0 else [x], ffn1_norm[l], *ffn1_w, layer=l)
        z = _norm_matmul(x, mix_norm[l], w_in_bf, layer=l, tn=w_in.shape[2])

        grp_major = lambda a: a.reshape(a.shape[0], n_merged, SSM_MERGE * SSM_STATE).transpose(1, 0, 2)
        seq_major = lambda a: a.transpose(1, 0, 2).reshape(a.shape[1], n_groups, SSM_STATE)
        y_ssm, hp_re, hp_im = _ssm_mix(z, None, tab_p, l, zeros_state, zeros_state, 0, n_prompt, bsz, PROMPT_CHUNK)
        y_ssm, hs_re, hs_im = _ssm_mix(z, y_ssm, tab_s, l, grp_major(state_ssm_re[l]), grp_major(state_ssm_im[l]),
                                       n_prompt, dec_batch * dec_seq, dec_batch, dec_seq)
        hp_re, hp_im, hs_re, hs_im = map(seq_major, (hp_re, hp_im, hs_re, hs_im))

        y_attn, pk, pv = _prompt_attn(z, sinks[l], q_norm[l], k_norm[l], bsz, seq, n_tok, ssm_w, attn_w, kv_w)
        ya_s, *windows = _sample_attn(z, cache_k_flat, cache_v_flat, windows, l, sinks[l], q_norm[l], k_norm[l],
                                      n_prompt, dec_seq, ssm_w, attn_w, kv_w)
        y_attn = lax.dynamic_update_slice(y_attn, ya_s, (n_prompt, 0))

        x = _out_proj(x, y_ssm, y_attn, ssm_out_norm[l], attn_out_norm[l], w_out_bf, layer=l)
        if l < depth - 1:
            x = _ffn([x], ffn2_norm[l], *ffn2_w, layer=l)

        kv_shape = (-1, N_KV_HEADS, HEAD_DIM)
        for dst, val in zip(outs, (pk.reshape(bsz, *kv_shape), pv.reshape(bsz, *kv_shape), hp_re, hp_im, hs_re, hs_im)):
            dst.append(val)
    y_prompt, y_sample = _ffn([x], ffn2_norm[depth - 1], *ffn2_w, layer=depth - 1, out_rows=(n_prompt, n_tok - n_prompt))
    y_prompt, y_sample = y_prompt.reshape(bsz, seq, d), y_sample.reshape(dec_batch, dec_seq, d)
    pk, pv, hp_re, hp_im, hs_re, hs_im = (jnp.stack(o) for o in outs)
    sk, sv = (w.reshape(cache_k.shape) for w in windows)
    return (y_prompt, y_sample, pk, pv, hp_re, hp_im, sk, sv, hs_re, hs_im)
```

```python
import functools

import jax
import jax.numpy as jnp
import numpy as np
from jax import lax
from jax.experimental import pallas as pl
from jax.experimental.pallas import tpu as pltpu

F32 = jnp.float32
BF16 = jnp.bfloat16

EPS = 1e-6
SSM_GROUP_CH = 16
SSM_STATE = 64
HEAD_DIM = 64
N_KV_HEADS = 4
WINDOW = 128
PAST_LEN = 8192
QK_SCALE = HEAD_DIM ** -0.5
PROMPT_CHUNK = 16

TOKEN_TILE = 512
FFN_TOKEN_TILE = 1088
FFN_VMEM_LIMIT_BYTES = 60 * 1024 * 1024
FF_TILE = 256
SAMPLE_BATCH_TILE = 8
PROMPT_BLOCKS_PER_STEP = 4
VMEM_LIMIT_BYTES = 48 * 1024 * 1024

HIGHEST = lax.Precision.HIGHEST
LANES = 128
SSM_MERGE = 2


def _rms(x, gain):
    return x * lax.rsqrt(jnp.mean(x * x, axis=-1, keepdims=True) + EPS) * gain


def _dot(a, b, precision=None):
    return jnp.dot(a, b, preferred_element_type=F32, precision=precision)


def _dot_nt(a, b, precision=None):
    return lax.dot_general(a, b, (((1,), (1,)), ((), ())), preferred_element_type=F32, precision=precision)


def _compiler_params(*semantics):
    return pltpu.CompilerParams(dimension_semantics=semantics, vmem_limit_bytes=VMEM_LIMIT_BYTES)


def _ffn_kernel(*refs, n_in, n_out, split_row):
    x_refs, (g_ref, wg_ref, wu_ref, wd_ref) = refs[:n_in], refs[n_in:n_in + 4]
    o_refs, xn_ref = refs[n_in + 4:n_in + 4 + n_out], refs[-1]
    acc_ref = o_refs[0]
    i, j = pl.program_id(0), pl.program_id(1)
    boundary = i == pl.num_programs(0) - 1

    def with_x_tile(fn):
        if n_in == 1:
            fn(x_refs[0][...])
        else:
            pl.when(jnp.logical_not(boundary))(lambda: fn(x_refs[0][...]))
            pl.when(boundary)(lambda: fn(jnp.concatenate([x_refs[0][0:split_row, :], x_refs[1][...]], axis=0)))

    @pl.when(j == 0)
    def _():
        acc_ref[...] = jnp.zeros_like(acc_ref)

        def normalise(x):
            xn_ref[...] = _rms(x, g_ref[...]).astype(BF16)
        with_x_tile(normalise)

    xn = xn_ref[...]
    gate = _dot(xn, wg_ref[...].astype(BF16))
    up = _dot(xn, wu_ref[...].astype(BF16))
    h = (gate * jax.nn.sigmoid(gate) * up).astype(BF16)
    acc_ref[...] += _dot(h, wd_ref[...].astype(BF16))

    @pl.when(j == pl.num_programs(1) - 1)
    def _():
        def residual(x):
            acc_ref[...] = x + 0.5 * acc_ref[...]
        with_x_tile(residual)
        if n_out == 2:
            @pl.when(boundary)
            def _():
                o_refs[1][...] = acc_ref[split_row:, :]


def _ffn(xs, gain, w_gate, w_up, w_down, layer, out_rows=None):
    d = xs[0].shape[1]
    n_tok = sum(x.shape[0] for x in xs)
    d_ff = w_gate.shape[2]
    if len(xs) == 2 or out_rows is not None:
        tm = FFN_TOKEN_TILE if n_tok % FFN_TOKEN_TILE == 0 else n_tok
        n0, n1 = (x.shape[0] for x in xs) if len(xs) == 2 else out_rows
        split_row = n0 - (n_tok // tm - 1) * tm
        assert split_row + n1 == tm and split_row % 8 == 0
    else:
        tm = FFN_TOKEN_TILE if n_tok % FFN_TOKEN_TILE == 0 else TOKEN_TILE
        split_row = None
    tf = FF_TILE
    tile_spec = pl.BlockSpec((tm, d), lambda i, j: (i, 0))
    tail_spec = lambda rows: pl.BlockSpec((rows, d), lambda i, j: (0, 0))
    x_specs = [tile_spec] + ([tail_spec(xs[1].shape[0])] if len(xs) == 2 else [])
    if out_rows is None:
        out_specs, out_shape = [tile_spec], [jax.ShapeDtypeStruct((n_tok, d), F32)]
    else:
        out_specs = [tile_spec, tail_spec(out_rows[1])]
        out_shape = [jax.ShapeDtypeStruct((rows, d), F32) for rows in out_rows]
    outs = pl.pallas_call(
        functools.partial(_ffn_kernel, n_in=len(xs), n_out=len(out_specs), split_row=split_row),
        grid=(n_tok // tm, d_ff // tf),
        in_specs=x_specs + [
            pl.BlockSpec((1, d), lambda i, j: (0, 0)),
            pl.BlockSpec((None, d, tf), lambda i, j: (layer, 0, j)),
            pl.BlockSpec((None, d, tf), lambda i, j: (layer, 0, j)),
            pl.BlockSpec((None, tf, d), lambda i, j: (layer, j, 0)),
        ],
        out_specs=out_specs,
        out_shape=out_shape,
        scratch_shapes=[pltpu.VMEM((tm, d), BF16)],
        compiler_params=pltpu.CompilerParams(dimension_semantics=("arbitrary", "arbitrary"),
                                             vmem_limit_bytes=FFN_VMEM_LIMIT_BYTES),
        name="ffn",
    )(*xs, gain.reshape(1, d), w_gate, w_up, w_down)
    return outs[0] if out_rows is None else outs


def _norm_matmul_kernel(x_ref, g_ref, w_ref, o_ref, xn_ref):
    @pl.when(pl.program_id(1) == 0)
    def _():
        xn_ref[...] = _rms(x_ref[...], g_ref[...]).astype(BF16)

    o_ref[...] = _dot(xn_ref[...], w_ref[...])


def _norm_matmul(x, gain, w, layer, tn):
    n_tok, d = x.shape
    n_out = w.shape[2]
    tm = TOKEN_TILE
    return pl.pallas_call(
        _norm_matmul_kernel,
        grid=(n_tok // tm, n_out // tn),
        in_specs=[
            pl.BlockSpec((tm, d), lambda i, j: (i, 0)),
            pl.BlockSpec((1, d), lambda i, j: (0, 0)),
            pl.BlockSpec((None, d, tn), lambda i, j: (layer, 0, j)),
        ],
        out_specs=pl.BlockSpec((tm, tn), lambda i, j: (i, j)),
        out_shape=jax.ShapeDtypeStruct((n_tok, n_out), F32),
        scratch_shapes=[pltpu.VMEM((tm, d), BF16)],
        compiler_params=_compiler_params("parallel", "arbitrary"),
        name="in_proj",
    )(x, gain.reshape(1, d), w)


def _out_proj_kernel(x_ref, ys_ref, ya_ref, gs_ref, ga_ref, ws_ref, wa_ref, o_ref):
    ys = _rms(ys_ref[...], gs_ref[...]).astype(BF16)
    ya = _rms(ya_ref[...], ga_ref[...]).astype(BF16)
    o_ref[...] = x_ref[...] + _dot(ys, ws_ref[...]) + _dot(ya, wa_ref[...])


def _out_proj(x, y_ssm, y_attn, g_ssm, g_attn, w_out, layer):
    n_tok, d = x.shape
    ssm_w, attn_w = y_ssm.shape[1], y_attn.shape[1]
    tm = TOKEN_TILE
    return pl.pallas_call(
        _out_proj_kernel,
        grid=(n_tok // tm,),
        in_specs=[
            pl.BlockSpec((tm, d), lambda i: (i, 0)),
            pl.BlockSpec((tm, ssm_w), lambda i: (i, 0)),
            pl.BlockSpec((tm, attn_w), lambda i: (i, 0)),
            pl.BlockSpec((1, ssm_w), lambda i: (0, 0)),
            pl.BlockSpec((1, attn_w), lambda i: (0, 0)),
            pl.BlockSpec((None, ssm_w, d), lambda i: (layer, 0, 0)),
            pl.BlockSpec((None, attn_w, d), lambda i: (layer, ssm_w // attn_w, 0)),
        ],
        out_specs=pl.BlockSpec((tm, d), lambda i: (i, 0)),
        out_shape=jax.ShapeDtypeStruct((n_tok, d), F32),
        compiler_params=_compiler_params("parallel"),
        name="out_proj",
    )(x, y_ssm, y_attn, g_ssm.reshape(1, ssm_w), g_attn.reshape(1, attn_w), w_out, w_out)


def _ssm_prep_kernel(*refs, t_small):
    for gi in range(refs[0].shape[0]):
        _ssm_prep_group(*[r.at[pl.ds(gi, 1)] for r in refs], t_small=t_small)


def _ssm_prep_group(are_ref, aim_ref, ldt_ref, btre_ref, btim_ref, cre_ref, cim_ref, d_ref, wglu_ref,
                    *table_refs, t_small):
    per_set = len(table_refs) // 2
    full_refs, small_refs = table_refs[:per_set], table_refs[per_set:]
    a_re, a_im = are_ref[0], aim_ref[0]
    dt = jnp.exp(ldt_ref[0])
    mag = jnp.exp(a_re * dt)
    lb_re, lb_im = mag * jnp.cos(a_im * dt), mag * jnp.sin(a_im * dt)
    den = a_re * a_re + a_im * a_im
    q_re = ((lb_re - 1.0) * a_re + lb_im * a_im) / den
    q_im = (lb_im * a_re - (lb_re - 1.0) * a_im) / den
    bt_re, bt_im = btre_ref[0], btim_ref[0]
    bb_re = q_re * bt_re - q_im * bt_im
    bb_im = q_re * bt_im + q_im * bt_re
    c_re, c_im = cre_ref[0], cim_ref[0]

    pw_re, pw_im = [jnp.ones_like(lb_re)], [jnp.zeros_like(lb_im)]
    for _ in range(PROMPT_CHUNK):
        pr, pi = pw_re[-1], pw_im[-1]
        pw_re.append(pr * lb_re - pi * lb_im)
        pw_im.append(pr * lb_im + pi * lb_re)

    steps = PROMPT_CHUNK
    c = c_re.shape[0]
    w_re = jnp.concatenate([bb_re * pw_re[steps - 1 - s] - bb_im * pw_im[steps - 1 - s] for s in range(steps)], axis=0)
    w_im = jnp.concatenate([bb_re * pw_im[steps - 1 - s] + bb_im * pw_re[steps - 1 - s] for s in range(steps)], axis=0)
    zt_re = jnp.concatenate([c_re * pw_re[t + 1] - c_im * pw_im[t + 1] for t in range(steps)], axis=0)
    zt_im = jnp.concatenate([-(c_re * pw_im[t + 1] + c_im * pw_re[t + 1]) for t in range(steps)], axis=0)

    lag = _dot_nt(w_re, c_re, HIGHEST) - _dot_nt(w_im, c_im, HIGHEST)
    cols = []
    for t in range(steps):
        live = lag[(steps - 1 - t) * c:, :]
        cols.append(live if t == steps - 1 else
                    jnp.concatenate([live, jnp.zeros(((steps - 1 - t) * c, c), F32)], axis=0))
    m = jnp.concatenate(cols, axis=1)

    tiled = jnp.concatenate([jnp.concatenate([wglu_ref[0]] * steps, axis=0)] * steps, axis=1)
    shift = c.bit_length() - 1
    row_step = lax.shift_right_logical(lax.broadcasted_iota(jnp.int32, tiled.shape, 0), shift)
    col_step = lax.shift_right_logical(lax.broadcasted_iota(jnp.int32, tiled.shape, 1), shift)
    glu = jnp.where(row_step == col_step, tiled, 0.0)
    d_t = jnp.concatenate([d_ref[0]] * steps, axis=1)

    for t, refs in ((steps, full_refs), (t_small, small_refs)):
        m_ref, wre_ref, wim_ref, ztre_ref, ztim_ref, glu_ref, dt_ref, ore_ref, oim_ref = refs
        n, full = t * c, steps * c
        m_ref[0] = m[:n, :n].astype(BF16)
        wre_ref[0] = w_re[full - n:].astype(BF16)
        wim_ref[0] = w_im[full - n:].astype(BF16)
        ztre_ref[0] = zt_re[:n].astype(BF16)
        ztim_ref[0] = zt_im[:n].astype(BF16)
        glu_ref[0] = glu[:n, :n].astype(BF16)
        dt_ref[0] = d_t[:, :n]
        ore_ref[0] = pw_re[t]
        oim_ref[0] = pw_im[t]


def _merge_groups(a_re, a_im, log_dt, b_re, b_im, c_re, c_im, d_skip, w_glu, n):
    g, p = a_re.shape
    gm = g // n
    own = np.eye(n, dtype=np.float32)[None, :, :, None, None]

    def block_diag(x):
        r, q = x.shape[1:]
        return (x.reshape(gm, n, 1, r, q) * own).transpose(0, 1, 3, 2, 4).reshape(gm, n * r, n * q)

    lanes = lambda x: x.reshape(gm, 1, -1)
    return (lanes(a_re), lanes(a_im), lanes(jnp.repeat(log_dt, p)),
            block_diag(jnp.swapaxes(b_re, 1, 2)), block_diag(jnp.swapaxes(b_im, 1, 2)),
            block_diag(c_re), block_diag(c_im), lanes(d_skip), block_diag(w_glu))


def _ssm_prep(a_re, a_im, log_dt, bt_re, bt_im, c_re, c_im, d_skip, w_glu, t_small):
    g, c, p = c_re.shape
    assert c & (c - 1) == 0
    gps = LANES // c
    gs = lambda *shape: pl.BlockSpec((gps,) + shape, lambda i: (i,) + (0,) * len(shape))
    names = ("m", "w_re", "w_im", "zt_re", "zt_im", "glu", "d", "a_re", "a_im")
    out_specs, out_shape = [], []
    for t in (PROMPT_CHUNK, t_small):
        tc = t * c
        shapes = [(tc, tc), (tc, p), (tc, p), (tc, p), (tc, p), (tc, tc), (1, tc), (1, p), (1, p)]
        dtypes = [BF16] * 6 + [F32] * 3
        out_specs += [gs(*shape) for shape in shapes]
        out_shape += [jax.ShapeDtypeStruct((g,) + shape, dtype) for shape, dtype in zip(shapes, dtypes)]
    outs = pl.pallas_call(
        functools.partial(_ssm_prep_kernel, t_small=t_small),
        grid=(g // gps,),
        in_specs=[gs(1, p), gs(1, p), gs(1, p), gs(c, p), gs(c, p), gs(c, p), gs(c, p), gs(1, c), gs(c, c)],
        out_specs=out_specs,
        out_shape=out_shape,
        compiler_params=_compiler_params("parallel"),
        name="ssm_prep",
    )(a_re, a_im, log_dt, bt_re, bt_im, c_re, c_im, d_skip, w_glu)
    return dict(zip(names, outs[:len(names)])), dict(zip(names, outs[len(names):]))


def _ssm_mix_kernel(*refs, bsz, t, aliased):
    (u_ref, m_ref, wre_ref, wim_ref, ztre_ref, ztim_ref, are_ref, aim_ref, d_ref, glu_ref, h0re_ref, h0im_ref) = refs[:12]
    y_ref, hre_ref, him_ref, xs_ref, ug_ref, vre_ref, vim_ref, pre_ref, pim_ref = refs[12 + aliased:]
    rows = u_ref.shape[0] // t
    chunks = rows // bsz
    n_grp, tc = m_ref.shape[0], m_ref.shape[1]
    c = tc // t

    def token_rows(b, s):
        return pl.ds(b * chunks * t + s, chunks, stride=t)

    def chunk_rows(b):
        return pl.ds(b, chunks, stride=bsz)

    for s in range(t):
        if chunks == 1:
            xs_ref[s] = u_ref[pl.ds(s, rows, stride=t), :]
        else:
            for b in range(bsz):
                xs_ref[s, chunk_rows(b), :] = u_ref[token_rows(b, s), :]
    for gi in range(n_grp):
        lanes = slice(gi * c, (gi + 1) * c)
        u = jnp.concatenate([xs_ref[s, :, lanes] for s in range(t)], axis=1)
        ug_ref[gi] = u
        vre_ref[gi] = _dot(u.astype(BF16), wre_ref[gi])
        vim_ref[gi] = _dot(u.astype(BF16), wim_ref[gi])

    per_tile = max(1, 8 // bsz)
    tile_rows = per_tile * bsz

    def chunk_tile(i, h):
        sel = pl.ds(pl.multiple_of(i * tile_rows, tile_rows), tile_rows)
        nxt = []
        for gi in range(n_grp):
            h_re, h_im = h[2 * gi], h[2 * gi + 1]
            v_re, v_im = vre_ref[gi, sel, :], vim_ref[gi, sel, :]
            a_re, a_im = are_ref[gi], aim_ref[gi]
            ent_re, ent_im = [], []
            for j in range(per_tile):
                ent_re.append(h_re)
                ent_im.append(h_im)
                part = slice(j * bsz, (j + 1) * bsz)
                h_re, h_im = a_re * h_re - a_im * h_im + v_re[part], a_re * h_im + a_im * h_re + v_im[part]
            pre_ref[gi, sel, :] = jnp.concatenate(ent_re, axis=0)
            pim_ref[gi, sel, :] = jnp.concatenate(ent_im, axis=0)
            nxt += [h_re, h_im]
        return tuple(nxt)

    h = tuple(r[gi] for gi in range(n_grp) for r in (h0re_ref, h0im_ref))
    n_tiles = chunks // per_tile
    h = lax.fori_loop(0, n_tiles, chunk_tile, h, unroll=min(n_tiles, 4))
    for gi in range(n_grp):
        hre_ref[gi] = h[2 * gi]
        him_ref[gi] = h[2 * gi + 1]

    for gi in range(n_grp):
        u = ug_ref[gi]
        y = (_dot(u.astype(BF16), m_ref[gi]) + _dot_nt(pre_ref[gi].astype(BF16), ztre_ref[gi])
             + _dot_nt(pim_ref[gi].astype(BF16), ztim_ref[gi]) + d_ref[gi] * u)
        y = jax.nn.gelu(y)
        y = y * jax.nn.sigmoid(_dot(y.astype(BF16), glu_ref[gi]))
        for s in range(t):
            xs_ref[s, :, gi * c:(gi + 1) * c] = y[:, s * c:(s + 1) * c]
    for s in range(t):
        if chunks == 1:
            y_ref[pl.ds(s, rows, stride=t), :] = xs_ref[s]
        else:
            for b in range(bsz):
                y_ref[token_rows(b, s), :] = xs_ref[s, chunk_rows(b), :]


def _ssm_mix(z, y_prev, tab, layer, h0_re, h0_im, row0, n_rows, bsz, t):
    g, _, p = h0_re.shape
    tc = tab["m"].shape[1]
    n_tok = z.shape[0]
    c = tc // t
    width = g * c
    lanes = LANES
    gps = lanes // c
    rows = n_rows // t
    first = layer * (g // gps)
    blk = lambda *shape: pl.BlockSpec((gps,) + shape, lambda i: (i,) + (0,) * len(shape))
    tbl = lambda *shape: pl.BlockSpec((gps,) + shape, lambda i: (first + i,) + (0,) * len(shape))
    tok = pl.BlockSpec((n_rows, lanes), lambda i: (row0 // n_rows, i))
    aliased = y_prev is not None
    in_specs = [tok, tbl(tc, tc), tbl(tc, p), tbl(tc, p), tbl(tc, p), tbl(tc, p), tbl(1, p), tbl(1, p), tbl(1, tc),
                tbl(tc, tc), blk(bsz, p), blk(bsz, p)]
    args = [z, tab["m"], tab["w_re"], tab["w_im"], tab["zt_re"], tab["zt_im"], tab["a_re"], tab["a_im"], tab["d"],
            tab["glu"], h0_re, h0_im]
    if aliased:
        in_specs.append(pl.BlockSpec(memory_space=pl.ANY))
        args.append(y_prev)
    return pl.pallas_call(
        functools.partial(_ssm_mix_kernel, bsz=bsz, t=t, aliased=aliased),
        grid=(g // gps,),
        in_specs=in_specs,
        out_specs=[tok, blk(bsz, p), blk(bsz, p)],
        out_shape=[jax.ShapeDtypeStruct((n_tok, width), F32), jax.ShapeDtypeStruct((g, bsz, p), F32),
                   jax.ShapeDtypeStruct((g, bsz, p), F32)],
        scratch_shapes=[pltpu.VMEM((t, rows, lanes), F32), pltpu.VMEM((gps, rows, tc), F32)]
        + [pltpu.VMEM((gps, rows, p), F32)] * 4,
        input_output_aliases={len(args) - 1: 0} if aliased else {},
        compiler_params=_compiler_params("parallel"),
        name="ssm_mix",
    )(*args)


def _bias_table(q_pos, k_pos, n_heads):
    slopes = np.exp2(-8.0 * np.arange(1, n_heads + 1, dtype=np.float64) / n_heads)
    dist = q_pos[:, None] - k_pos[None, :]
    valid = (dist >= 0) & (dist < WINDOW) & (k_pos[None, :] >= 0)
    bias = -slopes[:, None, None] * dist[None].astype(np.float64)
    return np.where(valid[None], bias, -np.inf).astype(np.float32)


def _head_rms(x, gain):
    heads = [_rms(x[:, i:i + HEAD_DIM], gain) for i in range(0, x.shape[1], HEAD_DIM)]
    return jnp.concatenate(heads, axis=1)


def _softmax_sink_pv(logits, sink, v):
    m = jnp.maximum(jnp.max(logits, axis=-1, keepdims=True), sink)
    p = jnp.exp(logits - m)
    denom = jnp.sum(p, axis=-1, keepdims=True) + jnp.exp(sink - m)
    return _dot(p.astype(BF16), v) / denom


def _prompt_attn_kernel(sinks_ref, q_ref, k_ref, v_ref, qg_ref, kg_ref, bias_ref, y_ref, ks_ref, vs_ref,
                        kp_ref, vp_ref, *, n_heads):
    n = pl.program_id(1)

    @pl.when(n == 0)
    def _():
        kp_ref[...] = jnp.zeros_like(kp_ref)
        vp_ref[...] = jnp.zeros_like(vp_ref)

    blk = WINDOW
    half = 2 * HEAD_DIM
    assert n_heads // N_KV_HEADS == 4 and half == 128
    k_all = _head_rms(k_ref[...], kg_ref[...])
    v_all = v_ref[...]
    from_prev = (lax.broadcasted_iota(jnp.int32, (blk, blk), 0) > lax.broadcasted_iota(jnp.int32, (blk, blk), 1))
    low_lanes = lax.broadcasted_iota(jnp.int32, (2 * blk, half), 1) < HEAD_DIM
    low8 = lax.broadcasted_iota(jnp.int32, (8, half), 1) < HEAD_DIM
    pick = jnp.concatenate([jnp.where(low8, 1.0, 0.0), jnp.where(low8, 0.0, 1.0)], axis=0)
    k_prev, v_prev = kp_ref[...], vp_ref[...]
    for sub in range(q_ref.shape[0] // blk):
        rows = slice(sub * blk, (sub + 1) * blk)
        k_cur, v_cur = k_all[rows], v_all[rows]
        table = jnp.minimum(n, 1) if sub == 0 else 1
        _attend_block(q_ref[rows, :], k_prev, v_prev, k_cur, v_cur, qg_ref[...], table, sinks_ref, bias_ref,
                      y_ref.at[rows, :], from_prev, low_lanes, pick)
        k_prev, v_prev = k_cur, v_cur
    kp_ref[...] = k_prev
    vp_ref[...] = v_prev

    @pl.when(n == pl.num_programs(1) - 1)
    def _():
        ks_ref[0] = k_prev
        vs_ref[0] = v_prev


def _attend_block(q_all, k_prev, v_prev, k_cur, v_cur, q_gain, table, sinks_ref, bias_ref, y_ref,
                  from_prev, low_lanes, pick):
    blk = q_all.shape[0]
    half = 2 * HEAD_DIM
    k2 = jnp.concatenate([k_prev, k_cur], axis=0)
    v2t = jnp.concatenate([v_prev, v_cur], axis=0).T.astype(BF16)
    qs = (q_all * q_gain * QK_SCALE).astype(BF16)
    qsq = q_all * q_all
    for kv in range(N_KV_HEADS):
        k_tile = k2[:, (kv // 2) * half:(kv // 2 + 1) * half]
        k_own = jnp.where(low_lanes if kv % 2 == 0 else ~low_lanes, k_tile, 0.0)
        k_swap = pltpu.roll(k_own, HEAD_DIM, axis=1)
        k_lhs = jnp.concatenate([k_own, k_swap] if kv % 2 == 0 else [k_swap, k_own], axis=0).astype(BF16)
        tiles = slice(2 * kv * half, (2 * kv + 1) * half), slice((2 * kv + 1) * half, (2 * kv + 2) * half)
        q_rows = jnp.concatenate([qs[:, t] for t in tiles], axis=0)
        qsq_rows = jnp.concatenate([qsq[:, t] for t in tiles], axis=0)
        s = _dot_nt(k_lhs, q_rows)
        ssq = _dot_nt(pick, qsq_rows, HIGHEST)
        probs, inv_denoms = [], []
        for i in range(4):
            odd, tile = i % 2, i // 2
            lanes = slice(tile * blk, (tile + 1) * blk)
            s_h = s[odd * 2 * blk:(odd + 1) * 2 * blk, lanes]
            q_rms = lax.rsqrt(ssq[odd * 8:odd * 8 + 1, lanes] * (1.0 / HEAD_DIM) + EPS)
            logits = jnp.where(from_prev, s_h[:blk], s_h[blk:]) * q_rms + bias_ref[table, 4 * kv + i]
            sink = sinks_ref[4 * kv + i]
            m = jnp.maximum(jnp.max(logits, axis=0, keepdims=True), sink)
            p = jnp.exp(logits - m)
            inv_denoms.append(1.0 / (jnp.sum(p, axis=0, keepdims=True) + jnp.exp(sink - m)))
            probs.append(jnp.concatenate([jnp.where(from_prev, p, 0.0), jnp.where(from_prev, 0.0, p)], axis=0))
        o = _dot(v2t[kv * HEAD_DIM:(kv + 1) * HEAD_DIM], jnp.concatenate(probs, axis=1).astype(BF16))
        for tile in range(2):
            pair = [o[:, i * blk:(i + 1) * blk] * inv_denoms[i] for i in (2 * tile, 2 * tile + 1)]
            y_ref[:, tiles[tile]] = jnp.concatenate(pair, axis=0).T


def _prompt_attn(z, sinks, q_gain, k_gain, bsz, seq, n_tok, col_q, attn_w, kv_w):
    blk = WINDOW
    rows = PROMPT_BLOCKS_PER_STEP * blk
    nb = seq // rows
    assert seq % rows == 0
    n_heads = attn_w // HEAD_DIM
    row = lambda b, n: b * nb + n
    q_blk, k_blk, v_blk = col_q // attn_w, (col_q + attn_w) // kv_w, (col_q + attn_w + kv_w) // kv_w
    qi, kj = np.arange(blk)[:, None], np.arange(blk)[None, :]
    first, later = [np.take_along_axis(_bias_table(first_pos + np.arange(blk), first_pos - blk + np.arange(2 * blk), n_heads),
                                       np.broadcast_to(np.where(kj > qi, kj, kj + blk), (n_heads, blk, blk)), axis=2)
                    for first_pos in (0, blk)]
    bias = np.stack([first, later]).swapaxes(2, 3)
    return pl.pallas_call(
        functools.partial(_prompt_attn_kernel, n_heads=n_heads),
        grid=(bsz, nb),
        in_specs=[
            pl.BlockSpec(memory_space=pltpu.SMEM),
            pl.BlockSpec((rows, attn_w), lambda b, n: (row(b, n), q_blk)),
            pl.BlockSpec((rows, kv_w), lambda b, n: (row(b, n), k_blk)),
            pl.BlockSpec((rows, kv_w), lambda b, n: (row(b, n), v_blk)),
            pl.BlockSpec((1, attn_w), lambda b, n: (0, 0)),
            pl.BlockSpec((1, HEAD_DIM), lambda b, n: (0, 0)),
            pl.BlockSpec(bias.shape, lambda b, n: (0, 0, 0, 0)),
        ],
        out_specs=[
            pl.BlockSpec((rows, attn_w), lambda b, n: (row(b, n), 0)),
            pl.BlockSpec((1, blk, kv_w), lambda b, n: (b, 0, 0)),
            pl.BlockSpec((1, blk, kv_w), lambda b, n: (b, 0, 0)),
        ],
        out_shape=[jax.ShapeDtypeStruct((n_tok, attn_w), F32),
                   jax.ShapeDtypeStruct((bsz, blk, kv_w), F32),
                   jax.ShapeDtypeStruct((bsz, blk, kv_w), F32)],
        scratch_shapes=[pltpu.VMEM((blk, kv_w), F32)] * 2,
        compiler_params=_compiler_params("parallel", "arbitrary"),
        name="prompt_attn",
    )(sinks, z, z, z, jnp.tile(q_gain, n_heads).reshape(1, attn_w), k_gain.reshape(1, HEAD_DIM), jnp.asarray(bias))


def _sample_attn_kernel(*refs, dec_seq, aliased):
    q_ref, k_ref, v_ref, ck_ref, cv_ref, qg_ref, kg_ref, bias_ref, sink_ref = refs[:9]
    y_ref, ks_ref, vs_ref, kbuf_ref, vbuf_ref, p_ref = refs[9 + 2 * aliased:]
    tile, q_rows, _ = q_ref.shape
    cache_w = ck_ref.shape[1]
    keys = kbuf_ref.shape[0] // tile
    k_new = _head_rms(k_ref[...], kg_ref[...])
    v_new = v_ref[...]
    pad = jnp.zeros((keys - cache_w, kbuf_ref.shape[1]), F32)
    for b in range(tile):
        rows = slice(b * dec_seq, (b + 1) * dec_seq)
        kbuf_ref[b * keys:b * keys + cache_w, :] = ck_ref[b]
        vbuf_ref[b * keys:b * keys + cache_w, :] = cv_ref[b]
        kbuf_ref[b * keys + cache_w:(b + 1) * keys, :] = pad
        vbuf_ref[b * keys + cache_w:(b + 1) * keys, :] = pad
        kbuf_ref[b * keys + cache_w:b * keys + cache_w + dec_seq, :] = k_new[rows]
        vbuf_ref[b * keys + cache_w:b * keys + cache_w + dec_seq, :] = v_new[rows]
    q = q_ref[...].reshape(tile * q_rows, q_ref.shape[2])
    q = q * lax.rsqrt(jnp.sum(q * q, axis=-1, keepdims=True) * (1.0 / HEAD_DIM) + EPS) * qg_ref[...]
    s = _dot_nt((q * QK_SCALE).astype(BF16), kbuf_ref[...].astype(BF16))
    p_ref[...] = jnp.zeros_like(p_ref)
    inv_denoms = []
    for b in range(tile):
        own_rows, own_keys = slice(b * q_rows, (b + 1) * q_rows), slice(b * keys, (b + 1) * keys)
        logits = s[own_rows, own_keys] + bias_ref[...]
        sink = sink_ref[...]
        m = jnp.maximum(jnp.max(logits, axis=-1, keepdims=True), sink)
        p = jnp.exp(logits - m)
        inv_denoms.append(1.0 / (jnp.sum(p, axis=-1, keepdims=True) + jnp.exp(sink - m)))
        p_ref[own_rows, own_keys] = p.astype(BF16)
    o = _dot(p_ref[...], vbuf_ref[...].astype(BF16))
    for b in range(tile):
        rows = slice(b * dec_seq, (b + 1) * dec_seq)
        y_ref[b] = o[b * q_rows:(b + 1) * q_rows] * inv_denoms[b]
        ks_ref[b, 0:cache_w - dec_seq, :] = ck_ref[b, dec_seq:cache_w, :]
        vs_ref[b, 0:cache_w - dec_seq, :] = cv_ref[b, dec_seq:cache_w, :]
        ks_ref[b, cache_w - dec_seq:cache_w, :] = k_new[rows]
        vs_ref[b, cache_w - dec_seq:cache_w, :] = v_new[rows]


def _sample_attn(z, cache_k, cache_v, windows, layer, sinks, q_gain, k_gain, row0, dec_seq, col_q, attn_w, kv_w):
    _, dec_batch, cache_w, _ = cache_k.shape
    n_heads = attn_w // HEAD_DIM
    q_per_kv = n_heads // N_KV_HEADS
    tile = SAMPLE_BATCH_TILE
    r0 = row0 // (tile * dec_seq)
    k_blk, v_blk = (col_q + attn_w) // kv_w, (col_q + attn_w + kv_w) // kv_w
    keys = 2 * cache_w
    k_pos = np.full(keys, -1)
    k_pos[:cache_w] = PAST_LEN - cache_w + np.arange(cache_w)
    k_pos[cache_w:cache_w + dec_seq] = PAST_LEN + np.arange(dec_seq)
    bias = _bias_table(PAST_LEN + np.arange(dec_seq), k_pos, n_heads).reshape(n_heads * dec_seq, keys)
    sink_col = jnp.repeat(sinks, dec_seq).reshape(n_heads * dec_seq, 1)
    q = z[row0:, col_q:col_q + attn_w].reshape(dec_batch, dec_seq, N_KV_HEADS, q_per_kv, 1, HEAD_DIM)
    own_kv = np.eye(N_KV_HEADS, dtype=np.float32)[None, :, None, None, :, None]
    q = (q.transpose(0, 2, 3, 1, 4, 5) * own_kv).reshape(dec_batch, n_heads * dec_seq, kv_w)
    window = pl.BlockSpec((None, tile, cache_w, kv_w), lambda i: (layer, i, 0, 0))
    aliased = windows is not None
    in_specs = [
        pl.BlockSpec((tile, n_heads * dec_seq, kv_w), lambda i: (i, 0, 0)),
        pl.BlockSpec((tile * dec_seq, kv_w), lambda i: (r0 + i, k_blk)),
        pl.BlockSpec((tile * dec_seq, kv_w), lambda i: (r0 + i, v_blk)),
        window,
        window,
        pl.BlockSpec((1, kv_w), lambda i: (0, 0)),
        pl.BlockSpec((1, HEAD_DIM), lambda i: (0, 0)),
        pl.BlockSpec(bias.shape, lambda i: (0, 0)),
        pl.BlockSpec(sink_col.shape, lambda i: (0, 0)),
    ]
    args = [q, z, z, cache_k, cache_v, jnp.tile(q_gain, N_KV_HEADS).reshape(1, kv_w), k_gain.reshape(1, HEAD_DIM),
            jnp.asarray(bias), sink_col]
    if aliased:
        in_specs += [pl.BlockSpec(memory_space=pl.ANY)] * 2
        args += list(windows)
    y, ks, vs = pl.pallas_call(
        functools.partial(_sample_attn_kernel, dec_seq=dec_seq, aliased=aliased),
        grid=(dec_batch // tile,),
        in_specs=in_specs,
        out_specs=[pl.BlockSpec((tile, n_heads * dec_seq, kv_w), lambda i: (i, 0, 0)), window, window],
        out_shape=[jax.ShapeDtypeStruct(q.shape, F32),
                   jax.ShapeDtypeStruct(cache_k.shape, F32),
                   jax.ShapeDtypeStruct(cache_v.shape, F32)],
        scratch_shapes=[pltpu.VMEM((tile * keys, kv_w), F32)] * 2
        + [pltpu.VMEM((tile * n_heads * dec_seq, tile * keys), BF16)],
        input_output_aliases={len(args) - 2: 1, len(args) - 1: 2} if aliased else {},
        compiler_params=_compiler_params("parallel"),
        name="sample_attn",
    )(*args)
    y = y.reshape(dec_batch, N_KV_HEADS, q_per_kv, dec_seq, N_KV_HEADS, HEAD_DIM)
    y = jnp.stack([y[:, kv, :, :, kv] for kv in range(N_KV_HEADS)], axis=1)
    return y.transpose(0, 3, 1, 2, 4).reshape(dec_batch * dec_seq, attn_w), ks, vs


def kernel(x_prompt, x_sample, cache_k, cache_v, state_ssm_re, state_ssm_im, ffn1_norm, ffn1_w_gate, ffn1_w_up, ffn1_w_down, mix_norm, w_in, ssm_A_re, ssm_A_im, ssm_B_re, ssm_B_im, ssm_C_re, ssm_C_im, ssm_D, ssm_log_dt, ssm_w_glu, q_norm, k_norm, sinks, ssm_out_norm, attn_out_norm, w_out, ffn2_norm, ffn2_w_gate, ffn2_w_up, ffn2_w_down):
    bsz, seq, d = x_prompt.shape
    dec_batch, dec_seq, _ = x_sample.shape
    depth = w_in.shape[0]
    ssm_w = ssm_A_re.shape[1] * SSM_GROUP_CH
    attn_w = w_out.shape[1] - ssm_w
    kv_w = N_KV_HEADS * HEAD_DIM
    cache_w = cache_k.shape[2]
    n_prompt = bsz * seq
    n_tok = n_prompt + dec_batch * dec_seq

    xs = [x_prompt.reshape(n_prompt, d), x_sample.reshape(dec_batch * dec_seq, d)]
    n_groups = ssm_w // SSM_GROUP_CH
    n_merged = n_groups // SSM_MERGE
    zeros_state = jnp.zeros((n_merged, bsz, SSM_MERGE * SSM_STATE), F32)
    outs = [[] for _ in range(6)]
    cache_k_flat = cache_k.reshape(depth, dec_batch, cache_w, kv_w)
    cache_v_flat = cache_v.reshape(depth, dec_batch, cache_w, kv_w)
    windows = None
    ffn1_w, ffn2_w = (ffn1_w_gate, ffn1_w_up, ffn1_w_down), (ffn2_w_gate, ffn2_w_up, ffn2_w_down)
    w_in_bf, w_out_bf = w_in.astype(BF16), w_out.astype(BF16)
    all_layers = lambda a: a.reshape((-1,) + a.shape[2:])
    tab_p, tab_s = _ssm_prep(*_merge_groups(*map(all_layers, (ssm_A_re, ssm_A_im, ssm_log_dt, ssm_B_re, ssm_B_im,
                                                               ssm_C_re, ssm_C_im, ssm_D, ssm_w_glu)), SSM_MERGE),
                             t_small=dec_seq)
    for l in range(depth):
        x = _ffn(xs if l == 0 else [x], ffn1_norm[l], *ffn1_w, layer=l)
        z = _norm_matmul(x, mix_norm[l], w_in_bf, layer=l, tn=w_in.shape[2])

        grp_major = lambda a: a.reshape(a.shape[0], n_merged, SSM_MERGE * SSM_STATE).transpose(1, 0, 2)
        seq_major = lambda a: a.transpose(1, 0, 2).reshape(a.shape[1], n_groups, SSM_STATE)
        y_ssm, hp_re, hp_im = _ssm_mix(z, None, tab_p, l, zeros_state, zeros_state, 0, n_prompt, bsz, PROMPT_CHUNK)
        y_ssm, hs_re, hs_im = _ssm_mix(z, y_ssm, tab_s, l, grp_major(state_ssm_re[l]), grp_major(state_ssm_im[l]),
                                       n_prompt, dec_batch * dec_seq, dec_batch, dec_seq)
        hp_re, hp_im, hs_re, hs_im = map(seq_major, (hp_re, hp_im, hs_re, hs_im))

        y_attn, pk, pv = _prompt_attn(z, sinks[l], q_norm[l], k_norm[l], bsz, seq, n_tok, ssm_w, attn_w, kv_w)
        ya_s, *windows = _sample_attn(z, cache_k_flat, cache_v_flat, windows, l, sinks[l], q_norm[l], k_norm[l],
                                      n_prompt, dec_seq, ssm_w, attn_w, kv_w)
        y_attn = lax.dynamic_update_slice(y_attn, ya_s, (n_prompt, 0))

        x = _out_proj(x, y_ssm, y_attn, ssm_out_norm[l], attn_out_norm[l], w_out_bf, layer=l)
        if l < depth - 1:
            x = _ffn([x], ffn2_norm[l], *ffn2_w, layer=l)

        kv_shape = (-1, N_KV_HEADS, HEAD_DIM)
        for dst, val in zip(outs, (pk.reshape(bsz, *kv_shape), pv.reshape(bsz, *kv_shape), hp_re, hp_im, hs_re, hs_im)):
            dst.append(val)
    y_prompt, y_sample = _ffn([x], ffn2_norm[depth - 1], *ffn2_w, layer=depth - 1, out_rows=(n_prompt, n_tok - n_prompt))
    y_prompt, y_sample = y_prompt.reshape(bsz, seq, d), y_sample.reshape(dec_batch, dec_seq, d)
    pk, pv, hp_re, hp_im, hs_re, hs_im = (jnp.stack(o) for o in outs)
    sk, sv = (w.reshape(cache_k.shape) for w in windows)
    return (y_prompt, y_sample, pk, pv, hp_re, hp_im, sk, sv, hs_re, hs_im)
```

```python
import functools

import jax
import jax.numpy as jnp
import numpy as np
from jax import lax
from jax.experimental import pallas as pl
from jax.experimental.pallas import tpu as pltpu

F32 = jnp.float32
BF16 = jnp.bfloat16

EPS = 1e-6
SSM_GROUP_CH = 16
SSM_STATE = 64
HEAD_DIM = 64
N_KV_HEADS = 4
WINDOW = 128
PAST_LEN = 8192
QK_SCALE = HEAD_DIM ** -0.5
PROMPT_CHUNK = 8

TOKEN_TILE = 512
FFN_TOKEN_TILE = 1088
FFN_VMEM_LIMIT_BYTES = 60 * 1024 * 1024
FF_TILE = 256
SAMPLE_BATCH_TILE = 8
PROMPT_BLOCKS_PER_STEP = 4
VMEM_LIMIT_BYTES = 48 * 1024 * 1024

HIGHEST = lax.Precision.HIGHEST
LANES = 128
SSM_MERGE = 4


def _rms(x, gain):
    return x * lax.rsqrt(jnp.mean(x * x, axis=-1, keepdims=True) + EPS) * gain


def _dot(a, b, precision=None):
    return jnp.dot(a, b, preferred_element_type=F32, precision=precision)


def _dot_nt(a, b, precision=None):
    return lax.dot_general(a, b, (((1,), (1,)), ((), ())), preferred_element_type=F32, precision=precision)


def _compiler_params(*semantics):
    return pltpu.CompilerParams(dimension_semantics=semantics, vmem_limit_bytes=VMEM_LIMIT_BYTES)


def _ffn_kernel(*refs, n_in, n_out, split_row):
    x_refs, (g_ref, wg_ref, wu_ref, wd_ref) = refs[:n_in], refs[n_in:n_in + 4]
    o_refs, xn_ref = refs[n_in + 4:n_in + 4 + n_out], refs[-1]
    acc_ref = o_refs[0]
    i, j = pl.program_id(0), pl.program_id(1)
    boundary = i == pl.num_programs(0) - 1

    def with_x_tile(fn):
        if n_in == 1:
            fn(x_refs[0][...])
        else:
            pl.when(jnp.logical_not(boundary))(lambda: fn(x_refs[0][...]))
            pl.when(boundary)(lambda: fn(jnp.concatenate([x_refs[0][0:split_row, :], x_refs[1][...]], axis=0)))

    @pl.when(j == 0)
    def _():
        acc_ref[...] = jnp.zeros_like(acc_ref)

        def normalise(x):
            xn_ref[...] = _rms(x, g_ref[...]).astype(BF16)
        with_x_tile(normalise)

    xn = xn_ref[...]
    gate = _dot(xn, wg_ref[...].astype(BF16))
    up = _dot(xn, wu_ref[...].astype(BF16))
    h = (gate * jax.nn.sigmoid(gate) * up).astype(BF16)
    acc_ref[...] += _dot(h, wd_ref[...].astype(BF16))

    @pl.when(j == pl.num_programs(1) - 1)
    def _():
        def residual(x):
            acc_ref[...] = x + 0.5 * acc_ref[...]
        with_x_tile(residual)
        if n_out == 2:
            @pl.when(boundary)
            def _():
                o_refs[1][...] = acc_ref[split_row:, :]


def _ffn(xs, gain, w_gate, w_up, w_down, layer, out_rows=None):
    d = xs[0].shape[1]
    n_tok = sum(x.shape[0] for x in xs)
    d_ff = w_gate.shape[2]
    if len(xs) == 2 or out_rows is not None:
        tm = FFN_TOKEN_TILE if n_tok % FFN_TOKEN_TILE == 0 else n_tok
        n0, n1 = (x.shape[0] for x in xs) if len(xs) == 2 else out_rows
        split_row = n0 - (n_tok // tm - 1) * tm
        assert split_row + n1 == tm and split_row % 8 == 0
    else:
        tm = FFN_TOKEN_TILE if n_tok % FFN_TOKEN_TILE == 0 else TOKEN_TILE
        split_row = None
    tf = FF_TILE
    tile_spec = pl.BlockSpec((tm, d), lambda i, j: (i, 0))
    tail_spec = lambda rows: pl.BlockSpec((rows, d), lambda i, j: (0, 0))
    x_specs = [tile_spec] + ([tail_spec(xs[1].shape[0])] if len(xs) == 2 else [])
    if out_rows is None:
        out_specs, out_shape = [tile_spec], [jax.ShapeDtypeStruct((n_tok, d), F32)]
    else:
        out_specs = [tile_spec, tail_spec(out_rows[1])]
        out_shape = [jax.ShapeDtypeStruct((rows, d), F32) for rows in out_rows]
    outs = pl.pallas_call(
        functools.partial(_ffn_kernel, n_in=len(xs), n_out=len(out_specs), split_row=split_row),
        grid=(n_tok // tm, d_ff // tf),
        in_specs=x_specs + [
            pl.BlockSpec((1, d), lambda i, j: (0, 0)),
            pl.BlockSpec((None, d, tf), lambda i, j: (layer, 0, j)),
            pl.BlockSpec((None, d, tf), lambda i, j: (layer, 0, j)),
            pl.BlockSpec((None, tf, d), lambda i, j: (layer, j, 0)),
        ],
        out_specs=out_specs,
        out_shape=out_shape,
        scratch_shapes=[pltpu.VMEM((tm, d), BF16)],
        compiler_params=pltpu.CompilerParams(dimension_semantics=("arbitrary", "arbitrary"),
                                             vmem_limit_bytes=FFN_VMEM_LIMIT_BYTES),
        name="ffn",
    )(*xs, gain.reshape(1, d), w_gate, w_up, w_down)
    return outs[0] if out_rows is None else outs


def _norm_matmul_kernel(x_ref, g_ref, w_ref, o_ref, xn_ref):
    @pl.when(pl.program_id(1) == 0)
    def _():
        xn_ref[...] = _rms(x_ref[...], g_ref[...]).astype(BF16)

    o_ref[...] = _dot(xn_ref[...], w_ref[...])


def _norm_matmul(x, gain, w, layer, tn):
    n_tok, d = x.shape
    n_out = w.shape[2]
    tm = TOKEN_TILE
    return pl.pallas_call(
        _norm_matmul_kernel,
        grid=(n_tok // tm, n_out // tn),
        in_specs=[
            pl.BlockSpec((tm, d), lambda i, j: (i, 0)),
            pl.BlockSpec((1, d), lambda i, j: (0, 0)),
            pl.BlockSpec((None, d, tn), lambda i, j: (layer, 0, j)),
        ],
        out_specs=pl.BlockSpec((tm, tn), lambda i, j: (i, j)),
        out_shape=jax.ShapeDtypeStruct((n_tok, n_out), F32),
        scratch_shapes=[pltpu.VMEM((tm, d), BF16)],
        compiler_params=_compiler_params("parallel", "arbitrary"),
        name="in_proj",
    )(x, gain.reshape(1, d), w)


def _out_proj_kernel(x_ref, ys_ref, ya_ref, gs_ref, ga_ref, ws_ref, wa_ref, o_ref):
    ys = _rms(ys_ref[...], gs_ref[...]).astype(BF16)
    ya = _rms(ya_ref[...], ga_ref[...]).astype(BF16)
    o_ref[...] = x_ref[...] + _dot(ys, ws_ref[...]) + _dot(ya, wa_ref[...])


def _out_proj(x, y_ssm, y_attn, g_ssm, g_attn, w_out, layer):
    n_tok, d = x.shape
    ssm_w, attn_w = y_ssm.shape[1], y_attn.shape[1]
    tm = TOKEN_TILE
    return pl.pallas_call(
        _out_proj_kernel,
        grid=(n_tok // tm,),
        in_specs=[
            pl.BlockSpec((tm, d), lambda i: (i, 0)),
            pl.BlockSpec((tm, ssm_w), lambda i: (i, 0)),
            pl.BlockSpec((tm, attn_w), lambda i: (i, 0)),
            pl.BlockSpec((1, ssm_w), lambda i: (0, 0)),
            pl.BlockSpec((1, attn_w), lambda i: (0, 0)),
            pl.BlockSpec((None, ssm_w, d), lambda i: (layer, 0, 0)),
            pl.BlockSpec((None, attn_w, d), lambda i: (layer, ssm_w // attn_w, 0)),
        ],
        out_specs=pl.BlockSpec((tm, d), lambda i: (i, 0)),
        out_shape=jax.ShapeDtypeStruct((n_tok, d), F32),
        compiler_params=_compiler_params("parallel"),
        name="out_proj",
    )(x, y_ssm, y_attn, g_ssm.reshape(1, ssm_w), g_attn.reshape(1, attn_w), w_out, w_out)


def _ssm_prep_kernel(*refs, t_small):
    for gi in range(refs[0].shape[0]):
        _ssm_prep_group(*[r.at[pl.ds(gi, 1)] for r in refs], t_small=t_small)


def _ssm_prep_group(are_ref, aim_ref, ldt_ref, btre_ref, btim_ref, cre_ref, cim_ref, d_ref, wglu_ref,
                    *table_refs, t_small):
    per_set = len(table_refs) // 2
    full_refs, small_refs = table_refs[:per_set], table_refs[per_set:]
    a_re, a_im = are_ref[0], aim_ref[0]
    dt = jnp.exp(ldt_ref[0])
    mag = jnp.exp(a_re * dt)
    lb_re, lb_im = mag * jnp.cos(a_im * dt), mag * jnp.sin(a_im * dt)
    den = a_re * a_re + a_im * a_im
    q_re = ((lb_re - 1.0) * a_re + lb_im * a_im) / den
    q_im = (lb_im * a_re - (lb_re - 1.0) * a_im) / den
    bt_re, bt_im = btre_ref[0], btim_ref[0]
    bb_re = q_re * bt_re - q_im * bt_im
    bb_im = q_re * bt_im + q_im * bt_re
    c_re, c_im = cre_ref[0], cim_ref[0]

    pw_re, pw_im = [jnp.ones_like(lb_re)], [jnp.zeros_like(lb_im)]
    for _ in range(PROMPT_CHUNK):
        pr, pi = pw_re[-1], pw_im[-1]
        pw_re.append(pr * lb_re - pi * lb_im)
        pw_im.append(pr * lb_im + pi * lb_re)

    steps = PROMPT_CHUNK
    c = c_re.shape[0]
    w_re = jnp.concatenate([bb_re * pw_re[steps - 1 - s] - bb_im * pw_im[steps - 1 - s] for s in range(steps)], axis=0)
    w_im = jnp.concatenate([bb_re * pw_im[steps - 1 - s] + bb_im * pw_re[steps - 1 - s] for s in range(steps)], axis=0)
    zt_re = jnp.concatenate([c_re * pw_re[t + 1] - c_im * pw_im[t + 1] for t in range(steps)], axis=0)
    zt_im = jnp.concatenate([-(c_re * pw_im[t + 1] + c_im * pw_re[t + 1]) for t in range(steps)], axis=0)

    lag = _dot_nt(w_re, c_re, HIGHEST) - _dot_nt(w_im, c_im, HIGHEST)
    cols = []
    for t in range(steps):
        live = lag[(steps - 1 - t) * c:, :]
        cols.append(live if t == steps - 1 else
                    jnp.concatenate([live, jnp.zeros(((steps - 1 - t) * c, c), F32)], axis=0))
    m = jnp.concatenate(cols, axis=1)

    tiled = jnp.concatenate([jnp.concatenate([wglu_ref[0]] * steps, axis=0)] * steps, axis=1)
    shift = c.bit_length() - 1
    row_step = lax.shift_right_logical(lax.broadcasted_iota(jnp.int32, tiled.shape, 0), shift)
    col_step = lax.shift_right_logical(lax.broadcasted_iota(jnp.int32, tiled.shape, 1), shift)
    glu = jnp.where(row_step == col_step, tiled, 0.0)
    d_t = jnp.concatenate([d_ref[0]] * steps, axis=1)

    for t, refs in ((steps, full_refs), (t_small, small_refs)):
        m_ref, wre_ref, wim_ref, ztre_ref, ztim_ref, glu_ref, dt_ref, ore_ref, oim_ref = refs
        n, full = t * c, steps * c
        m_ref[0] = m[:n, :n].astype(BF16)
        wre_ref[0] = w_re[full - n:].astype(BF16)
        wim_ref[0] = w_im[full - n:].astype(BF16)
        ztre_ref[0] = zt_re[:n].astype(BF16)
        ztim_ref[0] = zt_im[:n].astype(BF16)
        glu_ref[0] = glu[:n, :n].astype(BF16)
        dt_ref[0] = d_t[:, :n]
        ore_ref[0] = pw_re[t]
        oim_ref[0] = pw_im[t]


def _merge_groups(a_re, a_im, log_dt, b_re, b_im, c_re, c_im, d_skip, w_glu, n):
    g, p = a_re.shape
    gm = g // n
    own = np.eye(n, dtype=np.float32)[None, :, :, None, None]

    def block_diag(x):
        r, q = x.shape[1:]
        return (x.reshape(gm, n, 1, r, q) * own).transpose(0, 1, 3, 2, 4).reshape(gm, n * r, n * q)

    lanes = lambda x: x.reshape(gm, 1, -1)
    return (lanes(a_re), lanes(a_im), lanes(jnp.repeat(log_dt, p)),
            block_diag(jnp.swapaxes(b_re, 1, 2)), block_diag(jnp.swapaxes(b_im, 1, 2)),
            block_diag(c_re), block_diag(c_im), lanes(d_skip), block_diag(w_glu))


def _ssm_prep(a_re, a_im, log_dt, bt_re, bt_im, c_re, c_im, d_skip, w_glu, t_small):
    g, c, p = c_re.shape
    assert c & (c - 1) == 0
    gps = LANES // c
    gs = lambda *shape: pl.BlockSpec((gps,) + shape, lambda i: (i,) + (0,) * len(shape))
    names = ("m", "w_re", "w_im", "zt_re", "zt_im", "glu", "d", "a_re", "a_im")
    out_specs, out_shape = [], []
    for t in (PROMPT_CHUNK, t_small):
        tc = t * c
        shapes = [(tc, tc), (tc, p), (tc, p), (tc, p), (tc, p), (tc, tc), (1, tc), (1, p), (1, p)]
        dtypes = [BF16] * 6 + [F32] * 3
        out_specs += [gs(*shape) for shape in shapes]
        out_shape += [jax.ShapeDtypeStruct((g,) + shape, dtype) for shape, dtype in zip(shapes, dtypes)]
    outs = pl.pallas_call(
        functools.partial(_ssm_prep_kernel, t_small=t_small),
        grid=(g // gps,),
        in_specs=[gs(1, p), gs(1, p), gs(1, p), gs(c, p), gs(c, p), gs(c, p), gs(c, p), gs(1, c), gs(c, c)],
        out_specs=out_specs,
        out_shape=out_shape,
        compiler_params=_compiler_params("parallel"),
        name="ssm_prep",
    )(a_re, a_im, log_dt, bt_re, bt_im, c_re, c_im, d_skip, w_glu)
    return dict(zip(names, outs[:len(names)])), dict(zip(names, outs[len(names):]))


def _ssm_mix_kernel(*refs, bsz, t, aliased):
    (u_ref, m_ref, wre_ref, wim_ref, ztre_ref, ztim_ref, are_ref, aim_ref, d_ref, glu_ref, h0re_ref, h0im_ref) = refs[:12]
    y_ref, hre_ref, him_ref, xs_ref, ug_ref, vre_ref, vim_ref, pre_ref, pim_ref = refs[12 + aliased:]
    rows = u_ref.shape[0] // t
    chunks = rows // bsz
    n_grp, tc = m_ref.shape[0], m_ref.shape[1]
    c = tc // t

    def token_rows(b, s):
        return pl.ds(b * chunks * t + s, chunks, stride=t)

    def chunk_rows(b):
        return pl.ds(b, chunks, stride=bsz)

    for s in range(t):
        if chunks == 1:
            xs_ref[s] = u_ref[pl.ds(s, rows, stride=t), :]
        else:
            for b in range(bsz):
                xs_ref[s, chunk_rows(b), :] = u_ref[token_rows(b, s), :]
    for gi in range(n_grp):
        lanes = slice(gi * c, (gi + 1) * c)
        u = jnp.concatenate([xs_ref[s, :, lanes] for s in range(t)], axis=1)
        ug_ref[gi] = u
        vre_ref[gi] = _dot(u.astype(BF16), wre_ref[gi])
        vim_ref[gi] = _dot(u.astype(BF16), wim_ref[gi])

    per_tile = max(1, 8 // bsz)
    tile_rows = per_tile * bsz

    def chunk_tile(i, h):
        sel = pl.ds(pl.multiple_of(i * tile_rows, tile_rows), tile_rows)
        nxt = []
        for gi in range(n_grp):
            h_re, h_im = h[2 * gi], h[2 * gi + 1]
            v_re, v_im = vre_ref[gi, sel, :], vim_ref[gi, sel, :]
            a_re, a_im = are_ref[gi], aim_ref[gi]
            ent_re, ent_im = [], []
            for j in range(per_tile):
                ent_re.append(h_re)
                ent_im.append(h_im)
                part = slice(j * bsz, (j + 1) * bsz)
                h_re, h_im = a_re * h_re - a_im * h_im + v_re[part], a_re * h_im + a_im * h_re + v_im[part]
            pre_ref[gi, sel, :] = jnp.concatenate(ent_re, axis=0)
            pim_ref[gi, sel, :] = jnp.concatenate(ent_im, axis=0)
            nxt += [h_re, h_im]
        return tuple(nxt)

    h = tuple(r[gi] for gi in range(n_grp) for r in (h0re_ref, h0im_ref))
    n_tiles = chunks // per_tile
    h = lax.fori_loop(0, n_tiles, chunk_tile, h, unroll=min(n_tiles, 4))
    for gi in range(n_grp):
        hre_ref[gi] = h[2 * gi]
        him_ref[gi] = h[2 * gi + 1]

    for gi in range(n_grp):
        u = ug_ref[gi]
        y = (_dot(u.astype(BF16), m_ref[gi]) + _dot_nt(pre_ref[gi].astype(BF16), ztre_ref[gi])
             + _dot_nt(pim_ref[gi].astype(BF16), ztim_ref[gi]) + d_ref[gi] * u)
        y = jax.nn.gelu(y)
        y = y * jax.nn.sigmoid(_dot(y.astype(BF16), glu_ref[gi]))
        for s in range(t):
            xs_ref[s, :, gi * c:(gi + 1) * c] = y[:, s * c:(s + 1) * c]
    for s in range(t):
        if chunks == 1:
            y_ref[pl.ds(s, rows, stride=t), :] = xs_ref[s]
        else:
            for b in range(bsz):
                y_ref[token_rows(b, s), :] = xs_ref[s, chunk_rows(b), :]


def _ssm_mix(z, y_prev, tab, layer, h0_re, h0_im, row0, n_rows, bsz, t):
    g, _, p = h0_re.shape
    tc = tab["m"].shape[1]
    n_tok = z.shape[0]
    c = tc // t
    width = g * c
    lanes = LANES
    gps = lanes // c
    rows = n_rows // t
    first = layer * (g // gps)
    blk = lambda *shape: pl.BlockSpec((gps,) + shape, lambda i: (i,) + (0,) * len(shape))
    tbl = lambda *shape: pl.BlockSpec((gps,) + shape, lambda i: (first + i,) + (0,) * len(shape))
    tok = pl.BlockSpec((n_rows, lanes), lambda i: (row0 // n_rows, i))
    aliased = y_prev is not None
    in_specs = [tok, tbl(tc, tc), tbl(tc, p), tbl(tc, p), tbl(tc, p), tbl(tc, p), tbl(1, p), tbl(1, p), tbl(1, tc),
                tbl(tc, tc), blk(bsz, p), blk(bsz, p)]
    args = [z, tab["m"], tab["w_re"], tab["w_im"], tab["zt_re"], tab["zt_im"], tab["a_re"], tab["a_im"], tab["d"],
            tab["glu"], h0_re, h0_im]
    if aliased:
        in_specs.append(pl.BlockSpec(memory_space=pl.ANY))
        args.append(y_prev)
    return pl.pallas_call(
        functools.partial(_ssm_mix_kernel, bsz=bsz, t=t, aliased=aliased),
        grid=(g // gps,),
        in_specs=in_specs,
        out_specs=[tok, blk(bsz, p), blk(bsz, p)],
        out_shape=[jax.ShapeDtypeStruct((n_tok, width), F32), jax.ShapeDtypeStruct((g, bsz, p), F32),
                   jax.ShapeDtypeStruct((g, bsz, p), F32)],
        scratch_shapes=[pltpu.VMEM((t, rows, lanes), F32), pltpu.VMEM((gps, rows, tc), F32)]
        + [pltpu.VMEM((gps, rows, p), F32)] * 4,
        input_output_aliases={len(args) - 1: 0} if aliased else {},
        compiler_params=_compiler_params("parallel"),
        name="ssm_mix",
    )(*args)


def _bias_table(q_pos, k_pos, n_heads):
    slopes = np.exp2(-8.0 * np.arange(1, n_heads + 1, dtype=np.float64) / n_heads)
    dist = q_pos[:, None] - k_pos[None, :]
    valid = (dist >= 0) & (dist < WINDOW) & (k_pos[None, :] >= 0)
    bias = -slopes[:, None, None] * dist[None].astype(np.float64)
    return np.where(valid[None], bias, -np.inf).astype(np.float32)


def _head_rms(x, gain):
    heads = [_rms(x[:, i:i + HEAD_DIM], gain) for i in range(0, x.shape[1], HEAD_DIM)]
    return jnp.concatenate(heads, axis=1)


def _softmax_sink_pv(logits, sink, v):
    m = jnp.maximum(jnp.max(logits, axis=-1, keepdims=True), sink)
    p = jnp.exp(logits - m)
    denom = jnp.sum(p, axis=-1, keepdims=True) + jnp.exp(sink - m)
    return _dot(p.astype(BF16), v) / denom


def _prompt_attn_kernel(sinks_ref, q_ref, k_ref, v_ref, qg_ref, kg_ref, bias_ref, y_ref, ks_ref, vs_ref,
                        kp_ref, vp_ref, *, n_heads):
    n = pl.program_id(1)

    @pl.when(n == 0)
    def _():
        kp_ref[...] = jnp.zeros_like(kp_ref)
        vp_ref[...] = jnp.zeros_like(vp_ref)

    blk = WINDOW
    half = 2 * HEAD_DIM
    assert n_heads // N_KV_HEADS == 4 and half == 128
    k_all = _head_rms(k_ref[...], kg_ref[...])
    v_all = v_ref[...]
    from_prev = (lax.broadcasted_iota(jnp.int32, (blk, blk), 0) > lax.broadcasted_iota(jnp.int32, (blk, blk), 1))
    low_lanes = lax.broadcasted_iota(jnp.int32, (2 * blk, half), 1) < HEAD_DIM
    low8 = lax.broadcasted_iota(jnp.int32, (8, half), 1) < HEAD_DIM
    pick = jnp.concatenate([jnp.where(low8, 1.0, 0.0), jnp.where(low8, 0.0, 1.0)], axis=0)
    k_prev, v_prev = kp_ref[...], vp_ref[...]
    for sub in range(q_ref.shape[0] // blk):
        rows = slice(sub * blk, (sub + 1) * blk)
        k_cur, v_cur = k_all[rows], v_all[rows]
        table = jnp.minimum(n, 1) if sub == 0 else 1
        _attend_block(q_ref[rows, :], k_prev, v_prev, k_cur, v_cur, qg_ref[...], table, sinks_ref, bias_ref,
                      y_ref.at[rows, :], from_prev, low_lanes, pick)
        k_prev, v_prev = k_cur, v_cur
    kp_ref[...] = k_prev
    vp_ref[...] = v_prev

    @pl.when(n == pl.num_programs(1) - 1)
    def _():
        ks_ref[0] = k_prev
        vs_ref[0] = v_prev


def _attend_block(q_all, k_prev, v_prev, k_cur, v_cur, q_gain, table, sinks_ref, bias_ref, y_ref,
                  from_prev, low_lanes, pick):
    blk = q_all.shape[0]
    half = 2 * HEAD_DIM
    k2 = jnp.concatenate([k_prev, k_cur], axis=0)
    v2t = jnp.concatenate([v_prev, v_cur], axis=0).T.astype(BF16)
    qs = (q_all * q_gain * QK_SCALE).astype(BF16)
    qsq = q_all * q_all
    for kv in range(N_KV_HEADS):
        k_tile = k2[:, (kv // 2) * half:(kv // 2 + 1) * half]
        k_own = jnp.where(low_lanes if kv % 2 == 0 else ~low_lanes, k_tile, 0.0)
        k_swap = pltpu.roll(k_own, HEAD_DIM, axis=1)
        k_lhs = jnp.concatenate([k_own, k_swap] if kv % 2 == 0 else [k_swap, k_own], axis=0).astype(BF16)
        tiles = slice(2 * kv * half, (2 * kv + 1) * half), slice((2 * kv + 1) * half, (2 * kv + 2) * half)
        q_rows = jnp.concatenate([qs[:, t] for t in tiles], axis=0)
        qsq_rows = jnp.concatenate([qsq[:, t] for t in tiles], axis=0)
        s = _dot_nt(k_lhs, q_rows)
        ssq = _dot_nt(pick, qsq_rows, HIGHEST)
        probs, inv_denoms = [], []
        for i in range(4):
            odd, tile = i % 2, i // 2
            lanes = slice(tile * blk, (tile + 1) * blk)
            s_h = s[odd * 2 * blk:(odd + 1) * 2 * blk, lanes]
            q_rms = lax.rsqrt(ssq[odd * 8:odd * 8 + 1, lanes] * (1.0 / HEAD_DIM) + EPS)
            logits = jnp.where(from_prev, s_h[:blk], s_h[blk:]) * q_rms + bias_ref[table, 4 * kv + i]
            sink = sinks_ref[4 * kv + i]
            m = jnp.maximum(jnp.max(logits, axis=0, keepdims=True), sink)
            p = jnp.exp(logits - m)
            inv_denoms.append(1.0 / (jnp.sum(p, axis=0, keepdims=True) + jnp.exp(sink - m)))
            probs.append(jnp.concatenate([jnp.where(from_prev, p, 0.0), jnp.where(from_prev, 0.0, p)], axis=0))
        o = _dot(v2t[kv * HEAD_DIM:(kv + 1) * HEAD_DIM], jnp.concatenate(probs, axis=1).astype(BF16))
        for tile in range(2):
            pair = [o[:, i * blk:(i + 1) * blk] * inv_denoms[i] for i in (2 * tile, 2 * tile + 1)]
            y_ref[:, tiles[tile]] = jnp.concatenate(pair, axis=0).T


def _prompt_attn(z, sinks, q_gain, k_gain, bsz, seq, n_tok, col_q, attn_w, kv_w):
    blk = WINDOW
    rows = PROMPT_BLOCKS_PER_STEP * blk
    nb = seq // rows
    assert seq % rows == 0
    n_heads = attn_w // HEAD_DIM
    row = lambda b, n: b * nb + n
    q_blk, k_blk, v_blk = col_q // attn_w, (col_q + attn_w) // kv_w, (col_q + attn_w + kv_w) // kv_w
    qi, kj = np.arange(blk)[:, None], np.arange(blk)[None, :]
    first, later = [np.take_along_axis(_bias_table(first_pos + np.arange(blk), first_pos - blk + np.arange(2 * blk), n_heads),
                                       np.broadcast_to(np.where(kj > qi, kj, kj + blk), (n_heads, blk, blk)), axis=2)
                    for first_pos in (0, blk)]
    bias = np.stack([first, later]).swapaxes(2, 3)
    return pl.pallas_call(
        functools.partial(_prompt_attn_kernel, n_heads=n_heads),
        grid=(bsz, nb),
        in_specs=[
            pl.BlockSpec(memory_space=pltpu.SMEM),
            pl.BlockSpec((rows, attn_w), lambda b, n: (row(b, n), q_blk)),
            pl.BlockSpec((rows, kv_w), lambda b, n: (row(b, n), k_blk)),
            pl.BlockSpec((rows, kv_w), lambda b, n: (row(b, n), v_blk)),
            pl.BlockSpec((1, attn_w), lambda b, n: (0, 0)),
            pl.BlockSpec((1, HEAD_DIM), lambda b, n: (0, 0)),
            pl.BlockSpec(bias.shape, lambda b, n: (0, 0, 0, 0)),
        ],
        out_specs=[
            pl.BlockSpec((rows, attn_w), lambda b, n: (row(b, n), 0)),
            pl.BlockSpec((1, blk, kv_w), lambda b, n: (b, 0, 0)),
            pl.BlockSpec((1, blk, kv_w), lambda b, n: (b, 0, 0)),
        ],
        out_shape=[jax.ShapeDtypeStruct((n_tok, attn_w), F32),
                   jax.ShapeDtypeStruct((bsz, blk, kv_w), F32),
                   jax.ShapeDtypeStruct((bsz, blk, kv_w), F32)],
        scratch_shapes=[pltpu.VMEM((blk, kv_w), F32)] * 2,
        compiler_params=_compiler_params("parallel", "arbitrary"),
        name="prompt_attn",
    )(sinks, z, z, z, jnp.tile(q_gain, n_heads).reshape(1, attn_w), k_gain.reshape(1, HEAD_DIM), jnp.asarray(bias))


def _sample_attn_kernel(*refs, dec_seq, aliased):
    q_ref, k_ref, v_ref, ck_ref, cv_ref, qg_ref, kg_ref, bias_ref, sink_ref = refs[:9]
    y_ref, ks_ref, vs_ref, kbuf_ref, vbuf_ref, p_ref = refs[9 + 2 * aliased:]
    tile, q_rows, _ = q_ref.shape
    cache_w = ck_ref.shape[1]
    keys = kbuf_ref.shape[0] // tile
    k_new = _head_rms(k_ref[...], kg_ref[...])
    v_new = v_ref[...]
    pad = jnp.zeros((keys - cache_w, kbuf_ref.shape[1]), F32)
    for b in range(tile):
        rows = slice(b * dec_seq, (b + 1) * dec_seq)
        kbuf_ref[b * keys:b * keys + cache_w, :] = ck_ref[b]
        vbuf_ref[b * keys:b * keys + cache_w, :] = cv_ref[b]
        kbuf_ref[b * keys + cache_w:(b + 1) * keys, :] = pad
        vbuf_ref[b * keys + cache_w:(b + 1) * keys, :] = pad
        kbuf_ref[b * keys + cache_w:b * keys + cache_w + dec_seq, :] = k_new[rows]
        vbuf_ref[b * keys + cache_w:b * keys + cache_w + dec_seq, :] = v_new[rows]
    q = q_ref[...].reshape(tile * q_rows, q_ref.shape[2])
    q = q * lax.rsqrt(jnp.sum(q * q, axis=-1, keepdims=True) * (1.0 / HEAD_DIM) + EPS) * qg_ref[...]
    s = _dot_nt((q * QK_SCALE).astype(BF16), kbuf_ref[...].astype(BF16))
    p_ref[...] = jnp.zeros_like(p_ref)
    inv_denoms = []
    for b in range(tile):
        own_rows, own_keys = slice(b * q_rows, (b + 1) * q_rows), slice(b * keys, (b + 1) * keys)
        logits = s[own_rows, own_keys] + bias_ref[...]
        sink = sink_ref[...]
        m = jnp.maximum(jnp.max(logits, axis=-1, keepdims=True), sink)
        p = jnp.exp(logits - m)
        inv_denoms.append(1.0 / (jnp.sum(p, axis=-1, keepdims=True) + jnp.exp(sink - m)))
        p_ref[own_rows, own_keys] = p.astype(BF16)
    o = _dot(p_ref[...], vbuf_ref[...].astype(BF16))
    for b in range(tile):
        rows = slice(b * dec_seq, (b + 1) * dec_seq)
        y_ref[b] = o[b * q_rows:(b + 1) * q_rows] * inv_denoms[b]
        ks_ref[b, 0:cache_w - dec_seq, :] = ck_ref[b, dec_seq:cache_w, :]
        vs_ref[b, 0:cache_w - dec_seq, :] = cv_ref[b, dec_seq:cache_w, :]
        ks_ref[b, cache_w - dec_seq:cache_w, :] = k_new[rows]
        vs_ref[b, cache_w - dec_seq:cache_w, :] = v_new[rows]


def _sample_attn(z, cache_k, cache_v, windows, layer, sinks, q_gain, k_gain, row0, dec_seq, col_q, attn_w, kv_w):
    _, dec_batch, cache_w, _ = cache_k.shape
    n_heads = attn_w // HEAD_DIM
    q_per_kv = n_heads // N_KV_HEADS
    tile = SAMPLE_BATCH_TILE
    r0 = row0 // (tile * dec_seq)
    k_blk, v_blk = (col_q + attn_w) // kv_w, (col_q + attn_w + kv_w) // kv_w
    keys = 2 * cache_w
    k_pos = np.full(keys, -1)
    k_pos[:cache_w] = PAST_LEN - cache_w + np.arange(cache_w)
    k_pos[cache_w:cache_w + dec_seq] = PAST_LEN + np.arange(dec_seq)
    bias = _bias_table(PAST_LEN + np.arange(dec_seq), k_pos, n_heads).reshape(n_heads * dec_seq, keys)
    sink_col = jnp.repeat(sinks, dec_seq).reshape(n_heads * dec_seq, 1)
    q = z[row0:, col_q:col_q + attn_w].reshape(dec_batch, dec_seq, N_KV_HEADS, q_per_kv, 1, HEAD_DIM)
    own_kv = np.eye(N_KV_HEADS, dtype=np.float32)[None, :, None, None, :, None]
    q = (q.transpose(0, 2, 3, 1, 4, 5) * own_kv).reshape(dec_batch, n_heads * dec_seq, kv_w)
    window = pl.BlockSpec((None, tile, cache_w, kv_w), lambda i: (layer, i, 0, 0))
    aliased = windows is not None
    in_specs = [
        pl.BlockSpec((tile, n_heads * dec_seq, kv_w), lambda i: (i, 0, 0)),
        pl.BlockSpec((tile * dec_seq, kv_w), lambda i: (r0 + i, k_blk)),
        pl.BlockSpec((tile * dec_seq, kv_w), lambda i: (r0 + i, v_blk)),
        window,
        window,
        pl.BlockSpec((1, kv_w), lambda i: (0, 0)),
        pl.BlockSpec((1, HEAD_DIM), lambda i: (0, 0)),
        pl.BlockSpec(bias.shape, lambda i: (0, 0)),
        pl.BlockSpec(sink_col.shape, lambda i: (0, 0)),
    ]
    args = [q, z, z, cache_k, cache_v, jnp.tile(q_gain, N_KV_HEADS).reshape(1, kv_w), k_gain.reshape(1, HEAD_DIM),
            jnp.asarray(bias), sink_col]
    if aliased:
        in_specs += [pl.BlockSpec(memory_space=pl.ANY)] * 2
        args += list(windows)
    y, ks, vs = pl.pallas_call(
        functools.partial(_sample_attn_kernel, dec_seq=dec_seq, aliased=aliased),
        grid=(dec_batch // tile,),
        in_specs=in_specs,
        out_specs=[pl.BlockSpec((tile, n_heads * dec_seq, kv_w), lambda i: (i, 0, 0)), window, window],
        out_shape=[jax.ShapeDtypeStruct(q.shape, F32),
                   jax.ShapeDtypeStruct(cache_k.shape, F32),
                   jax.ShapeDtypeStruct(cache_v.shape, F32)],
        scratch_shapes=[pltpu.VMEM((tile * keys, kv_w), F32)] * 2
        + [pltpu.VMEM((tile * n_heads * dec_seq, tile * keys), BF16)],
        input_output_aliases={len(args) - 2: 1, len(args) - 1: 2} if aliased else {},
        compiler_params=_compiler_params("parallel"),
        name="sample_attn",
    )(*args)
    y = y.reshape(dec_batch, N_KV_HEADS, q_per_kv, dec_seq, N_KV_HEADS, HEAD_DIM)
    y = jnp.stack([y[:, kv, :, :, kv] for kv in range(N_KV_HEADS)], axis=1)
    return y.transpose(0, 3, 1, 2, 4).reshape(dec_batch * dec_seq, attn_w), ks, vs


def kernel(x_prompt, x_sample, cache_k, cache_v, state_ssm_re, state_ssm_im, ffn1_norm, ffn1_w_gate, ffn1_w_up, ffn1_w_down, mix_norm, w_in, ssm_A_re, ssm_A_im, ssm_B_re, ssm_B_im, ssm_C_re, ssm_C_im, ssm_D, ssm_log_dt, ssm_w_glu, q_norm, k_norm, sinks, ssm_out_norm, attn_out_norm, w_out, ffn2_norm, ffn2_w_gate, ffn2_w_up, ffn2_w_down):
    bsz, seq, d = x_prompt.shape
    dec_batch, dec_seq, _ = x_sample.shape
    depth = w_in.shape[0]
    ssm_w = ssm_A_re.shape[1] * SSM_GROUP_CH
    attn_w = w_out.shape[1] - ssm_w
    kv_w = N_KV_HEADS * HEAD_DIM
    cache_w = cache_k.shape[2]
    n_prompt = bsz * seq
    n_tok = n_prompt + dec_batch * dec_seq

    xs = [x_prompt.reshape(n_prompt, d), x_sample.reshape(dec_batch * dec_seq, d)]
    n_groups = ssm_w // SSM_GROUP_CH
    n_merged = n_groups // SSM_MERGE
    zeros_state = jnp.zeros((n_merged, bsz, SSM_MERGE * SSM_STATE), F32)
    outs = [[] for _ in range(6)]
    cache_k_flat = cache_k.reshape(depth, dec_batch, cache_w, kv_w)
    cache_v_flat = cache_v.reshape(depth, dec_batch, cache_w, kv_w)
    windows = None
    ffn1_w, ffn2_w = (ffn1_w_gate, ffn1_w_up, ffn1_w_down), (ffn2_w_gate, ffn2_w_up, ffn2_w_down)
    w_in_bf, w_out_bf = w_in.astype(BF16), w_out.astype(BF16)
    all_layers = lambda a: a.reshape((-1,) + a.shape[2:])
    tab_p, tab_s = _ssm_prep(*_merge_groups(*map(all_layers, (ssm_A_re, ssm_A_im, ssm_log_dt, ssm_B_re, ssm_B_im,
                                                               ssm_C_re, ssm_C_im, ssm_D, ssm_w_glu)), SSM_MERGE),
                             t_small=dec_seq)
    for l in range(depth):
        x = _ffn(xs if l == 0 else [x], ffn1_norm[l], *ffn1_w, layer=l)
        z = _norm_matmul(x, mix_norm[l], w_in_bf, layer=l, tn=w_in.shape[2])

        grp_major = lambda a: a.reshape(a.shape[0], n_merged, SSM_MERGE * SSM_STATE).transpose(1, 0, 2)
        seq_major = lambda a: a.transpose(1, 0, 2).reshape(a.shape[1], n_groups, SSM_STATE)
        y_ssm, hp_re, hp_im = _ssm_mix(z, None, tab_p, l, zeros_state, zeros_state, 0, n_prompt, bsz, PROMPT_CHUNK)
        y_ssm, hs_re, hs_im = _ssm_mix(z, y_ssm, tab_s, l, grp_major(state_ssm_re[l]), grp_major(state_ssm_im[l]),
                                       n_prompt, dec_batch * dec_seq, dec_batch, dec_seq)
        hp_re, hp_im, hs_re, hs_im = map(seq_major, (hp_re, hp_im, hs_re, hs_im))

        y_attn, pk, pv = _prompt_attn(z, sinks[l], q_norm[l], k_norm[l], bsz, seq, n_tok, ssm_w, attn_w, kv_w)
        ya_s, *windows = _sample_attn(z, cache_k_flat, cache_v_flat, windows, l, sinks[l], q_norm[l], k_norm[l],
                                      n_prompt, dec_seq, ssm_w, attn_w, kv_w)
        y_attn = lax.dynamic_update_slice(y_attn, ya_s, (n_prompt, 0))

        x = _out_proj(x, y_ssm, y_attn, ssm_out_norm[l], attn_out_norm[l], w_out_bf, layer=l)
        if l < depth - 1:
            x = _ffn([x], ffn2_norm[l], *ffn2_w, layer=l)

        kv_shape = (-1, N_KV_HEADS, HEAD_DIM)
        for dst, val in zip(outs, (pk.reshape(bsz, *kv_shape), pv.reshape(bsz, *kv_shape), hp_re, hp_im, hs_re, hs_im)):
            dst.append(val)
    y_prompt, y_sample = _ffn([x], ffn2_norm[depth - 1], *ffn2_w, layer=depth - 1, out_rows=(n_prompt, n_tok - n_prompt))
    y_prompt, y_sample = y_prompt.reshape(bsz, seq, d), y_sample.reshape(dec_batch, dec_seq, d)
    pk, pv, hp_re, hp_im, hs_re, hs_im = (jnp.stack(o) for o in outs)
    sk, sv = (w.reshape(cache_k.shape) for w in windows)
    return (y_prompt, y_sample, pk, pv, hp_re, hp_im, sk, sv, hs_re, hs_im)
```

```python
import functools

import jax
import jax.numpy as jnp
import numpy as np
from jax import lax
from jax.experimental import pallas as pl
from jax.experimental.pallas import tpu as pltpu

F32 = jnp.float32
BF16 = jnp.bfloat16

EPS = 1e-6
SSM_GROUP_CH = 16
SSM_STATE = 64
HEAD_DIM = 64
N_KV_HEADS = 4
WINDOW = 128
PAST_LEN = 8192
QK_SCALE = HEAD_DIM ** -0.5
PROMPT_CHUNK = 8

TOKEN_TILE = 512
FFN_TOKEN_TILE = 1088
FFN_VMEM_LIMIT_BYTES = 60 * 1024 * 1024
FF_TILE = 256
SAMPLE_BATCH_TILE = 8
PROMPT_BLOCKS_PER_STEP = 4
VMEM_LIMIT_BYTES = 48 * 1024 * 1024
PROJ_VMEM_LIMIT_BYTES = 56 * 1024 * 1024

HIGHEST = lax.Precision.HIGHEST
LANES = 128
SSM_MERGE = 4


def _rms(x, gain):
    return x * lax.rsqrt(jnp.mean(x * x, axis=-1, keepdims=True) + EPS) * gain


def _dot(a, b, precision=None):
    return jnp.dot(a, b, preferred_element_type=F32, precision=precision)


def _dot_nt(a, b, precision=None):
    return lax.dot_general(a, b, (((1,), (1,)), ((), ())), preferred_element_type=F32, precision=precision)


def _compiler_params(*semantics):
    return pltpu.CompilerParams(dimension_semantics=semantics, vmem_limit_bytes=VMEM_LIMIT_BYTES)


def _ffn_kernel(*refs, n_in, n_out, split_row):
    x_refs, (g_ref, wg_ref, wu_ref, wd_ref) = refs[:n_in], refs[n_in:n_in + 4]
    o_refs, xn_ref = refs[n_in + 4:n_in + 4 + n_out], refs[-1]
    acc_ref = o_refs[0]
    i, j = pl.program_id(0), pl.program_id(1)
    boundary = i == pl.num_programs(0) - 1

    def with_x_tile(fn):
        if n_in == 1:
            fn(x_refs[0][...])
        else:
            pl.when(jnp.logical_not(boundary))(lambda: fn(x_refs[0][...]))
            pl.when(boundary)(lambda: fn(jnp.concatenate([x_refs[0][0:split_row, :], x_refs[1][...]], axis=0)))

    @pl.when(j == 0)
    def _():
        acc_ref[...] = jnp.zeros_like(acc_ref)

        def normalise(x):
            xn_ref[...] = _rms(x, g_ref[...]).astype(BF16)
        with_x_tile(normalise)

    xn = xn_ref[...]
    gate = _dot(xn, wg_ref[...].astype(BF16))
    up = _dot(xn, wu_ref[...].astype(BF16))
    h = (gate * jax.nn.sigmoid(gate) * up).astype(BF16)
    acc_ref[...] += _dot(h, wd_ref[...].astype(BF16))

    @pl.when(j == pl.num_programs(1) - 1)
    def _():
        def residual(x):
            acc_ref[...] = x + 0.5 * acc_ref[...]
        with_x_tile(residual)
        if n_out == 2:
            @pl.when(boundary)
            def _():
                o_refs[1][...] = acc_ref[split_row:, :]


def _ffn(xs, gain, w_gate, w_up, w_down, layer, out_rows=None):
    d = xs[0].shape[1]
    n_tok = sum(x.shape[0] for x in xs)
    d_ff = w_gate.shape[2]
    if len(xs) == 2 or out_rows is not None:
        tm = FFN_TOKEN_TILE if n_tok % FFN_TOKEN_TILE == 0 else n_tok
        n0, n1 = (x.shape[0] for x in xs) if len(xs) == 2 else out_rows
        split_row = n0 - (n_tok // tm - 1) * tm
        assert split_row + n1 == tm and split_row % 8 == 0
    else:
        tm = FFN_TOKEN_TILE if n_tok % FFN_TOKEN_TILE == 0 else TOKEN_TILE
        split_row = None
    tf = FF_TILE
    tile_spec = pl.BlockSpec((tm, d), lambda i, j: (i, 0))
    tail_spec = lambda rows: pl.BlockSpec((rows, d), lambda i, j: (0, 0))
    x_specs = [tile_spec] + ([tail_spec(xs[1].shape[0])] if len(xs) == 2 else [])
    if out_rows is None:
        out_specs, out_shape = [tile_spec], [jax.ShapeDtypeStruct((n_tok, d), F32)]
    else:
        out_specs = [tile_spec, tail_spec(out_rows[1])]
        out_shape = [jax.ShapeDtypeStruct((rows, d), F32) for rows in out_rows]
    outs = pl.pallas_call(
        functools.partial(_ffn_kernel, n_in=len(xs), n_out=len(out_specs), split_row=split_row),
        grid=(n_tok // tm, d_ff // tf),
        in_specs=x_specs + [
            pl.BlockSpec((1, d), lambda i, j: (0, 0)),
            pl.BlockSpec((None, d, tf), lambda i, j: (layer, 0, j)),
            pl.BlockSpec((None, d, tf), lambda i, j: (layer, 0, j)),
            pl.BlockSpec((None, tf, d), lambda i, j: (layer, j, 0)),
        ],
        out_specs=out_specs,
        out_shape=out_shape,
        scratch_shapes=[pltpu.VMEM((tm, d), BF16)],
        compiler_params=pltpu.CompilerParams(dimension_semantics=("arbitrary", "arbitrary"),
                                             vmem_limit_bytes=FFN_VMEM_LIMIT_BYTES),
        name="ffn",
    )(*xs, gain.reshape(1, d), w_gate, w_up, w_down)
    return outs[0] if out_rows is None else outs


def _norm_matmul_kernel(x_ref, g_ref, w_ref, o_ref, wb_ref):
    @pl.when(pl.program_id(0) == 0)
    def _():
        wb_ref[...] = w_ref[...].astype(BF16)

    o_ref[...] = _dot(_rms(x_ref[...], g_ref[...]).astype(BF16), wb_ref[...])


def _norm_matmul(x, gain, w, layer):
    n_tok, d = x.shape
    n_out = w.shape[2]
    tm = TOKEN_TILE
    return pl.pallas_call(
        _norm_matmul_kernel,
        grid=(n_tok // tm,),
        in_specs=[
            pl.BlockSpec((tm, d), lambda i: (i, 0)),
            pl.BlockSpec((1, d), lambda i: (0, 0)),
            pl.BlockSpec((None, d, n_out), lambda i: (layer, 0, 0), pipeline_mode=pl.Buffered(1)),
        ],
        out_specs=pl.BlockSpec((tm, n_out), lambda i: (i, 0)),
        out_shape=jax.ShapeDtypeStruct((n_tok, n_out), F32),
        scratch_shapes=[pltpu.VMEM((d, n_out), BF16)],
        compiler_params=pltpu.CompilerParams(dimension_semantics=("arbitrary",), vmem_limit_bytes=PROJ_VMEM_LIMIT_BYTES),
        name="in_proj",
    )(x, gain.reshape(1, d), w)


def _out_proj_kernel(x_ref, ys_ref, ya_ref, gs_ref, ga_ref, ws_ref, wa_ref, o_ref, wsb_ref, wab_ref):
    @pl.when(pl.program_id(0) == 0)
    def _():
        wsb_ref[...] = ws_ref[...].astype(BF16)
        wab_ref[...] = wa_ref[...].astype(BF16)

    ys = _rms(ys_ref[...], gs_ref[...]).astype(BF16)
    ya = _rms(ya_ref[...], ga_ref[...]).astype(BF16)
    o_ref[...] = x_ref[...] + _dot(ys, wsb_ref[...]) + _dot(ya, wab_ref[...])


def _out_proj(x, y_ssm, y_attn, g_ssm, g_attn, w_out, layer):
    n_tok, d = x.shape
    ssm_w, attn_w = y_ssm.shape[1], y_attn.shape[1]
    tm = TOKEN_TILE
    return pl.pallas_call(
        _out_proj_kernel,
        grid=(n_tok // tm,),
        in_specs=[
            pl.BlockSpec((tm, d), lambda i: (i, 0)),
            pl.BlockSpec((tm, ssm_w), lambda i: (i, 0)),
            pl.BlockSpec((tm, attn_w), lambda i: (i, 0)),
            pl.BlockSpec((1, ssm_w), lambda i: (0, 0)),
            pl.BlockSpec((1, attn_w), lambda i: (0, 0)),
            pl.BlockSpec((None, ssm_w, d), lambda i: (layer, 0, 0), pipeline_mode=pl.Buffered(1)),
            pl.BlockSpec((None, attn_w, d), lambda i: (layer, ssm_w // attn_w, 0), pipeline_mode=pl.Buffered(1)),
        ],
        out_specs=pl.BlockSpec((tm, d), lambda i: (i, 0)),
        out_shape=jax.ShapeDtypeStruct((n_tok, d), F32),
        scratch_shapes=[pltpu.VMEM((ssm_w, d), BF16), pltpu.VMEM((attn_w, d), BF16)],
        compiler_params=pltpu.CompilerParams(dimension_semantics=("arbitrary",), vmem_limit_bytes=PROJ_VMEM_LIMIT_BYTES),
        name="out_proj",
    )(x, y_ssm, y_attn, g_ssm.reshape(1, ssm_w), g_attn.reshape(1, attn_w), w_out, w_out)


def _ssm_prep_kernel(*refs, t_small):
    for gi in range(refs[0].shape[0]):
        _ssm_prep_group(*[r.at[pl.ds(gi, 1)] for r in refs], t_small=t_small)


def _ssm_prep_group(are_ref, aim_ref, ldt_ref, btre_ref, btim_ref, cre_ref, cim_ref, d_ref, wglu_ref,
                    *table_refs, t_small):
    per_set = len(table_refs) // 2
    full_refs, small_refs = table_refs[:per_set], table_refs[per_set:]
    a_re, a_im = are_ref[0], aim_ref[0]
    dt = jnp.exp(ldt_ref[0])
    mag = jnp.exp(a_re * dt)
    lb_re, lb_im = mag * jnp.cos(a_im * dt), mag * jnp.sin(a_im * dt)
    den = a_re * a_re + a_im * a_im
    q_re = ((lb_re - 1.0) * a_re + lb_im * a_im) / den
    q_im = (lb_im * a_re - (lb_re - 1.0) * a_im) / den
    bt_re, bt_im = btre_ref[0], btim_ref[0]
    bb_re = q_re * bt_re - q_im * bt_im
    bb_im = q_re * bt_im + q_im * bt_re
    c_re, c_im = cre_ref[0], cim_ref[0]

    pw_re, pw_im = [jnp.ones_like(lb_re)], [jnp.zeros_like(lb_im)]
    for _ in range(PROMPT_CHUNK):
        pr, pi = pw_re[-1], pw_im[-1]
        pw_re.append(pr * lb_re - pi * lb_im)
        pw_im.append(pr * lb_im + pi * lb_re)

    steps = PROMPT_CHUNK
    c = c_re.shape[0]
    w_re = jnp.concatenate([bb_re * pw_re[steps - 1 - s] - bb_im * pw_im[steps - 1 - s] for s in range(steps)], axis=0)
    w_im = jnp.concatenate([bb_re * pw_im[steps - 1 - s] + bb_im * pw_re[steps - 1 - s] for s in range(steps)], axis=0)
    zt_re = jnp.concatenate([c_re * pw_re[t + 1] - c_im * pw_im[t + 1] for t in range(steps)], axis=0)
    zt_im = jnp.concatenate([-(c_re * pw_im[t + 1] + c_im * pw_re[t + 1]) for t in range(steps)], axis=0)

    lag = _dot_nt(w_re, c_re, HIGHEST) - _dot_nt(w_im, c_im, HIGHEST)
    cols = []
    for t in range(steps):
        live = lag[(steps - 1 - t) * c:, :]
        cols.append(live if t == steps - 1 else
                    jnp.concatenate([live, jnp.zeros(((steps - 1 - t) * c, c), F32)], axis=0))
    m = jnp.concatenate(cols, axis=1)

    tiled = jnp.concatenate([jnp.concatenate([wglu_ref[0]] * steps, axis=0)] * steps, axis=1)
    shift = c.bit_length() - 1
    row_step = lax.shift_right_logical(lax.broadcasted_iota(jnp.int32, tiled.shape, 0), shift)
    col_step = lax.shift_right_logical(lax.broadcasted_iota(jnp.int32, tiled.shape, 1), shift)
    glu = jnp.where(row_step == col_step, tiled, 0.0)
    d_t = jnp.concatenate([d_ref[0]] * steps, axis=1)

    for t, refs in ((steps, full_refs), (t_small, small_refs)):
        m_ref, wre_ref, wim_ref, ztre_ref, ztim_ref, glu_ref, dt_ref, ore_ref, oim_ref = refs
        n, full = t * c, steps * c
        m_ref[0] = m[:n, :n].astype(BF16)
        wre_ref[0] = w_re[full - n:].astype(BF16)
        wim_ref[0] = w_im[full - n:].astype(BF16)
        ztre_ref[0] = zt_re[:n].astype(BF16)
        ztim_ref[0] = zt_im[:n].astype(BF16)
        glu_ref[0] = glu[:n, :n].astype(BF16)
        dt_ref[0] = d_t[:, :n]
        ore_ref[0] = pw_re[t]
        oim_ref[0] = pw_im[t]


def _merge_groups(a_re, a_im, log_dt, b_re, b_im, c_re, c_im, d_skip, w_glu, n):
    g, p = a_re.shape
    gm = g // n
    own = np.eye(n, dtype=np.float32)[None, :, :, None, None]

    def block_diag(x):
        r, q = x.shape[1:]
        return (x.reshape(gm, n, 1, r, q) * own).transpose(0, 1, 3, 2, 4).reshape(gm, n * r, n * q)

    lanes = lambda x: x.reshape(gm, 1, -1)
    return (lanes(a_re), lanes(a_im), lanes(jnp.repeat(log_dt, p)),
            block_diag(jnp.swapaxes(b_re, 1, 2)), block_diag(jnp.swapaxes(b_im, 1, 2)),
            block_diag(c_re), block_diag(c_im), lanes(d_skip), block_diag(w_glu))


def _ssm_prep(a_re, a_im, log_dt, bt_re, bt_im, c_re, c_im, d_skip, w_glu, t_small):
    g, c, p = c_re.shape
    assert c & (c - 1) == 0
    gps = LANES // c
    gs = lambda *shape: pl.BlockSpec((gps,) + shape, lambda i: (i,) + (0,) * len(shape))
    names = ("m", "w_re", "w_im", "zt_re", "zt_im", "glu", "d", "a_re", "a_im")
    out_specs, out_shape = [], []
    for t in (PROMPT_CHUNK, t_small):
        tc = t * c
        shapes = [(tc, tc), (tc, p), (tc, p), (tc, p), (tc, p), (tc, tc), (1, tc), (1, p), (1, p)]
        dtypes = [BF16] * 6 + [F32] * 3
        out_specs += [gs(*shape) for shape in shapes]
        out_shape += [jax.ShapeDtypeStruct((g,) + shape, dtype) for shape, dtype in zip(shapes, dtypes)]
    outs = pl.pallas_call(
        functools.partial(_ssm_prep_kernel, t_small=t_small),
        grid=(g // gps,),
        in_specs=[gs(1, p), gs(1, p), gs(1, p), gs(c, p), gs(c, p), gs(c, p), gs(c, p), gs(1, c), gs(c, c)],
        out_specs=out_specs,
        out_shape=out_shape,
        compiler_params=_compiler_params("parallel"),
        name="ssm_prep",
    )(a_re, a_im, log_dt, bt_re, bt_im, c_re, c_im, d_skip, w_glu)
    return dict(zip(names, outs[:len(names)])), dict(zip(names, outs[len(names):]))


def _ssm_mix_kernel(*refs, bsz, t, aliased):
    (u_ref, m_ref, wre_ref, wim_ref, ztre_ref, ztim_ref, are_ref, aim_ref, d_ref, glu_ref, h0re_ref, h0im_ref) = refs[:12]
    y_ref, hre_ref, him_ref, xs_ref, ug_ref, vre_ref, vim_ref, pre_ref, pim_ref = refs[12 + aliased:]
    rows = u_ref.shape[0] // t
    chunks = rows // bsz
    n_grp, tc = m_ref.shape[0], m_ref.shape[1]
    c = tc // t

    def token_rows(b, s):
        return pl.ds(b * chunks * t + s, chunks, stride=t)

    def chunk_rows(b):
        return pl.ds(b, chunks, stride=bsz)

    for s in range(t):
        if chunks == 1:
            xs_ref[s] = u_ref[pl.ds(s, rows, stride=t), :]
        else:
            for b in range(bsz):
                xs_ref[s, chunk_rows(b), :] = u_ref[token_rows(b, s), :]
    for gi in range(n_grp):
        lanes = slice(gi * c, (gi + 1) * c)
        u = jnp.concatenate([xs_ref[s, :, lanes] for s in range(t)], axis=1)
        ug_ref[gi] = u
        vre_ref[gi] = _dot(u.astype(BF16), wre_ref[gi])
        vim_ref[gi] = _dot(u.astype(BF16), wim_ref[gi])

    per_tile = max(1, 8 // bsz)
    tile_rows = per_tile * bsz

    def chunk_tile(i, h):
        sel = pl.ds(pl.multiple_of(i * tile_rows, tile_rows), tile_rows)
        nxt = []
        for gi in range(n_grp):
            h_re, h_im = h[2 * gi], h[2 * gi + 1]
            v_re, v_im = vre_ref[gi, sel, :], vim_ref[gi, sel, :]
            a_re, a_im = are_ref[gi], aim_ref[gi]
            ent_re, ent_im = [], []
            for j in range(per_tile):
                ent_re.append(h_re)
                ent_im.append(h_im)
                part = slice(j * bsz, (j + 1) * bsz)
                h_re, h_im = a_re * h_re - a_im * h_im + v_re[part], a_re * h_im + a_im * h_re + v_im[part]
            pre_ref[gi, sel, :] = jnp.concatenate(ent_re, axis=0)
            pim_ref[gi, sel, :] = jnp.concatenate(ent_im, axis=0)
            nxt += [h_re, h_im]
        return tuple(nxt)

    h = tuple(r[gi] for gi in range(n_grp) for r in (h0re_ref, h0im_ref))
    n_tiles = chunks // per_tile
    h = lax.fori_loop(0, n_tiles, chunk_tile, h, unroll=min(n_tiles, 4))
    for gi in range(n_grp):
        hre_ref[gi] = h[2 * gi]
        him_ref[gi] = h[2 * gi + 1]

    for gi in range(n_grp):
        u = ug_ref[gi]
        y = (_dot(u.astype(BF16), m_ref[gi]) + _dot_nt(pre_ref[gi].astype(BF16), ztre_ref[gi])
             + _dot_nt(pim_ref[gi].astype(BF16), ztim_ref[gi]) + d_ref[gi] * u)
        y = jax.nn.gelu(y)
        y = y * jax.nn.sigmoid(_dot(y.astype(BF16), glu_ref[gi]))
        for s in range(t):
            xs_ref[s, :, gi * c:(gi + 1) * c] = y[:, s * c:(s + 1) * c]
    for s in range(t):
        if chunks == 1:
            y_ref[pl.ds(s, rows, stride=t), :] = xs_ref[s]
        else:
            for b in range(bsz):
                y_ref[token_rows(b, s), :] = xs_ref[s, chunk_rows(b), :]


def _ssm_mix(z, y_prev, tab, layer, h0_re, h0_im, row0, n_rows, bsz, t):
    g, _, p = h0_re.shape
    tc = tab["m"].shape[1]
    n_tok = z.shape[0]
    c = tc // t
    width = g * c
    lanes = LANES
    gps = lanes // c
    rows = n_rows // t
    first = layer * (g // gps)
    blk = lambda *shape: pl.BlockSpec((gps,) + shape, lambda i: (i,) + (0,) * len(shape))
    tbl = lambda *shape: pl.BlockSpec((gps,) + shape, lambda i: (first + i,) + (0,) * len(shape))
    tok = pl.BlockSpec((n_rows, lanes), lambda i: (row0 // n_rows, i))
    aliased = y_prev is not None
    in_specs = [tok, tbl(tc, tc), tbl(tc, p), tbl(tc, p), tbl(tc, p), tbl(tc, p), tbl(1, p), tbl(1, p), tbl(1, tc),
                tbl(tc, tc), blk(bsz, p), blk(bsz, p)]
    args = [z, tab["m"], tab["w_re"], tab["w_im"], tab["zt_re"], tab["zt_im"], tab["a_re"], tab["a_im"], tab["d"],
            tab["glu"], h0_re, h0_im]
    if aliased:
        in_specs.append(pl.BlockSpec(memory_space=pl.ANY))
        args.append(y_prev)
    return pl.pallas_call(
        functools.partial(_ssm_mix_kernel, bsz=bsz, t=t, aliased=aliased),
        grid=(g // gps,),
        in_specs=in_specs,
        out_specs=[tok, blk(bsz, p), blk(bsz, p)],
        out_shape=[jax.ShapeDtypeStruct((n_tok, width), F32), jax.ShapeDtypeStruct((g, bsz, p), F32),
                   jax.ShapeDtypeStruct((g, bsz, p), F32)],
        scratch_shapes=[pltpu.VMEM((t, rows, lanes), F32), pltpu.VMEM((gps, rows, tc), F32)]
        + [pltpu.VMEM((gps, rows, p), F32)] * 4,
        input_output_aliases={len(args) - 1: 0} if aliased else {},
        compiler_params=_compiler_params("parallel"),
        name="ssm_mix",
    )(*args)


def _bias_table(q_pos, k_pos, n_heads):
    slopes = np.exp2(-8.0 * np.arange(1, n_heads + 1, dtype=np.float64) / n_heads)
    dist = q_pos[:, None] - k_pos[None, :]
    valid = (dist >= 0) & (dist < WINDOW) & (k_pos[None, :] >= 0)
    bias = -slopes[:, None, None] * dist[None].astype(np.float64)
    return np.where(valid[None], bias, -np.inf).astype(np.float32)


def _head_rms(x, gain):
    heads = [_rms(x[:, i:i + HEAD_DIM], gain) for i in range(0, x.shape[1], HEAD_DIM)]
    return jnp.concatenate(heads, axis=1)


def _softmax_sink_pv(logits, sink, v):
    m = jnp.maximum(jnp.max(logits, axis=-1, keepdims=True), sink)
    p = jnp.exp(logits - m)
    denom = jnp.sum(p, axis=-1, keepdims=True) + jnp.exp(sink - m)
    return _dot(p.astype(BF16), v) / denom


def _prompt_attn_kernel(sinks_ref, q_ref, k_ref, v_ref, qg_ref, kg_ref, bias_ref, y_ref, ks_ref, vs_ref,
                        kp_ref, vp_ref, *, n_heads):
    n = pl.program_id(1)

    @pl.when(n == 0)
    def _():
        kp_ref[...] = jnp.zeros_like(kp_ref)
        vp_ref[...] = jnp.zeros_like(vp_ref)

    blk = WINDOW
    half = 2 * HEAD_DIM
    assert n_heads // N_KV_HEADS == 4 and half == 128
    k_all = _head_rms(k_ref[...], kg_ref[...])
    v_all = v_ref[...]
    from_prev = (lax.broadcasted_iota(jnp.int32, (blk, blk), 0) > lax.broadcasted_iota(jnp.int32, (blk, blk), 1))
    low_lanes = lax.broadcasted_iota(jnp.int32, (2 * blk, half), 1) < HEAD_DIM
    low8 = lax.broadcasted_iota(jnp.int32, (8, half), 1) < HEAD_DIM
    pick = jnp.concatenate([jnp.where(low8, 1.0, 0.0), jnp.where(low8, 0.0, 1.0)], axis=0)
    k_prev, v_prev = kp_ref[...], vp_ref[...]
    for sub in range(q_ref.shape[0] // blk):
        rows = slice(sub * blk, (sub + 1) * blk)
        k_cur, v_cur = k_all[rows], v_all[rows]
        table = jnp.minimum(n, 1) if sub == 0 else 1
        _attend_block(q_ref[rows, :], k_prev, v_prev, k_cur, v_cur, qg_ref[...], table, sinks_ref, bias_ref,
                      y_ref.at[rows, :], from_prev, low_lanes, pick)
        k_prev, v_prev = k_cur, v_cur
    kp_ref[...] = k_prev
    vp_ref[...] = v_prev

    @pl.when(n == pl.num_programs(1) - 1)
    def _():
        ks_ref[0] = k_prev
        vs_ref[0] = v_prev


def _attend_block(q_all, k_prev, v_prev, k_cur, v_cur, q_gain, table, sinks_ref, bias_ref, y_ref,
                  from_prev, low_lanes, pick):
    blk = q_all.shape[0]
    half = 2 * HEAD_DIM
    k2 = jnp.concatenate([k_prev, k_cur], axis=0)
    v2t = jnp.concatenate([v_prev, v_cur], axis=0).T.astype(BF16)
    qs = (q_all * q_gain * QK_SCALE).astype(BF16)
    qsq = q_all * q_all
    for kv in range(N_KV_HEADS):
        k_tile = k2[:, (kv // 2) * half:(kv // 2 + 1) * half]
        k_own = jnp.where(low_lanes if kv % 2 == 0 else ~low_lanes, k_tile, 0.0)
        k_swap = pltpu.roll(k_own, HEAD_DIM, axis=1)
        k_lhs = jnp.concatenate([k_own, k_swap] if kv % 2 == 0 else [k_swap, k_own], axis=0).astype(BF16)
        tiles = slice(2 * kv * half, (2 * kv + 1) * half), slice((2 * kv + 1) * half, (2 * kv + 2) * half)
        q_rows = jnp.concatenate([qs[:, t] for t in tiles], axis=0)
        qsq_rows = jnp.concatenate([qsq[:, t] for t in tiles], axis=0)
        s = _dot_nt(k_lhs, q_rows)
        ssq = _dot_nt(pick, qsq_rows, HIGHEST)
        probs, inv_denoms = [], []
        for i in range(4):
            odd, tile = i % 2, i // 2
            lanes = slice(tile * blk, (tile + 1) * blk)
            s_h = s[odd * 2 * blk:(odd + 1) * 2 * blk, lanes]
            q_rms = lax.rsqrt(ssq[odd * 8:odd * 8 + 1, lanes] * (1.0 / HEAD_DIM) + EPS)
            logits = jnp.where(from_prev, s_h[:blk], s_h[blk:]) * q_rms + bias_ref[table, 4 * kv + i]
            sink = sinks_ref[4 * kv + i]
            m = jnp.maximum(jnp.max(logits, axis=0, keepdims=True), sink)
            p = jnp.exp(logits - m)
            inv_denoms.append(1.0 / (jnp.sum(p, axis=0, keepdims=True) + jnp.exp(sink - m)))
            probs.append(jnp.concatenate([jnp.where(from_prev, p, 0.0), jnp.where(from_prev, 0.0, p)], axis=0))
        o = _dot(v2t[kv * HEAD_DIM:(kv + 1) * HEAD_DIM], jnp.concatenate(probs, axis=1).astype(BF16))
        for tile in range(2):
            pair = [o[:, i * blk:(i + 1) * blk] * inv_denoms[i] for i in (2 * tile, 2 * tile + 1)]
            y_ref[:, tiles[tile]] = jnp.concatenate(pair, axis=0).T


def _prompt_attn(z, sinks, q_gain, k_gain, bsz, seq, n_tok, col_q, attn_w, kv_w):
    blk = WINDOW
    rows = PROMPT_BLOCKS_PER_STEP * blk
    nb = seq // rows
    assert seq % rows == 0
    n_heads = attn_w // HEAD_DIM
    row = lambda b, n: b * nb + n
    q_blk, k_blk, v_blk = col_q // attn_w, (col_q + attn_w) // kv_w, (col_q + attn_w + kv_w) // kv_w
    qi, kj = np.arange(blk)[:, None], np.arange(blk)[None, :]
    first, later = [np.take_along_axis(_bias_table(first_pos + np.arange(blk), first_pos - blk + np.arange(2 * blk), n_heads),
                                       np.broadcast_to(np.where(kj > qi, kj, kj + blk), (n_heads, blk, blk)), axis=2)
                    for first_pos in (0, blk)]
    bias = np.stack([first, later]).swapaxes(2, 3)
    return pl.pallas_call(
        functools.partial(_prompt_attn_kernel, n_heads=n_heads),
        grid=(bsz, nb),
        in_specs=[
            pl.BlockSpec(memory_space=pltpu.SMEM),
            pl.BlockSpec((rows, attn_w), lambda b, n: (row(b, n), q_blk)),
            pl.BlockSpec((rows, kv_w), lambda b, n: (row(b, n), k_blk)),
            pl.BlockSpec((rows, kv_w), lambda b, n: (row(b, n), v_blk)),
            pl.BlockSpec((1, attn_w), lambda b, n: (0, 0)),
            pl.BlockSpec((1, HEAD_DIM), lambda b, n: (0, 0)),
            pl.BlockSpec(bias.shape, lambda b, n: (0, 0, 0, 0)),
        ],
        out_specs=[
            pl.BlockSpec((rows, attn_w), lambda b, n: (row(b, n), 0)),
            pl.BlockSpec((1, blk, kv_w), lambda b, n: (b, 0, 0)),
            pl.BlockSpec((1, blk, kv_w), lambda b, n: (b, 0, 0)),
        ],
        out_shape=[jax.ShapeDtypeStruct((n_tok, attn_w), F32),
                   jax.ShapeDtypeStruct((bsz, blk, kv_w), F32),
                   jax.ShapeDtypeStruct((bsz, blk, kv_w), F32)],
        scratch_shapes=[pltpu.VMEM((blk, kv_w), F32)] * 2,
        compiler_params=_compiler_params("parallel", "arbitrary"),
        name="prompt_attn",
    )(sinks, z, z, z, jnp.tile(q_gain, n_heads).reshape(1, attn_w), k_gain.reshape(1, HEAD_DIM), jnp.asarray(bias))


def _sample_attn_kernel(*refs, dec_seq, aliased):
    q_ref, k_ref, v_ref, ck_ref, cv_ref, qg_ref, kg_ref, bias_ref, sink_ref = refs[:9]
    y_ref, ks_ref, vs_ref, kbuf_ref, vbuf_ref, p_ref = refs[9 + 2 * aliased:]
    tile, q_rows, _ = q_ref.shape
    cache_w = ck_ref.shape[1]
    keys = kbuf_ref.shape[0] // tile
    k_new = _head_rms(k_ref[...], kg_ref[...])
    v_new = v_ref[...]
    pad = jnp.zeros((keys - cache_w, kbuf_ref.shape[1]), F32)
    for b in range(tile):
        rows = slice(b * dec_seq, (b + 1) * dec_seq)
        kbuf_ref[b * keys:b * keys + cache_w, :] = ck_ref[b]
        vbuf_ref[b * keys:b * keys + cache_w, :] = cv_ref[b]
        kbuf_ref[b * keys + cache_w:(b + 1) * keys, :] = pad
        vbuf_ref[b * keys + cache_w:(b + 1) * keys, :] = pad
        kbuf_ref[b * keys + cache_w:b * keys + cache_w + dec_seq, :] = k_new[rows]
        vbuf_ref[b * keys + cache_w:b * keys + cache_w + dec_seq, :] = v_new[rows]
    q = q_ref[...].reshape(tile * q_rows, q_ref.shape[2])
    q = q * lax.rsqrt(jnp.sum(q * q, axis=-1, keepdims=True) * (1.0 / HEAD_DIM) + EPS) * qg_ref[...]
    s = _dot_nt((q * QK_SCALE).astype(BF16), kbuf_ref[...].astype(BF16))
    p_ref[...] = jnp.zeros_like(p_ref)
    inv_denoms = []
    for b in range(tile):
        own_rows, own_keys = slice(b * q_rows, (b + 1) * q_rows), slice(b * keys, (b + 1) * keys)
        logits = s[own_rows, own_keys] + bias_ref[...]
        sink = sink_ref[...]
        m = jnp.maximum(jnp.max(logits, axis=-1, keepdims=True), sink)
        p = jnp.exp(logits - m)
        inv_denoms.append(1.0 / (jnp.sum(p, axis=-1, keepdims=True) + jnp.exp(sink - m)))
        p_ref[own_rows, own_keys] = p.astype(BF16)
    o = _dot(p_ref[...], vbuf_ref[...].astype(BF16))
    for b in range(tile):
        rows = slice(b * dec_seq, (b + 1) * dec_seq)
        y_ref[b] = o[b * q_rows:(b + 1) * q_rows] * inv_denoms[b]
        ks_ref[b, 0:cache_w - dec_seq, :] = ck_ref[b, dec_seq:cache_w, :]
        vs_ref[b, 0:cache_w - dec_seq, :] = cv_ref[b, dec_seq:cache_w, :]
        ks_ref[b, cache_w - dec_seq:cache_w, :] = k_new[rows]
        vs_ref[b, cache_w - dec_seq:cache_w, :] = v_new[rows]


def _sample_attn(z, cache_k, cache_v, windows, layer, sinks, q_gain, k_gain, row0, dec_seq, col_q, attn_w, kv_w):
    _, dec_batch, cache_w, _ = cache_k.shape
    n_heads = attn_w // HEAD_DIM
    q_per_kv = n_heads // N_KV_HEADS
    tile = SAMPLE_BATCH_TILE
    r0 = row0 // (tile * dec_seq)
    k_blk, v_blk = (col_q + attn_w) // kv_w, (col_q + attn_w + kv_w) // kv_w
    keys = 2 * cache_w
    k_pos = np.full(keys, -1)
    k_pos[:cache_w] = PAST_LEN - cache_w + np.arange(cache_w)
    k_pos[cache_w:cache_w + dec_seq] = PAST_LEN + np.arange(dec_seq)
    bias = _bias_table(PAST_LEN + np.arange(dec_seq), k_pos, n_heads).reshape(n_heads * dec_seq, keys)
    sink_col = jnp.repeat(sinks, dec_seq).reshape(n_heads * dec_seq, 1)
    q = z[row0:, col_q:col_q + attn_w].reshape(dec_batch, dec_seq, N_KV_HEADS, q_per_kv, 1, HEAD_DIM)
    own_kv = np.eye(N_KV_HEADS, dtype=np.float32)[None, :, None, None, :, None]
    q = (q.transpose(0, 2, 3, 1, 4, 5) * own_kv).reshape(dec_batch, n_heads * dec_seq, kv_w)
    window = pl.BlockSpec((None, tile, cache_w, kv_w), lambda i: (layer, i, 0, 0))
    aliased = windows is not None
    in_specs = [
        pl.BlockSpec((tile, n_heads * dec_seq, kv_w), lambda i: (i, 0, 0)),
        pl.BlockSpec((tile * dec_seq, kv_w), lambda i: (r0 + i, k_blk)),
        pl.BlockSpec((tile * dec_seq, kv_w), lambda i: (r0 + i, v_blk)),
        window,
        window,
        pl.BlockSpec((1, kv_w), lambda i: (0, 0)),
        pl.BlockSpec((1, HEAD_DIM), lambda i: (0, 0)),
        pl.BlockSpec(bias.shape, lambda i: (0, 0)),
        pl.BlockSpec(sink_col.shape, lambda i: (0, 0)),
    ]
    args = [q, z, z, cache_k, cache_v, jnp.tile(q_gain, N_KV_HEADS).reshape(1, kv_w), k_gain.reshape(1, HEAD_DIM),
            jnp.asarray(bias), sink_col]
    if aliased:
        in_specs += [pl.BlockSpec(memory_space=pl.ANY)] * 2
        args += list(windows)
    y, ks, vs = pl.pallas_call(
        functools.partial(_sample_attn_kernel, dec_seq=dec_seq, aliased=aliased),
        grid=(dec_batch // tile,),
        in_specs=in_specs,
        out_specs=[pl.BlockSpec((tile, n_heads * dec_seq, kv_w), lambda i: (i, 0, 0)), window, window],
        out_shape=[jax.ShapeDtypeStruct(q.shape, F32),
                   jax.ShapeDtypeStruct(cache_k.shape, F32),
                   jax.ShapeDtypeStruct(cache_v.shape, F32)],
        scratch_shapes=[pltpu.VMEM((tile * keys, kv_w), F32)] * 2
        + [pltpu.VMEM((tile * n_heads * dec_seq, tile * keys), BF16)],
        input_output_aliases={len(args) - 2: 1, len(args) - 1: 2} if aliased else {},
        compiler_params=_compiler_params("parallel"),
        name="sample_attn",
    )(*args)
    y = y.reshape(dec_batch, N_KV_HEADS, q_per_kv, dec_seq, N_KV_HEADS, HEAD_DIM)
    y = jnp.stack([y[:, kv, :, :, kv] for kv in range(N_KV_HEADS)], axis=1)
    return y.transpose(0, 3, 1, 2, 4).reshape(dec_batch * dec_seq, attn_w), ks, vs


def kernel(x_prompt, x_sample, cache_k, cache_v, state_ssm_re, state_ssm_im, ffn1_norm, ffn1_w_gate, ffn1_w_up, ffn1_w_down, mix_norm, w_in, ssm_A_re, ssm_A_im, ssm_B_re, ssm_B_im, ssm_C_re, ssm_C_im, ssm_D, ssm_log_dt, ssm_w_glu, q_norm, k_norm, sinks, ssm_out_norm, attn_out_norm, w_out, ffn2_norm, ffn2_w_gate, ffn2_w_up, ffn2_w_down):
    bsz, seq, d = x_prompt.shape
    dec_batch, dec_seq, _ = x_sample.shape
    depth = w_in.shape[0]
    ssm_w = ssm_A_re.shape[1] * SSM_GROUP_CH
    attn_w = w_out.shape[1] - ssm_w
    kv_w = N_KV_HEADS * HEAD_DIM
    cache_w = cache_k.shape[2]
    n_prompt = bsz * seq
    n_tok = n_prompt + dec_batch * dec_seq

    xs = [x_prompt.reshape(n_prompt, d), x_sample.reshape(dec_batch * dec_seq, d)]
    n_groups = ssm_w // SSM_GROUP_CH
    n_merged = n_groups // SSM_MERGE
    zeros_state = jnp.zeros((n_merged, bsz, SSM_MERGE * SSM_STATE), F32)
    outs = [[] for _ in range(6)]
    cache_k_flat = cache_k.reshape(depth, dec_batch, cache_w, kv_w)
    cache_v_flat = cache_v.reshape(depth, dec_batch, cache_w, kv_w)
    windows = None
    ffn1_w, ffn2_w = (ffn1_w_gate, ffn1_w_up, ffn1_w_down), (ffn2_w_gate, ffn2_w_up, ffn2_w_down)
    all_layers = lambda a: a.reshape((-1,) + a.shape[2:])
    tab_p, tab_s = _ssm_prep(*_merge_groups(*map(all_layers, (ssm_A_re, ssm_A_im, ssm_log_dt, ssm_B_re, ssm_B_im,
                                                               ssm_C_re, ssm_C_im, ssm_D, ssm_w_glu)), SSM_MERGE),
                             t_small=dec_seq)
    for l in range(depth):
        x = _ffn(xs if l == 0 else [x], ffn1_norm[l], *ffn1_w, layer=l)
        z = _norm_matmul(x, mix_norm[l], w_in, layer=l)

        grp_major = lambda a: a.reshape(a.shape[0], n_merged, SSM_MERGE * SSM_STATE).transpose(1, 0, 2)
        seq_major = lambda a: a.transpose(1, 0, 2).reshape(a.shape[1], n_groups, SSM_STATE)
        y_ssm, hp_re, hp_im = _ssm_mix(z, None, tab_p, l, zeros_state, zeros_state, 0, n_prompt, bsz, PROMPT_CHUNK)
        y_ssm, hs_re, hs_im = _ssm_mix(z, y_ssm, tab_s, l, grp_major(state_ssm_re[l]), grp_major(state_ssm_im[l]),
                                       n_prompt, dec_batch * dec_seq, dec_batch, dec_seq)
        hp_re, hp_im, hs_re, hs_im = map(seq_major, (hp_re, hp_im, hs_re, hs_im))

        y_attn, pk, pv = _prompt_attn(z, sinks[l], q_norm[l], k_norm[l], bsz, seq, n_tok, ssm_w, attn_w, kv_w)
        ya_s, *windows = _sample_attn(z, cache_k_flat, cache_v_flat, windows, l, sinks[l], q_norm[l], k_norm[l],
                                      n_prompt, dec_seq, ssm_w, attn_w, kv_w)
        y_attn = lax.dynamic_update_slice(y_attn, ya_s, (n_prompt, 0))

        x = _out_proj(x, y_ssm, y_attn, ssm_out_norm[l], attn_out_norm[l], w_out, layer=l)
        if l < depth - 1:
            x = _ffn([x], ffn2_norm[l], *ffn2_w, layer=l)

        kv_shape = (-1, N_KV_HEADS, HEAD_DIM)
        for dst, val in zip(outs, (pk.reshape(bsz, *kv_shape), pv.reshape(bsz, *kv_shape), hp_re, hp_im, hs_re, hs_im)):
            dst.append(val)
    y_prompt, y_sample = _ffn([x], ffn2_norm[depth - 1], *ffn2_w, layer=depth - 1, out_rows=(n_prompt, n_tok - n_prompt))
    y_prompt, y_sample = y_prompt.reshape(bsz, seq, d), y_sample.reshape(dec_batch, dec_seq, d)
    pk, pv, hp_re, hp_im, hs_re, hs_im = (jnp.stack(o) for o in outs)
    sk, sv = (w.reshape(cache_k.shape) for w in windows)
    return (y_prompt, y_sample, pk, pv, hp_re, hp_im, sk, sv, hs_re, hs_im)
```

```python
import functools

import jax
import jax.numpy as jnp
import numpy as np
from jax import lax
from jax.experimental import pallas as pl
from jax.experimental.pallas import tpu as pltpu

F32 = jnp.float32
BF16 = jnp.bfloat16

EPS = 1e-6
SSM_GROUP_CH = 16
SSM_STATE = 64
HEAD_DIM = 64
N_KV_HEADS = 4
WINDOW = 128
PAST_LEN = 8192
QK_SCALE = HEAD_DIM ** -0.5
PROMPT_CHUNK = 8

TOKEN_TILE = 512
FFN_TOKEN_TILE = 1088
FFN_VMEM_LIMIT_BYTES = 60 * 1024 * 1024
FF_TILE = 256
SAMPLE_BATCH_TILE = 8
PROMPT_BLOCKS_PER_STEP = 4
VMEM_LIMIT_BYTES = 48 * 1024 * 1024
PROJ_VMEM_LIMIT_BYTES = 56 * 1024 * 1024

HIGHEST = lax.Precision.HIGHEST
LANES = 128
SSM_MERGE = 4


def _rms(x, gain):
    return x * lax.rsqrt(jnp.mean(x * x, axis=-1, keepdims=True) + EPS) * gain


def _dot(a, b, precision=None):
    return jnp.dot(a, b, preferred_element_type=F32, precision=precision)


def _dot_nt(a, b, precision=None):
    return lax.dot_general(a, b, (((1,), (1,)), ((), ())), preferred_element_type=F32, precision=precision)


def _compiler_params(*semantics):
    return pltpu.CompilerParams(dimension_semantics=semantics, vmem_limit_bytes=VMEM_LIMIT_BYTES)


def _ffn_kernel(*refs, n_in, n_out, split_row):
    x_refs, (g_ref, wg_ref, wu_ref, wd_ref) = refs[:n_in], refs[n_in:n_in + 4]
    o_refs, xn_ref = refs[n_in + 4:n_in + 4 + n_out], refs[-1]
    acc_ref = o_refs[0]
    i, j = pl.program_id(0), pl.program_id(1)
    boundary = i == pl.num_programs(0) - 1

    def with_x_tile(fn):
        if n_in == 1:
            fn(x_refs[0][...])
        else:
            pl.when(jnp.logical_not(boundary))(lambda: fn(x_refs[0][...]))
            pl.when(boundary)(lambda: fn(jnp.concatenate([x_refs[0][0:split_row, :], x_refs[1][...]], axis=0)))

    @pl.when(j == 0)
    def _():
        acc_ref[...] = jnp.zeros_like(acc_ref)

        def normalise(x):
            xn_ref[...] = _rms(x, g_ref[...]).astype(BF16)
        with_x_tile(normalise)

    xn = xn_ref[...]
    gate = _dot(xn, wg_ref[...].astype(BF16))
    up = _dot(xn, wu_ref[...].astype(BF16))
    h = (gate * jax.nn.sigmoid(gate) * up).astype(BF16)
    acc_ref[...] += _dot(h, wd_ref[...].astype(BF16))

    @pl.when(j == pl.num_programs(1) - 1)
    def _():
        def residual(x):
            acc_ref[...] = x + 0.5 * acc_ref[...]
        with_x_tile(residual)
        if n_out == 2:
            @pl.when(boundary)
            def _():
                o_refs[1][...] = acc_ref[split_row:, :]


def _ffn(xs, gain, w_gate, w_up, w_down, layer, out_rows=None):
    d = xs[0].shape[1]
    n_tok = sum(x.shape[0] for x in xs)
    d_ff = w_gate.shape[2]
    if len(xs) == 2 or out_rows is not None:
        tm = FFN_TOKEN_TILE if n_tok % FFN_TOKEN_TILE == 0 else n_tok
        n0, n1 = (x.shape[0] for x in xs) if len(xs) == 2 else out_rows
        split_row = n0 - (n_tok // tm - 1) * tm
        assert split_row + n1 == tm and split_row % 8 == 0
    else:
        tm = FFN_TOKEN_TILE if n_tok % FFN_TOKEN_TILE == 0 else TOKEN_TILE
        split_row = None
    tf = FF_TILE
    tile_spec = pl.BlockSpec((tm, d), lambda i, j: (i, 0))
    tail_spec = lambda rows: pl.BlockSpec((rows, d), lambda i, j: (0, 0))
    x_specs = [tile_spec] + ([tail_spec(xs[1].shape[0])] if len(xs) == 2 else [])
    if out_rows is None:
        out_specs, out_shape = [tile_spec], [jax.ShapeDtypeStruct((n_tok, d), F32)]
    else:
        out_specs = [tile_spec, tail_spec(out_rows[1])]
        out_shape = [jax.ShapeDtypeStruct((rows, d), F32) for rows in out_rows]
    outs = pl.pallas_call(
        functools.partial(_ffn_kernel, n_in=len(xs), n_out=len(out_specs), split_row=split_row),
        grid=(n_tok // tm, d_ff // tf),
        in_specs=x_specs + [
            pl.BlockSpec((1, d), lambda i, j: (0, 0)),
            pl.BlockSpec((None, d, tf), lambda i, j: (layer, 0, j)),
            pl.BlockSpec((None, d, tf), lambda i, j: (layer, 0, j)),
            pl.BlockSpec((None, tf, d), lambda i, j: (layer, j, 0)),
        ],
        out_specs=out_specs,
        out_shape=out_shape,
        scratch_shapes=[pltpu.VMEM((tm, d), BF16)],
        compiler_params=pltpu.CompilerParams(dimension_semantics=("arbitrary", "arbitrary"),
                                             vmem_limit_bytes=FFN_VMEM_LIMIT_BYTES),
        name="ffn",
    )(*xs, gain.reshape(1, d), w_gate, w_up, w_down)
    return outs[0] if out_rows is None else outs


def _norm_matmul_kernel(x_ref, g_ref, w_ref, o_ref, wb_ref):
    @pl.when(pl.program_id(0) == 0)
    def _():
        wb_ref[...] = w_ref[...].astype(BF16)

    o_ref[...] = _dot(_rms(x_ref[...], g_ref[...]).astype(BF16), wb_ref[...])


def _norm_matmul(x, gain, w, layer):
    n_tok, d = x.shape
    n_out = w.shape[2]
    tm = TOKEN_TILE
    return pl.pallas_call(
        _norm_matmul_kernel,
        grid=(n_tok // tm,),
        in_specs=[
            pl.BlockSpec((tm, d), lambda i: (i, 0)),
            pl.BlockSpec((1, d), lambda i: (0, 0)),
            pl.BlockSpec((None, d, n_out), lambda i: (layer, 0, 0), pipeline_mode=pl.Buffered(1)),
        ],
        out_specs=pl.BlockSpec((tm, n_out), lambda i: (i, 0)),
        out_shape=jax.ShapeDtypeStruct((n_tok, n_out), F32),
        scratch_shapes=[pltpu.VMEM((d, n_out), BF16)],
        compiler_params=pltpu.CompilerParams(dimension_semantics=("arbitrary",), vmem_limit_bytes=PROJ_VMEM_LIMIT_BYTES),
        name="in_proj",
    )(x, gain.reshape(1, d), w)


def _out_proj_kernel(x_ref, ys_ref, ya_ref, gs_ref, ga_ref, ws_ref, wa_ref, o_ref, wsb_ref, wab_ref):
    @pl.when(pl.program_id(0) == 0)
    def _():
        wsb_ref[...] = ws_ref[...].astype(BF16)
        wab_ref[...] = wa_ref[...].astype(BF16)

    ys = _rms(ys_ref[...], gs_ref[...]).astype(BF16)
    ya = _rms(ya_ref[...].astype(F32), ga_ref[...]).astype(BF16)
    o_ref[...] = x_ref[...] + _dot(ys, wsb_ref[...]) + _dot(ya, wab_ref[...])


def _out_proj(x, y_ssm, y_attn, g_ssm, g_attn, w_out, layer):
    n_tok, d = x.shape
    ssm_w, attn_w = y_ssm.shape[1], y_attn.shape[1]
    tm = TOKEN_TILE
    return pl.pallas_call(
        _out_proj_kernel,
        grid=(n_tok // tm,),
        in_specs=[
            pl.BlockSpec((tm, d), lambda i: (i, 0)),
            pl.BlockSpec((tm, ssm_w), lambda i: (i, 0)),
            pl.BlockSpec((tm, attn_w), lambda i: (i, 0)),
            pl.BlockSpec((1, ssm_w), lambda i: (0, 0)),
            pl.BlockSpec((1, attn_w), lambda i: (0, 0)),
            pl.BlockSpec((None, ssm_w, d), lambda i: (layer, 0, 0), pipeline_mode=pl.Buffered(1)),
            pl.BlockSpec((None, attn_w, d), lambda i: (layer, ssm_w // attn_w, 0), pipeline_mode=pl.Buffered(1)),
        ],
        out_specs=pl.BlockSpec((tm, d), lambda i: (i, 0)),
        out_shape=jax.ShapeDtypeStruct((n_tok, d), F32),
        scratch_shapes=[pltpu.VMEM((ssm_w, d), BF16), pltpu.VMEM((attn_w, d), BF16)],
        compiler_params=pltpu.CompilerParams(dimension_semantics=("arbitrary",), vmem_limit_bytes=PROJ_VMEM_LIMIT_BYTES),
        name="out_proj",
    )(x, y_ssm, y_attn, g_ssm.reshape(1, ssm_w), g_attn.reshape(1, attn_w), w_out, w_out)


def _ssm_prep_kernel(*refs, t_small):
    for gi in range(refs[0].shape[0]):
        _ssm_prep_group(*[r.at[pl.ds(gi, 1)] for r in refs], t_small=t_small)


def _ssm_prep_group(are_ref, aim_ref, ldt_ref, btre_ref, btim_ref, cre_ref, cim_ref, d_ref, wglu_ref,
                    *table_refs, t_small):
    per_set = len(table_refs) // 2
    full_refs, small_refs = table_refs[:per_set], table_refs[per_set:]
    a_re, a_im = are_ref[0], aim_ref[0]
    dt = jnp.exp(ldt_ref[0])
    mag = jnp.exp(a_re * dt)
    lb_re, lb_im = mag * jnp.cos(a_im * dt), mag * jnp.sin(a_im * dt)
    den = a_re * a_re + a_im * a_im
    q_re = ((lb_re - 1.0) * a_re + lb_im * a_im) / den
    q_im = (lb_im * a_re - (lb_re - 1.0) * a_im) / den
    bt_re, bt_im = btre_ref[0], btim_ref[0]
    bb_re = q_re * bt_re - q_im * bt_im
    bb_im = q_re * bt_im + q_im * bt_re
    c_re, c_im = cre_ref[0], cim_ref[0]

    pw_re, pw_im = [jnp.ones_like(lb_re)], [jnp.zeros_like(lb_im)]
    for _ in range(PROMPT_CHUNK):
        pr, pi = pw_re[-1], pw_im[-1]
        pw_re.append(pr * lb_re - pi * lb_im)
        pw_im.append(pr * lb_im + pi * lb_re)

    steps = PROMPT_CHUNK
    c = c_re.shape[0]
    w_re = jnp.concatenate([bb_re * pw_re[steps - 1 - s] - bb_im * pw_im[steps - 1 - s] for s in range(steps)], axis=0)
    w_im = jnp.concatenate([bb_re * pw_im[steps - 1 - s] + bb_im * pw_re[steps - 1 - s] for s in range(steps)], axis=0)
    zt_re = jnp.concatenate([c_re * pw_re[t + 1] - c_im * pw_im[t + 1] for t in range(steps)], axis=0)
    zt_im = jnp.concatenate([-(c_re * pw_im[t + 1] + c_im * pw_re[t + 1]) for t in range(steps)], axis=0)

    lag = _dot_nt(w_re, c_re, HIGHEST) - _dot_nt(w_im, c_im, HIGHEST)
    cols = []
    for t in range(steps):
        live = lag[(steps - 1 - t) * c:, :]
        cols.append(live if t == steps - 1 else
                    jnp.concatenate([live, jnp.zeros(((steps - 1 - t) * c, c), F32)], axis=0))
    m = jnp.concatenate(cols, axis=1)

    tiled = jnp.concatenate([jnp.concatenate([wglu_ref[0]] * steps, axis=0)] * steps, axis=1)
    shift = c.bit_length() - 1
    row_step = lax.shift_right_logical(lax.broadcasted_iota(jnp.int32, tiled.shape, 0), shift)
    col_step = lax.shift_right_logical(lax.broadcasted_iota(jnp.int32, tiled.shape, 1), shift)
    glu = jnp.where(row_step == col_step, tiled, 0.0)
    d_t = jnp.concatenate([d_ref[0]] * steps, axis=1)

    for t, refs in ((steps, full_refs), (t_small, small_refs)):
        m_ref, wre_ref, wim_ref, ztre_ref, ztim_ref, glu_ref, dt_ref, ore_ref, oim_ref = refs
        n, full = t * c, steps * c
        m_ref[0] = m[:n, :n].astype(BF16)
        wre_ref[0] = w_re[full - n:].astype(BF16)
        wim_ref[0] = w_im[full - n:].astype(BF16)
        ztre_ref[0] = zt_re[:n].astype(BF16)
        ztim_ref[0] = zt_im[:n].astype(BF16)
        glu_ref[0] = glu[:n, :n].astype(BF16)
        dt_ref[0] = d_t[:, :n]
        ore_ref[0] = pw_re[t]
        oim_ref[0] = pw_im[t]


def _merge_groups(a_re, a_im, log_dt, b_re, b_im, c_re, c_im, d_skip, w_glu, n):
    g, p = a_re.shape
    gm = g // n
    own = np.eye(n, dtype=np.float32)[None, :, :, None, None]

    def block_diag(x):
        r, q = x.shape[1:]
        return (x.reshape(gm, n, 1, r, q) * own).transpose(0, 1, 3, 2, 4).reshape(gm, n * r, n * q)

    lanes = lambda x: x.reshape(gm, 1, -1)
    return (lanes(a_re), lanes(a_im), lanes(jnp.repeat(log_dt, p)),
            block_diag(jnp.swapaxes(b_re, 1, 2)), block_diag(jnp.swapaxes(b_im, 1, 2)),
            block_diag(c_re), block_diag(c_im), lanes(d_skip), block_diag(w_glu))


def _ssm_prep(a_re, a_im, log_dt, bt_re, bt_im, c_re, c_im, d_skip, w_glu, t_small):
    g, c, p = c_re.shape
    assert c & (c - 1) == 0
    gps = LANES // c
    gs = lambda *shape: pl.BlockSpec((gps,) + shape, lambda i: (i,) + (0,) * len(shape))
    names = ("m", "w_re", "w_im", "zt_re", "zt_im", "glu", "d", "a_re", "a_im")
    out_specs, out_shape = [], []
    for t in (PROMPT_CHUNK, t_small):
        tc = t * c
        shapes = [(tc, tc), (tc, p), (tc, p), (tc, p), (tc, p), (tc, tc), (1, tc), (1, p), (1, p)]
        dtypes = [BF16] * 6 + [F32] * 3
        out_specs += [gs(*shape) for shape in shapes]
        out_shape += [jax.ShapeDtypeStruct((g,) + shape, dtype) for shape, dtype in zip(shapes, dtypes)]
    outs = pl.pallas_call(
        functools.partial(_ssm_prep_kernel, t_small=t_small),
        grid=(g // gps,),
        in_specs=[gs(1, p), gs(1, p), gs(1, p), gs(c, p), gs(c, p), gs(c, p), gs(c, p), gs(1, c), gs(c, c)],
        out_specs=out_specs,
        out_shape=out_shape,
        compiler_params=_compiler_params("parallel"),
        name="ssm_prep",
    )(a_re, a_im, log_dt, bt_re, bt_im, c_re, c_im, d_skip, w_glu)
    return dict(zip(names, outs[:len(names)])), dict(zip(names, outs[len(names):]))


def _ssm_mix_kernel(*refs, bsz, t, aliased):
    (u_ref, m_ref, wre_ref, wim_ref, ztre_ref, ztim_ref, are_ref, aim_ref, d_ref, glu_ref, h0re_ref, h0im_ref) = refs[:12]
    y_ref, hre_ref, him_ref, xs_ref, ug_ref, vre_ref, vim_ref, pre_ref, pim_ref = refs[12 + aliased:]
    rows = u_ref.shape[0] // t
    chunks = rows // bsz
    n_grp, tc = m_ref.shape[0], m_ref.shape[1]
    c = tc // t

    def token_rows(b, s):
        return pl.ds(b * chunks * t + s, chunks, stride=t)

    def chunk_rows(b):
        return pl.ds(b, chunks, stride=bsz)

    for s in range(t):
        if chunks == 1:
            xs_ref[s] = u_ref[pl.ds(s, rows, stride=t), :]
        else:
            for b in range(bsz):
                xs_ref[s, chunk_rows(b), :] = u_ref[token_rows(b, s), :]
    for gi in range(n_grp):
        lanes = slice(gi * c, (gi + 1) * c)
        u = jnp.concatenate([xs_ref[s, :, lanes] for s in range(t)], axis=1)
        ug_ref[gi] = u
        vre_ref[gi] = _dot(u.astype(BF16), wre_ref[gi])
        vim_ref[gi] = _dot(u.astype(BF16), wim_ref[gi])

    per_tile = max(1, 8 // bsz)
    tile_rows = per_tile * bsz

    def chunk_tile(i, h):
        sel = pl.ds(pl.multiple_of(i * tile_rows, tile_rows), tile_rows)
        nxt = []
        for gi in range(n_grp):
            h_re, h_im = h[2 * gi], h[2 * gi + 1]
            v_re, v_im = vre_ref[gi, sel, :], vim_ref[gi, sel, :]
            a_re, a_im = are_ref[gi], aim_ref[gi]
            ent_re, ent_im = [], []
            for j in range(per_tile):
                ent_re.append(h_re)
                ent_im.append(h_im)
                part = slice(j * bsz, (j + 1) * bsz)
                h_re, h_im = a_re * h_re - a_im * h_im + v_re[part], a_re * h_im + a_im * h_re + v_im[part]
            pre_ref[gi, sel, :] = jnp.concatenate(ent_re, axis=0)
            pim_ref[gi, sel, :] = jnp.concatenate(ent_im, axis=0)
            nxt += [h_re, h_im]
        return tuple(nxt)

    h = tuple(r[gi] for gi in range(n_grp) for r in (h0re_ref, h0im_ref))
    n_tiles = chunks // per_tile
    h = lax.fori_loop(0, n_tiles, chunk_tile, h, unroll=min(n_tiles, 4))
    for gi in range(n_grp):
        hre_ref[gi] = h[2 * gi]
        him_ref[gi] = h[2 * gi + 1]

    for gi in range(n_grp):
        u = ug_ref[gi]
        y = (_dot(u.astype(BF16), m_ref[gi]) + _dot_nt(pre_ref[gi].astype(BF16), ztre_ref[gi])
             + _dot_nt(pim_ref[gi].astype(BF16), ztim_ref[gi]) + d_ref[gi] * u)
        y = jax.nn.gelu(y)
        y = y * jax.nn.sigmoid(_dot(y.astype(BF16), glu_ref[gi]))
        for s in range(t):
            xs_ref[s, :, gi * c:(gi + 1) * c] = y[:, s * c:(s + 1) * c]
    for s in range(t):
        if chunks == 1:
            y_ref[pl.ds(s, rows, stride=t), :] = xs_ref[s]
        else:
            for b in range(bsz):
                y_ref[token_rows(b, s), :] = xs_ref[s, chunk_rows(b), :]


def _ssm_mix(z, y_prev, tab, layer, h0_re, h0_im, row0, n_rows, bsz, t):
    g, _, p = h0_re.shape
    tc = tab["m"].shape[1]
    n_tok = z.shape[0]
    c = tc // t
    width = g * c
    lanes = LANES
    gps = lanes // c
    rows = n_rows // t
    first = layer * (g // gps)
    blk = lambda *shape: pl.BlockSpec((gps,) + shape, lambda i: (i,) + (0,) * len(shape))
    tbl = lambda *shape: pl.BlockSpec((gps,) + shape, lambda i: (first + i,) + (0,) * len(shape))
    tok = pl.BlockSpec((n_rows, lanes), lambda i: (row0 // n_rows, i))
    aliased = y_prev is not None
    in_specs = [tok, tbl(tc, tc), tbl(tc, p), tbl(tc, p), tbl(tc, p), tbl(tc, p), tbl(1, p), tbl(1, p), tbl(1, tc),
                tbl(tc, tc), blk(bsz, p), blk(bsz, p)]
    args = [z, tab["m"], tab["w_re"], tab["w_im"], tab["zt_re"], tab["zt_im"], tab["a_re"], tab["a_im"], tab["d"],
            tab["glu"], h0_re, h0_im]
    if aliased:
        in_specs.append(pl.BlockSpec(memory_space=pl.ANY))
        args.append(y_prev)
    return pl.pallas_call(
        functools.partial(_ssm_mix_kernel, bsz=bsz, t=t, aliased=aliased),
        grid=(g // gps,),
        in_specs=in_specs,
        out_specs=[tok, blk(bsz, p), blk(bsz, p)],
        out_shape=[jax.ShapeDtypeStruct((n_tok, width), F32), jax.ShapeDtypeStruct((g, bsz, p), F32),
                   jax.ShapeDtypeStruct((g, bsz, p), F32)],
        scratch_shapes=[pltpu.VMEM((t, rows, lanes), F32), pltpu.VMEM((gps, rows, tc), F32)]
        + [pltpu.VMEM((gps, rows, p), F32)] * 4,
        input_output_aliases={len(args) - 1: 0} if aliased else {},
        compiler_params=_compiler_params("parallel"),
        name="ssm_mix",
    )(*args)


def _bias_table(q_pos, k_pos, n_heads):
    slopes = np.exp2(-8.0 * np.arange(1, n_heads + 1, dtype=np.float64) / n_heads)
    dist = q_pos[:, None] - k_pos[None, :]
    valid = (dist >= 0) & (dist < WINDOW) & (k_pos[None, :] >= 0)
    bias = -slopes[:, None, None] * dist[None].astype(np.float64)
    return np.where(valid[None], bias, -np.inf).astype(np.float32)


def _head_rms(x, gain):
    heads = [_rms(x[:, i:i + HEAD_DIM], gain) for i in range(0, x.shape[1], HEAD_DIM)]
    return jnp.concatenate(heads, axis=1)


def _softmax_sink_pv(logits, sink, v):
    m = jnp.maximum(jnp.max(logits, axis=-1, keepdims=True), sink)
    p = jnp.exp(logits - m)
    denom = jnp.sum(p, axis=-1, keepdims=True) + jnp.exp(sink - m)
    return _dot(p.astype(BF16), v) / denom


def _prompt_attn_kernel(sinks_ref, q_ref, k_ref, v_ref, qg_ref, kg_ref, bias_ref, y_ref, ks_ref, vs_ref,
                        kp_ref, vp_ref, *, n_heads):
    n = pl.program_id(1)

    @pl.when(n == 0)
    def _():
        kp_ref[...] = jnp.zeros_like(kp_ref)
        vp_ref[...] = jnp.zeros_like(vp_ref)

    blk = WINDOW
    half = 2 * HEAD_DIM
    assert n_heads // N_KV_HEADS == 4 and half == 128
    k_all = _head_rms(k_ref[...], kg_ref[...])
    v_all = v_ref[...]
    from_prev = (lax.broadcasted_iota(jnp.int32, (blk, blk), 0) > lax.broadcasted_iota(jnp.int32, (blk, blk), 1))
    low_lanes = lax.broadcasted_iota(jnp.int32, (2 * blk, half), 1) < HEAD_DIM
    low8 = lax.broadcasted_iota(jnp.int32, (8, half), 1) < HEAD_DIM
    pick = jnp.concatenate([jnp.where(low8, 1.0, 0.0), jnp.where(low8, 0.0, 1.0)], axis=0)
    k_prev, v_prev = kp_ref[...], vp_ref[...]
    for sub in range(q_ref.shape[0] // blk):
        rows = slice(sub * blk, (sub + 1) * blk)
        k_cur, v_cur = k_all[rows], v_all[rows]
        table = jnp.minimum(n, 1) if sub == 0 else 1
        _attend_block(q_ref[rows, :], k_prev, v_prev, k_cur, v_cur, qg_ref[...], table, sinks_ref, bias_ref,
                      y_ref.at[rows, :], from_prev, low_lanes, pick)
        k_prev, v_prev = k_cur, v_cur
    kp_ref[...] = k_prev
    vp_ref[...] = v_prev

    @pl.when(n == pl.num_programs(1) - 1)
    def _():
        ks_ref[0] = k_prev
        vs_ref[0] = v_prev


def _attend_block(q_all, k_prev, v_prev, k_cur, v_cur, q_gain, table, sinks_ref, bias_ref, y_ref,
                  from_prev, low_lanes, pick):
    blk = q_all.shape[0]
    half = 2 * HEAD_DIM
    k2 = jnp.concatenate([k_prev, k_cur], axis=0)
    v2t = jnp.concatenate([v_prev, v_cur], axis=0).T.astype(BF16)
    qs = (q_all * q_gain * QK_SCALE).astype(BF16)
    qsq = q_all * q_all
    for kv in range(N_KV_HEADS):
        k_tile = k2[:, (kv // 2) * half:(kv // 2 + 1) * half]
        k_own = jnp.where(low_lanes if kv % 2 == 0 else ~low_lanes, k_tile, 0.0)
        k_swap = pltpu.roll(k_own, HEAD_DIM, axis=1)
        k_lhs = jnp.concatenate([k_own, k_swap] if kv % 2 == 0 else [k_swap, k_own], axis=0).astype(BF16)
        tiles = slice(2 * kv * half, (2 * kv + 1) * half), slice((2 * kv + 1) * half, (2 * kv + 2) * half)
        q_rows = jnp.concatenate([qs[:, t] for t in tiles], axis=0)
        qsq_rows = jnp.concatenate([qsq[:, t] for t in tiles], axis=0)
        s = _dot_nt(k_lhs, q_rows)
        ssq = _dot_nt(pick, qsq_rows, HIGHEST)
        probs, inv_denoms = [], []
        for i in range(4):
            odd, tile = i % 2, i // 2
            lanes = slice(tile * blk, (tile + 1) * blk)
            s_h = s[odd * 2 * blk:(odd + 1) * 2 * blk, lanes]
            q_rms = lax.rsqrt(ssq[odd * 8:odd * 8 + 1, lanes] * (1.0 / HEAD_DIM) + EPS)
            logits = jnp.where(from_prev, s_h[:blk], s_h[blk:]) * q_rms + bias_ref[table, 4 * kv + i]
            sink = sinks_ref[4 * kv + i]
            m = jnp.maximum(jnp.max(logits, axis=0, keepdims=True), sink)
            p = jnp.exp(logits - m)
            inv_denoms.append(1.0 / (jnp.sum(p, axis=0, keepdims=True) + jnp.exp(sink - m)))
            probs.append(jnp.concatenate([jnp.where(from_prev, p, 0.0), jnp.where(from_prev, 0.0, p)], axis=0))
        o = _dot(v2t[kv * HEAD_DIM:(kv + 1) * HEAD_DIM], jnp.concatenate(probs, axis=1).astype(BF16))
        for tile in range(2):
            pair = [o[:, i * blk:(i + 1) * blk] * inv_denoms[i] for i in (2 * tile, 2 * tile + 1)]
            y_ref[:, tiles[tile]] = jnp.concatenate(pair, axis=0).T.astype(y_ref.dtype)


def _prompt_attn(z, sinks, q_gain, k_gain, bsz, seq, n_tok, col_q, attn_w, kv_w):
    blk = WINDOW
    rows = PROMPT_BLOCKS_PER_STEP * blk
    nb = seq // rows
    assert seq % rows == 0
    n_heads = attn_w // HEAD_DIM
    row = lambda b, n: b * nb + n
    q_blk, k_blk, v_blk = col_q // attn_w, (col_q + attn_w) // kv_w, (col_q + attn_w + kv_w) // kv_w
    qi, kj = np.arange(blk)[:, None], np.arange(blk)[None, :]
    first, later = [np.take_along_axis(_bias_table(first_pos + np.arange(blk), first_pos - blk + np.arange(2 * blk), n_heads),
                                       np.broadcast_to(np.where(kj > qi, kj, kj + blk), (n_heads, blk, blk)), axis=2)
                    for first_pos in (0, blk)]
    bias = np.stack([first, later]).swapaxes(2, 3)
    return pl.pallas_call(
        functools.partial(_prompt_attn_kernel, n_heads=n_heads),
        grid=(bsz, nb),
        in_specs=[
            pl.BlockSpec(memory_space=pltpu.SMEM),
            pl.BlockSpec((rows, attn_w), lambda b, n: (row(b, n), q_blk)),
            pl.BlockSpec((rows, kv_w), lambda b, n: (row(b, n), k_blk)),
            pl.BlockSpec((rows, kv_w), lambda b, n: (row(b, n), v_blk)),
            pl.BlockSpec((1, attn_w), lambda b, n: (0, 0)),
            pl.BlockSpec((1, HEAD_DIM), lambda b, n: (0, 0)),
            pl.BlockSpec(bias.shape, lambda b, n: (0, 0, 0, 0)),
        ],
        out_specs=[
            pl.BlockSpec((rows, attn_w), lambda b, n: (row(b, n), 0)),
            pl.BlockSpec((1, blk, kv_w), lambda b, n: (b, 0, 0)),
            pl.BlockSpec((1, blk, kv_w), lambda b, n: (b, 0, 0)),
        ],
        out_shape=[jax.ShapeDtypeStruct((n_tok, attn_w), BF16),
                   jax.ShapeDtypeStruct((bsz, blk, kv_w), F32),
                   jax.ShapeDtypeStruct((bsz, blk, kv_w), F32)],
        scratch_shapes=[pltpu.VMEM((blk, kv_w), F32)] * 2,
        compiler_params=_compiler_params("parallel", "arbitrary"),
        name="prompt_attn",
    )(sinks, z, z, z, jnp.tile(q_gain, n_heads).reshape(1, attn_w), k_gain.reshape(1, HEAD_DIM), jnp.asarray(bias))


def _sample_attn_kernel(*refs, dec_seq, aliased):
    q_ref, k_ref, v_ref, ck_ref, cv_ref, qg_ref, kg_ref, bias_ref, sink_ref = refs[:9]
    y_ref, ks_ref, vs_ref, kbuf_ref, vbuf_ref, p_ref = refs[9 + 2 * aliased:]
    tile, q_rows, _ = q_ref.shape
    cache_w = ck_ref.shape[1]
    keys = kbuf_ref.shape[0] // tile
    k_new = _head_rms(k_ref[...], kg_ref[...])
    v_new = v_ref[...]
    pad = jnp.zeros((keys - cache_w, kbuf_ref.shape[1]), F32)
    for b in range(tile):
        rows = slice(b * dec_seq, (b + 1) * dec_seq)
        kbuf_ref[b * keys:b * keys + cache_w, :] = ck_ref[b]
        vbuf_ref[b * keys:b * keys + cache_w, :] = cv_ref[b]
        kbuf_ref[b * keys + cache_w:(b + 1) * keys, :] = pad
        vbuf_ref[b * keys + cache_w:(b + 1) * keys, :] = pad
        kbuf_ref[b * keys + cache_w:b * keys + cache_w + dec_seq, :] = k_new[rows]
        vbuf_ref[b * keys + cache_w:b * keys + cache_w + dec_seq, :] = v_new[rows]
    q = q_ref[...].reshape(tile * q_rows, q_ref.shape[2])
    q = q * lax.rsqrt(jnp.sum(q * q, axis=-1, keepdims=True) * (1.0 / HEAD_DIM) + EPS) * qg_ref[...]
    s = _dot_nt((q * QK_SCALE).astype(BF16), kbuf_ref[...].astype(BF16))
    p_ref[...] = jnp.zeros_like(p_ref)
    inv_denoms = []
    for b in range(tile):
        own_rows, own_keys = slice(b * q_rows, (b + 1) * q_rows), slice(b * keys, (b + 1) * keys)
        logits = s[own_rows, own_keys] + bias_ref[...]
        sink = sink_ref[...]
        m = jnp.maximum(jnp.max(logits, axis=-1, keepdims=True), sink)
        p = jnp.exp(logits - m)
        inv_denoms.append(1.0 / (jnp.sum(p, axis=-1, keepdims=True) + jnp.exp(sink - m)))
        p_ref[own_rows, own_keys] = p.astype(BF16)
    o = _dot(p_ref[...], vbuf_ref[...].astype(BF16))
    for b in range(tile):
        rows = slice(b * dec_seq, (b + 1) * dec_seq)
        y_ref[b] = o[b * q_rows:(b + 1) * q_rows] * inv_denoms[b]
        ks_ref[b, 0:cache_w - dec_seq, :] = ck_ref[b, dec_seq:cache_w, :]
        vs_ref[b, 0:cache_w - dec_seq, :] = cv_ref[b, dec_seq:cache_w, :]
        ks_ref[b, cache_w - dec_seq:cache_w, :] = k_new[rows]
        vs_ref[b, cache_w - dec_seq:cache_w, :] = v_new[rows]


def _sample_attn(z, cache_k, cache_v, windows, layer, sinks, q_gain, k_gain, row0, dec_seq, col_q, attn_w, kv_w):
    _, dec_batch, cache_w, _ = cache_k.shape
    n_heads = attn_w // HEAD_DIM
    q_per_kv = n_heads // N_KV_HEADS
    tile = SAMPLE_BATCH_TILE
    r0 = row0 // (tile * dec_seq)
    k_blk, v_blk = (col_q + attn_w) // kv_w, (col_q + attn_w + kv_w) // kv_w
    keys = 2 * cache_w
    k_pos = np.full(keys, -1)
    k_pos[:cache_w] = PAST_LEN - cache_w + np.arange(cache_w)
    k_pos[cache_w:cache_w + dec_seq] = PAST_LEN + np.arange(dec_seq)
    bias = _bias_table(PAST_LEN + np.arange(dec_seq), k_pos, n_heads).reshape(n_heads * dec_seq, keys)
    sink_col = jnp.repeat(sinks, dec_seq).reshape(n_heads * dec_seq, 1)
    q = z[row0:, col_q:col_q + attn_w].reshape(dec_batch, dec_seq, N_KV_HEADS, q_per_kv, 1, HEAD_DIM)
    own_kv = np.eye(N_KV_HEADS, dtype=np.float32)[None, :, None, None, :, None]
    q = (q.transpose(0, 2, 3, 1, 4, 5) * own_kv).reshape(dec_batch, n_heads * dec_seq, kv_w)
    window = pl.BlockSpec((None, tile, cache_w, kv_w), lambda i: (layer, i, 0, 0))
    aliased = windows is not None
    in_specs = [
        pl.BlockSpec((tile, n_heads * dec_seq, kv_w), lambda i: (i, 0, 0)),
        pl.BlockSpec((tile * dec_seq, kv_w), lambda i: (r0 + i, k_blk)),
        pl.BlockSpec((tile * dec_seq, kv_w), lambda i: (r0 + i, v_blk)),
        window,
        window,
        pl.BlockSpec((1, kv_w), lambda i: (0, 0)),
        pl.BlockSpec((1, HEAD_DIM), lambda i: (0, 0)),
        pl.BlockSpec(bias.shape, lambda i: (0, 0)),
        pl.BlockSpec(sink_col.shape, lambda i: (0, 0)),
    ]
    args = [q, z, z, cache_k, cache_v, jnp.tile(q_gain, N_KV_HEADS).reshape(1, kv_w), k_gain.reshape(1, HEAD_DIM),
            jnp.asarray(bias), sink_col]
    if aliased:
        in_specs += [pl.BlockSpec(memory_space=pl.ANY)] * 2
        args += list(windows)
    y, ks, vs = pl.pallas_call(
        functools.partial(_sample_attn_kernel, dec_seq=dec_seq, aliased=aliased),
        grid=(dec_batch // tile,),
        in_specs=in_specs,
        out_specs=[pl.BlockSpec((tile, n_heads * dec_seq, kv_w), lambda i: (i, 0, 0)), window, window],
        out_shape=[jax.ShapeDtypeStruct(q.shape, F32),
                   jax.ShapeDtypeStruct(cache_k.shape, F32),
                   jax.ShapeDtypeStruct(cache_v.shape, F32)],
        scratch_shapes=[pltpu.VMEM((tile * keys, kv_w), F32)] * 2
        + [pltpu.VMEM((tile * n_heads * dec_seq, tile * keys), BF16)],
        input_output_aliases={len(args) - 2: 1, len(args) - 1: 2} if aliased else {},
        compiler_params=_compiler_params("parallel"),
        name="sample_attn",
    )(*args)
    y = y.reshape(dec_batch, N_KV_HEADS, q_per_kv, dec_seq, N_KV_HEADS, HEAD_DIM)
    y = jnp.stack([y[:, kv, :, :, kv] for kv in range(N_KV_HEADS)], axis=1)
    return y.transpose(0, 3, 1, 2, 4).reshape(dec_batch * dec_seq, attn_w), ks, vs


def kernel(x_prompt, x_sample, cache_k, cache_v, state_ssm_re, state_ssm_im, ffn1_norm, ffn1_w_gate, ffn1_w_up, ffn1_w_down, mix_norm, w_in, ssm_A_re, ssm_A_im, ssm_B_re, ssm_B_im, ssm_C_re, ssm_C_im, ssm_D, ssm_log_dt, ssm_w_glu, q_norm, k_norm, sinks, ssm_out_norm, attn_out_norm, w_out, ffn2_norm, ffn2_w_gate, ffn2_w_up, ffn2_w_down):
    bsz, seq, d = x_prompt.shape
    dec_batch, dec_seq, _ = x_sample.shape
    depth = w_in.shape[0]
    ssm_w = ssm_A_re.shape[1] * SSM_GROUP_CH
    attn_w = w_out.shape[1] - ssm_w
    kv_w = N_KV_HEADS * HEAD_DIM
    cache_w = cache_k.shape[2]
    n_prompt = bsz * seq
    n_tok = n_prompt + dec_batch * dec_seq

    xs = [x_prompt.reshape(n_prompt, d), x_sample.reshape(dec_batch * dec_seq, d)]
    n_groups = ssm_w // SSM_GROUP_CH
    n_merged = n_groups // SSM_MERGE
    zeros_state = jnp.zeros((n_merged, bsz, SSM_MERGE * SSM_STATE), F32)
    outs = [[] for _ in range(6)]
    cache_k_flat = cache_k.reshape(depth, dec_batch, cache_w, kv_w)
    cache_v_flat = cache_v.reshape(depth, dec_batch, cache_w, kv_w)
    windows = None
    ffn1_w, ffn2_w = (ffn1_w_gate, ffn1_w_up, ffn1_w_down), (ffn2_w_gate, ffn2_w_up, ffn2_w_down)
    all_layers = lambda a: a.reshape((-1,) + a.shape[2:])
    tab_p, tab_s = _ssm_prep(*_merge_groups(*map(all_layers, (ssm_A_re, ssm_A_im, ssm_log_dt, ssm_B_re, ssm_B_im,
                                                               ssm_C_re, ssm_C_im, ssm_D, ssm_w_glu)), SSM_MERGE),
                             t_small=dec_seq)
    for l in range(depth):
        x = _ffn(xs if l == 0 else [x], ffn1_norm[l], *ffn1_w, layer=l)
        z = _norm_matmul(x, mix_norm[l], w_in, layer=l)

        grp_major = lambda a: a.reshape(a.shape[0], n_merged, SSM_MERGE * SSM_STATE).transpose(1, 0, 2)
        seq_major = lambda a: a.transpose(1, 0, 2).reshape(a.shape[1], n_groups, SSM_STATE)
        y_ssm, hp_re, hp_im = _ssm_mix(z, None, tab_p, l, zeros_state, zeros_state, 0, n_prompt, bsz, PROMPT_CHUNK)
        y_ssm, hs_re, hs_im = _ssm_mix(z, y_ssm, tab_s, l, grp_major(state_ssm_re[l]), grp_major(state_ssm_im[l]),
                                       n_prompt, dec_batch * dec_seq, dec_batch, dec_seq)
        hp_re, hp_im, hs_re, hs_im = map(seq_major, (hp_re, hp_im, hs_re, hs_im))

        y_attn, pk, pv = _prompt_attn(z, sinks[l], q_norm[l], k_norm[l], bsz, seq, n_tok, ssm_w, attn_w, kv_w)
        ya_s, *windows = _sample_attn(z, cache_k_flat, cache_v_flat, windows, l, sinks[l], q_norm[l], k_norm[l],
                                      n_prompt, dec_seq, ssm_w, attn_w, kv_w)
        y_attn = lax.dynamic_update_slice(y_attn, ya_s.astype(y_attn.dtype), (n_prompt, 0))

        x = _out_proj(x, y_ssm, y_attn, ssm_out_norm[l], attn_out_norm[l], w_out, layer=l)
        if l < depth - 1:
            x = _ffn([x], ffn2_norm[l], *ffn2_w, layer=l)

        kv_shape = (-1, N_KV_HEADS, HEAD_DIM)
        for dst, val in zip(outs, (pk.reshape(bsz, *kv_shape), pv.reshape(bsz, *kv_shape), hp_re, hp_im, hs_re, hs_im)):
            dst.append(val)
    y_prompt, y_sample = _ffn([x], ffn2_norm[depth - 1], *ffn2_w, layer=depth - 1, out_rows=(n_prompt, n_tok - n_prompt))
    y_prompt, y_sample = y_prompt.reshape(bsz, seq, d), y_sample.reshape(dec_batch, dec_seq, d)
    pk, pv, hp_re, hp_im, hs_re, hs_im = (jnp.stack(o) for o in outs)
    sk, sv = (w.reshape(cache_k.shape) for w in windows)
    return (y_prompt, y_sample, pk, pv, hp_re, hp_im, sk, sv, hs_re, hs_im)
```
